```python
import math
import jax
import jax.numpy as jnp
from jax import lax
import numpy as np

D_MODEL = 2048
BATCH = 2
SEQ = 4096
DEPTH = 4
DEC_BATCH = 8
DEC_SEQ = 1
PAST_LEN = 16384
PAGE_SIZE = 128

N_A_LAYERS = DEPTH // 2
N_B_LAYERS = DEPTH - N_A_LAYERS
HEAD_DIM = 128
SSM_HEAD_DIM = 64
SSM_D_INNER = 3 * D_MODEL // 4
SSM_HEADS = SSM_D_INNER // SSM_HEAD_DIM
SSM_GROUPS = 4
SSM_D_STATE = 128
CONV_W = 4
CONV_DIM = SSM_D_INNER + 2 * SSM_GROUPS * SSM_D_STATE
SSD_CHUNK = 128
MEM_LEN = 256
MEM_HEADS = 4
MEM_WIDTH = MEM_HEADS * HEAD_DIM
DIL_GROUPS = ((128, 1), (512, 4), (2048, 16))
N_DIL_GROUPS = len(DIL_GROUPS)
DIL_HEADS = 4
DIL_Q_WIDTH = N_DIL_GROUPS * DIL_HEADS * HEAD_DIM
KV_WIDTH = DIL_HEADS * HEAD_DIM
WIN_MAX = max(w for w, _ in DIL_GROUPS)
Q_BLOCK = 128
D_FF = 4 * D_MODEL
ROPE_THETA = 10000.0
EPS = 1e-6
A_IN_WIDTH = SSM_D_INNER + CONV_DIM + SSM_HEADS + MEM_WIDTH
A_OUT_WIDTH = SSM_D_INNER + MEM_WIDTH
B_IN_WIDTH = DIL_Q_WIDTH + MEM_WIDTH
B_OUT_WIDTH = DIL_HEADS * HEAD_DIM + MEM_WIDTH

kernel_name = 'yoco_ssd_dilated_swa_memxattn_step'

F32 = jnp.float32


def rmsnorm(x, g):
    xf = x.astype(F32)
    y = xf * lax.rsqrt(jnp.mean(xf * xf, axis=-1, keepdims=True) + EPS)
    return (y * g.astype(F32)).astype(x.dtype)


def rope(x, pos):
    half = HEAD_DIM // 2
    inv = ROPE_THETA ** (-jnp.arange(half, dtype=F32) * (2.0 / HEAD_DIM))
    ang = pos.astype(F32)[:, None] * inv[None, :]
    cos = jnp.cos(ang)[None, :, None, :]
    sin = jnp.sin(ang)[None, :, None, :]
    xf = x.astype(F32)
    x1, x2 = xf[..., :half], xf[..., half:]
    return jnp.concatenate([x1 * cos - x2 * sin, x2 * cos + x1 * sin], axis=-1).astype(x.dtype)


def mlp(x, w_up, w_down):
    return jnp.square(jax.nn.relu(x @ w_up)) @ w_down


def segsum(a):
    t = a.shape[-1]
    xx = jnp.broadcast_to(a[..., :, None], a.shape + (t,))
    strict = jnp.tril(jnp.ones((t, t), dtype=bool), -1)
    ss = jnp.cumsum(jnp.where(strict, xx, 0.0), axis=-2)
    return jnp.where(jnp.tril(jnp.ones((t, t), dtype=bool)), ss, -jnp.inf)


def ssd_scan(x, dt, a, b, c, h0, chunk):
    bt, t, g, r, p = x.shape
    n = b.shape[-1]
    nc = t // chunk
    xd = (x * dt[..., None]).reshape(bt, nc, chunk, g, r, p)
    da = (dt * a).reshape(bt, nc, chunk, g, r).transpose(0, 3, 4, 1, 2)
    bb = b.reshape(bt, nc, chunk, g, n)
    cc = c.reshape(bt, nc, chunk, g, n)
    a_cs = jnp.cumsum(da, axis=-1)
    decay_in = jnp.exp(segsum(da))
    cb = jnp.einsum('bclgn,bcsgn->bgcls', cc, bb)
    y_diag = jnp.einsum('bgcls,bgrcls,bcsgrp->bclgrp', cb, decay_in, xd)
    decay_states = jnp.exp(a_cs[..., -1:] - a_cs)
    states = jnp.einsum('bclgn,bgrcl,bclgrp->bcgrpn', bb, decay_states, xd)
    states = jnp.concatenate([h0[:, None], states], axis=1)
    chunk_tot = jnp.pad(a_cs[..., -1], ((0, 0), (0, 0), (0, 0), (1, 0)))
    decay_chunk = jnp.exp(segsum(chunk_tot))
    new_states = jnp.einsum('bgrzc,bcgrpn->bzgrpn', decay_chunk, states)
    y_off = jnp.einsum('bclgn,bcgrpn,bgrcl->bclgrp', cc, new_states[:, :-1], jnp.exp(a_cs))
    y = (y_diag + y_off).reshape(bt, t, g, r, p)
    return y, new_states[:, -1]


def mem_kv(mem, g_in, w_k, w_v, k_norm_g):
    bt, m, _ = mem.shape
    u = rmsnorm(mem, g_in)
    k = rmsnorm((u @ w_k).reshape(bt, m, MEM_HEADS, HEAD_DIM), k_norm_g)
    v = (u @ w_v).reshape(bt, m, MEM_HEADS, HEAD_DIM)
    return k, v


def mem_attend(qm, q_norm_g, mk, mv):
    bt, t, _ = qm.shape
    q = rmsnorm(qm.reshape(bt, t, MEM_HEADS, HEAD_DIM), q_norm_g)
    s = jnp.einsum('bthd,bmhd->bhtm', q.astype(F32), mk.astype(F32)) * (HEAD_DIM ** -0.5)
    p = jax.nn.softmax(s, axis=-1)
    o = jnp.einsum('bhtm,bmhd->bthd', p, mv.astype(F32))
    return o.reshape(bt, t, MEM_WIDTH)


def dilated_block(q, k, v, q_idx):
    scale = HEAD_DIM ** -0.5
    outs, lses = [], []
    for gi, (win, dil) in enumerate(DIL_GROUPS):
        idx = q_idx[:, None] - dil * jnp.arange(win // dil + 1, dtype=jnp.int32)[None, :]
        valid = idx >= 0
        idx = jnp.maximum(idx, 0)
        kg = k[:, idx].astype(F32)
        vg = v[:, idx].astype(F32)
        s = jnp.einsum('bqhd,bqjhd->bqhj', q[:, :, gi].astype(F32), kg) * scale
        s = jnp.where(valid[None, :, None, :], s, -jnp.inf)
        m = jnp.max(s, axis=-1, keepdims=True)
        e = jnp.exp(s - m)
        den = jnp.sum(e, axis=-1, keepdims=True)
        outs.append(jnp.einsum('bqhj,bqjhd->bqhd', e / den, vg))
        lses.append((m + jnp.log(den))[..., 0])
    w = jax.nn.softmax(jnp.stack(lses), axis=0)
    return jnp.sum(w[..., None] * jnp.stack(outs), axis=0)


def dilated_attention(q, k, v, q_idx, q_block):
    bt, t = q.shape[:2]
    nb = t // q_block
    qb = jnp.moveaxis(q.reshape((bt, nb, q_block) + q.shape[2:]), 1, 0)
    ib = q_idx.reshape(nb, q_block)
    ob = lax.map(lambda a: dilated_block(a[0], k, v, a[1]), (qb, ib))
    return jnp.moveaxis(ob, 0, 1).reshape(bt, t, DIL_HEADS * HEAD_DIM)


def a_layer(x, mk, mv, conv_prev, h0, chunk, norm_g, mem_q_g, mlp_g, w_up, w_down,
            w_in, conv_w, conv_b, dt_bias, a_log, d_skip, gate_g, w_out):
    bt, t, _ = x.shape
    proj = rmsnorm(x, norm_g) @ w_in
    s1 = SSM_D_INNER
    s2 = s1 + CONV_DIM
    s3 = s2 + SSM_HEADS
    z, xbc, dt_raw, qm = proj[..., :s1], proj[..., s1:s2], proj[..., s2:s3], proj[..., s3:]
    xbc_ext = jnp.concatenate([conv_prev.astype(xbc.dtype), xbc], axis=1)
    new_conv = xbc_ext[:, t:]
    conv = lax.conv_general_dilated(xbc_ext, conv_w.astype(xbc.dtype)[:, None, :],
                                    window_strides=(1,), padding='VALID',
                                    dimension_numbers=('NWC', 'WIO', 'NWC'),
                                    feature_group_count=CONV_DIM)
    xbc = jax.nn.silu(conv + conv_b.astype(conv.dtype))
    gn = SSM_GROUPS * SSM_D_STATE
    r = SSM_HEADS // SSM_GROUPS
    x5 = xbc[..., :SSM_D_INNER].reshape(bt, t, SSM_GROUPS, r, SSM_HEAD_DIM).astype(F32)
    bs = xbc[..., SSM_D_INNER:SSM_D_INNER + gn].reshape(bt, t, SSM_GROUPS, SSM_D_STATE).astype(F32)
    cs = xbc[..., SSM_D_INNER + gn:].reshape(bt, t, SSM_GROUPS, SSM_D_STATE).astype(F32)
    dt = jax.nn.softplus(dt_raw.astype(F32) + dt_bias.astype(F32)).reshape(bt, t, SSM_GROUPS, r)
    a = -jnp.exp(a_log.astype(F32)).reshape(SSM_GROUPS, r)
    h0 = h0.astype(F32).reshape(bt, SSM_GROUPS, r, SSM_HEAD_DIM, SSM_D_STATE)
    y, h_new = ssd_scan(x5, dt, a, bs, cs, h0, chunk)
    y = y + d_skip.astype(F32).reshape(SSM_GROUPS, r)[:, :, None] * x5
    gated = y.reshape(bt, t, SSM_D_INNER) * jax.nn.silu(z.astype(F32))
    gg = gated.reshape(bt, t, SSM_GROUPS, SSM_D_INNER // SSM_GROUPS)
    gg = gg * lax.rsqrt(jnp.mean(gg * gg, axis=-1, keepdims=True) + EPS)
    ssd_out = gg.reshape(bt, t, SSM_D_INNER) * gate_g.astype(F32)
    mem_out = mem_attend(qm, mem_q_g, mk, mv)
    mix = jnp.concatenate([ssd_out.astype(x.dtype), mem_out.astype(x.dtype)], axis=-1)
    x = x + mix @ w_out
    x = x + mlp(rmsnorm(x, mlp_g), w_up, w_down)
    h_new = h_new.reshape(bt, SSM_HEADS, SSM_HEAD_DIM, SSM_D_STATE).astype(x.dtype)
    return x, new_conv, h_new


def shared_kv(h, pos, kv_norm_g, w_k, w_v, k_norm_g):
    bt, t, _ = h.shape
    u = rmsnorm(h, kv_norm_g)
    k = rope(rmsnorm((u @ w_k).reshape(bt, t, DIL_HEADS, HEAD_DIM), k_norm_g), pos)
    v = (u @ w_v).reshape(bt, t, DIL_HEADS, HEAD_DIM)
    return k, v


def b_layer(x, pos, k_all, v_all, q_idx, q_block, mk, mv, norm_g, mem_q_g, mlp_g,
            w_up, w_down, w_in, q_norm_g, w_out):
    bt, t, _ = x.shape
    proj = rmsnorm(x, norm_g) @ w_in
    qd = proj[..., :DIL_Q_WIDTH].reshape(bt, t, N_DIL_GROUPS * DIL_HEADS, HEAD_DIM)
    qd = rope(rmsnorm(qd, q_norm_g), pos).reshape(bt, t, N_DIL_GROUPS, DIL_HEADS, HEAD_DIM)
    att = dilated_attention(qd, k_all, v_all, q_idx, q_block)
    mem_out = mem_attend(proj[..., DIL_Q_WIDTH:], mem_q_g, mk, mv)
    mix = jnp.concatenate([att.astype(x.dtype), mem_out.astype(x.dtype)], axis=-1)
    x = x + mix @ w_out
    x = x + mlp(rmsnorm(x, mlp_g), w_up, w_down)
    return x


def setup_inputs(seed: int = 0) -> dict:
    key = jax.random.key(seed)
    ks = iter(jax.random.split(key, 48))

    def nrm(shape, scale):
        return jax.random.normal(next(ks), shape, F32) * scale

    def gain(shape):
        return 1.0 + 0.02 * jax.random.normal(next(ks), shape, F32)

    l_buf = min(WIN_MAX, PAST_LEN)
    dt_init = jnp.exp(jax.random.uniform(next(ks), (N_A_LAYERS, SSM_HEADS), F32,
                                         minval=math.log(1e-3), maxval=math.log(1e-1)))
    dt_bias = dt_init + jnp.log(-jnp.expm1(-dt_init))
    a_log = jnp.log(jax.random.uniform(next(ks), (N_A_LAYERS, SSM_HEADS), F32, minval=1.0, maxval=16.0))
    return {
        'x_prompt': nrm((BATCH, SEQ, D_MODEL), 1.0),
        'x_sample': nrm((DEC_BATCH, DEC_SEQ, D_MODEL), 1.0),
        'state_conv': nrm((N_A_LAYERS, DEC_BATCH, CONV_W - 1, CONV_DIM), 1.0),
        'state_ssm': nrm((N_A_LAYERS, DEC_BATCH, SSM_HEADS, SSM_HEAD_DIM, SSM_D_STATE), 0.5),
        'cache_win_k': nrm((DEC_BATCH, l_buf, DIL_HEADS, HEAD_DIM), 1.0),
        'cache_win_v': nrm((DEC_BATCH, l_buf, DIL_HEADS, HEAD_DIM), 1.0),
        'cache_mem_k': nrm((DEPTH, DEC_BATCH, MEM_LEN, MEM_HEADS, HEAD_DIM), 1.0),
        'cache_mem_v': nrm((DEPTH, DEC_BATCH, MEM_LEN, MEM_HEADS, HEAD_DIM), 1.0),
        'mem_prompt': nrm((BATCH, MEM_LEN, D_MODEL), 1.0),
        'norm_mix_g': gain((DEPTH, D_MODEL)),
        'norm_mlp_g': gain((DEPTH, D_MODEL)),
        'norm_mem_g': gain((DEPTH, D_MODEL)),
        'w_mem_k': nrm((DEPTH, D_MODEL, MEM_WIDTH), D_MODEL ** -0.5),
        'w_mem_v': nrm((DEPTH, D_MODEL, MEM_WIDTH), D_MODEL ** -0.5),
        'mem_q_norm_g': gain((DEPTH, HEAD_DIM)),
        'mem_k_norm_g': gain((DEPTH, HEAD_DIM)),
        'w_up': nrm((DEPTH, D_MODEL, D_FF), D_MODEL ** -0.5),
        'w_down': nrm((DEPTH, D_FF, D_MODEL), D_FF ** -0.5),
        'w_in_a': nrm((N_A_LAYERS, D_MODEL, A_IN_WIDTH), D_MODEL ** -0.5),
        'conv_w': nrm((N_A_LAYERS, CONV_W, CONV_DIM), CONV_W ** -0.5),
        'conv_b': nrm((N_A_LAYERS, CONV_DIM), 0.02),
        'dt_bias': dt_bias,
        'a_log': a_log,
        'd_skip': gain((N_A_LAYERS, SSM_HEADS)),
        'gate_norm_g': gain((N_A_LAYERS, SSM_D_INNER)),
        'w_out_a': nrm((N_A_LAYERS, A_OUT_WIDTH, D_MODEL), A_OUT_WIDTH ** -0.5),
        'w_in_b': nrm((N_B_LAYERS, D_MODEL, B_IN_WIDTH), D_MODEL ** -0.5),
        'q_norm_g': gain((N_B_LAYERS, HEAD_DIM)),
        'w_out_b': nrm((N_B_LAYERS, B_OUT_WIDTH, D_MODEL), B_OUT_WIDTH ** -0.5),
        'kv_norm_g': gain((D_MODEL,)),
        'w_k_shared': nrm((D_MODEL, KV_WIDTH), D_MODEL ** -0.5),
        'w_v_shared': nrm((D_MODEL, KV_WIDTH), D_MODEL ** -0.5),
        'k_norm_g': gain((HEAD_DIM,)),
    }


def reference(x_prompt, x_sample, state_conv, state_ssm, cache_win_k, cache_win_v,
              cache_mem_k, cache_mem_v, mem_prompt,
              norm_mix_g, norm_mlp_g, norm_mem_g, w_mem_k, w_mem_v, mem_q_norm_g, mem_k_norm_g,
              w_up, w_down, w_in_a, conv_w, conv_b, dt_bias, a_log, d_skip, gate_norm_g, w_out_a,
              w_in_b, q_norm_g, w_out_b, kv_norm_g, w_k_shared, w_v_shared, k_norm_g):

    def run(x, pos, conv_prev, ssm_prev, k_past, v_past, mem_k, mem_v, ssd_chunk, q_block):
        conv_new, ssm_new = [], []
        k_new = v_new = k_all = v_all = q_idx = None
        for l in range(DEPTH):
            if l < N_A_LAYERS:
                x, c_l, h_l = a_layer(x, mem_k[l], mem_v[l], conv_prev[l], ssm_prev[l], ssd_chunk,
                                      norm_mix_g[l], mem_q_norm_g[l], norm_mlp_g[l], w_up[l], w_down[l],
                                      w_in_a[l], conv_w[l], conv_b[l], dt_bias[l], a_log[l], d_skip[l],
                                      gate_norm_g[l], w_out_a[l])
                conv_new.append(c_l)
                ssm_new.append(h_l)
            else:
                if l == N_A_LAYERS:
                    k_new, v_new = shared_kv(x, pos, kv_norm_g, w_k_shared, w_v_shared, k_norm_g)
                    k_all = jnp.concatenate([k_past.astype(k_new.dtype), k_new], axis=1)
                    v_all = jnp.concatenate([v_past.astype(v_new.dtype), v_new], axis=1)
                    q_idx = k_past.shape[1] + jnp.arange(x.shape[1], dtype=jnp.int32)
                j = l - N_A_LAYERS
                x = b_layer(x, pos, k_all, v_all, q_idx, q_block, mem_k[l], mem_v[l],
                            norm_mix_g[l], mem_q_norm_g[l], norm_mlp_g[l], w_up[l], w_down[l],
                            w_in_b[j], q_norm_g[j], w_out_b[j])
        return x, jnp.stack(conv_new), jnp.stack(ssm_new), k_new, v_new

    bp, t_p, _ = x_prompt.shape
    mk_list, mv_list = [], []
    for l in range(DEPTH):
        k_l, v_l = mem_kv(mem_prompt, norm_mem_g[l], w_mem_k[l], w_mem_v[l], mem_k_norm_g[l])
        mk_list.append(k_l)
        mv_list.append(v_l)
    mem_k_p = jnp.stack(mk_list)
    mem_v_p = jnp.stack(mv_list)
    pos_p = jnp.arange(t_p, dtype=jnp.int32)
    conv0 = jnp.zeros((N_A_LAYERS, bp, CONV_W - 1, CONV_DIM), x_prompt.dtype)
    ssm0 = jnp.zeros((N_A_LAYERS, bp, SSM_HEADS, SSM_HEAD_DIM, SSM_D_STATE), x_prompt.dtype)
    kv0 = jnp.zeros((bp, 0, DIL_HEADS, HEAD_DIM), x_prompt.dtype)
    y_p, conv_p, ssm_p, k_p, v_p = run(x_prompt, pos_p, conv0, ssm0, kv0, kv0,
                                       mem_k_p, mem_v_p, SSD_CHUNK, Q_BLOCK)
    keep = min(WIN_MAX, t_p)
    win_k_p = k_p[:, t_p - keep:]
    win_v_p = v_p[:, t_p - keep:]

    t_s = x_sample.shape[1]
    pos_s = PAST_LEN + jnp.arange(t_s, dtype=jnp.int32)
    y_s, conv_s, ssm_s, k_s, v_s = run(x_sample, pos_s, state_conv, state_ssm, cache_win_k, cache_win_v,
                                       cache_mem_k, cache_mem_v, t_s, t_s)

    return (y_p, y_s, conv_p, ssm_p, win_k_p, win_v_p, mem_k_p, mem_v_p, conv_s, ssm_s, k_s, v_s)
```

```python
import functools

import jax
import jax.numpy as jnp
from jax import lax
from jax.experimental import pallas as pl
from jax.experimental.pallas import tpu as pltpu

F32 = jnp.float32
BF16 = jnp.bfloat16

HEAD_DIM = 128
SSM_HEAD_DIM = 64
SSM_GROUPS = 4
SSM_D_STATE = 128
CONV_W = 4
SSD_CHUNK = 128
MEM_HEADS = 4
DIL_GROUPS = ((128, 1), (512, 4), (2048, 16))
DIL_HEADS = 4
PAST_LEN = 16384
ROPE_THETA = 10000.0
EPS = 1e-6

LANES = 128
SUBLANES = 8
VMEM_LIMIT = 48 * 1024 * 1024
COL_TILE = 512
ROW_TILE = 512
SAMPLE_ROWS = SUBLANES

NT_DIMS = (((1,), (1,)), ((), ()))


def _cparams(*sem):
    return pltpu.CompilerParams(dimension_semantics=sem, vmem_limit_bytes=VMEM_LIMIT)


def _sigmoid(x):
    return 1.0 / (1.0 + jnp.exp(-x))


def _rope_table_kernel(pos_ref, inv_ref, cos_ref, sin_ref):
    ang = pos_ref[...] * inv_ref[...]
    lane = lax.broadcasted_iota(jnp.int32, ang.shape, 1)
    cos_ref[...] = jnp.cos(ang)
    s = jnp.sin(ang)
    sin_ref[...] = jnp.where(lane < HEAD_DIM // 2, -s, s)


def rope_tables(pos):
    r = pos.shape[0]
    half = HEAD_DIM // 2
    inv = ROPE_THETA ** (-jnp.arange(half, dtype=F32) * (2.0 / HEAD_DIM))
    inv = jnp.concatenate([inv, inv])[None, :]
    posf = jnp.broadcast_to(pos.astype(F32)[:, None], (r, HEAD_DIM))
    tr = min(r, ROW_TILE)
    return pl.pallas_call(
        _rope_table_kernel,
        grid=(r // tr,),
        in_specs=[pl.BlockSpec((tr, HEAD_DIM), lambda i: (i, 0)),
                  pl.BlockSpec((1, HEAD_DIM), lambda i: (0, 0))],
        out_specs=[pl.BlockSpec((tr, HEAD_DIM), lambda i: (i, 0))] * 2,
        out_shape=[jax.ShapeDtypeStruct((r, HEAD_DIM), F32)] * 2,
        compiler_params=_cparams("parallel"),
        name="rope_tables",
    )(posf, inv)


def _proj_kernel(*refs, rope_tiles, norm_tiles, n_tiles):
    if rope_tiles[1] > rope_tiles[0]:
        x_ref, g_ref, w_ref, hg_ref, cos_ref, sin_ref, o_ref, xn_ref = refs
    else:
        x_ref, g_ref, w_ref, hg_ref, o_ref, xn_ref = refs
        cos_ref = sin_ref = None
    n = pl.program_id(1)

    @pl.when(n == 0)
    def _():
        x = x_ref[...]
        ms = jnp.mean(x * x, axis=-1, keepdims=True)
        xn_ref[...] = (x * lax.rsqrt(ms + EPS) * g_ref[...]).astype(BF16)

    def body(kind):
        y = jnp.dot(xn_ref[...], w_ref[...], preferred_element_type=F32)
        if kind == "plain":
            o_ref[...] = y
            return
        for j in range(COL_TILE // HEAD_DIM):
            sl = slice(j * HEAD_DIM, (j + 1) * HEAD_DIM)
            yh = y[:, sl]
            ms = jnp.mean(yh * yh, axis=-1, keepdims=True)
            yh = yh * lax.rsqrt(ms + EPS) * hg_ref[:, sl]
            if kind == "rope":
                yh = yh * cos_ref[...] + pltpu.roll(yh, HEAD_DIM // 2, axis=1) * sin_ref[...]
            o_ref[:, sl] = yh

    ranges = (("rope", rope_tiles), ("norm", norm_tiles))
    covered = 0
    for kind, (lo, hi) in ranges:
        if hi > lo:
            covered += hi - lo
            pl.when((n >= lo) & (n < hi))(functools.partial(body, kind))
    if covered < n_tiles:
        in_special = jnp.bool_(False)
        for _, (lo, hi) in ranges:
            if hi > lo:
                in_special = in_special | ((n >= lo) & (n < hi))
        pl.when(jnp.logical_not(in_special))(functools.partial(body, "plain"))


def norm_proj(x, g, w, head_gain, rope_tiles=(0, 0), norm_tiles=(0, 0), cos=None, sin=None):
    m, d = x.shape
    n_cols = w.shape[1]
    assert n_cols % COL_TILE == 0
    tm = min(m, ROW_TILE)
    assert m % tm == 0
    n_tiles = n_cols // COL_TILE
    has_rope = rope_tiles[1] > rope_tiles[0]
    in_specs = [pl.BlockSpec((tm, d), lambda i, j: (i, 0)),
                pl.BlockSpec((1, d), lambda i, j: (0, 0)),
                pl.BlockSpec((d, COL_TILE), lambda i, j: (0, j)),
                pl.BlockSpec((1, COL_TILE), lambda i, j: (0, j))]
    args = [x, g.reshape(1, d), w, head_gain]
    if has_rope:
        tab_blocks = cos.shape[0] // tm
        assert cos.shape[0] % tm == 0
        in_specs += [pl.BlockSpec((tm, HEAD_DIM), lambda i, j: (i % tab_blocks, 0))] * 2
        args += [cos, sin]
    return pl.pallas_call(
        functools.partial(_proj_kernel, rope_tiles=rope_tiles, norm_tiles=norm_tiles, n_tiles=n_tiles),
        grid=(m // tm, n_tiles),
        in_specs=in_specs,
        out_specs=pl.BlockSpec((tm, COL_TILE), lambda i, j: (i, j)),
        out_shape=jax.ShapeDtypeStruct((m, n_cols), F32),
        scratch_shapes=[pltpu.VMEM((tm, d), BF16)],
        compiler_params=_cparams("parallel", "arbitrary"),
        name="norm_proj",
    )(*args)


def _out_proj_kernel(x_ref, a_ref, b_ref, wa_ref, wb_ref, o_ref):
    acc = jnp.dot(a_ref[...], wa_ref[...], preferred_element_type=F32)
    acc = acc + jnp.dot(b_ref[...], wb_ref[...], preferred_element_type=F32)
    o_ref[...] = x_ref[...] + acc


def out_proj(x, a, b, wa, wb):
    m, d = x.shape
    tm = min(m, ROW_TILE)
    ka, kb = a.shape[1], b.shape[1]
    return pl.pallas_call(
        _out_proj_kernel,
        grid=(m // tm,),
        in_specs=[pl.BlockSpec((tm, d), lambda i: (i, 0)),
                  pl.BlockSpec((tm, ka), lambda i: (i, 0)),
                  pl.BlockSpec((tm, kb), lambda i: (i, 0)),
                  pl.BlockSpec((ka, d), lambda i: (0, 0)),
                  pl.BlockSpec((kb, d), lambda i: (0, 0))],
        out_specs=pl.BlockSpec((tm, d), lambda i: (i, 0)),
        out_shape=jax.ShapeDtypeStruct((m, d), F32),
        compiler_params=_cparams("parallel"),
        name="out_proj",
    )(x, a, b, wa, wb)


def _mlp_kernel(x_ref, g_ref, wu_ref, wd_ref, o_ref, xn_ref):
    f = pl.program_id(1)

    @pl.when(f == 0)
    def _():
        x = x_ref[...]
        ms = jnp.mean(x * x, axis=-1, keepdims=True)
        xn_ref[...] = (x * lax.rsqrt(ms + EPS) * g_ref[...]).astype(BF16)
        o_ref[...] = x

    h = jnp.dot(xn_ref[...], wu_ref[...], preferred_element_type=F32)
    h = jnp.maximum(h, 0.0)
    h = (h * h).astype(BF16)
    o_ref[...] += jnp.dot(h, wd_ref[...], preferred_element_type=F32)


def mlp(x, g, wu, wd, tf=1024):
    m, d = x.shape
    ff = wu.shape[1]
    tm = min(m, ROW_TILE)
    return pl.pallas_call(
        _mlp_kernel,
        grid=(m // tm, ff // tf),
        in_specs=[pl.BlockSpec((tm, d), lambda i, j: (i, 0)),
                  pl.BlockSpec((1, d), lambda i, j: (0, 0)),
                  pl.BlockSpec((d, tf), lambda i, j: (0, j)),
                  pl.BlockSpec((tf, d), lambda i, j: (j, 0))],
        out_specs=pl.BlockSpec((tm, d), lambda i, j: (i, 0)),
        out_shape=jax.ShapeDtypeStruct((m, d), F32),
        scratch_shapes=[pltpu.VMEM((tm, d), BF16)],
        compiler_params=_cparams("parallel", "arbitrary"),
        name="mlp",
    )(x, g.reshape(1, d), wu, wd)


def _ssd_kernel(xbc_ref, dt_ref, z_ref, cw_ref, cb_ref, dtb_ref, alog_ref, dsk_ref, gg_ref,
                conv0_ref, h0_ref, y_ref, convo_ref, ho_ref,
                ext_ref, ht_ref, yd_ref, st_ref, *, t_real, n_heads, d_inner):
    L = SSD_CHUNK
    P = SSM_HEAD_DIM
    N = SSM_D_STATE
    G = SSM_GROUPS
    hpg = n_heads // G
    gw = hpg * P
    hist = CONV_W - 1
    base = SUBLANES - hist
    c = pl.program_id(1)
    nc = pl.num_programs(1)

    @pl.when(c == 0)
    def _():
        ext_ref[base:SUBLANES, :] = conv0_ref[...]
        for g in range(G):
            ht_ref[g] = h0_ref[g].T

    ext_ref[SUBLANES:SUBLANES + L, :] = xbc_ref[...]
    acc = jnp.broadcast_to(cb_ref[...], (L, cb_ref.shape[1]))
    for w in range(CONV_W):
        acc = acc + ext_ref[base + w:base + w + L, :] * cw_ref[w:w + 1, :]
    xc = acc * _sigmoid(acc)

    @pl.when(c == nc - 1)
    def _():
        n_in_last = t_real - ((t_real - 1) // L) * L
        convo_ref[...] = ext_ref[base + n_in_last:base + n_in_last + hist, :]

    ext_ref[base:SUBLANES, :] = ext_ref[base + L:SUBLANES + L, :]

    dt_raw = dt_ref[...] + dtb_ref[...]
    dt = jnp.maximum(dt_raw, 0.0) + jnp.log1p(jnp.exp(-jnp.abs(dt_raw)))
    row = lax.broadcasted_iota(jnp.int32, (L, LANES), 0) + c * L
    dt = jnp.where(row < t_real, dt, 0.0)
    da = dt * (-jnp.exp(alog_ref[...]))
    li = lax.broadcasted_iota(jnp.int32, (L, L), 0)
    si = lax.broadcasted_iota(jnp.int32, (L, L), 1)
    causal = li >= si
    tri = jnp.where(causal, 1.0, 0.0).astype(F32)
    a_cs = jnp.dot(tri, da, precision=lax.Precision.HIGHEST, preferred_element_type=F32)
    a_cs_t = a_cs.T
    dt_t = dt.T
    w_t = dt_t * jnp.exp(a_cs_t[:, L - 1:L] - a_cs_t)
    hi = lax.broadcasted_iota(jnp.int32, (LANES, d_inner), 0)
    ci = lax.broadcasted_iota(jnp.int32, (LANES, d_inner), 1)
    expand = jnp.where(ci // P == hi, 1.0, 0.0).astype(F32)
    e_exp = jnp.dot(jnp.exp(a_cs), expand, precision=lax.Precision.HIGHEST, preferred_element_type=F32)

    for g in range(G):
        b_g = xc[:, d_inner + g * N:d_inner + (g + 1) * N]
        c_g = xc[:, d_inner + G * N + g * N:d_inner + G * N + (g + 1) * N].astype(BF16)
        cbm = lax.dot_general(c_g, b_g.astype(BF16), NT_DIMS, preferred_element_type=F32)
        b_t = b_g.T
        y_off = jnp.dot(c_g, ht_ref[g].astype(BF16), preferred_element_type=F32)
        for k in range(hpg):
            h = g * hpg + k
            xh = xc[:, h * P:(h + 1) * P].astype(BF16)
            seg = a_cs[:, h:h + 1] - a_cs_t[h:h + 1, :]
            dec = jnp.exp(jnp.where(causal, seg, -jnp.inf))
            mh = (cbm * dec * dt_t[h:h + 1, :]).astype(BF16)
            yd_ref[:, k * P:(k + 1) * P] = jnp.dot(mh, xh, preferred_element_type=F32)
            btw = (b_t * w_t[h:h + 1, :]).astype(BF16)
            st_ref[:, k * P:(k + 1) * P] = jnp.dot(btw, xh, preferred_element_type=F32)
        gs = slice(g * gw, (g + 1) * gw)
        e_g = e_exp[:, gs]
        y_g = yd_ref[...] + y_off * e_g + dsk_ref[:, gs] * xc[:, gs]
        ht_ref[g] = ht_ref[g] * e_g[L - 1:L, :] + st_ref[...]
        z_g = z_ref[:, gs]
        gated = y_g * (z_g * _sigmoid(z_g))
        ms = jnp.mean(gated * gated, axis=-1, keepdims=True)
        y_ref[:, gs] = (gated * lax.rsqrt(ms + EPS) * gg_ref[:, gs]).astype(y_ref.dtype)

    @pl.when(c == nc - 1)
    def _():
        for g in range(G):
            ho_ref[g] = ht_ref[g].T


def ssd_mixer(proj, col_xbc, col_dt, col_z, t_real, conv_w, conv_b, dt_bias, a_log, d_skip, gate_g,
              conv_prev, h_prev):
    bt, tp, _ = proj.shape
    conv_dim = conv_w.shape[1]
    n_heads = dt_bias.shape[0]
    d_inner = n_heads * SSM_HEAD_DIM
    gw = d_inner // SSM_GROUPS
    L = SSD_CHUNK
    assert tp % L == 0 and (tp - t_real) < L
    assert col_xbc % conv_dim == 0 and col_dt % LANES == 0 and col_z % d_inner == 0
    pad = LANES - n_heads
    dtb = jnp.pad(dt_bias, (0, pad)).reshape(1, LANES)
    alog = jnp.pad(a_log, (0, pad)).reshape(1, LANES)
    dsk = jnp.repeat(d_skip, SSM_HEAD_DIM).reshape(1, d_inner)
    kern = functools.partial(_ssd_kernel, t_real=t_real, n_heads=n_heads, d_inner=d_inner)
    vec = lambda n: pl.BlockSpec((1, n), lambda b, c: (0, 0))
    return pl.pallas_call(
        kern,
        grid=(bt, tp // L),
        in_specs=[pl.BlockSpec((None, L, conv_dim), lambda b, c: (b, c, col_xbc // conv_dim)),
                  pl.BlockSpec((None, L, LANES), lambda b, c: (b, c, col_dt // LANES)),
                  pl.BlockSpec((None, L, d_inner), lambda b, c: (b, c, col_z // d_inner)),
                  pl.BlockSpec((CONV_W, conv_dim), lambda b, c: (0, 0)),
                  vec(conv_dim), vec(LANES), vec(LANES), vec(d_inner), vec(d_inner),
                  pl.BlockSpec((None, CONV_W - 1, conv_dim), lambda b, c: (b, 0, 0)),
                  pl.BlockSpec((None, SSM_GROUPS, gw, SSM_D_STATE), lambda b, c: (b, 0, 0, 0))],
        out_specs=[pl.BlockSpec((None, L, d_inner), lambda b, c: (b, c, 0)),
                   pl.BlockSpec((None, CONV_W - 1, conv_dim), lambda b, c: (b, 0, 0)),
                   pl.BlockSpec((None, SSM_GROUPS, gw, SSM_D_STATE), lambda b, c: (b, 0, 0, 0))],
        out_shape=[jax.ShapeDtypeStruct((bt, tp, d_inner), BF16),
                   jax.ShapeDtypeStruct((bt, CONV_W - 1, conv_dim), F32),
                   jax.ShapeDtypeStruct((bt, SSM_GROUPS, gw, SSM_D_STATE), F32)],
        scratch_shapes=[pltpu.VMEM((SUBLANES + L, conv_dim), F32),
                        pltpu.VMEM((SSM_GROUPS, SSM_D_STATE, gw), F32),
                        pltpu.VMEM((L, gw), F32),
                        pltpu.VMEM((SSM_D_STATE, gw), F32)],
        compiler_params=_cparams("parallel", "arbitrary"),
        name="ssd_mixer",
    )(proj, proj, proj, conv_w, conv_b.reshape(1, conv_dim), dtb, alog, dsk, gate_g.reshape(1, d_inner),
      conv_prev, h_prev)


def _mem_attn_kernel(q_ref, k_ref, v_ref, o_ref):
    scale = HEAD_DIM ** -0.5
    for h in range(MEM_HEADS):
        sl = slice(h * HEAD_DIM, (h + 1) * HEAD_DIM)
        q = q_ref[:, sl].astype(BF16)
        k = k_ref[:, sl].astype(BF16)
        v = v_ref[:, sl].astype(BF16)
        s = lax.dot_general(q, k, NT_DIMS, preferred_element_type=F32) * scale
        e = jnp.exp(s - jnp.max(s, axis=-1, keepdims=True))
        den = jnp.sum(e, axis=-1, keepdims=True)
        o = jnp.dot(e.astype(BF16), v, preferred_element_type=F32) / den
        o_ref[:, sl] = o.astype(o_ref.dtype)


def mem_attention(proj, col_q, k, col_k, v, col_v):
    bt, t, _ = proj.shape
    width = MEM_HEADS * HEAD_DIM
    mlen = k.shape[1]
    tq = min(t, ROW_TILE)
    return pl.pallas_call(
        _mem_attn_kernel,
        grid=(bt, t // tq),
        in_specs=[pl.BlockSpec((None, tq, width), lambda b, i: (b, i, col_q // width)),
                  pl.BlockSpec((None, mlen, width), lambda b, i: (b, 0, col_k // width)),
                  pl.BlockSpec((None, mlen, width), lambda b, i: (b, 0, col_v // width))],
        out_specs=pl.BlockSpec((None, tq, width), lambda b, i: (b, i, 0)),
        out_shape=jax.ShapeDtypeStruct((bt, t, width), BF16),
        compiler_params=_cparams("parallel", "parallel"),
        name="mem_attention",
    )(proj, k, v)


def _dil_group_kernel(q_ref, kp_ref, kc_ref, vp_ref, vc_ref, o_ref, lse_ref):
    w = q_ref.shape[0]
    i = pl.program_id(2)
    scale = HEAD_DIM ** -0.5
    qi = lax.broadcasted_iota(jnp.int32, (w, w), 0)
    ki = lax.broadcasted_iota(jnp.int32, (w, w), 1)
    ok_prev = (ki >= qi) & (i > 0)
    ok_cur = ki <= qi
    for h in range(DIL_HEADS):
        sl = slice(h * HEAD_DIM, (h + 1) * HEAD_DIM)
        q = q_ref[:, sl].astype(BF16)
        sp = lax.dot_general(q, kp_ref[:, sl].astype(BF16), NT_DIMS, preferred_element_type=F32) * scale
        sc = lax.dot_general(q, kc_ref[:, sl].astype(BF16), NT_DIMS, preferred_element_type=F32) * scale
        sp = jnp.where(ok_prev, sp, -jnp.inf)
        sc = jnp.where(ok_cur, sc, -jnp.inf)
        m = jnp.maximum(jnp.max(sp, axis=-1, keepdims=True), jnp.max(sc, axis=-1, keepdims=True))
        ep = jnp.exp(sp - m)
        ec = jnp.exp(sc - m)
        den = jnp.sum(ep, axis=-1, keepdims=True) + jnp.sum(ec, axis=-1, keepdims=True)
        o = jnp.dot(ep.astype(BF16), vp_ref[:, sl].astype(BF16), preferred_element_type=F32)
        o = o + jnp.dot(ec.astype(BF16), vc_ref[:, sl].astype(BF16), preferred_element_type=F32)
        o_ref[:, sl] = o / den
        lse_ref[:, sl] = jnp.broadcast_to(m + jnp.log(den), (w, HEAD_DIM))


def dilated_group(proj, gi, kv, win, dil):
    bt, t, nq = proj.shape
    nkv = kv.shape[2]
    width = DIL_HEADS * HEAD_DIM
    w = win // dil
    ts = t // dil
    assert t % (dil * w) == 0 and nq % width == 0 and nkv == 2 * width
    qv = proj.reshape(bt, ts, dil * nq)
    kvv = kv.reshape(bt, ts, dil * nkv)
    qpb, kpb = nq // width, nkv // width
    blk = lambda f: pl.BlockSpec((None, w, width), f)
    o, lse = pl.pallas_call(
        _dil_group_kernel,
        grid=(bt, dil, ts // w),
        in_specs=[blk(lambda b, r, i: (b, i, r * qpb + gi)),
                  blk(lambda b, r, i: (b, jnp.maximum(i - 1, 0), r * kpb)),
                  blk(lambda b, r, i: (b, i, r * kpb)),
                  blk(lambda b, r, i: (b, jnp.maximum(i - 1, 0), r * kpb + 1)),
                  blk(lambda b, r, i: (b, i, r * kpb + 1))],
        out_specs=[blk(lambda b, r, i: (b, i, r))] * 2,
        out_shape=[jax.ShapeDtypeStruct((bt, ts, dil * width), F32)] * 2,
        compiler_params=_cparams("parallel", "parallel", "parallel"),
        name=f"dilated_group_{dil}",
    )(qv, kvv, kvv, kvv, kvv)
    return o.reshape(bt * t, width), lse.reshape(bt * t, width)


def _merge_kernel(*refs):
    n = (len(refs) - 1) // 2
    o_refs, l_refs, out_ref = refs[:n], refs[n:2 * n], refs[-1]
    ls = [r[...] for r in l_refs]
    m = functools.reduce(jnp.maximum, ls)
    ws = [jnp.exp(l - m) for l in ls]
    num = sum(wg * r[...] for wg, r in zip(ws, o_refs))
    out_ref[...] = (num / sum(ws)).astype(out_ref.dtype)


def merge_groups(outs, lses):
    m, width = outs[0].shape
    tm = min(m, 2 * ROW_TILE)
    spec = pl.BlockSpec((tm, width), lambda i: (i, 0))
    return pl.pallas_call(
        _merge_kernel,
        grid=(m // tm,),
        in_specs=[spec] * (2 * len(outs)),
        out_specs=spec,
        out_shape=jax.ShapeDtypeStruct((m, width), BF16),
        compiler_params=_cparams("parallel"),
        name="merge_groups",
    )(*outs, *lses)


def _dil_decode_kernel(q_ref, kv_ref, kc_ref, vc_ref, o_ref):
    rows = q_ref.shape[0]
    lc = kc_ref.shape[0]
    width = DIL_HEADS * HEAD_DIM
    scale = HEAD_DIM ** -0.5
    dist = lc - lax.broadcasted_iota(jnp.int32, (rows, lc), 1)
    for h in range(DIL_HEADS):
        sl = slice(h * HEAD_DIM, (h + 1) * HEAD_DIM)
        kc = kc_ref[:, sl].astype(BF16)
        vc = vc_ref[:, sl].astype(BF16)
        k_new = kv_ref[:, h * HEAD_DIM:(h + 1) * HEAD_DIM]
        v_new = kv_ref[:, width + h * HEAD_DIM:width + (h + 1) * HEAD_DIM]
        outs, lses = [], []
        for gi, (win, dil) in enumerate(DIL_GROUPS):
            q = q_ref[:, gi * width + h * HEAD_DIM:gi * width + (h + 1) * HEAD_DIM]
            s = lax.dot_general(q.astype(BF16), kc, NT_DIMS, preferred_element_type=F32) * scale
            ok = ((dist & (dil - 1)) == 0) & (dist <= win)
            s = jnp.where(ok, s, -jnp.inf)
            s_self = jnp.sum(q * k_new, axis=-1, keepdims=True) * scale
            m = jnp.maximum(jnp.max(s, axis=-1, keepdims=True), s_self)
            e = jnp.exp(s - m)
            e_self = jnp.exp(s_self - m)
            den = jnp.sum(e, axis=-1, keepdims=True) + e_self
            o = jnp.dot(e.astype(BF16), vc, preferred_element_type=F32) + e_self * v_new
            outs.append(o / den)
            lses.append(m + jnp.log(den))
        mm = functools.reduce(jnp.maximum, lses)
        ws = [jnp.exp(l - mm) for l in lses]
        num = sum(wg * o for wg, o in zip(ws, outs))
        o_ref[:, sl] = (num / sum(ws)).astype(o_ref.dtype)


def dilated_decode(proj, kv_new, k_cache, v_cache):
    bt, rows, nq = proj.shape
    lc = k_cache.shape[1]
    width = DIL_HEADS * HEAD_DIM
    for _, dil in DIL_GROUPS:
        assert dil & (dil - 1) == 0
    return pl.pallas_call(
        _dil_decode_kernel,
        grid=(bt,),
        in_specs=[pl.BlockSpec((None, rows, nq), lambda b: (b, 0, 0)),
                  pl.BlockSpec((None, rows, 2 * width), lambda b: (b, 0, 0)),
                  pl.BlockSpec((None, lc, width), lambda b: (b, 0, 0)),
                  pl.BlockSpec((None, lc, width), lambda b: (b, 0, 0))],
        out_specs=pl.BlockSpec((None, rows, width), lambda b: (b, 0, 0)),
        out_shape=jax.ShapeDtypeStruct((bt, rows, width), BF16),
        compiler_params=_cparams("parallel"),
        name="dilated_decode",
    )(proj, kv_new, k_cache, v_cache)


def _tile_gain(g, reps):
    return jnp.tile(g, reps)


def kernel(x_prompt, x_sample, state_conv, state_ssm, cache_win_k, cache_win_v, cache_mem_k, cache_mem_v,
           mem_prompt, norm_mix_g, norm_mlp_g, norm_mem_g, w_mem_k, w_mem_v, mem_q_norm_g, mem_k_norm_g,
           w_up, w_down, w_in_a, conv_w, conv_b, dt_bias, a_log, d_skip, gate_norm_g, w_out_a,
           w_in_b, q_norm_g, w_out_b, kv_norm_g, w_k_shared, w_v_shared, k_norm_g):
    depth = w_up.shape[0]
    n_a = w_in_a.shape[0]
    d_model = x_prompt.shape[-1]
    n_heads = dt_bias.shape[1]
    d_inner = n_heads * SSM_HEAD_DIM
    conv_dim = conv_w.shape[2]
    mem_w = MEM_HEADS * HEAD_DIM
    kv_w = DIL_HEADS * HEAD_DIM
    dil_q_w = len(DIL_GROUPS) * kv_w
    heads_per_tile = COL_TILE // HEAD_DIM

    s1, s2, s3 = d_inner, d_inner + conv_dim, d_inner + conv_dim + n_heads
    dt_pad = jnp.zeros((n_a, d_model, COL_TILE - n_heads), F32)
    w_a = jnp.concatenate([w_in_a[..., s1:s2], w_in_a[..., s2:s3], dt_pad, w_in_a[..., :s1], w_in_a[..., s3:]],
                          axis=-1).astype(BF16)
    col_xbc, col_dt, col_z = 0, conv_dim, conv_dim + COL_TILE
    col_qm_a = col_z + d_inner
    n_a_tiles = w_a.shape[-1] // COL_TILE
    w_b = w_in_b.astype(BF16)
    w_kv = jnp.concatenate([w_k_shared, w_v_shared], axis=-1).astype(BF16)
    w_mkv = jnp.concatenate([w_mem_k, w_mem_v], axis=-1).astype(BF16)
    w_up_b, w_down_b = w_up.astype(BF16), w_down.astype(BF16)
    w_out_a_b, w_out_b_b = w_out_a.astype(BF16), w_out_b.astype(BF16)

    def gain_row(width, pieces):
        row = jnp.ones((width,), F32)
        for start, g, reps in pieces:
            row = lax.dynamic_update_slice(row, _tile_gain(g, reps), (start,))
        return row.reshape(1, width)

    def run(x, pos_rows, conv_prev, ssm_prev, mem_k, col_mk, mem_v, col_mv, k_past, v_past, t_real):
        bt, t, _ = x.shape
        m = bt * t
        x2 = x.reshape(m, d_model)
        cos, sin = rope_tables(pos_rows)
        conv_new, ssm_new = [], []
        for l in range(n_a):
            hg = gain_row(w_a.shape[-1], [(col_qm_a, mem_q_norm_g[l], MEM_HEADS)])
            proj = norm_proj(x2, norm_mix_g[l], w_a[l], hg, norm_tiles=(n_a_tiles - 1, n_a_tiles))
            proj3 = proj.reshape(bt, t, -1)
            tp = -(-t // SSD_CHUNK) * SSD_CHUNK
            proj_ssd = proj3 if tp == t else jnp.pad(proj3, ((0, 0), (0, tp - t), (0, 0)))
            y, c_new, h_new = ssd_mixer(proj_ssd, col_xbc, col_dt, col_z, t_real, conv_w[l], conv_b[l],
                                        dt_bias[l], a_log[l], d_skip[l], gate_norm_g[l],
                                        conv_prev[l], ssm_prev[l].reshape(bt, SSM_GROUPS, -1, SSM_D_STATE))
            y = y[:, :t].reshape(m, d_inner)
            mo = mem_attention(proj3, col_qm_a, mem_k[l], col_mk, mem_v[l], col_mv).reshape(m, mem_w)
            x2 = out_proj(x2, y, mo, w_out_a_b[l, :d_inner], w_out_a_b[l, d_inner:])
            x2 = mlp(x2, norm_mlp_g[l], w_up_b[l], w_down_b[l])
            conv_new.append(c_new)
            ssm_new.append(h_new.reshape(bt, n_heads, SSM_HEAD_DIM, SSM_D_STATE))
        hg = gain_row(2 * kv_w, [(0, k_norm_g, DIL_HEADS)])
        kv = norm_proj(x2, kv_norm_g, w_kv, hg, rope_tiles=(0, kv_w // COL_TILE), cos=cos, sin=sin)
        kv3 = kv.reshape(bt, t, 2 * kv_w)
        for l in range(n_a, depth):
            j = l - n_a
            hg = gain_row(dil_q_w + mem_w, [(0, q_norm_g[j], dil_q_w // HEAD_DIM),
                                            (dil_q_w, mem_q_norm_g[l], MEM_HEADS)])
            n_rope = dil_q_w // COL_TILE
            proj = norm_proj(x2, norm_mix_g[l], w_b[j], hg, rope_tiles=(0, n_rope),
                             norm_tiles=(n_rope, n_rope + mem_w // COL_TILE), cos=cos, sin=sin)
            proj3 = proj.reshape(bt, t, -1)
            if k_past is None:
                outs, lses = zip(*[dilated_group(proj3, gi, kv3, win, dil)
                                   for gi, (win, dil) in enumerate(DIL_GROUPS)])
                att = merge_groups(outs, lses)
            else:
                att = dilated_decode(proj3, kv3, k_past, v_past).reshape(m, kv_w)
            mo = mem_attention(proj3, dil_q_w, mem_k[l], col_mk, mem_v[l], col_mv).reshape(m, mem_w)
            x2 = out_proj(x2, att, mo, w_out_b_b[j, :kv_w], w_out_b_b[j, kv_w:])
            x2 = mlp(x2, norm_mlp_g[l], w_up_b[l], w_down_b[l])
        return x2.reshape(bt, t, d_model), jnp.stack(conv_new), jnp.stack(ssm_new), kv3

    bp, t_p, _ = x_prompt.shape
    mlen = mem_prompt.shape[1]
    mem2 = mem_prompt.reshape(bp * mlen, d_model)
    mkv_p = []
    for l in range(depth):
        hg = gain_row(2 * mem_w, [(0, mem_k_norm_g[l], MEM_HEADS)])
        mkv = norm_proj(mem2, norm_mem_g[l], w_mkv[l], hg, norm_tiles=(0, mem_w // COL_TILE))
        mkv_p.append(mkv.reshape(bp, mlen, 2 * mem_w))
    conv0 = jnp.zeros((n_a, bp, CONV_W - 1, conv_dim), F32)
    ssm0 = jnp.zeros((n_a, bp, n_heads, SSM_HEAD_DIM, SSM_D_STATE), F32)
    y_p, conv_p, ssm_p, kv_p = run(x_prompt, jnp.arange(t_p, dtype=jnp.int32), conv0, ssm0,
                                   mkv_p, 0, mkv_p, mem_w, None, None, t_p)
    mkv_all = jnp.stack(mkv_p)
    mem_k_p = mkv_all[..., :mem_w].reshape(depth, bp, mlen, MEM_HEADS, HEAD_DIM)
    mem_v_p = mkv_all[..., mem_w:].reshape(depth, bp, mlen, MEM_HEADS, HEAD_DIM)
    keep = min(max(w for w, _ in DIL_GROUPS), t_p)
    win_k_p = kv_p[:, t_p - keep:, :kv_w].reshape(bp, keep, DIL_HEADS, HEAD_DIM)
    win_v_p = kv_p[:, t_p - keep:, kv_w:].reshape(bp, keep, DIL_HEADS, HEAD_DIM)

    bs, t_s, _ = x_sample.shape
    assert t_s == 1
    xs = jnp.pad(x_sample, ((0, 0), (0, SAMPLE_ROWS - t_s), (0, 0)))
    pos_s = jnp.full((bs * SAMPLE_ROWS,), PAST_LEN, jnp.int32)
    lc = cache_win_k.shape[1]
    cmk = cache_mem_k.reshape(depth, bs, mlen, mem_w)
    cmv = cache_mem_v.reshape(depth, bs, mlen, mem_w)
    y_s, conv_s, ssm_s, kv_s = run(xs, pos_s, state_conv, state_ssm, cmk, 0, cmv, 0,
                                   cache_win_k.reshape(bs, lc, kv_w), cache_win_v.reshape(bs, lc, kv_w), t_s)
    y_s = y_s[:, :t_s]
    k_s = kv_s[:, :t_s, :kv_w].reshape(bs, t_s, DIL_HEADS, HEAD_DIM)
    v_s = kv_s[:, :t_s, kv_w:].reshape(bs, t_s, DIL_HEADS, HEAD_DIM)

    return (y_p, y_s, conv_p, ssm_p, win_k_p, win_v_p, mem_k_p, mem_v_p, conv_s, ssm_s, k_s, v_s)
```

```python
import functools

import jax
import jax.numpy as jnp
from jax import lax
from jax.experimental import pallas as pl
from jax.experimental.pallas import tpu as pltpu

F32 = jnp.float32
BF16 = jnp.bfloat16

HEAD_DIM = 128
SSM_HEAD_DIM = 64
SSM_GROUPS = 4
SSM_D_STATE = 128
CONV_W = 4
SSD_CHUNK = 128
MEM_HEADS = 4
DIL_GROUPS = ((128, 1), (512, 4), (2048, 16))
DIL_HEADS = 4
PAST_LEN = 16384
ROPE_THETA = 10000.0
EPS = 1e-6

LANES = 128
SUBLANES = 8
VMEM_LIMIT = 48 * 1024 * 1024
COL_TILE = 512
ROW_TILE = 512
PROJ_OUT_BLOCK_BYTES = 5 * 1024 * 1024
SAMPLE_ROWS = SUBLANES

NT_DIMS = (((1,), (1,)), ((), ()))
RESIDENT = pl.Buffered(1)


def _cparams(*sem):
    return pltpu.CompilerParams(dimension_semantics=sem, vmem_limit_bytes=VMEM_LIMIT)


def _sigmoid(x):
    return 1.0 / (1.0 + jnp.exp(-x))


def _rms_scale(x):
    return lax.rsqrt(jnp.mean(x * x, axis=-1, keepdims=True) + EPS)


def _rope_table_kernel(pos_ref, inv_ref, cos_ref, sin_ref):
    ang = pos_ref[...] * inv_ref[...]
    lane = lax.broadcasted_iota(jnp.int32, ang.shape, 1)
    cos_ref[...] = jnp.cos(ang)
    s = jnp.sin(ang)
    sin_ref[...] = jnp.where(lane < HEAD_DIM // 2, -s, s)


def rope_tables(pos):
    r = pos.shape[0]
    half = HEAD_DIM // 2
    inv = ROPE_THETA ** (-jnp.arange(half, dtype=F32) * (2.0 / HEAD_DIM))
    inv = jnp.concatenate([inv, inv])[None, :]
    posf = jnp.broadcast_to(pos.astype(F32)[:, None], (r, HEAD_DIM))
    tr = min(r, ROW_TILE)
    return pl.pallas_call(
        _rope_table_kernel,
        grid=(r // tr,),
        in_specs=[pl.BlockSpec((tr, HEAD_DIM), lambda i: (i, 0)),
                  pl.BlockSpec((1, HEAD_DIM), lambda i: (0, 0))],
        out_specs=[pl.BlockSpec((tr, HEAD_DIM), lambda i: (i, 0))] * 2,
        out_shape=[jax.ShapeDtypeStruct((r, HEAD_DIM), F32)] * 2,
        compiler_params=_cparams("parallel"),
        name="rope_tables",
    )(posf, inv)


def _proj_kernel(*refs, kinds):
    if "rope" in kinds:
        x_ref, g_ref, w_ref, hg_ref, cos_ref, sin_ref, o_ref, xn_ref = refs
    else:
        x_ref, g_ref, w_ref, hg_ref, o_ref, xn_ref = refs
    x = x_ref[...]
    xn_ref[...] = (x * _rms_scale(x) * g_ref[...]).astype(BF16)
    for t, kind in enumerate(kinds):
        cols = slice(t * COL_TILE, (t + 1) * COL_TILE)
        y = jnp.dot(xn_ref[...], w_ref[:, cols], preferred_element_type=F32)
        if kind == "plain":
            o_ref[:, cols] = y
            continue
        for j in range(COL_TILE // HEAD_DIM):
            sl = slice(j * HEAD_DIM, (j + 1) * HEAD_DIM)
            osl = slice(t * COL_TILE + j * HEAD_DIM, t * COL_TILE + (j + 1) * HEAD_DIM)
            yh = y[:, sl]
            yh = yh * _rms_scale(yh) * hg_ref[:, osl]
            if kind == "rope":
                yh = yh * cos_ref[...] + pltpu.roll(yh, HEAD_DIM // 2, axis=1) * sin_ref[...]
            o_ref[:, osl] = yh


def norm_proj(x, g, w, layer, head_gain, kinds, cos=None, sin=None):
    m, d = x.shape
    n_cols = w.shape[2]
    assert n_cols == COL_TILE * len(kinds)
    tm = min(m, ROW_TILE)
    while tm * n_cols * 4 > PROJ_OUT_BLOCK_BYTES and tm % (2 * SUBLANES) == 0:
        tm //= 2
    assert m % tm == 0
    in_specs = [pl.BlockSpec((tm, d), lambda i: (i, 0)),
                pl.BlockSpec((1, d), lambda i: (0, 0)),
                pl.BlockSpec((None, d, n_cols), lambda i: (layer, 0, 0), pipeline_mode=RESIDENT),
                pl.BlockSpec((1, n_cols), lambda i: (0, 0))]
    args = [x, g.reshape(1, d), w, head_gain]
    if "rope" in kinds:
        assert cos.shape[0] % tm == 0
        tab_blocks = cos.shape[0] // tm
        in_specs += [pl.BlockSpec((tm, HEAD_DIM), lambda i: (i % tab_blocks, 0))] * 2
        args += [cos, sin]
    return pl.pallas_call(
        functools.partial(_proj_kernel, kinds=tuple(kinds)),
        grid=(m // tm,),
        in_specs=in_specs,
        out_specs=pl.BlockSpec((tm, n_cols), lambda i: (i, 0)),
        out_shape=jax.ShapeDtypeStruct((m, n_cols), F32),
        scratch_shapes=[pltpu.VMEM((tm, d), BF16)],
        compiler_params=_cparams("parallel"),
        name="norm_proj",
    )(*args)


def _out_proj_kernel(x_ref, a_ref, b_ref, wa_ref, wb_ref, o_ref):
    acc = jnp.dot(a_ref[...], wa_ref[...], preferred_element_type=F32)
    acc = acc + jnp.dot(b_ref[...], wb_ref[...], preferred_element_type=F32)
    o_ref[...] = x_ref[...] + acc


def out_proj(x, a, b, w, layer):
    m, d = x.shape
    tm = min(m, ROW_TILE)
    ka, kb = a.shape[1], b.shape[1]
    assert w.shape[1] == ka + kb and ka % kb == 0
    return pl.pallas_call(
        _out_proj_kernel,
        grid=(m // tm,),
        in_specs=[pl.BlockSpec((tm, d), lambda i: (i, 0)),
                  pl.BlockSpec((tm, ka), lambda i: (i, 0)),
                  pl.BlockSpec((tm, kb), lambda i: (i, 0)),
                  pl.BlockSpec((None, ka, d), lambda i: (layer, 0, 0), pipeline_mode=RESIDENT),
                  pl.BlockSpec((None, kb, d), lambda i: (layer, ka // kb, 0), pipeline_mode=RESIDENT)],
        out_specs=pl.BlockSpec((tm, d), lambda i: (i, 0)),
        out_shape=jax.ShapeDtypeStruct((m, d), F32),
        compiler_params=_cparams("parallel"),
        name="out_proj",
    )(x, a, b, w, w)


def _mlp_kernel(x_ref, g_ref, wu_ref, wd_ref, o_ref, xn_ref):
    f = pl.program_id(1)

    @pl.when(f == 0)
    def _():
        x = x_ref[...]
        xn_ref[...] = (x * _rms_scale(x) * g_ref[...]).astype(BF16)
        o_ref[...] = x

    h = jnp.dot(xn_ref[...], wu_ref[...], preferred_element_type=F32)
    h = jnp.maximum(h, 0.0)
    h = (h * h).astype(BF16)
    o_ref[...] += jnp.dot(h, wd_ref[...], preferred_element_type=F32)


def mlp(x, g, wu, wd, layer, tf=1024):
    m, d = x.shape
    ff = wu.shape[2]
    tm = min(m, ROW_TILE)
    return pl.pallas_call(
        _mlp_kernel,
        grid=(m // tm, ff // tf),
        in_specs=[pl.BlockSpec((tm, d), lambda i, j: (i, 0)),
                  pl.BlockSpec((1, d), lambda i, j: (0, 0)),
                  pl.BlockSpec((None, d, tf), lambda i, j: (layer, 0, j)),
                  pl.BlockSpec((None, tf, d), lambda i, j: (layer, j, 0))],
        out_specs=pl.BlockSpec((tm, d), lambda i, j: (i, 0)),
        out_shape=jax.ShapeDtypeStruct((m, d), F32),
        scratch_shapes=[pltpu.VMEM((tm, d), BF16)],
        compiler_params=_cparams("parallel", "arbitrary"),
        name="mlp",
    )(x, g.reshape(1, d), wu, wd)


def _ssd_kernel(xbc_ref, dt_ref, z_ref, cw_ref, cb_ref, dtb_ref, alog_ref, dsk_ref, gg_ref,
                conv0_ref, h0_ref, y_ref, convo_ref, ho_ref,
                ext_ref, ht_ref, yd_ref, st_ref, *, t_real, n_heads, d_inner):
    L = SSD_CHUNK
    P = SSM_HEAD_DIM
    N = SSM_D_STATE
    G = SSM_GROUPS
    hpg = n_heads // G
    gw = hpg * P
    hist = CONV_W - 1
    base = SUBLANES - hist
    c = pl.program_id(1)
    nc = pl.num_programs(1)

    @pl.when(c == 0)
    def _():
        ext_ref[base:SUBLANES, :] = conv0_ref[...]
        for g in range(G):
            ht_ref[g] = h0_ref[g].T

    ext_ref[SUBLANES:SUBLANES + L, :] = xbc_ref[...]
    acc = jnp.broadcast_to(cb_ref[...], (L, cb_ref.shape[1]))
    for w in range(CONV_W):
        acc = acc + ext_ref[base + w:base + w + L, :] * cw_ref[w:w + 1, :]
    xc = acc * _sigmoid(acc)

    @pl.when(c == nc - 1)
    def _():
        n_in_last = t_real - ((t_real - 1) // L) * L
        convo_ref[...] = ext_ref[base + n_in_last:base + n_in_last + hist, :]

    ext_ref[base:SUBLANES, :] = ext_ref[base + L:SUBLANES + L, :]

    dt_raw = dt_ref[...] + dtb_ref[...]
    dt = jnp.maximum(dt_raw, 0.0) + jnp.log1p(jnp.exp(-jnp.abs(dt_raw)))
    row = lax.broadcasted_iota(jnp.int32, (L, LANES), 0) + c * L
    dt = jnp.where(row < t_real, dt, 0.0)
    da = dt * (-jnp.exp(alog_ref[...]))
    li = lax.broadcasted_iota(jnp.int32, (L, L), 0)
    si = lax.broadcasted_iota(jnp.int32, (L, L), 1)
    causal = li >= si
    tri = jnp.where(causal, 1.0, 0.0).astype(F32)
    a_cs = jnp.dot(tri, da, precision=lax.Precision.HIGHEST, preferred_element_type=F32)
    a_cs_t = a_cs.T
    dt_t = dt.T
    w_t = dt_t * jnp.exp(a_cs_t[:, L - 1:L] - a_cs_t)
    hi = lax.broadcasted_iota(jnp.int32, (LANES, d_inner), 0)
    ci = lax.broadcasted_iota(jnp.int32, (LANES, d_inner), 1)
    expand = jnp.where(ci // P == hi, 1.0, 0.0).astype(F32)
    e_exp = jnp.dot(jnp.exp(a_cs), expand, precision=lax.Precision.HIGHEST, preferred_element_type=F32)

    for g in range(G):
        b_g = xc[:, d_inner + g * N:d_inner + (g + 1) * N]
        c_g = xc[:, d_inner + G * N + g * N:d_inner + G * N + (g + 1) * N].astype(BF16)
        cbm = lax.dot_general(c_g, b_g.astype(BF16), NT_DIMS, preferred_element_type=F32)
        b_t = b_g.T
        y_off = jnp.dot(c_g, ht_ref[g].astype(BF16), preferred_element_type=F32)
        for k in range(hpg):
            h = g * hpg + k
            xh = xc[:, h * P:(h + 1) * P].astype(BF16)
            seg = a_cs[:, h:h + 1] - a_cs_t[h:h + 1, :]
            dec = jnp.exp(jnp.where(causal, seg, -jnp.inf))
            mh = (cbm * dec * dt_t[h:h + 1, :]).astype(BF16)
            yd_ref[:, k * P:(k + 1) * P] = jnp.dot(mh, xh, preferred_element_type=F32)
            btw = (b_t * w_t[h:h + 1, :]).astype(BF16)
            st_ref[:, k * P:(k + 1) * P] = jnp.dot(btw, xh, preferred_element_type=F32)
        gs = slice(g * gw, (g + 1) * gw)
        e_g = e_exp[:, gs]
        y_g = yd_ref[...] + y_off * e_g + dsk_ref[:, gs] * xc[:, gs]
        ht_ref[g] = ht_ref[g] * e_g[L - 1:L, :] + st_ref[...]
        z_g = z_ref[:, gs]
        gated = y_g * (z_g * _sigmoid(z_g))
        y_ref[:, gs] = (gated * _rms_scale(gated) * gg_ref[:, gs]).astype(y_ref.dtype)

    @pl.when(c == nc - 1)
    def _():
        for g in range(G):
            ho_ref[g] = ht_ref[g].T


def ssd_mixer(proj, col_xbc, col_dt, col_z, t_real, conv_w, conv_b, dt_bias, a_log, d_skip, gate_g,
              conv_prev, h_prev):
    bt, tp, _ = proj.shape
    conv_dim = conv_w.shape[1]
    n_heads = dt_bias.shape[0]
    d_inner = n_heads * SSM_HEAD_DIM
    gw = d_inner // SSM_GROUPS
    L = SSD_CHUNK
    assert tp % L == 0 and (tp - t_real) < L
    assert col_xbc % conv_dim == 0 and col_dt % LANES == 0 and col_z % d_inner == 0
    pad = LANES - n_heads
    dtb = jnp.pad(dt_bias, (0, pad)).reshape(1, LANES)
    alog = jnp.pad(a_log, (0, pad)).reshape(1, LANES)
    dsk = jnp.repeat(d_skip, SSM_HEAD_DIM).reshape(1, d_inner)
    kern = functools.partial(_ssd_kernel, t_real=t_real, n_heads=n_heads, d_inner=d_inner)
    vec = lambda n: pl.BlockSpec((1, n), lambda b, c: (0, 0))
    return pl.pallas_call(
        kern,
        grid=(bt, tp // L),
        in_specs=[pl.BlockSpec((None, L, conv_dim), lambda b, c: (b, c, col_xbc // conv_dim)),
                  pl.BlockSpec((None, L, LANES), lambda b, c: (b, c, col_dt // LANES)),
                  pl.BlockSpec((None, L, d_inner), lambda b, c: (b, c, col_z // d_inner)),
                  pl.BlockSpec((CONV_W, conv_dim), lambda b, c: (0, 0)),
                  vec(conv_dim), vec(LANES), vec(LANES), vec(d_inner), vec(d_inner),
                  pl.BlockSpec((None, CONV_W - 1, conv_dim), lambda b, c: (b, 0, 0)),
                  pl.BlockSpec((None, SSM_GROUPS, gw, SSM_D_STATE), lambda b, c: (b, 0, 0, 0))],
        out_specs=[pl.BlockSpec((None, L, d_inner), lambda b, c: (b, c, 0)),
                   pl.BlockSpec((None, CONV_W - 1, conv_dim), lambda b, c: (b, 0, 0)),
                   pl.BlockSpec((None, SSM_GROUPS, gw, SSM_D_STATE), lambda b, c: (b, 0, 0, 0))],
        out_shape=[jax.ShapeDtypeStruct((bt, tp, d_inner), BF16),
                   jax.ShapeDtypeStruct((bt, CONV_W - 1, conv_dim), F32),
                   jax.ShapeDtypeStruct((bt, SSM_GROUPS, gw, SSM_D_STATE), F32)],
        scratch_shapes=[pltpu.VMEM((SUBLANES + L, conv_dim), F32),
                        pltpu.VMEM((SSM_GROUPS, SSM_D_STATE, gw), F32),
                        pltpu.VMEM((L, gw), F32),
                        pltpu.VMEM((SSM_D_STATE, gw), F32)],
        compiler_params=_cparams("parallel", "arbitrary"),
        name="ssd_mixer",
    )(proj, proj, proj, conv_w, conv_b.reshape(1, conv_dim), dtb, alog, dsk, gate_g.reshape(1, d_inner),
      conv_prev, h_prev)


def _mem_attn_kernel(q_ref, k_ref, v_ref, o_ref):
    scale = HEAD_DIM ** -0.5
    for h in range(MEM_HEADS):
        sl = slice(h * HEAD_DIM, (h + 1) * HEAD_DIM)
        q = q_ref[:, sl].astype(BF16)
        k = k_ref[:, sl].astype(BF16)
        v = v_ref[:, sl].astype(BF16)
        s = lax.dot_general(q, k, NT_DIMS, preferred_element_type=F32) * scale
        e = jnp.exp(s - jnp.max(s, axis=-1, keepdims=True))
        den = jnp.sum(e, axis=-1, keepdims=True)
        o = jnp.dot(e.astype(BF16), v, preferred_element_type=F32) / den
        o_ref[:, sl] = o.astype(o_ref.dtype)


def mem_attention(proj, col_q, k, col_k, v, col_v):
    bt, t, _ = proj.shape
    width = MEM_HEADS * HEAD_DIM
    mlen = k.shape[1]
    tq = min(t, ROW_TILE)
    return pl.pallas_call(
        _mem_attn_kernel,
        grid=(bt, t // tq),
        in_specs=[pl.BlockSpec((None, tq, width), lambda b, i: (b, i, col_q // width)),
                  pl.BlockSpec((None, mlen, width), lambda b, i: (b, 0, col_k // width)),
                  pl.BlockSpec((None, mlen, width), lambda b, i: (b, 0, col_v // width))],
        out_specs=pl.BlockSpec((None, tq, width), lambda b, i: (b, i, 0)),
        out_shape=jax.ShapeDtypeStruct((bt, t, width), BF16),
        compiler_params=_cparams("parallel", "parallel"),
        name="mem_attention",
    )(proj, k, v)


def _dil_attn_kernel(*refs, groups, rb):
    ng = len(groups)
    q_refs = refs[:ng]
    kp_ref, kc_ref, vp_ref, vc_ref, o_ref, kbuf, vbuf = refs[ng:ng + 7]
    og = refs[ng + 7:2 * ng + 7]
    lg = refs[2 * ng + 7:3 * ng + 7]
    i = pl.program_id(1)
    scale = HEAD_DIM ** -0.5
    kbuf[0:rb, :] = kp_ref[...]
    kbuf[rb:2 * rb, :] = kc_ref[...]
    vbuf[0:rb, :] = vp_ref[...]
    vbuf[rb:2 * rb, :] = vc_ref[...]

    for gi, (win, d) in enumerate(groups):
        w = win // d
        dw = d * w
        qi = lax.broadcasted_iota(jnp.int32, (w, 2 * w), 0)
        kj = lax.broadcasted_iota(jnp.int32, (w, 2 * w), 1)
        band = (kj >= qi) & (kj <= qi + w)

        def unit(u, carry, gi=gi, d=d, w=w, dw=dw, band=band, kj=kj):
            sb = u // d
            start_q = sb * dw + (u - sb * d)
            start_k = rb + start_q - dw
            if d == 1:
                rows_q = pl.ds(pl.multiple_of(start_q, w), w)
                rows_k = pl.ds(pl.multiple_of(start_k, w), 2 * w)
            else:
                rows_q = pl.ds(start_q, w, stride=d)
                rows_k = pl.ds(start_k, 2 * w, stride=d)
            q = q_refs[gi][rows_q, :].astype(BF16)
            kk = kbuf[rows_k, :].astype(BF16)
            vv = vbuf[rows_k, :].astype(BF16)
            s = lax.dot_general(q, kk, NT_DIMS, preferred_element_type=F32) * scale
            first_valid = jnp.where((i > 0) | (sb > 0), 0, w)
            s = jnp.where(band & (kj >= first_valid), s, -jnp.inf)
            m = jnp.max(s, axis=-1, keepdims=True)
            e = jnp.exp(s - m)
            den = jnp.sum(e, axis=-1, keepdims=True)
            o = jnp.dot(e.astype(BF16), vv, preferred_element_type=F32) / den
            og[gi][rows_q, :] = o
            lg[gi][rows_q, :] = jnp.broadcast_to(m + jnp.log(den), (w, HEAD_DIM))
            return carry

        lax.fori_loop(0, rb // w, unit, 0)

    ls = [r[...] for r in lg]
    mm = functools.reduce(jnp.maximum, ls)
    ws = [jnp.exp(l - mm) for l in ls]
    num = sum(wg * r[...] for wg, r in zip(ws, og))
    o_ref[...] = (num / sum(ws)).astype(o_ref.dtype)


def dilated_attention(proj, kv):
    bt, t, _ = proj.shape
    ng = len(DIL_GROUPS)
    rb = min(t, max(win for win, _ in DIL_GROUPS))
    for win, d in DIL_GROUPS:
        assert win % d == 0 and rb % win == 0
    assert t % rb == 0
    blk = lambda f: pl.BlockSpec((None, rb, HEAD_DIM), f)
    prev = lambda i: jnp.maximum(i - 1, 0)
    q_specs = [blk(lambda b, i, h, g=g: (b, i, g * DIL_HEADS + h)) for g in range(ng)]
    kv_specs = [blk(lambda b, i, h: (b, prev(i), h)), blk(lambda b, i, h: (b, i, h)),
                blk(lambda b, i, h: (b, prev(i), DIL_HEADS + h)), blk(lambda b, i, h: (b, i, DIL_HEADS + h))]
    return pl.pallas_call(
        functools.partial(_dil_attn_kernel, groups=DIL_GROUPS, rb=rb),
        grid=(bt, t // rb, DIL_HEADS),
        in_specs=q_specs + kv_specs,
        out_specs=blk(lambda b, i, h: (b, i, h)),
        out_shape=jax.ShapeDtypeStruct((bt, t, DIL_HEADS * HEAD_DIM), BF16),
        scratch_shapes=[pltpu.VMEM((2 * rb, HEAD_DIM), F32)] * 2 + [pltpu.VMEM((rb, HEAD_DIM), F32)] * (2 * ng),
        compiler_params=_cparams("parallel", "parallel", "parallel"),
        name="dilated_attention",
    )(*([proj] * ng), kv, kv, kv, kv)


def _dil_decode_kernel(q_ref, kv_ref, kc_ref, vc_ref, o_ref):
    rows = q_ref.shape[0]
    lc = kc_ref.shape[0]
    width = DIL_HEADS * HEAD_DIM
    scale = HEAD_DIM ** -0.5
    dist = lc - lax.broadcasted_iota(jnp.int32, (rows, lc), 1)
    for h in range(DIL_HEADS):
        sl = slice(h * HEAD_DIM, (h + 1) * HEAD_DIM)
        kc = kc_ref[:, sl].astype(BF16)
        vc = vc_ref[:, sl].astype(BF16)
        k_new = kv_ref[:, h * HEAD_DIM:(h + 1) * HEAD_DIM]
        v_new = kv_ref[:, width + h * HEAD_DIM:width + (h + 1) * HEAD_DIM]
        outs, lses = [], []
        for gi, (win, dil) in enumerate(DIL_GROUPS):
            q = q_ref[:, gi * width + h * HEAD_DIM:gi * width + (h + 1) * HEAD_DIM]
            s = lax.dot_general(q.astype(BF16), kc, NT_DIMS, preferred_element_type=F32) * scale
            ok = ((dist & (dil - 1)) == 0) & (dist <= win)
            s = jnp.where(ok, s, -jnp.inf)
            s_self = jnp.sum(q * k_new, axis=-1, keepdims=True) * scale
            m = jnp.maximum(jnp.max(s, axis=-1, keepdims=True), s_self)
            e = jnp.exp(s - m)
            e_self = jnp.exp(s_self - m)
            den = jnp.sum(e, axis=-1, keepdims=True) + e_self
            o = jnp.dot(e.astype(BF16), vc, preferred_element_type=F32) + e_self * v_new
            outs.append(o / den)
            lses.append(m + jnp.log(den))
        mm = functools.reduce(jnp.maximum, lses)
        ws = [jnp.exp(l - mm) for l in lses]
        num = sum(wg * o for wg, o in zip(ws, outs))
        o_ref[:, sl] = (num / sum(ws)).astype(o_ref.dtype)


def dilated_decode(proj, kv_new, k_cache, v_cache):
    bt, rows, nq = proj.shape
    lc = k_cache.shape[1]
    width = DIL_HEADS * HEAD_DIM
    for _, dil in DIL_GROUPS:
        assert dil & (dil - 1) == 0
    return pl.pallas_call(
        _dil_decode_kernel,
        grid=(bt,),
        in_specs=[pl.BlockSpec((None, rows, nq), lambda b: (b, 0, 0)),
                  pl.BlockSpec((None, rows, 2 * width), lambda b: (b, 0, 0)),
                  pl.BlockSpec((None, lc, width), lambda b: (b, 0, 0)),
                  pl.BlockSpec((None, lc, width), lambda b: (b, 0, 0))],
        out_specs=pl.BlockSpec((None, rows, width), lambda b: (b, 0, 0)),
        out_shape=jax.ShapeDtypeStruct((bt, rows, width), BF16),
        compiler_params=_cparams("parallel"),
        name="dilated_decode",
    )(proj, kv_new, k_cache, v_cache)


def kernel(x_prompt, x_sample, state_conv, state_ssm, cache_win_k, cache_win_v, cache_mem_k, cache_mem_v,
           mem_prompt, norm_mix_g, norm_mlp_g, norm_mem_g, w_mem_k, w_mem_v, mem_q_norm_g, mem_k_norm_g,
           w_up, w_down, w_in_a, conv_w, conv_b, dt_bias, a_log, d_skip, gate_norm_g, w_out_a,
           w_in_b, q_norm_g, w_out_b, kv_norm_g, w_k_shared, w_v_shared, k_norm_g):
    depth = w_up.shape[0]
    n_a = w_in_a.shape[0]
    d_model = x_prompt.shape[-1]
    n_heads = dt_bias.shape[1]
    d_inner = n_heads * SSM_HEAD_DIM
    conv_dim = conv_w.shape[2]
    mem_w = MEM_HEADS * HEAD_DIM
    kv_w = DIL_HEADS * HEAD_DIM
    dil_q_w = len(DIL_GROUPS) * kv_w

    s1, s2, s3 = d_inner, d_inner + conv_dim, d_inner + conv_dim + n_heads
    dt_pad = jnp.zeros((n_a, d_model, COL_TILE - n_heads), F32)
    w_a = jnp.concatenate([w_in_a[..., s1:s2], w_in_a[..., s2:s3], dt_pad, w_in_a[..., :s1], w_in_a[..., s3:]],
                          axis=-1).astype(BF16)
    col_xbc, col_dt, col_z = 0, conv_dim, conv_dim + COL_TILE
    col_qm_a = col_z + d_inner
    kinds_a = ["plain"] * (col_qm_a // COL_TILE) + ["norm"] * (mem_w // COL_TILE)
    kinds_b = ["rope"] * (dil_q_w // COL_TILE) + ["norm"] * (mem_w // COL_TILE)
    kinds_kv = ["rope"] * (kv_w // COL_TILE) + ["plain"] * (kv_w // COL_TILE)
    kinds_mkv = ["norm"] * (mem_w // COL_TILE) + ["plain"] * (mem_w // COL_TILE)
    w_b = w_in_b.astype(BF16)
    w_kv = jnp.concatenate([w_k_shared, w_v_shared], axis=-1).astype(BF16)[None]
    w_mkv = jnp.concatenate([w_mem_k, w_mem_v], axis=-1).astype(BF16)
    w_up_b, w_down_b = w_up.astype(BF16), w_down.astype(BF16)
    w_out_a_b, w_out_b_b = w_out_a.astype(BF16), w_out_b.astype(BF16)

    def gain_row(width, pieces):
        parts, at = [], 0
        for start, g, reps in pieces:
            parts += [jnp.ones((start - at,), F32), jnp.tile(g, reps)]
            at = start + reps * HEAD_DIM
        parts.append(jnp.ones((width - at,), F32))
        return jnp.concatenate(parts).reshape(1, width)

    def run(x, pos_rows, conv_prev, ssm_prev, mem_k, col_mk, mem_v, col_mv, k_past, v_past, t_real):
        bt, t, _ = x.shape
        m = bt * t
        x2 = x.reshape(m, d_model)
        cos, sin = rope_tables(pos_rows)
        conv_new, ssm_new = [], []
        for l in range(n_a):
            hg = gain_row(w_a.shape[-1], [(col_qm_a, mem_q_norm_g[l], MEM_HEADS)])
            proj3 = norm_proj(x2, norm_mix_g[l], w_a, l, hg, kinds_a).reshape(bt, t, -1)
            tp = -(-t // SSD_CHUNK) * SSD_CHUNK
            proj_ssd = proj3 if tp == t else jnp.pad(proj3, ((0, 0), (0, tp - t), (0, 0)))
            y, c_new, h_new = ssd_mixer(proj_ssd, col_xbc, col_dt, col_z, t_real, conv_w[l], conv_b[l],
                                        dt_bias[l], a_log[l], d_skip[l], gate_norm_g[l],
                                        conv_prev[l], ssm_prev[l].reshape(bt, SSM_GROUPS, -1, SSM_D_STATE))
            y = y[:, :t].reshape(m, d_inner)
            mo = mem_attention(proj3, col_qm_a, mem_k[l], col_mk, mem_v[l], col_mv).reshape(m, mem_w)
            x2 = out_proj(x2, y, mo, w_out_a_b, l)
            x2 = mlp(x2, norm_mlp_g[l], w_up_b, w_down_b, l)
            conv_new.append(c_new)
            ssm_new.append(h_new.reshape(bt, n_heads, SSM_HEAD_DIM, SSM_D_STATE))
        hg = gain_row(2 * kv_w, [(0, k_norm_g, DIL_HEADS)])
        kv3 = norm_proj(x2, kv_norm_g, w_kv, 0, hg, kinds_kv, cos, sin).reshape(bt, t, 2 * kv_w)
        for l in range(n_a, depth):
            j = l - n_a
            hg = gain_row(dil_q_w + mem_w, [(0, q_norm_g[j], dil_q_w // HEAD_DIM),
                                            (dil_q_w, mem_q_norm_g[l], MEM_HEADS)])
            proj3 = norm_proj(x2, norm_mix_g[l], w_b, j, hg, kinds_b, cos, sin).reshape(bt, t, -1)
            if k_past is None:
                att = dilated_attention(proj3, kv3)
            else:
                att = dilated_decode(proj3, kv3, k_past, v_past)
            mo = mem_attention(proj3, dil_q_w, mem_k[l], col_mk, mem_v[l], col_mv).reshape(m, mem_w)
            x2 = out_proj(x2, att.reshape(m, kv_w), mo, w_out_b_b, j)
            x2 = mlp(x2, norm_mlp_g[l], w_up_b, w_down_b, l)
        return x2.reshape(bt, t, d_model), jnp.stack(conv_new), jnp.stack(ssm_new), kv3

    bp, t_p, _ = x_prompt.shape
    mlen = mem_prompt.shape[1]
    mem2 = mem_prompt.reshape(bp * mlen, d_model)
    mkv_p = []
    for l in range(depth):
        hg = gain_row(2 * mem_w, [(0, mem_k_norm_g[l], MEM_HEADS)])
        mkv_p.append(norm_proj(mem2, norm_mem_g[l], w_mkv, l, hg, kinds_mkv).reshape(bp, mlen, 2 * mem_w))
    conv0 = jnp.zeros((n_a, bp, CONV_W - 1, conv_dim), F32)
    ssm0 = jnp.zeros((n_a, bp, n_heads, SSM_HEAD_DIM, SSM_D_STATE), F32)
    y_p, conv_p, ssm_p, kv_p = run(x_prompt, jnp.arange(t_p, dtype=jnp.int32), conv0, ssm0,
                                   mkv_p, 0, mkv_p, mem_w, None, None, t_p)
    mkv_all = jnp.stack(mkv_p)
    mem_k_p = mkv_all[..., :mem_w].reshape(depth, bp, mlen, MEM_HEADS, HEAD_DIM)
    mem_v_p = mkv_all[..., mem_w:].reshape(depth, bp, mlen, MEM_HEADS, HEAD_DIM)
    keep = min(max(w for w, _ in DIL_GROUPS), t_p)
    win_k_p = kv_p[:, t_p - keep:, :kv_w].reshape(bp, keep, DIL_HEADS, HEAD_DIM)
    win_v_p = kv_p[:, t_p - keep:, kv_w:].reshape(bp, keep, DIL_HEADS, HEAD_DIM)

    bs, t_s, _ = x_sample.shape
    assert t_s == 1
    xs = jnp.pad(x_sample, ((0, 0), (0, SAMPLE_ROWS - t_s), (0, 0)))
    pos_s = jnp.full((bs * SAMPLE_ROWS,), PAST_LEN, jnp.int32)
    lc = cache_win_k.shape[1]
    cmk = cache_mem_k.reshape(depth, bs, mlen, mem_w)
    cmv = cache_mem_v.reshape(depth, bs, mlen, mem_w)
    y_s, conv_s, ssm_s, kv_s = run(xs, pos_s, state_conv, state_ssm, cmk, 0, cmv, 0,
                                   cache_win_k.reshape(bs, lc, kv_w), cache_win_v.reshape(bs, lc, kv_w), t_s)
    y_s = y_s[:, :t_s]
    k_s = kv_s[:, :t_s, :kv_w].reshape(bs, t_s, DIL_HEADS, HEAD_DIM)
    v_s = kv_s[:, :t_s, kv_w:].reshape(bs, t_s, DIL_HEADS, HEAD_DIM)

    return (y_p, y_s, conv_p, ssm_p, win_k_p, win_v_p, mem_k_p, mem_v_p, conv_s, ssm_s, k_s, v_s)
```

```python
import functools

import jax
import jax.numpy as jnp
from jax import lax
from jax.experimental import pallas as pl
from jax.experimental.pallas import tpu as pltpu

F32 = jnp.float32
BF16 = jnp.bfloat16

HEAD_DIM = 128
SSM_HEAD_DIM = 64
SSM_GROUPS = 4
SSM_D_STATE = 128
CONV_W = 4
SSD_CHUNK = 128
MEM_HEADS = 4
DIL_GROUPS = ((128, 1), (512, 4), (2048, 16))
DIL_HEADS = 4
PAST_LEN = 16384
ROPE_THETA = 10000.0
EPS = 1e-6

LANES = 128
SUBLANES = 8
VMEM_LIMIT = 48 * 1024 * 1024
COL_TILE = 512
ROW_TILE = 512
PROJ_OUT_BLOCK_BYTES = 5 * 1024 * 1024
SAMPLE_ROWS = SUBLANES
DIL_UNROLL = 4

NT_DIMS = (((1,), (1,)), ((), ()))
RESIDENT = pl.Buffered(1)


def _cparams(*sem):
    return pltpu.CompilerParams(dimension_semantics=sem, vmem_limit_bytes=VMEM_LIMIT)


def _sigmoid(x):
    return 1.0 / (1.0 + jnp.exp(-x))


def _rms_scale(x):
    return lax.rsqrt(jnp.mean(x * x, axis=-1, keepdims=True) + EPS)


def _rope_table_kernel(pos_ref, inv_ref, cos_ref, sin_ref):
    ang = pos_ref[...] * inv_ref[...]
    lane = lax.broadcasted_iota(jnp.int32, ang.shape, 1)
    cos_ref[...] = jnp.cos(ang)
    s = jnp.sin(ang)
    sin_ref[...] = jnp.where(lane < HEAD_DIM // 2, -s, s)


def rope_tables(pos):
    r = pos.shape[0]
    half = HEAD_DIM // 2
    inv = ROPE_THETA ** (-jnp.arange(half, dtype=F32) * (2.0 / HEAD_DIM))
    inv = jnp.concatenate([inv, inv])[None, :]
    posf = jnp.broadcast_to(pos.astype(F32)[:, None], (r, HEAD_DIM))
    tr = min(r, ROW_TILE)
    return pl.pallas_call(
        _rope_table_kernel,
        grid=(r // tr,),
        in_specs=[pl.BlockSpec((tr, HEAD_DIM), lambda i: (i, 0)),
                  pl.BlockSpec((1, HEAD_DIM), lambda i: (0, 0))],
        out_specs=[pl.BlockSpec((tr, HEAD_DIM), lambda i: (i, 0))] * 2,
        out_shape=[jax.ShapeDtypeStruct((r, HEAD_DIM), F32)] * 2,
        compiler_params=_cparams("parallel"),
        name="rope_tables",
    )(posf, inv)


def _proj_kernel(*refs, kinds):
    if "rope" in kinds:
        x_ref, g_ref, w_ref, hg_ref, cos_ref, sin_ref, o_ref, xn_ref = refs
    else:
        x_ref, g_ref, w_ref, hg_ref, o_ref, xn_ref = refs
    x = x_ref[...]
    xn_ref[...] = (x * _rms_scale(x) * g_ref[...]).astype(BF16)
    for t, kind in enumerate(kinds):
        cols = slice(t * COL_TILE, (t + 1) * COL_TILE)
        y = jnp.dot(xn_ref[...], w_ref[:, cols], preferred_element_type=F32)
        if kind == "plain":
            o_ref[:, cols] = y
            continue
        for j in range(COL_TILE // HEAD_DIM):
            sl = slice(j * HEAD_DIM, (j + 1) * HEAD_DIM)
            osl = slice(t * COL_TILE + j * HEAD_DIM, t * COL_TILE + (j + 1) * HEAD_DIM)
            yh = y[:, sl]
            yh = yh * _rms_scale(yh) * hg_ref[:, osl]
            if kind == "rope":
                yh = yh * cos_ref[...] + pltpu.roll(yh, HEAD_DIM // 2, axis=1) * sin_ref[...]
            o_ref[:, osl] = yh


def norm_proj(x, g, w, layer, head_gain, kinds, cos=None, sin=None):
    m, d = x.shape
    n_cols = w.shape[2]
    assert n_cols == COL_TILE * len(kinds)
    tm = min(m, ROW_TILE)
    while tm * n_cols * 4 > PROJ_OUT_BLOCK_BYTES and tm % (2 * SUBLANES) == 0:
        tm //= 2
    assert m % tm == 0
    in_specs = [pl.BlockSpec((tm, d), lambda i: (i, 0)),
                pl.BlockSpec((1, d), lambda i: (0, 0)),
                pl.BlockSpec((None, d, n_cols), lambda i: (layer, 0, 0), pipeline_mode=RESIDENT),
                pl.BlockSpec((1, n_cols), lambda i: (0, 0))]
    args = [x, g.reshape(1, d), w, head_gain]
    if "rope" in kinds:
        assert cos.shape[0] % tm == 0
        tab_blocks = cos.shape[0] // tm
        in_specs += [pl.BlockSpec((tm, HEAD_DIM), lambda i: (i % tab_blocks, 0))] * 2
        args += [cos, sin]
    return pl.pallas_call(
        functools.partial(_proj_kernel, kinds=tuple(kinds)),
        grid=(m // tm,),
        in_specs=in_specs,
        out_specs=pl.BlockSpec((tm, n_cols), lambda i: (i, 0)),
        out_shape=jax.ShapeDtypeStruct((m, n_cols), F32),
        scratch_shapes=[pltpu.VMEM((tm, d), BF16)],
        compiler_params=_cparams("parallel"),
        name="norm_proj",
    )(*args)


def _out_proj_kernel(x_ref, a_ref, b_ref, wa_ref, wb_ref, o_ref):
    acc = jnp.dot(a_ref[...], wa_ref[...], preferred_element_type=F32)
    acc = acc + jnp.dot(b_ref[...], wb_ref[...], preferred_element_type=F32)
    o_ref[...] = x_ref[...] + acc


def out_proj(x, a, b, w, layer):
    m, d = x.shape
    tm = min(m, ROW_TILE)
    ka, kb = a.shape[1], b.shape[1]
    assert w.shape[1] == ka + kb and ka % kb == 0
    return pl.pallas_call(
        _out_proj_kernel,
        grid=(m // tm,),
        in_specs=[pl.BlockSpec((tm, d), lambda i: (i, 0)),
                  pl.BlockSpec((tm, ka), lambda i: (i, 0)),
                  pl.BlockSpec((tm, kb), lambda i: (i, 0)),
                  pl.BlockSpec((None, ka, d), lambda i: (layer, 0, 0), pipeline_mode=RESIDENT),
                  pl.BlockSpec((None, kb, d), lambda i: (layer, ka // kb, 0), pipeline_mode=RESIDENT)],
        out_specs=pl.BlockSpec((tm, d), lambda i: (i, 0)),
        out_shape=jax.ShapeDtypeStruct((m, d), F32),
        compiler_params=_cparams("parallel"),
        name="out_proj",
    )(x, a, b, w, w)


def _mlp_kernel(x_ref, g_ref, wu_ref, wd_ref, o_ref, xn_ref):
    f = pl.program_id(1)

    @pl.when(f == 0)
    def _():
        x = x_ref[...]
        xn_ref[...] = (x * _rms_scale(x) * g_ref[...]).astype(BF16)
        o_ref[...] = x

    h = jnp.dot(xn_ref[...], wu_ref[...], preferred_element_type=F32)
    h = jnp.maximum(h, 0.0)
    h = (h * h).astype(BF16)
    o_ref[...] += jnp.dot(h, wd_ref[...], preferred_element_type=F32)


def mlp(x, g, wu, wd, layer, tf=1024):
    m, d = x.shape
    ff = wu.shape[2]
    tm = min(m, ROW_TILE)
    return pl.pallas_call(
        _mlp_kernel,
        grid=(m // tm, ff // tf),
        in_specs=[pl.BlockSpec((tm, d), lambda i, j: (i, 0)),
                  pl.BlockSpec((1, d), lambda i, j: (0, 0)),
                  pl.BlockSpec((None, d, tf), lambda i, j: (layer, 0, j)),
                  pl.BlockSpec((None, tf, d), lambda i, j: (layer, j, 0))],
        out_specs=pl.BlockSpec((tm, d), lambda i, j: (i, 0)),
        out_shape=jax.ShapeDtypeStruct((m, d), F32),
        scratch_shapes=[pltpu.VMEM((tm, d), BF16)],
        compiler_params=_cparams("parallel", "arbitrary"),
        name="mlp",
    )(x, g.reshape(1, d), wu, wd)


def _split3(x):
    hi = x.astype(BF16)
    r1 = x - hi.astype(F32)
    mid = r1.astype(BF16)
    lo = (r1 - mid.astype(F32)).astype(BF16)
    return hi, mid, lo


def _ssd_kernel(xbc_ref, dt_ref, z_ref, cw_ref, cb_ref, dtb_ref, alog_ref, dsk_ref, gg_ref, tri_ref, expand_ref,
                conv0_ref, h0_ref, y_ref, convo_ref, ho_ref,
                ext_ref, ht_ref, yd_ref, st_ref, *, t_real, n_heads, d_inner):
    L = SSD_CHUNK
    P = SSM_HEAD_DIM
    N = SSM_D_STATE
    G = SSM_GROUPS
    hpg = n_heads // G
    gw = hpg * P
    hist = CONV_W - 1
    base = SUBLANES - hist
    c = pl.program_id(1)
    nc = pl.num_programs(1)

    @pl.when(c == 0)
    def _():
        ext_ref[base:SUBLANES, :] = conv0_ref[...]
        for g in range(G):
            ht_ref[g] = h0_ref[g].T

    ext_ref[SUBLANES:SUBLANES + L, :] = xbc_ref[...]
    acc = jnp.broadcast_to(cb_ref[...], (L, cb_ref.shape[1]))
    for w in range(CONV_W):
        acc = acc + ext_ref[base + w:base + w + L, :] * cw_ref[w:w + 1, :]
    xc = acc * _sigmoid(acc)

    @pl.when(c == nc - 1)
    def _():
        n_in_last = t_real - ((t_real - 1) // L) * L
        convo_ref[...] = ext_ref[base + n_in_last:base + n_in_last + hist, :]

    ext_ref[base:SUBLANES, :] = ext_ref[base + L:SUBLANES + L, :]

    dt_raw = dt_ref[...] + dtb_ref[...]
    dt = jnp.maximum(dt_raw, 0.0) + jnp.log1p(jnp.exp(-jnp.abs(dt_raw)))
    row = lax.broadcasted_iota(jnp.int32, (L, LANES), 0) + c * L
    dt = jnp.where(row < t_real, dt, 0.0)
    da = dt * (-jnp.exp(alog_ref[...]))
    li = lax.broadcasted_iota(jnp.int32, (L, L), 0)
    si = lax.broadcasted_iota(jnp.int32, (L, L), 1)
    causal = li >= si
    a_cs = sum(jnp.dot(tri_ref[...], part, preferred_element_type=F32) for part in _split3(da))
    a_cs_t = a_cs.T
    dt_t = dt.T
    w_t = dt_t * jnp.exp(a_cs_t[:, L - 1:L] - a_cs_t)
    e_exp = sum(jnp.dot(part, expand_ref[...], preferred_element_type=F32) for part in _split3(jnp.exp(a_cs)))
    xb = xc[:, :d_inner].astype(BF16)
    low_half = lax.broadcasted_iota(jnp.int32, (L, 2 * P), 1) < P

    for g in range(G):
        b_g = xc[:, d_inner + g * N:d_inner + (g + 1) * N]
        c_g = xc[:, d_inner + G * N + g * N:d_inner + G * N + (g + 1) * N].astype(BF16)
        cbm = lax.dot_general(c_g, b_g.astype(BF16), NT_DIMS, preferred_element_type=F32)
        b_t = b_g.T
        y_off = jnp.dot(c_g, ht_ref[g].astype(BF16), preferred_element_type=F32)
        for kp in range(hpg // 2):
            h0 = g * hpg + 2 * kp
            x_pair = xb[:, h0 * P:(h0 + 2) * P]
            yd, st = [], []
            for h in (h0, h0 + 1):
                seg = a_cs[:, h:h + 1] - a_cs_t[h:h + 1, :]
                dec = jnp.exp(jnp.where(causal, seg, -jnp.inf))
                mh = (cbm * dec * dt_t[h:h + 1, :]).astype(BF16)
                yd.append(jnp.dot(mh, x_pair, preferred_element_type=F32))
                btw = (b_t * w_t[h:h + 1, :]).astype(BF16)
                st.append(jnp.dot(btw, x_pair, preferred_element_type=F32))
            yd_ref[:, 2 * kp * P:(2 * kp + 2) * P] = jnp.where(low_half, yd[0], yd[1])
            st_ref[:, 2 * kp * P:(2 * kp + 2) * P] = jnp.where(low_half, st[0], st[1])
        gs = slice(g * gw, (g + 1) * gw)
        e_g = e_exp[:, gs]
        y_g = yd_ref[...] + y_off * e_g + dsk_ref[:, gs] * xc[:, gs]
        ht_ref[g] = ht_ref[g] * e_g[L - 1:L, :] + st_ref[...]
        z_g = z_ref[:, gs]
        gated = y_g * (z_g * _sigmoid(z_g))
        y_ref[:, gs] = (gated * _rms_scale(gated) * gg_ref[:, gs]).astype(y_ref.dtype)

    @pl.when(c == nc - 1)
    def _():
        for g in range(G):
            ho_ref[g] = ht_ref[g].T


def ssd_mixer(proj, col_xbc, col_dt, col_z, t_real, conv_w, conv_b, dt_bias, a_log, d_skip, gate_g,
              conv_prev, h_prev):
    bt, tp, _ = proj.shape
    conv_dim = conv_w.shape[1]
    n_heads = dt_bias.shape[0]
    d_inner = n_heads * SSM_HEAD_DIM
    gw = d_inner // SSM_GROUPS
    L = SSD_CHUNK
    assert tp % L == 0 and (tp - t_real) < L
    assert col_xbc % conv_dim == 0 and col_dt % LANES == 0 and col_z % d_inner == 0
    pad = LANES - n_heads
    dtb = jnp.pad(dt_bias, (0, pad)).reshape(1, LANES)
    alog = jnp.pad(a_log, (0, pad)).reshape(1, LANES)
    dsk = jnp.repeat(d_skip, SSM_HEAD_DIM).reshape(1, d_inner)
    tri = jnp.tril(jnp.ones((L, L), BF16))
    expand = jnp.repeat(jnp.eye(LANES, dtype=BF16)[:, :n_heads], SSM_HEAD_DIM, axis=1)
    kern = functools.partial(_ssd_kernel, t_real=t_real, n_heads=n_heads, d_inner=d_inner)
    vec = lambda n: pl.BlockSpec((1, n), lambda b, c: (0, 0))
    return pl.pallas_call(
        kern,
        grid=(bt, tp // L),
        in_specs=[pl.BlockSpec((None, L, conv_dim), lambda b, c: (b, c, col_xbc // conv_dim)),
                  pl.BlockSpec((None, L, LANES), lambda b, c: (b, c, col_dt // LANES)),
                  pl.BlockSpec((None, L, d_inner), lambda b, c: (b, c, col_z // d_inner)),
                  pl.BlockSpec((CONV_W, conv_dim), lambda b, c: (0, 0)),
                  vec(conv_dim), vec(LANES), vec(LANES), vec(d_inner), vec(d_inner),
                  pl.BlockSpec((L, L), lambda b, c: (0, 0)),
                  pl.BlockSpec((LANES, d_inner), lambda b, c: (0, 0)),
                  pl.BlockSpec((None, CONV_W - 1, conv_dim), lambda b, c: (b, 0, 0)),
                  pl.BlockSpec((None, SSM_GROUPS, gw, SSM_D_STATE), lambda b, c: (b, 0, 0, 0))],
        out_specs=[pl.BlockSpec((None, L, d_inner), lambda b, c: (b, c, 0)),
                   pl.BlockSpec((None, CONV_W - 1, conv_dim), lambda b, c: (b, 0, 0)),
                   pl.BlockSpec((None, SSM_GROUPS, gw, SSM_D_STATE), lambda b, c: (b, 0, 0, 0))],
        out_shape=[jax.ShapeDtypeStruct((bt, tp, d_inner), BF16),
                   jax.ShapeDtypeStruct((bt, CONV_W - 1, conv_dim), F32),
                   jax.ShapeDtypeStruct((bt, SSM_GROUPS, gw, SSM_D_STATE), F32)],
        scratch_shapes=[pltpu.VMEM((SUBLANES + L, conv_dim), F32),
                        pltpu.VMEM((SSM_GROUPS, SSM_D_STATE, gw), F32),
                        pltpu.VMEM((L, gw), F32),
                        pltpu.VMEM((SSM_D_STATE, gw), F32)],
        compiler_params=_cparams("parallel", "arbitrary"),
        name="ssd_mixer",
    )(proj, proj, proj, conv_w, conv_b.reshape(1, conv_dim), dtb, alog, dsk, gate_g.reshape(1, d_inner),
      tri, expand, conv_prev, h_prev)


def _mem_attn_kernel(q_ref, k_ref, v_ref, o_ref):
    scale = HEAD_DIM ** -0.5
    headed = len(k_ref.shape) == 3
    for h in range(MEM_HEADS):
        sl = slice(h * HEAD_DIM, (h + 1) * HEAD_DIM)
        q = q_ref[:, sl].astype(BF16)
        k = (k_ref[:, h, :] if headed else k_ref[:, sl]).astype(BF16)
        v = (v_ref[:, h, :] if headed else v_ref[:, sl]).astype(BF16)
        s = lax.dot_general(q, k, NT_DIMS, preferred_element_type=F32) * scale
        e = jnp.exp(s - jnp.max(s, axis=-1, keepdims=True))
        den = jnp.sum(e, axis=-1, keepdims=True)
        o = jnp.dot(e.astype(BF16), v, preferred_element_type=F32) / den
        o_ref[:, sl] = o.astype(o_ref.dtype)


def mem_attention(proj, col_q, k, v, kv_at):
    bt, t, _ = proj.shape
    width = MEM_HEADS * HEAD_DIM
    tq = min(t, ROW_TILE)
    if k.ndim == 5:
        mlen = k.shape[2]
        kv_specs = [pl.BlockSpec((None, None, mlen, MEM_HEADS, HEAD_DIM), lambda b, i: (kv_at, b, 0, 0, 0))] * 2
    else:
        mlen = k.shape[1]
        kv_specs = [pl.BlockSpec((None, mlen, width), lambda b, i, c=c: (b, 0, c // width)) for c in kv_at]
    return pl.pallas_call(
        _mem_attn_kernel,
        grid=(bt, t // tq),
        in_specs=[pl.BlockSpec((None, tq, width), lambda b, i: (b, i, col_q // width))] + kv_specs,
        out_specs=pl.BlockSpec((None, tq, width), lambda b, i: (b, i, 0)),
        out_shape=jax.ShapeDtypeStruct((bt, t, width), BF16),
        compiler_params=_cparams("parallel", "parallel"),
        name="mem_attention",
    )(proj, k, v)


def _run_units(first, count, unit, unroll):
    trips = count // unroll
    if trips:
        def trip(it, carry):
            for k in range(unroll):
                unit(first + it * unroll + k)
            return carry
        lax.fori_loop(0, trips, trip, 0)
    for k in range(trips * unroll, count):
        unit(first + k)


def _dil_attn_kernel(*refs, groups, rb, unroll):
    ng = len(groups)
    q_refs = refs[:ng]
    kp_ref, kc_ref, vp_ref, vc_ref, o_ref = refs[ng:ng + 5]
    og = refs[ng + 5:2 * ng + 5]
    lg = refs[2 * ng + 5:3 * ng + 5]
    i = pl.program_id(1)
    scale = HEAD_DIM ** -0.5

    for gi, (win, d) in enumerate(groups):
        w = win // d
        dw = d * w
        qi = lax.broadcasted_iota(jnp.int32, (w, 2 * w), 0)
        kj = lax.broadcasted_iota(jnp.int32, (w, 2 * w), 1)
        band = (kj >= qi) & (kj <= qi + w)
        band_first = band & (kj >= jnp.where(i > 0, 0, w))

        def rows(start, size, d=d, w=w):
            if d == 1:
                return pl.ds(pl.multiple_of(start, w), size)
            return pl.ds(start, size, stride=d)

        def attend(rows_q, kk, vv, mask, gi=gi, w=w):
            q = q_refs[gi][rows_q, :].astype(BF16)
            s = lax.dot_general(q, kk.astype(BF16), NT_DIMS, preferred_element_type=F32) * scale
            s = jnp.where(mask, s, -jnp.inf)
            m = jnp.max(s, axis=-1, keepdims=True)
            e = jnp.exp(s - m)
            den = jnp.sum(e, axis=-1, keepdims=True)
            o = jnp.dot(e.astype(BF16), vv.astype(BF16), preferred_element_type=F32) / den
            og[gi][rows_q, :] = o
            lg[gi][rows_q, :] = jnp.broadcast_to(m + jnp.log(den), (w, HEAD_DIM))

        def first_unit(r, rows=rows, attend=attend, w=w, dw=dw, mask=band_first):
            rows_q, rows_p = rows(r, w), rows(rb - dw + r, w)
            kk = jnp.concatenate([kp_ref[rows_p, :], kc_ref[rows_q, :]], axis=0)
            vv = jnp.concatenate([vp_ref[rows_p, :], vc_ref[rows_q, :]], axis=0)
            attend(rows_q, kk, vv, mask)

        def later_unit(u, rows=rows, attend=attend, d=d, w=w, dw=dw, mask=band):
            sb = u // d
            start_q = sb * dw + (u - sb * d)
            rows_k = rows(start_q - dw, 2 * w)
            attend(rows(start_q, w), kc_ref[rows_k, :], vc_ref[rows_k, :], mask)

        _run_units(0, d, first_unit, unroll)
        _run_units(d, rb // w - d, later_unit, unroll)

    ls = [r[...] for r in lg]
    mm = functools.reduce(jnp.maximum, ls)
    ws = [jnp.exp(l - mm) for l in ls]
    num = sum(wg * r[...] for wg, r in zip(ws, og))
    o_ref[...] = (num / sum(ws)).astype(o_ref.dtype)


def dilated_attention(proj, kv):
    bt, t, _ = proj.shape
    ng = len(DIL_GROUPS)
    rb = min(t, max(win for win, _ in DIL_GROUPS))
    for win, d in DIL_GROUPS:
        assert win % d == 0 and rb % win == 0
    assert t % rb == 0
    blk = lambda f: pl.BlockSpec((None, rb, HEAD_DIM), f)
    prev = lambda i: jnp.maximum(i - 1, 0)
    q_specs = [blk(lambda b, i, h, g=g: (b, i, g * DIL_HEADS + h)) for g in range(ng)]
    kv_specs = [blk(lambda b, i, h: (b, prev(i), h)), blk(lambda b, i, h: (b, i, h)),
                blk(lambda b, i, h: (b, prev(i), DIL_HEADS + h)), blk(lambda b, i, h: (b, i, DIL_HEADS + h))]
    return pl.pallas_call(
        functools.partial(_dil_attn_kernel, groups=DIL_GROUPS, rb=rb, unroll=DIL_UNROLL),
        grid=(bt, t // rb, DIL_HEADS),
        in_specs=q_specs + kv_specs,
        out_specs=blk(lambda b, i, h: (b, i, h)),
        out_shape=jax.ShapeDtypeStruct((bt, t, DIL_HEADS * HEAD_DIM), BF16),
        scratch_shapes=[pltpu.VMEM((rb, HEAD_DIM), F32)] * (2 * ng),
        compiler_params=_cparams("parallel", "parallel", "parallel"),
        name="dilated_attention",
    )(*([proj] * ng), kv, kv, kv, kv)


def _dil_decode_kernel(q_ref, kv_ref, kc_ref, vc_ref, o_ref):
    rows = q_ref.shape[0]
    lc = kc_ref.shape[0]
    width = DIL_HEADS * HEAD_DIM
    scale = HEAD_DIM ** -0.5
    dist = lc - lax.broadcasted_iota(jnp.int32, (rows, lc), 1)
    for h in range(DIL_HEADS):
        sl = slice(h * HEAD_DIM, (h + 1) * HEAD_DIM)
        kc = kc_ref[:, h, :].astype(BF16)
        vc = vc_ref[:, h, :].astype(BF16)
        k_new = kv_ref[:, h * HEAD_DIM:(h + 1) * HEAD_DIM]
        v_new = kv_ref[:, width + h * HEAD_DIM:width + (h + 1) * HEAD_DIM]
        outs, lses = [], []
        for gi, (win, dil) in enumerate(DIL_GROUPS):
            q = q_ref[:, gi * width + h * HEAD_DIM:gi * width + (h + 1) * HEAD_DIM]
            s = lax.dot_general(q.astype(BF16), kc, NT_DIMS, preferred_element_type=F32) * scale
            ok = ((dist & (dil - 1)) == 0) & (dist <= win)
            s = jnp.where(ok, s, -jnp.inf)
            s_self = jnp.sum(q * k_new, axis=-1, keepdims=True) * scale
            m = jnp.maximum(jnp.max(s, axis=-1, keepdims=True), s_self)
            e = jnp.exp(s - m)
            e_self = jnp.exp(s_self - m)
            den = jnp.sum(e, axis=-1, keepdims=True) + e_self
            o = jnp.dot(e.astype(BF16), vc, preferred_element_type=F32) + e_self * v_new
            outs.append(o / den)
            lses.append(m + jnp.log(den))
        mm = functools.reduce(jnp.maximum, lses)
        ws = [jnp.exp(l - mm) for l in lses]
        num = sum(wg * o for wg, o in zip(ws, outs))
        o_ref[:, sl] = (num / sum(ws)).astype(o_ref.dtype)


def dilated_decode(proj, kv_new, k_cache, v_cache):
    bt, rows, nq = proj.shape
    lc = k_cache.shape[1]
    width = DIL_HEADS * HEAD_DIM
    for _, dil in DIL_GROUPS:
        assert dil & (dil - 1) == 0
    return pl.pallas_call(
        _dil_decode_kernel,
        grid=(bt,),
        in_specs=[pl.BlockSpec((None, rows, nq), lambda b: (b, 0, 0)),
                  pl.BlockSpec((None, rows, 2 * width), lambda b: (b, 0, 0)),
                  pl.BlockSpec((None, lc, DIL_HEADS, HEAD_DIM), lambda b: (b, 0, 0, 0)),
                  pl.BlockSpec((None, lc, DIL_HEADS, HEAD_DIM), lambda b: (b, 0, 0, 0))],
        out_specs=pl.BlockSpec((None, rows, width), lambda b: (b, 0, 0)),
        out_shape=jax.ShapeDtypeStruct((bt, rows, width), BF16),
        compiler_params=_cparams("parallel"),
        name="dilated_decode",
    )(proj, kv_new, k_cache, v_cache)


def kernel(x_prompt, x_sample, state_conv, state_ssm, cache_win_k, cache_win_v, cache_mem_k, cache_mem_v,
           mem_prompt, norm_mix_g, norm_mlp_g, norm_mem_g, w_mem_k, w_mem_v, mem_q_norm_g, mem_k_norm_g,
           w_up, w_down, w_in_a, conv_w, conv_b, dt_bias, a_log, d_skip, gate_norm_g, w_out_a,
           w_in_b, q_norm_g, w_out_b, kv_norm_g, w_k_shared, w_v_shared, k_norm_g):
    depth = w_up.shape[0]
    n_a = w_in_a.shape[0]
    d_model = x_prompt.shape[-1]
    n_heads = dt_bias.shape[1]
    d_inner = n_heads * SSM_HEAD_DIM
    conv_dim = conv_w.shape[2]
    mem_w = MEM_HEADS * HEAD_DIM
    kv_w = DIL_HEADS * HEAD_DIM
    dil_q_w = len(DIL_GROUPS) * kv_w

    s1, s2, s3 = d_inner, d_inner + conv_dim, d_inner + conv_dim + n_heads
    dt_pad = jnp.zeros((n_a, d_model, COL_TILE - n_heads), BF16)
    w_in_a_b = w_in_a.astype(BF16)
    w_a = jnp.concatenate([w_in_a_b[..., s1:s2], w_in_a_b[..., s2:s3], dt_pad, w_in_a_b[..., :s1],
                           w_in_a_b[..., s3:]], axis=-1)
    col_xbc, col_dt, col_z = 0, conv_dim, conv_dim + COL_TILE
    col_qm_a = col_z + d_inner
    kinds_a = ["plain"] * (col_qm_a // COL_TILE) + ["norm"] * (mem_w // COL_TILE)
    kinds_b = ["rope"] * (dil_q_w // COL_TILE) + ["norm"] * (mem_w // COL_TILE)
    kinds_kv = ["rope"] * (kv_w // COL_TILE) + ["plain"] * (kv_w // COL_TILE)
    kinds_mkv = ["norm"] * (mem_w // COL_TILE) + ["plain"] * (mem_w // COL_TILE)
    w_b = w_in_b.astype(BF16)
    w_kv = jnp.concatenate([w_k_shared, w_v_shared], axis=-1).astype(BF16)[None]
    w_mkv = jnp.concatenate([w_mem_k, w_mem_v], axis=-1).astype(BF16)
    w_up_b, w_down_b = w_up.astype(BF16), w_down.astype(BF16)
    w_out_a_b, w_out_b_b = w_out_a.astype(BF16), w_out_b.astype(BF16)

    def gain_row(width, pieces):
        parts, at = [], 0
        for start, g, reps in pieces:
            parts += [jnp.ones((start - at,), F32), jnp.tile(g, reps)]
            at = start + reps * HEAD_DIM
        parts.append(jnp.ones((width - at,), F32))
        return jnp.concatenate(parts).reshape(1, width)

    def run(x, pos_rows, conv_prev, ssm_prev, mem_kv, k_past, v_past, t_real):
        bt, t, _ = x.shape
        m = bt * t
        x2 = x.reshape(m, d_model)
        cos, sin = rope_tables(pos_rows)
        conv_new, ssm_new = [], []
        for l in range(n_a):
            hg = gain_row(w_a.shape[-1], [(col_qm_a, mem_q_norm_g[l], MEM_HEADS)])
            proj3 = norm_proj(x2, norm_mix_g[l], w_a, l, hg, kinds_a).reshape(bt, t, -1)
            tp = -(-t // SSD_CHUNK) * SSD_CHUNK
            proj_ssd = proj3 if tp == t else jnp.pad(proj3, ((0, 0), (0, tp - t), (0, 0)))
            y, c_new, h_new = ssd_mixer(proj_ssd, col_xbc, col_dt, col_z, t_real, conv_w[l], conv_b[l],
                                        dt_bias[l], a_log[l], d_skip[l], gate_norm_g[l],
                                        conv_prev[l], ssm_prev[l].reshape(bt, SSM_GROUPS, -1, SSM_D_STATE))
            y = y[:, :t].reshape(m, d_inner)
            mo = mem_attention(proj3, col_qm_a, *mem_kv(l)).reshape(m, mem_w)
            x2 = out_proj(x2, y, mo, w_out_a_b, l)
            x2 = mlp(x2, norm_mlp_g[l], w_up_b, w_down_b, l)
            conv_new.append(c_new)
            ssm_new.append(h_new.reshape(bt, n_heads, SSM_HEAD_DIM, SSM_D_STATE))
        hg = gain_row(2 * kv_w, [(0, k_norm_g, DIL_HEADS)])
        kv3 = norm_proj(x2, kv_norm_g, w_kv, 0, hg, kinds_kv, cos, sin).reshape(bt, t, 2 * kv_w)
        for l in range(n_a, depth):
            j = l - n_a
            hg = gain_row(dil_q_w + mem_w, [(0, q_norm_g[j], dil_q_w // HEAD_DIM),
                                            (dil_q_w, mem_q_norm_g[l], MEM_HEADS)])
            proj3 = norm_proj(x2, norm_mix_g[l], w_b, j, hg, kinds_b, cos, sin).reshape(bt, t, -1)
            if k_past is None:
                att = dilated_attention(proj3, kv3)
            else:
                att = dilated_decode(proj3, kv3, k_past, v_past)
            mo = mem_attention(proj3, dil_q_w, *mem_kv(l)).reshape(m, mem_w)
            x2 = out_proj(x2, att.reshape(m, kv_w), mo, w_out_b_b, j)
            x2 = mlp(x2, norm_mlp_g[l], w_up_b, w_down_b, l)
        return x2.reshape(bt, t, d_model), jnp.stack(conv_new), jnp.stack(ssm_new), kv3

    bp, t_p, _ = x_prompt.shape
    mlen = mem_prompt.shape[1]
    mem2 = mem_prompt.reshape(bp * mlen, d_model)
    mkv_p = []
    for l in range(depth):
        hg = gain_row(2 * mem_w, [(0, mem_k_norm_g[l], MEM_HEADS)])
        mkv_p.append(norm_proj(mem2, norm_mem_g[l], w_mkv, l, hg, kinds_mkv).reshape(bp, mlen, 2 * mem_w))
    conv0 = jnp.zeros((n_a, bp, CONV_W - 1, conv_dim), F32)
    ssm0 = jnp.zeros((n_a, bp, n_heads, SSM_HEAD_DIM, SSM_D_STATE), F32)
    y_p, conv_p, ssm_p, kv_p = run(x_prompt, jnp.arange(t_p, dtype=jnp.int32), conv0, ssm0,
                                   lambda l: (mkv_p[l], mkv_p[l], (0, mem_w)), None, None, t_p)
    mkv_all = jnp.stack(mkv_p)
    mem_k_p = mkv_all[..., :mem_w].reshape(depth, bp, mlen, MEM_HEADS, HEAD_DIM)
    mem_v_p = mkv_all[..., mem_w:].reshape(depth, bp, mlen, MEM_HEADS, HEAD_DIM)
    keep = min(max(w for w, _ in DIL_GROUPS), t_p)
    win_k_p = kv_p[:, t_p - keep:, :kv_w].reshape(bp, keep, DIL_HEADS, HEAD_DIM)
    win_v_p = kv_p[:, t_p - keep:, kv_w:].reshape(bp, keep, DIL_HEADS, HEAD_DIM)

    bs, t_s, _ = x_sample.shape
    assert t_s == 1
    xs = jnp.pad(x_sample, ((0, 0), (0, SAMPLE_ROWS - t_s), (0, 0)))
    pos_s = jnp.full((bs * SAMPLE_ROWS,), PAST_LEN, jnp.int32)
    y_s, conv_s, ssm_s, kv_s = run(xs, pos_s, state_conv, state_ssm,
                                   lambda l: (cache_mem_k, cache_mem_v, l), cache_win_k, cache_win_v, t_s)
    y_s = y_s[:, :t_s]
    k_s = kv_s[:, :t_s, :kv_w].reshape(bs, t_s, DIL_HEADS, HEAD_DIM)
    v_s = kv_s[:, :t_s, kv_w:].reshape(bs, t_s, DIL_HEADS, HEAD_DIM)

    return (y_p, y_s, conv_p, ssm_p, win_k_p, win_v_p, mem_k_p, mem_v_p, conv_s, ssm_s, k_s, v_s)
```

```python
import functools

import jax
import jax.numpy as jnp
from jax import lax
from jax.experimental import pallas as pl
from jax.experimental.pallas import tpu as pltpu

F32 = jnp.float32
BF16 = jnp.bfloat16

HEAD_DIM = 128
SSM_HEAD_DIM = 64
SSM_GROUPS = 4
SSM_D_STATE = 128
CONV_W = 4
SSD_CHUNK = 128
MEM_HEADS = 4
DIL_GROUPS = ((128, 1), (512, 4), (2048, 16))
DIL_HEADS = 4
PAST_LEN = 16384
ROPE_THETA = 10000.0
EPS = 1e-6
LOG2_E = 1.4426950408889634
LN_2 = 0.6931471805599453

LANES = 128
SUBLANES = 8
VMEM_LIMIT = 48 * 1024 * 1024
COL_TILE = 512
ROW_TILE = 512
PROJ_OUT_BLOCK_BYTES = 5 * 1024 * 1024
SAMPLE_ROWS = SUBLANES
DIL_UNROLL = 16

NT_DIMS = (((1,), (1,)), ((), ()))
RESIDENT = pl.Buffered(1)


def _cparams(*sem):
    return pltpu.CompilerParams(dimension_semantics=sem, vmem_limit_bytes=VMEM_LIMIT)


def _sigmoid(x):
    return 1.0 / (1.0 + jnp.exp(-x))


def _rms_scale(x):
    return lax.rsqrt(jnp.mean(x * x, axis=-1, keepdims=True) + EPS)


def _rope_table_kernel(pos_ref, inv_ref, cos_ref, sin_ref):
    ang = pos_ref[...] * inv_ref[...]
    lane = lax.broadcasted_iota(jnp.int32, ang.shape, 1)
    cos_ref[...] = jnp.cos(ang)
    s = jnp.sin(ang)
    sin_ref[...] = jnp.where(lane < HEAD_DIM // 2, -s, s)


def rope_tables(pos):
    r = pos.shape[0]
    half = HEAD_DIM // 2
    inv = ROPE_THETA ** (-jnp.arange(half, dtype=F32) * (2.0 / HEAD_DIM))
    inv = jnp.concatenate([inv, inv])[None, :]
    posf = jnp.broadcast_to(pos.astype(F32)[:, None], (r, HEAD_DIM))
    tr = min(r, ROW_TILE)
    return pl.pallas_call(
        _rope_table_kernel,
        grid=(r // tr,),
        in_specs=[pl.BlockSpec((tr, HEAD_DIM), lambda i: (i, 0)),
                  pl.BlockSpec((1, HEAD_DIM), lambda i: (0, 0))],
        out_specs=[pl.BlockSpec((tr, HEAD_DIM), lambda i: (i, 0))] * 2,
        out_shape=[jax.ShapeDtypeStruct((r, HEAD_DIM), F32)] * 2,
        compiler_params=_cparams("parallel"),
        name="rope_tables",
    )(posf, inv)


def _proj_kernel(*refs, kinds):
    if "rope" in kinds:
        x_ref, g_ref, w_ref, hg_ref, cos_ref, sin_ref, o_ref, xn_ref = refs
    else:
        x_ref, g_ref, w_ref, hg_ref, o_ref, xn_ref = refs
    x = x_ref[...]
    xn_ref[...] = (x * _rms_scale(x) * g_ref[...]).astype(BF16)
    for t, kind in enumerate(kinds):
        cols = slice(t * COL_TILE, (t + 1) * COL_TILE)
        y = jnp.dot(xn_ref[...], w_ref[:, cols], preferred_element_type=F32)
        if kind == "plain":
            o_ref[:, cols] = y
            continue
        for j in range(COL_TILE // HEAD_DIM):
            sl = slice(j * HEAD_DIM, (j + 1) * HEAD_DIM)
            osl = slice(t * COL_TILE + j * HEAD_DIM, t * COL_TILE + (j + 1) * HEAD_DIM)
            yh = y[:, sl]
            yh = yh * _rms_scale(yh) * hg_ref[:, osl]
            if kind == "rope":
                yh = yh * cos_ref[...] + pltpu.roll(yh, HEAD_DIM // 2, axis=1) * sin_ref[...]
            o_ref[:, osl] = yh


def norm_proj(x, g, w, layer, head_gain, kinds, cos=None, sin=None):
    m, d = x.shape
    n_cols = w.shape[2]
    assert n_cols == COL_TILE * len(kinds)
    tm = min(m, ROW_TILE)
    while tm * n_cols * 4 > PROJ_OUT_BLOCK_BYTES and tm % (2 * SUBLANES) == 0:
        tm //= 2
    assert m % tm == 0
    in_specs = [pl.BlockSpec((tm, d), lambda i: (i, 0)),
                pl.BlockSpec((1, d), lambda i: (0, 0)),
                pl.BlockSpec((None, d, n_cols), lambda i: (layer, 0, 0), pipeline_mode=RESIDENT),
                pl.BlockSpec((1, n_cols), lambda i: (0, 0))]
    args = [x, g.reshape(1, d), w, head_gain]
    if "rope" in kinds:
        assert cos.shape[0] % tm == 0
        tab_blocks = cos.shape[0] // tm
        in_specs += [pl.BlockSpec((tm, HEAD_DIM), lambda i: (i % tab_blocks, 0))] * 2
        args += [cos, sin]
    return pl.pallas_call(
        functools.partial(_proj_kernel, kinds=tuple(kinds)),
        grid=(m // tm,),
        in_specs=in_specs,
        out_specs=pl.BlockSpec((tm, n_cols), lambda i: (i, 0)),
        out_shape=jax.ShapeDtypeStruct((m, n_cols), F32),
        scratch_shapes=[pltpu.VMEM((tm, d), BF16)],
        compiler_params=_cparams("parallel"),
        name="norm_proj",
    )(*args)


def _out_proj_kernel(x_ref, a_ref, b_ref, wa_ref, wb_ref, o_ref):
    acc = jnp.dot(a_ref[...], wa_ref[...], preferred_element_type=F32)
    acc = acc + jnp.dot(b_ref[...], wb_ref[...], preferred_element_type=F32)
    o_ref[...] = x_ref[...] + acc


def out_proj(x, a, b, w, layer):
    m, d = x.shape
    tm = min(m, ROW_TILE)
    ka, kb = a.shape[1], b.shape[1]
    assert w.shape[1] == ka + kb and ka % kb == 0
    return pl.pallas_call(
        _out_proj_kernel,
        grid=(m // tm,),
        in_specs=[pl.BlockSpec((tm, d), lambda i: (i, 0)),
                  pl.BlockSpec((tm, ka), lambda i: (i, 0)),
                  pl.BlockSpec((tm, kb), lambda i: (i, 0)),
                  pl.BlockSpec((None, ka, d), lambda i: (layer, 0, 0), pipeline_mode=RESIDENT),
                  pl.BlockSpec((None, kb, d), lambda i: (layer, ka // kb, 0), pipeline_mode=RESIDENT)],
        out_specs=pl.BlockSpec((tm, d), lambda i: (i, 0)),
        out_shape=jax.ShapeDtypeStruct((m, d), F32),
        compiler_params=_cparams("parallel"),
        name="out_proj",
    )(x, a, b, w, w)


def _mlp_kernel(x_ref, g_ref, wu_ref, wd_ref, o_ref, xn_ref):
    f = pl.program_id(1)

    @pl.when(f == 0)
    def _():
        x = x_ref[...]
        xn_ref[...] = (x * _rms_scale(x) * g_ref[...]).astype(BF16)
        o_ref[...] = x

    h = jnp.dot(xn_ref[...], wu_ref[...], preferred_element_type=F32)
    h = jnp.maximum(h, 0.0)
    h = (h * h).astype(BF16)
    o_ref[...] += jnp.dot(h, wd_ref[...], preferred_element_type=F32)


def mlp(x, g, wu, wd, layer, tf=1024):
    m, d = x.shape
    ff = wu.shape[2]
    tm = min(m, ROW_TILE)
    return pl.pallas_call(
        _mlp_kernel,
        grid=(m // tm, ff // tf),
        in_specs=[pl.BlockSpec((tm, d), lambda i, j: (i, 0)),
                  pl.BlockSpec((1, d), lambda i, j: (0, 0)),
                  pl.BlockSpec((None, d, tf), lambda i, j: (layer, 0, j)),
                  pl.BlockSpec((None, tf, d), lambda i, j: (layer, j, 0))],
        out_specs=pl.BlockSpec((tm, d), lambda i, j: (i, 0)),
        out_shape=jax.ShapeDtypeStruct((m, d), F32),
        scratch_shapes=[pltpu.VMEM((tm, d), BF16)],
        compiler_params=_cparams("parallel", "arbitrary"),
        name="mlp",
    )(x, g.reshape(1, d), wu, wd)


def _split3(x):
    hi = x.astype(BF16)
    r1 = x - hi.astype(F32)
    mid = r1.astype(BF16)
    lo = (r1 - mid.astype(F32)).astype(BF16)
    return hi, mid, lo


def _ssd_kernel(xbc_ref, dt_ref, z_ref, cw_ref, cb_ref, dtb_ref, alog_ref, dsk_ref, gg_ref, tri_ref, expand_ref,
                conv0_ref, h0_ref, y_ref, convo_ref, ho_ref,
                ext_ref, ht_ref, yd_ref, st_ref, *, t_real, n_heads, d_inner):
    L = SSD_CHUNK
    P = SSM_HEAD_DIM
    N = SSM_D_STATE
    G = SSM_GROUPS
    hpg = n_heads // G
    gw = hpg * P
    hist = CONV_W - 1
    base = SUBLANES - hist
    c = pl.program_id(1)
    nc = pl.num_programs(1)

    @pl.when(c == 0)
    def _():
        ext_ref[base:SUBLANES, :] = conv0_ref[...]
        for g in range(G):
            ht_ref[g] = h0_ref[g].T

    ext_ref[SUBLANES:SUBLANES + L, :] = xbc_ref[...]
    ext = ext_ref[...]
    acc = cb_ref[...] + ext[SUBLANES:, :] * cw_ref[hist:CONV_W, :]
    for k in range(1, CONV_W):
        acc = acc + pltpu.roll(ext, k, axis=0)[SUBLANES:, :] * cw_ref[hist - k:CONV_W - k, :]
    xc = acc * _sigmoid(acc)

    @pl.when(c == nc - 1)
    def _():
        n_in_last = t_real - ((t_real - 1) // L) * L
        convo_ref[...] = ext_ref[base + n_in_last:base + n_in_last + hist, :]

    ext_ref[base:SUBLANES, :] = ext_ref[base + L:SUBLANES + L, :]

    dt_raw = dt_ref[...] + dtb_ref[...]
    dt = jnp.maximum(dt_raw, 0.0) + jnp.log1p(jnp.exp(-jnp.abs(dt_raw)))
    row = lax.broadcasted_iota(jnp.int32, (L, LANES), 0) + c * L
    dt = jnp.where(row < t_real, dt, 0.0)
    da = dt * (-jnp.exp(alog_ref[...]))
    li = lax.broadcasted_iota(jnp.int32, (L, L), 0)
    si = lax.broadcasted_iota(jnp.int32, (L, L), 1)
    causal = li >= si
    a_cs = sum(jnp.dot(tri_ref[...], part, preferred_element_type=F32) for part in _split3(da))
    a_cs_t = a_cs.T
    dt_t = dt.T
    w_t = dt_t * jnp.exp(a_cs_t[:, L - 1:L] - a_cs_t)
    e_exp = sum(jnp.dot(part, expand_ref[...], preferred_element_type=F32) for part in _split3(jnp.exp(a_cs)))
    xb = xc[:, :d_inner].astype(BF16)
    low_half = lax.broadcasted_iota(jnp.int32, (L, 2 * P), 1) < P

    for g in range(G):
        b_g = xc[:, d_inner + g * N:d_inner + (g + 1) * N]
        c_g = xc[:, d_inner + G * N + g * N:d_inner + G * N + (g + 1) * N].astype(BF16)
        cbm = lax.dot_general(c_g, b_g.astype(BF16), NT_DIMS, preferred_element_type=F32)
        b_t = b_g.T
        y_off = jnp.dot(c_g, ht_ref[g].astype(BF16), preferred_element_type=F32)
        for kp in range(hpg // 2):
            h0 = g * hpg + 2 * kp
            x_pair = xb[:, h0 * P:(h0 + 2) * P]
            yd, st = [], []
            for h in (h0, h0 + 1):
                seg = a_cs[:, h:h + 1] - a_cs_t[h:h + 1, :]
                dec = jnp.exp(jnp.where(causal, seg, -jnp.inf))
                mh = (cbm * dec * dt_t[h:h + 1, :]).astype(BF16)
                yd.append(jnp.dot(mh, x_pair, preferred_element_type=F32))
                btw = (b_t * w_t[h:h + 1, :]).astype(BF16)
                st.append(jnp.dot(btw, x_pair, preferred_element_type=F32))
            yd_ref[:, 2 * kp * P:(2 * kp + 2) * P] = jnp.where(low_half, yd[0], yd[1])
            st_ref[:, 2 * kp * P:(2 * kp + 2) * P] = jnp.where(low_half, st[0], st[1])
        gs = slice(g * gw, (g + 1) * gw)
        e_g = e_exp[:, gs]
        y_g = yd_ref[...] + y_off * e_g + dsk_ref[:, gs] * xc[:, gs]
        ht_ref[g] = ht_ref[g] * e_g[L - 1:L, :] + st_ref[...]
        z_g = z_ref[:, gs]
        gated = y_g * (z_g * _sigmoid(z_g))
        y_ref[:, gs] = (gated * _rms_scale(gated) * gg_ref[:, gs]).astype(y_ref.dtype)

    @pl.when(c == nc - 1)
    def _():
        for g in range(G):
            ho_ref[g] = ht_ref[g].T


def ssd_mixer(proj, col_xbc, col_dt, col_z, t_real, conv_w, conv_b, dt_bias, a_log, d_skip, gate_g,
              conv_prev, h_prev):
    bt, tp, _ = proj.shape
    conv_dim = conv_w.shape[1]
    n_heads = dt_bias.shape[0]
    d_inner = n_heads * SSM_HEAD_DIM
    gw = d_inner // SSM_GROUPS
    L = SSD_CHUNK
    assert tp % L == 0 and (tp - t_real) < L
    assert col_xbc % conv_dim == 0 and col_dt % LANES == 0 and col_z % d_inner == 0
    pad = LANES - n_heads
    dtb = jnp.pad(dt_bias, (0, pad)).reshape(1, LANES)
    alog = jnp.pad(a_log, (0, pad)).reshape(1, LANES)
    dsk = jnp.repeat(d_skip, SSM_HEAD_DIM).reshape(1, d_inner)
    tri = jnp.tril(jnp.ones((L, L), BF16))
    expand = jnp.repeat(jnp.eye(LANES, dtype=BF16)[:, :n_heads], SSM_HEAD_DIM, axis=1)
    kern = functools.partial(_ssd_kernel, t_real=t_real, n_heads=n_heads, d_inner=d_inner)
    vec = lambda n: pl.BlockSpec((1, n), lambda b, c: (0, 0))
    return pl.pallas_call(
        kern,
        grid=(bt, tp // L),
        in_specs=[pl.BlockSpec((None, L, conv_dim), lambda b, c: (b, c, col_xbc // conv_dim)),
                  pl.BlockSpec((None, L, LANES), lambda b, c: (b, c, col_dt // LANES)),
                  pl.BlockSpec((None, L, d_inner), lambda b, c: (b, c, col_z // d_inner)),
                  pl.BlockSpec((CONV_W, conv_dim), lambda b, c: (0, 0)),
                  vec(conv_dim), vec(LANES), vec(LANES), vec(d_inner), vec(d_inner),
                  pl.BlockSpec((L, L), lambda b, c: (0, 0)),
                  pl.BlockSpec((LANES, d_inner), lambda b, c: (0, 0)),
                  pl.BlockSpec((None, CONV_W - 1, conv_dim), lambda b, c: (b, 0, 0)),
                  pl.BlockSpec((None, SSM_GROUPS, gw, SSM_D_STATE), lambda b, c: (b, 0, 0, 0))],
        out_specs=[pl.BlockSpec((None, L, d_inner), lambda b, c: (b, c, 0)),
                   pl.BlockSpec((None, CONV_W - 1, conv_dim), lambda b, c: (b, 0, 0)),
                   pl.BlockSpec((None, SSM_GROUPS, gw, SSM_D_STATE), lambda b, c: (b, 0, 0, 0))],
        out_shape=[jax.ShapeDtypeStruct((bt, tp, d_inner), BF16),
                   jax.ShapeDtypeStruct((bt, CONV_W - 1, conv_dim), F32),
                   jax.ShapeDtypeStruct((bt, SSM_GROUPS, gw, SSM_D_STATE), F32)],
        scratch_shapes=[pltpu.VMEM((SUBLANES + L, conv_dim), F32),
                        pltpu.VMEM((SSM_GROUPS, SSM_D_STATE, gw), F32),
                        pltpu.VMEM((L, gw), F32),
                        pltpu.VMEM((SSM_D_STATE, gw), F32)],
        compiler_params=_cparams("parallel", "arbitrary"),
        name="ssd_mixer",
    )(proj, proj, proj, conv_w, conv_b.reshape(1, conv_dim), dtb, alog, dsk, gate_g.reshape(1, d_inner),
      tri, expand, conv_prev, h_prev)


def _mem_attn_kernel(q_ref, k_ref, v_ref, o_ref):
    scale = HEAD_DIM ** -0.5
    headed = len(k_ref.shape) == 3
    for h in range(MEM_HEADS):
        sl = slice(h * HEAD_DIM, (h + 1) * HEAD_DIM)
        q = q_ref[:, sl].astype(BF16)
        k = (k_ref[:, h, :] if headed else k_ref[:, sl]).astype(BF16)
        v = (v_ref[:, h, :] if headed else v_ref[:, sl]).astype(BF16)
        s = lax.dot_general(q, k, NT_DIMS, preferred_element_type=F32) * scale
        e = jnp.exp(s - jnp.max(s, axis=-1, keepdims=True))
        den = jnp.sum(e, axis=-1, keepdims=True)
        o = jnp.dot(e.astype(BF16), v, preferred_element_type=F32) / den
        o_ref[:, sl] = o.astype(o_ref.dtype)


def mem_attention(proj, col_q, k, v, kv_at):
    bt, t, _ = proj.shape
    width = MEM_HEADS * HEAD_DIM
    tq = min(t, ROW_TILE)
    if k.ndim == 5:
        mlen = k.shape[2]
        kv_specs = [pl.BlockSpec((None, None, mlen, MEM_HEADS, HEAD_DIM), lambda b, i: (kv_at, b, 0, 0, 0))] * 2
    else:
        mlen = k.shape[1]
        kv_specs = [pl.BlockSpec((None, mlen, width), lambda b, i, c=c: (b, 0, c // width)) for c in kv_at]
    return pl.pallas_call(
        _mem_attn_kernel,
        grid=(bt, t // tq),
        in_specs=[pl.BlockSpec((None, tq, width), lambda b, i: (b, i, col_q // width))] + kv_specs,
        out_specs=pl.BlockSpec((None, tq, width), lambda b, i: (b, i, 0)),
        out_shape=jax.ShapeDtypeStruct((bt, t, width), BF16),
        compiler_params=_cparams("parallel", "parallel"),
        name="mem_attention",
    )(proj, k, v)


def _run_units(first, count, unit, unroll):
    trips = count // unroll
    if trips == 1:
        trips = 0
    if trips:
        def trip(it, carry):
            for k in range(unroll):
                unit(first + it * unroll + k)
            return carry
        lax.fori_loop(0, trips, trip, 0)
    for k in range(trips * unroll, count):
        unit(first + k)


def _dil_attn_kernel(*refs, groups, rb, unroll):
    ng = len(groups)
    q_refs = refs[:ng]
    kp_ref, kc_ref, vp_ref, vc_ref, o_ref = refs[ng:ng + 5]
    og = refs[ng + 5:2 * ng + 5]
    lg = refs[2 * ng + 5:3 * ng + 5]
    i = pl.program_id(1)
    scale = HEAD_DIM ** -0.5

    for gi, (win, d) in enumerate(groups):
        w = win // d
        dw = d * w
        qi = lax.broadcasted_iota(jnp.int32, (w, 2 * w), 0)
        kj = lax.broadcasted_iota(jnp.int32, (w, 2 * w), 1)
        band = (kj >= qi) & (kj <= qi + w)
        band_first = band & (kj >= jnp.where(i > 0, 0, w))

        def rows(start, size, d=d, w=w):
            if d == 1:
                return pl.ds(pl.multiple_of(start, w), size)
            return pl.ds(start, size, stride=d)

        def attend(rows_q, kk, vv, mask, gi=gi, w=w):
            q = (q_refs[gi][rows_q, :] * (scale * LOG2_E)).astype(BF16)
            s = lax.dot_general(q, kk.astype(BF16), NT_DIMS, preferred_element_type=F32)
            s = jnp.where(mask, s, -jnp.inf)
            m = jnp.max(s, axis=-1, keepdims=True)
            e = jnp.exp2(s - m)
            den = jnp.sum(e, axis=-1, keepdims=True)
            o = jnp.dot(e.astype(BF16), vv.astype(BF16), preferred_element_type=F32) / den
            og[gi][rows_q, :] = o
            lg[gi][rows_q, :] = jnp.broadcast_to(m * LN_2 + jnp.log(den), (w, HEAD_DIM))

        def first_unit(r, rows=rows, attend=attend, w=w, dw=dw, mask=band_first):
            rows_q, rows_p = rows(r, w), rows(rb - dw + r, w)
            kk = jnp.concatenate([kp_ref[rows_p, :], kc_ref[rows_q, :]], axis=0)
            vv = jnp.concatenate([vp_ref[rows_p, :], vc_ref[rows_q, :]], axis=0)
            attend(rows_q, kk, vv, mask)

        def later_unit(u, rows=rows, attend=attend, d=d, w=w, dw=dw, mask=band):
            sb = u // d
            start_q = sb * dw + (u - sb * d)
            rows_k = rows(start_q - dw, 2 * w)
            attend(rows(start_q, w), kc_ref[rows_k, :], vc_ref[rows_k, :], mask)

        _run_units(0, d, first_unit, unroll)
        _run_units(d, rb // w - d, later_unit, unroll)

    ls = [r[...] for r in lg]
    mm = functools.reduce(jnp.maximum, ls)
    ws = [jnp.exp(l - mm) for l in ls]
    num = sum(wg * r[...] for wg, r in zip(ws, og))
    o_ref[...] = (num / sum(ws)).astype(o_ref.dtype)


def dilated_attention(proj, kv):
    bt, t, _ = proj.shape
    ng = len(DIL_GROUPS)
    rb = min(t, max(win for win, _ in DIL_GROUPS))
    for win, d in DIL_GROUPS:
        assert win % d == 0 and rb % win == 0
    assert t % rb == 0
    blk = lambda f: pl.BlockSpec((None, rb, HEAD_DIM), f)
    prev = lambda i: jnp.maximum(i - 1, 0)
    q_specs = [blk(lambda b, i, h, g=g: (b, i, g * DIL_HEADS + h)) for g in range(ng)]
    kv_specs = [blk(lambda b, i, h: (b, prev(i), h)), blk(lambda b, i, h: (b, i, h)),
                blk(lambda b, i, h: (b, prev(i), DIL_HEADS + h)), blk(lambda b, i, h: (b, i, DIL_HEADS + h))]
    return pl.pallas_call(
        functools.partial(_dil_attn_kernel, groups=DIL_GROUPS, rb=rb, unroll=DIL_UNROLL),
        grid=(bt, t // rb, DIL_HEADS),
        in_specs=q_specs + kv_specs,
        out_specs=blk(lambda b, i, h: (b, i, h)),
        out_shape=jax.ShapeDtypeStruct((bt, t, DIL_HEADS * HEAD_DIM), BF16),
        scratch_shapes=[pltpu.VMEM((rb, HEAD_DIM), F32)] * (2 * ng),
        compiler_params=_cparams("parallel", "parallel", "parallel"),
        name="dilated_attention",
    )(*([proj] * ng), kv, kv, kv, kv)


def _dil_decode_kernel(*refs):
    ng = len(DIL_GROUPS)
    q_ref, kv_ref = refs[:2]
    k_refs, v_refs = refs[2:2 + ng], refs[2 + ng:2 + 2 * ng]
    o_ref = refs[-1]
    width = DIL_HEADS * HEAD_DIM
    scale = HEAD_DIM ** -0.5
    for h in range(DIL_HEADS):
        sl = slice(h * HEAD_DIM, (h + 1) * HEAD_DIM)
        k_new = kv_ref[:, h * HEAD_DIM:(h + 1) * HEAD_DIM]
        v_new = kv_ref[:, width + h * HEAD_DIM:width + (h + 1) * HEAD_DIM]
        outs, lses = [], []
        for gi in range(ng):
            kc = k_refs[gi][:, h, :].astype(BF16)
            vc = v_refs[gi][:, h, :].astype(BF16)
            q = q_ref[:, gi * width + h * HEAD_DIM:gi * width + (h + 1) * HEAD_DIM]
            s = lax.dot_general(q.astype(BF16), kc, NT_DIMS, preferred_element_type=F32) * scale
            s_self = jnp.sum(q * k_new, axis=-1, keepdims=True) * scale
            m = jnp.maximum(jnp.max(s, axis=-1, keepdims=True), s_self)
            e = jnp.exp(s - m)
            e_self = jnp.exp(s_self - m)
            den = jnp.sum(e, axis=-1, keepdims=True) + e_self
            o = jnp.dot(e.astype(BF16), vc, preferred_element_type=F32) + e_self * v_new
            outs.append(o / den)
            lses.append(m + jnp.log(den))
        mm = functools.reduce(jnp.maximum, lses)
        ws = [jnp.exp(l - mm) for l in lses]
        num = sum(wg * o for wg, o in zip(ws, outs))
        o_ref[:, sl] = (num / sum(ws)).astype(o_ref.dtype)


def dilated_decode(proj, kv_new, k_cache, v_cache):
    bt, rows, nq = proj.shape
    lc = k_cache.shape[1]
    width = DIL_HEADS * HEAD_DIM
    views, specs = [], []
    for cache in (k_cache, v_cache):
        for win, dil in DIL_GROUPS:
            assert lc % win == 0 and win % dil == 0
            w = win // dil
            views.append(cache.reshape(bt, lc // dil, dil, DIL_HEADS, HEAD_DIM))
            specs.append(pl.BlockSpec((None, w, None, DIL_HEADS, HEAD_DIM),
                                      lambda b, last=lc // win - 1: (b, last, 0, 0, 0)))
    return pl.pallas_call(
        _dil_decode_kernel,
        grid=(bt,),
        in_specs=[pl.BlockSpec((None, rows, nq), lambda b: (b, 0, 0)),
                  pl.BlockSpec((None, rows, 2 * width), lambda b: (b, 0, 0))] + specs,
        out_specs=pl.BlockSpec((None, rows, width), lambda b: (b, 0, 0)),
        out_shape=jax.ShapeDtypeStruct((bt, rows, width), BF16),
        compiler_params=_cparams("parallel"),
        name="dilated_decode",
    )(proj, kv_new, *views)


def kernel(x_prompt, x_sample, state_conv, state_ssm, cache_win_k, cache_win_v, cache_mem_k, cache_mem_v,
           mem_prompt, norm_mix_g, norm_mlp_g, norm_mem_g, w_mem_k, w_mem_v, mem_q_norm_g, mem_k_norm_g,
           w_up, w_down, w_in_a, conv_w, conv_b, dt_bias, a_log, d_skip, gate_norm_g, w_out_a,
           w_in_b, q_norm_g, w_out_b, kv_norm_g, w_k_shared, w_v_shared, k_norm_g):
    depth = w_up.shape[0]
    n_a = w_in_a.shape[0]
    d_model = x_prompt.shape[-1]
    n_heads = dt_bias.shape[1]
    d_inner = n_heads * SSM_HEAD_DIM
    conv_dim = conv_w.shape[2]
    mem_w = MEM_HEADS * HEAD_DIM
    kv_w = DIL_HEADS * HEAD_DIM
    dil_q_w = len(DIL_GROUPS) * kv_w

    s1, s2, s3 = d_inner, d_inner + conv_dim, d_inner + conv_dim + n_heads
    dt_pad = jnp.zeros((n_a, d_model, COL_TILE - n_heads), BF16)
    w_in_a_b = w_in_a.astype(BF16)
    w_a = jnp.concatenate([w_in_a_b[..., s1:s2], w_in_a_b[..., s2:s3], dt_pad, w_in_a_b[..., :s1],
                           w_in_a_b[..., s3:]], axis=-1)
    col_xbc, col_dt, col_z = 0, conv_dim, conv_dim + COL_TILE
    col_qm_a = col_z + d_inner
    kinds_a = ["plain"] * (col_qm_a // COL_TILE) + ["norm"] * (mem_w // COL_TILE)
    kinds_b = ["rope"] * (dil_q_w // COL_TILE) + ["norm"] * (mem_w // COL_TILE)
    kinds_kv = ["rope"] * (kv_w // COL_TILE) + ["plain"] * (kv_w // COL_TILE)
    kinds_mkv = ["norm"] * (mem_w // COL_TILE) + ["plain"] * (mem_w // COL_TILE)
    w_b = w_in_b.astype(BF16)
    w_kv = jnp.concatenate([w_k_shared, w_v_shared], axis=-1).astype(BF16)[None]
    w_mkv = jnp.concatenate([w_mem_k, w_mem_v], axis=-1).astype(BF16)
    w_up_b, w_down_b = w_up.astype(BF16), w_down.astype(BF16)
    w_out_a_b, w_out_b_b = w_out_a.astype(BF16), w_out_b.astype(BF16)

    def gain_row(width, pieces):
        parts, at = [], 0
        for start, g, reps in pieces:
            parts += [jnp.ones((start - at,), F32), jnp.tile(g, reps)]
            at = start + reps * HEAD_DIM
        parts.append(jnp.ones((width - at,), F32))
        return jnp.concatenate(parts).reshape(1, width)

    def run(x, pos_rows, conv_prev, ssm_prev, mem_kv, k_past, v_past, t_real):
        bt, t, _ = x.shape
        m = bt * t
        x2 = x.reshape(m, d_model)
        cos, sin = rope_tables(pos_rows)
        conv_new, ssm_new = [], []
        for l in range(n_a):
            hg = gain_row(w_a.shape[-1], [(col_qm_a, mem_q_norm_g[l], MEM_HEADS)])
            proj3 = norm_proj(x2, norm_mix_g[l], w_a, l, hg, kinds_a).reshape(bt, t, -1)
            tp = -(-t // SSD_CHUNK) * SSD_CHUNK
            proj_ssd = proj3 if tp == t else jnp.pad(proj3, ((0, 0), (0, tp - t), (0, 0)))
            y, c_new, h_new = ssd_mixer(proj_ssd, col_xbc, col_dt, col_z, t_real, conv_w[l], conv_b[l],
                                        dt_bias[l], a_log[l], d_skip[l], gate_norm_g[l],
                                        conv_prev[l], ssm_prev[l].reshape(bt, SSM_GROUPS, -1, SSM_D_STATE))
            y = y[:, :t].reshape(m, d_inner)
            mo = mem_attention(proj3, col_qm_a, *mem_kv(l)).reshape(m, mem_w)
            x2 = out_proj(x2, y, mo, w_out_a_b, l)
            x2 = mlp(x2, norm_mlp_g[l], w_up_b, w_down_b, l)
            conv_new.append(c_new)
            ssm_new.append(h_new.reshape(bt, n_heads, SSM_HEAD_DIM, SSM_D_STATE))
        hg = gain_row(2 * kv_w, [(0, k_norm_g, DIL_HEADS)])
        kv3 = norm_proj(x2, kv_norm_g, w_kv, 0, hg, kinds_kv, cos, sin).reshape(bt, t, 2 * kv_w)
        for l in range(n_a, depth):
            j = l - n_a
            hg = gain_row(dil_q_w + mem_w, [(0, q_norm_g[j], dil_q_w // HEAD_DIM),
                                            (dil_q_w, mem_q_norm_g[l], MEM_HEADS)])
            proj3 = norm_proj(x2, norm_mix_g[l], w_b, j, hg, kinds_b, cos, sin).reshape(bt, t, -1)
            if k_past is None:
                att = dilated_attention(proj3, kv3)
            else:
                att = dilated_decode(proj3, kv3, k_past, v_past)
            mo = mem_attention(proj3, dil_q_w, *mem_kv(l)).reshape(m, mem_w)
            x2 = out_proj(x2, att.reshape(m, kv_w), mo, w_out_b_b, j)
            x2 = mlp(x2, norm_mlp_g[l], w_up_b, w_down_b, l)
        return x2.reshape(bt, t, d_model), jnp.stack(conv_new), jnp.stack(ssm_new), kv3

    bp, t_p, _ = x_prompt.shape
    mlen = mem_prompt.shape[1]
    mem2 = mem_prompt.reshape(bp * mlen, d_model)
    mkv_p = []
    for l in range(depth):
        hg = gain_row(2 * mem_w, [(0, mem_k_norm_g[l], MEM_HEADS)])
        mkv_p.append(norm_proj(mem2, norm_mem_g[l], w_mkv, l, hg, kinds_mkv).reshape(bp, mlen, 2 * mem_w))
    conv0 = jnp.zeros((n_a, bp, CONV_W - 1, conv_dim), F32)
    ssm0 = jnp.zeros((n_a, bp, n_heads, SSM_HEAD_DIM, SSM_D_STATE), F32)
    y_p, conv_p, ssm_p, kv_p = run(x_prompt, jnp.arange(t_p, dtype=jnp.int32), conv0, ssm0,
                                   lambda l: (mkv_p[l], mkv_p[l], (0, mem_w)), None, None, t_p)
    mkv_all = jnp.stack(mkv_p)
    mem_k_p = mkv_all[..., :mem_w].reshape(depth, bp, mlen, MEM_HEADS, HEAD_DIM)
    mem_v_p = mkv_all[..., mem_w:].reshape(depth, bp, mlen, MEM_HEADS, HEAD_DIM)
    keep = min(max(w for w, _ in DIL_GROUPS), t_p)
    win_k_p = kv_p[:, t_p - keep:, :kv_w].reshape(bp, keep, DIL_HEADS, HEAD_DIM)
    win_v_p = kv_p[:, t_p - keep:, kv_w:].reshape(bp, keep, DIL_HEADS, HEAD_DIM)

    bs, t_s, _ = x_sample.shape
    assert t_s == 1
    xs = jnp.pad(x_sample, ((0, 0), (0, SAMPLE_ROWS - t_s), (0, 0)))
    pos_s = jnp.full((bs * SAMPLE_ROWS,), PAST_LEN, jnp.int32)
    y_s, conv_s, ssm_s, kv_s = run(xs, pos_s, state_conv, state_ssm,
                                   lambda l: (cache_mem_k, cache_mem_v, l), cache_win_k, cache_win_v, t_s)
    y_s = y_s[:, :t_s]
    k_s = kv_s[:, :t_s, :kv_w].reshape(bs, t_s, DIL_HEADS, HEAD_DIM)
    v_s = kv_s[:, :t_s, kv_w:].reshape(bs, t_s, DIL_HEADS, HEAD_DIM)

    return (y_p, y_s, conv_p, ssm_p, win_k_p, win_v_p, mem_k_p, mem_v_p, conv_s, ssm_s, k_s, v_s)
```

```python
import functools

import jax
import jax.numpy as jnp
from jax import lax
from jax.experimental import pallas as pl
from jax.experimental.pallas import tpu as pltpu

F32 = jnp.float32
BF16 = jnp.bfloat16

HEAD_DIM = 128
SSM_HEAD_DIM = 64
SSM_GROUPS = 4
SSM_D_STATE = 128
CONV_W = 4
SSD_CHUNK = 128
MEM_HEADS = 4
DIL_GROUPS = ((128, 1), (512, 4), (2048, 16))
DIL_HEADS = 4
PAST_LEN = 16384
ROPE_THETA = 10000.0
EPS = 1e-6
LOG2_E = 1.4426950408889634
LN_2 = 0.6931471805599453

LANES = 128
SUBLANES = 8
VMEM_LIMIT = 48 * 1024 * 1024
COL_TILE = 512
ROW_TILE = 512
PROJ_OUT_BLOCK_BYTES = 5 * 1024 * 1024
SAMPLE_ROWS = SUBLANES
DIL_UNROLL = 16

NT_DIMS = (((1,), (1,)), ((), ()))
RESIDENT = pl.Buffered(1)


def _cparams(*sem):
    return pltpu.CompilerParams(dimension_semantics=sem, vmem_limit_bytes=VMEM_LIMIT)


def _sigmoid(x):
    return 1.0 / (1.0 + jnp.exp(-x))


def _rms_scale(x):
    return lax.rsqrt(jnp.mean(x * x, axis=-1, keepdims=True) + EPS)


class _SideCast:
    def __init__(self, mats, steps, step_of):
        self.n = len(mats)
        self.shapes = [a.shape for a in mats]
        for a in mats:
            assert a.shape[0] % (steps * 2 * SUBLANES) == 0
        self.views = [a.reshape(steps, a.shape[0] // steps, a.shape[1]) for a in mats]
        spec = lambda v: pl.BlockSpec((None,) + v.shape[1:], lambda *ids: (step_of(*ids), 0, 0))
        self.in_specs = [spec(v) for v in self.views]
        self.out_specs = [spec(v) for v in self.views]
        self.out_shape = [jax.ShapeDtypeStruct(v.shape, BF16) for v in self.views]

    def wrap(self, body, n_in, n_out):
        n = self.n
        if not n:
            return body

        def kern(*refs):
            ins, srcs = refs[:n_in], refs[n_in:n_in + n]
            outs, dsts = refs[n_in + n:n_in + n + n_out], refs[n_in + n + n_out:n_in + 2 * n + n_out]
            for s, d in zip(srcs, dsts):
                d[...] = s[...].astype(BF16)
            body(*ins, *outs, *refs[n_in + 2 * n + n_out:])
        return kern

    def split(self, outs, n_out):
        main = outs[0] if n_out == 1 else tuple(outs[:n_out])
        if not self.n:
            return main
        return main, [o.reshape(s) for o, s in zip(outs[n_out:], self.shapes)]


def _rope_table_kernel(pos_ref, inv_ref, cos_ref, sin_ref):
    ang = pos_ref[...] * inv_ref[...]
    lane = lax.broadcasted_iota(jnp.int32, ang.shape, 1)
    cos_ref[...] = jnp.cos(ang)
    s = jnp.sin(ang)
    sin_ref[...] = jnp.where(lane < HEAD_DIM // 2, -s, s)


def rope_tables(pos):
    r = pos.shape[0]
    half = HEAD_DIM // 2
    inv = ROPE_THETA ** (-jnp.arange(half, dtype=F32) * (2.0 / HEAD_DIM))
    inv = jnp.concatenate([inv, inv])[None, :]
    posf = jnp.broadcast_to(pos.astype(F32)[:, None], (r, HEAD_DIM))
    tr = min(r, ROW_TILE)
    return pl.pallas_call(
        _rope_table_kernel,
        grid=(r // tr,),
        in_specs=[pl.BlockSpec((tr, HEAD_DIM), lambda i: (i, 0)),
                  pl.BlockSpec((1, HEAD_DIM), lambda i: (0, 0))],
        out_specs=[pl.BlockSpec((tr, HEAD_DIM), lambda i: (i, 0))] * 2,
        out_shape=[jax.ShapeDtypeStruct((r, HEAD_DIM), F32)] * 2,
        compiler_params=_cparams("parallel"),
        name="rope_tables",
    )(posf, inv)


def _proj_kernel(*refs, kinds, tile_src, n_w):
    x_ref, g_ref = refs[:2]
    w_refs = refs[2:2 + n_w]
    hg_ref = refs[2 + n_w]
    if "rope" in kinds:
        cos_ref, sin_ref = refs[3 + n_w:5 + n_w]
    o_ref, xn_ref = refs[-2:]
    x = x_ref[...]
    xn_ref[...] = (x * _rms_scale(x) * g_ref[...]).astype(BF16)
    for t, kind in enumerate(kinds):
        which, col = tile_src[t]
        cols = slice(t * COL_TILE, (t + 1) * COL_TILE)
        y = jnp.dot(xn_ref[...], w_refs[which][:, col:col + COL_TILE], preferred_element_type=F32)
        if kind == "plain":
            o_ref[:, cols] = y
            continue
        for j in range(COL_TILE // HEAD_DIM):
            sl = slice(j * HEAD_DIM, (j + 1) * HEAD_DIM)
            osl = slice(t * COL_TILE + j * HEAD_DIM, t * COL_TILE + (j + 1) * HEAD_DIM)
            yh = y[:, sl]
            yh = yh * _rms_scale(yh) * hg_ref[:, osl]
            if kind == "rope":
                yh = yh * cos_ref[...] + pltpu.roll(yh, HEAD_DIM // 2, axis=1) * sin_ref[...]
            o_ref[:, osl] = yh


def norm_proj(x, g, ws, layer, head_gain, kinds, cos=None, sin=None, cast=(), tile_src=None):
    m, d = x.shape
    ws = list(ws) if isinstance(ws, (list, tuple)) else [ws]
    if tile_src is None:
        tile_src = [(0, t * COL_TILE) for t in range(len(kinds))]
    n_cols = COL_TILE * len(kinds)
    assert len(tile_src) == len(kinds) and head_gain.shape == (1, n_cols)
    tm = min(m, ROW_TILE)
    while tm * n_cols * 4 > PROJ_OUT_BLOCK_BYTES and tm % (2 * SUBLANES) == 0:
        tm //= 2
    assert m % tm == 0
    in_specs = [pl.BlockSpec((tm, d), lambda i: (i, 0)),
                pl.BlockSpec((1, d), lambda i: (0, 0))]
    for k, w in enumerate(ws):
        used = max(col + COL_TILE for which, col in tile_src if which == k)
        assert used <= w.shape[2] and (used % LANES == 0 or used == w.shape[2])
        in_specs.append(pl.BlockSpec((None, d, used), lambda i: (layer, 0, 0), pipeline_mode=RESIDENT))
    in_specs.append(pl.BlockSpec((1, n_cols), lambda i: (0, 0)))
    args = [x, g.reshape(1, d), *ws, head_gain]
    if "rope" in kinds:
        assert cos.shape[0] % tm == 0
        tab_blocks = cos.shape[0] // tm
        in_specs += [pl.BlockSpec((tm, HEAD_DIM), lambda i: (i % tab_blocks, 0))] * 2
        args += [cos, sin]
    side = _SideCast(cast, m // tm, lambda i: i)
    outs = pl.pallas_call(
        side.wrap(functools.partial(_proj_kernel, kinds=tuple(kinds), tile_src=tuple(tile_src), n_w=len(ws)),
                  len(args), 1),
        grid=(m // tm,),
        in_specs=in_specs + side.in_specs,
        out_specs=[pl.BlockSpec((tm, n_cols), lambda i: (i, 0))] + side.out_specs,
        out_shape=[jax.ShapeDtypeStruct((m, n_cols), F32)] + side.out_shape,
        scratch_shapes=[pltpu.VMEM((tm, d), BF16)],
        compiler_params=_cparams("parallel"),
        name="norm_proj",
    )(*args, *side.views)
    return side.split(outs, 1)


def _out_proj_kernel(x_ref, a_ref, b_ref, wa_ref, wb_ref, o_ref):
    acc = jnp.dot(a_ref[...], wa_ref[...], preferred_element_type=F32)
    acc = acc + jnp.dot(b_ref[...], wb_ref[...], preferred_element_type=F32)
    o_ref[...] = x_ref[...] + acc


def out_proj(x, a, b, w, layer):
    m, d = x.shape
    tm = min(m, ROW_TILE)
    ka, kb = a.shape[1], b.shape[1]
    assert w.shape[1] == ka + kb and ka % kb == 0
    return pl.pallas_call(
        _out_proj_kernel,
        grid=(m // tm,),
        in_specs=[pl.BlockSpec((tm, d), lambda i: (i, 0)),
                  pl.BlockSpec((tm, ka), lambda i: (i, 0)),
                  pl.BlockSpec((tm, kb), lambda i: (i, 0)),
                  pl.BlockSpec((None, ka, d), lambda i: (layer, 0, 0), pipeline_mode=RESIDENT),
                  pl.BlockSpec((None, kb, d), lambda i: (layer, ka // kb, 0), pipeline_mode=RESIDENT)],
        out_specs=pl.BlockSpec((tm, d), lambda i: (i, 0)),
        out_shape=jax.ShapeDtypeStruct((m, d), F32),
        compiler_params=_cparams("parallel"),
        name="out_proj",
    )(x, a, b, w, w)


def _mlp_kernel(x_ref, g_ref, wu_ref, wd_ref, o_ref, xn_ref):
    f = pl.program_id(1)

    @pl.when(f == 0)
    def _():
        x = x_ref[...]
        xn_ref[...] = (x * _rms_scale(x) * g_ref[...]).astype(BF16)
        o_ref[...] = x

    h = jnp.dot(xn_ref[...], wu_ref[...], preferred_element_type=F32)
    h = jnp.maximum(h, 0.0)
    h = (h * h).astype(BF16)
    o_ref[...] += jnp.dot(h, wd_ref[...], preferred_element_type=F32)


def mlp(x, g, wu, wd, tf=1024, cast=()):
    m, d = x.shape
    ff = wu.shape[1]
    tm = min(m, ROW_TILE)
    nf = ff // tf
    side = _SideCast(cast, (m // tm) * nf, lambda i, j: i * nf + j)
    outs = pl.pallas_call(
        side.wrap(_mlp_kernel, 4, 1),
        grid=(m // tm, nf),
        in_specs=[pl.BlockSpec((tm, d), lambda i, j: (i, 0)),
                  pl.BlockSpec((1, d), lambda i, j: (0, 0)),
                  pl.BlockSpec((d, tf), lambda i, j: (0, j)),
                  pl.BlockSpec((tf, d), lambda i, j: (j, 0))] + side.in_specs,
        out_specs=[pl.BlockSpec((tm, d), lambda i, j: (i, 0))] + side.out_specs,
        out_shape=[jax.ShapeDtypeStruct((m, d), F32)] + side.out_shape,
        scratch_shapes=[pltpu.VMEM((tm, d), BF16)],
        compiler_params=_cparams("parallel", "arbitrary"),
        name="mlp",
    )(x, g.reshape(1, d), wu, wd, *side.views)
    return side.split(outs, 1)


def _split3(x):
    hi = x.astype(BF16)
    r1 = x - hi.astype(F32)
    mid = r1.astype(BF16)
    lo = (r1 - mid.astype(F32)).astype(BF16)
    return hi, mid, lo


def _ssd_kernel(xbc_ref, dt_ref, z_ref, cw_ref, cb_ref, dtb_ref, alog_ref, dsk_ref, gg_ref, tri_ref, expand_ref,
                conv0_ref, h0_ref, y_ref, convo_ref, ho_ref,
                ext_ref, ht_ref, yd_ref, st_ref, *, t_real, n_heads, d_inner):
    L = SSD_CHUNK
    P = SSM_HEAD_DIM
    N = SSM_D_STATE
    G = SSM_GROUPS
    hpg = n_heads // G
    gw = hpg * P
    hist = CONV_W - 1
    base = SUBLANES - hist
    c = pl.program_id(1)
    nc = pl.num_programs(1)

    @pl.when(c == 0)
    def _():
        ext_ref[base:SUBLANES, :] = conv0_ref[...]
        for g in range(G):
            ht_ref[g] = h0_ref[g].T

    ext_ref[SUBLANES:SUBLANES + L, :] = xbc_ref[...]
    ext = ext_ref[...]
    acc = cb_ref[...] + ext[SUBLANES:, :] * cw_ref[hist:CONV_W, :]
    for k in range(1, CONV_W):
        acc = acc + pltpu.roll(ext, k, axis=0)[SUBLANES:, :] * cw_ref[hist - k:CONV_W - k, :]
    xc = acc * _sigmoid(acc)

    @pl.when(c == nc - 1)
    def _():
        n_in_last = t_real - ((t_real - 1) // L) * L
        convo_ref[...] = ext_ref[base + n_in_last:base + n_in_last + hist, :]

    ext_ref[base:SUBLANES, :] = ext_ref[base + L:SUBLANES + L, :]

    dt_raw = dt_ref[...] + dtb_ref[...]
    dt = jnp.maximum(dt_raw, 0.0) + jnp.log1p(jnp.exp(-jnp.abs(dt_raw)))
    row = lax.broadcasted_iota(jnp.int32, (L, LANES), 0) + c * L
    dt = jnp.where(row < t_real, dt, 0.0)
    da = dt * (-jnp.exp(alog_ref[...]))
    li = lax.broadcasted_iota(jnp.int32, (L, L), 0)
    si = lax.broadcasted_iota(jnp.int32, (L, L), 1)
    causal = li >= si
    a_cs = sum(jnp.dot(tri_ref[...], part, preferred_element_type=F32) for part in _split3(da))
    a_cs_t = a_cs.T
    dt_t = dt.T
    w_t = dt_t * jnp.exp(a_cs_t[:, L - 1:L] - a_cs_t)
    e_exp = sum(jnp.dot(part, expand_ref[...], preferred_element_type=F32) for part in _split3(jnp.exp(a_cs)))
    xb = xc[:, :d_inner].astype(BF16)
    low_half = lax.broadcasted_iota(jnp.int32, (L, 2 * P), 1) < P

    for g in range(G):
        b_g = xc[:, d_inner + g * N:d_inner + (g + 1) * N]
        c_g = xc[:, d_inner + G * N + g * N:d_inner + G * N + (g + 1) * N].astype(BF16)
        cbm = lax.dot_general(c_g, b_g.astype(BF16), NT_DIMS, preferred_element_type=F32)
        b_t = b_g.T
        y_off = jnp.dot(c_g, ht_ref[g].astype(BF16), preferred_element_type=F32)
        for kp in range(hpg // 2):
            h0 = g * hpg + 2 * kp
            x_pair = xb[:, h0 * P:(h0 + 2) * P]
            yd, st = [], []
            for h in (h0, h0 + 1):
                seg = a_cs[:, h:h + 1] - a_cs_t[h:h + 1, :]
                dec = jnp.exp(jnp.where(causal, seg, -jnp.inf))
                mh = (cbm * dec * dt_t[h:h + 1, :]).astype(BF16)
                yd.append(jnp.dot(mh, x_pair, preferred_element_type=F32))
                btw = (b_t * w_t[h:h + 1, :]).astype(BF16)
                st.append(jnp.dot(btw, x_pair, preferred_element_type=F32))
            yd_ref[:, 2 * kp * P:(2 * kp + 2) * P] = jnp.where(low_half, yd[0], yd[1])
            st_ref[:, 2 * kp * P:(2 * kp + 2) * P] = jnp.where(low_half, st[0], st[1])
        gs = slice(g * gw, (g + 1) * gw)
        e_g = e_exp[:, gs]
        y_g = yd_ref[...] + y_off * e_g + dsk_ref[:, gs] * xc[:, gs]
        ht_ref[g] = ht_ref[g] * e_g[L - 1:L, :] + st_ref[...]
        z_g = z_ref[:, gs]
        gated = y_g * (z_g * _sigmoid(z_g))
        y_ref[:, gs] = (gated * _rms_scale(gated) * gg_ref[:, gs]).astype(y_ref.dtype)

    @pl.when(c == nc - 1)
    def _():
        for g in range(G):
            ho_ref[g] = ht_ref[g].T


def ssd_mixer(proj, col_xbc, col_dt, col_z, t_real, conv_w, conv_b, dt_bias, a_log, d_skip, gate_g,
              conv_prev, h_prev, cast=()):
    bt, tp, _ = proj.shape
    conv_dim = conv_w.shape[1]
    n_heads = dt_bias.shape[0]
    d_inner = n_heads * SSM_HEAD_DIM
    gw = d_inner // SSM_GROUPS
    L = SSD_CHUNK
    assert tp % L == 0 and (tp - t_real) < L
    assert col_xbc % conv_dim == 0 and col_dt % LANES == 0 and col_z % d_inner == 0
    pad = LANES - n_heads
    dtb = jnp.pad(dt_bias, (0, pad)).reshape(1, LANES)
    alog = jnp.pad(a_log, (0, pad)).reshape(1, LANES)
    dsk = jnp.repeat(d_skip, SSM_HEAD_DIM).reshape(1, d_inner)
    tri = jnp.tril(jnp.ones((L, L), BF16))
    expand = jnp.repeat(jnp.eye(LANES, dtype=BF16)[:, :n_heads], SSM_HEAD_DIM, axis=1)
    kern = functools.partial(_ssd_kernel, t_real=t_real, n_heads=n_heads, d_inner=d_inner)
    vec = lambda n: pl.BlockSpec((1, n), lambda b, c: (0, 0))
    nc = tp // L
    side = _SideCast(cast, bt * nc, lambda b, c: b * nc + c)
    outs = pl.pallas_call(
        side.wrap(kern, 13, 3),
        grid=(bt, nc),
        in_specs=[pl.BlockSpec((None, L, conv_dim), lambda b, c: (b, c, col_xbc // conv_dim)),
                  pl.BlockSpec((None, L, LANES), lambda b, c: (b, c, col_dt // LANES)),
                  pl.BlockSpec((None, L, d_inner), lambda b, c: (b, c, col_z // d_inner)),
                  pl.BlockSpec((CONV_W, conv_dim), lambda b, c: (0, 0)),
                  vec(conv_dim), vec(LANES), vec(LANES), vec(d_inner), vec(d_inner),
                  pl.BlockSpec((L, L), lambda b, c: (0, 0)),
                  pl.BlockSpec((LANES, d_inner), lambda b, c: (0, 0)),
                  pl.BlockSpec((None, CONV_W - 1, conv_dim), lambda b, c: (b, 0, 0)),
                  pl.BlockSpec((None, SSM_GROUPS, gw, SSM_D_STATE), lambda b, c: (b, 0, 0, 0))] + side.in_specs,
        out_specs=[pl.BlockSpec((None, L, d_inner), lambda b, c: (b, c, 0)),
                   pl.BlockSpec((None, CONV_W - 1, conv_dim), lambda b, c: (b, 0, 0)),
                   pl.BlockSpec((None, SSM_GROUPS, gw, SSM_D_STATE), lambda b, c: (b, 0, 0, 0))] + side.out_specs,
        out_shape=[jax.ShapeDtypeStruct((bt, tp, d_inner), BF16),
                   jax.ShapeDtypeStruct((bt, CONV_W - 1, conv_dim), F32),
                   jax.ShapeDtypeStruct((bt, SSM_GROUPS, gw, SSM_D_STATE), F32)] + side.out_shape,
        scratch_shapes=[pltpu.VMEM((SUBLANES + L, conv_dim), F32),
                        pltpu.VMEM((SSM_GROUPS, SSM_D_STATE, gw), F32),
                        pltpu.VMEM((L, gw), F32),
                        pltpu.VMEM((SSM_D_STATE, gw), F32)],
        compiler_params=_cparams("parallel", "arbitrary"),
        name="ssd_mixer",
    )(proj, proj, proj, conv_w, conv_b.reshape(1, conv_dim), dtb, alog, dsk, gate_g.reshape(1, d_inner),
      tri, expand, conv_prev, h_prev, *side.views)
    return side.split(outs, 3)


def _mem_attn_kernel(q_ref, k_ref, v_ref, o_ref):
    scale = HEAD_DIM ** -0.5
    headed = len(k_ref.shape) == 3
    for h in range(MEM_HEADS):
        sl = slice(h * HEAD_DIM, (h + 1) * HEAD_DIM)
        q = q_ref[:, sl].astype(BF16)
        k = (k_ref[:, h, :] if headed else k_ref[:, sl]).astype(BF16)
        v = (v_ref[:, h, :] if headed else v_ref[:, sl]).astype(BF16)
        s = lax.dot_general(q, k, NT_DIMS, preferred_element_type=F32) * scale
        e = jnp.exp(s - jnp.max(s, axis=-1, keepdims=True))
        den = jnp.sum(e, axis=-1, keepdims=True)
        o = jnp.dot(e.astype(BF16), v, preferred_element_type=F32) / den
        o_ref[:, sl] = o.astype(o_ref.dtype)


def mem_attention(proj, col_q, k, v, kv_at):
    bt, t, _ = proj.shape
    width = MEM_HEADS * HEAD_DIM
    tq = min(t, ROW_TILE)
    if k.ndim == 5:
        mlen = k.shape[2]
        kv_specs = [pl.BlockSpec((None, None, mlen, MEM_HEADS, HEAD_DIM), lambda b, i: (kv_at, b, 0, 0, 0))] * 2
    else:
        mlen = k.shape[1]
        kv_specs = [pl.BlockSpec((None, mlen, width), lambda b, i, c=c: (b, 0, c // width)) for c in kv_at]
    return pl.pallas_call(
        _mem_attn_kernel,
        grid=(bt, t // tq),
        in_specs=[pl.BlockSpec((None, tq, width), lambda b, i: (b, i, col_q // width))] + kv_specs,
        out_specs=pl.BlockSpec((None, tq, width), lambda b, i: (b, i, 0)),
        out_shape=jax.ShapeDtypeStruct((bt, t, width), BF16),
        compiler_params=_cparams("parallel", "parallel"),
        name="mem_attention",
    )(proj, k, v)


def _run_units(first, count, unit, unroll):
    trips = count // unroll
    if trips == 1:
        trips = 0
    if trips:
        def trip(it, carry):
            for k in range(unroll):
                unit(first + it * unroll + k)
            return carry
        lax.fori_loop(0, trips, trip, 0)
    for k in range(trips * unroll, count):
        unit(first + k)


def _dil_attn_kernel(*refs, groups, rb, unroll):
    ng = len(groups)
    q_refs = refs[:ng]
    kp_ref, kc_ref, vp_ref, vc_ref, o_ref = refs[ng:ng + 5]
    og = refs[ng + 5:2 * ng + 5]
    lg = refs[2 * ng + 5:3 * ng + 5]
    i = pl.program_id(1)
    scale = HEAD_DIM ** -0.5

    for gi, (win, d) in enumerate(groups):
        w = win // d
        dw = d * w
        qi = lax.broadcasted_iota(jnp.int32, (w, 2 * w), 0)
        kj = lax.broadcasted_iota(jnp.int32, (w, 2 * w), 1)
        band = (kj >= qi) & (kj <= qi + w)
        band_first = band & (kj >= jnp.where(i > 0, 0, w))

        def rows(start, size, d=d, w=w):
            if d == 1:
                return pl.ds(pl.multiple_of(start, w), size)
            return pl.ds(start, size, stride=d)

        def attend(rows_q, kk, vv, mask, gi=gi, w=w):
            q = (q_refs[gi][rows_q, :] * (scale * LOG2_E)).astype(BF16)
            s = lax.dot_general(q, kk.astype(BF16), NT_DIMS, preferred_element_type=F32)
            s = jnp.where(mask, s, -jnp.inf)
            m = jnp.max(s, axis=-1, keepdims=True)
            e = jnp.exp2(s - m)
            den = jnp.sum(e, axis=-1, keepdims=True)
            o = jnp.dot(e.astype(BF16), vv.astype(BF16), preferred_element_type=F32) / den
            og[gi][rows_q, :] = o
            lg[gi][rows_q, :] = jnp.broadcast_to(m * LN_2 + jnp.log(den), (w, HEAD_DIM))

        def first_unit(r, rows=rows, attend=attend, w=w, dw=dw, mask=band_first):
            rows_q, rows_p = rows(r, w), rows(rb - dw + r, w)
            kk = jnp.concatenate([kp_ref[rows_p, :], kc_ref[rows_q, :]], axis=0)
            vv = jnp.concatenate([vp_ref[rows_p, :], vc_ref[rows_q, :]], axis=0)
            attend(rows_q, kk, vv, mask)

        def later_unit(u, rows=rows, attend=attend, d=d, w=w, dw=dw, mask=band):
            sb = u // d
            start_q = sb * dw + (u - sb * d)
            rows_k = rows(start_q - dw, 2 * w)
            attend(rows(start_q, w), kc_ref[rows_k, :], vc_ref[rows_k, :], mask)

        _run_units(0, d, first_unit, unroll)
        _run_units(d, rb // w - d, later_unit, unroll)

    ls = [r[...] for r in lg]
    mm = functools.reduce(jnp.maximum, ls)
    ws = [jnp.exp(l - mm) for l in ls]
    num = sum(wg * r[...] for wg, r in zip(ws, og))
    o_ref[...] = (num / sum(ws)).astype(o_ref.dtype)


def dilated_attention(proj, kv):
    bt, t, _ = proj.shape
    ng = len(DIL_GROUPS)
    rb = min(t, max(win for win, _ in DIL_GROUPS))
    for win, d in DIL_GROUPS:
        assert win % d == 0 and rb % win == 0
    assert t % rb == 0
    blk = lambda f: pl.BlockSpec((None, rb, HEAD_DIM), f)
    prev = lambda i: jnp.maximum(i - 1, 0)
    q_specs = [blk(lambda b, i, h, g=g: (b, i, g * DIL_HEADS + h)) for g in range(ng)]
    kv_specs = [blk(lambda b, i, h: (b, prev(i), h)), blk(lambda b, i, h: (b, i, h)),
                blk(lambda b, i, h: (b, prev(i), DIL_HEADS + h)), blk(lambda b, i, h: (b, i, DIL_HEADS + h))]
    return pl.pallas_call(
        functools.partial(_dil_attn_kernel, groups=DIL_GROUPS, rb=rb, unroll=DIL_UNROLL),
        grid=(bt, t // rb, DIL_HEADS),
        in_specs=q_specs + kv_specs,
        out_specs=blk(lambda b, i, h: (b, i, h)),
        out_shape=jax.ShapeDtypeStruct((bt, t, DIL_HEADS * HEAD_DIM), BF16),
        scratch_shapes=[pltpu.VMEM((rb, HEAD_DIM), F32)] * (2 * ng),
        compiler_params=_cparams("parallel", "parallel", "parallel"),
        name="dilated_attention",
    )(*([proj] * ng), kv, kv, kv, kv)


def _dil_decode_kernel(*refs):
    ng = len(DIL_GROUPS)
    q_ref, kv_ref = refs[:2]
    k_refs, v_refs = refs[2:2 + ng], refs[2 + ng:2 + 2 * ng]
    o_ref = refs[-1]
    width = DIL_HEADS * HEAD_DIM
    scale = HEAD_DIM ** -0.5
    for h in range(DIL_HEADS):
        sl = slice(h * HEAD_DIM, (h + 1) * HEAD_DIM)
        k_new = kv_ref[:, h * HEAD_DIM:(h + 1) * HEAD_DIM]
        v_new = kv_ref[:, width + h * HEAD_DIM:width + (h + 1) * HEAD_DIM]
        outs, lses = [], []
        for gi in range(ng):
            kc = k_refs[gi][:, h, :].astype(BF16)
            vc = v_refs[gi][:, h, :].astype(BF16)
            q = q_ref[:, gi * width + h * HEAD_DIM:gi * width + (h + 1) * HEAD_DIM]
            s = lax.dot_general(q.astype(BF16), kc, NT_DIMS, preferred_element_type=F32) * scale
            s_self = jnp.sum(q * k_new, axis=-1, keepdims=True) * scale
            m = jnp.maximum(jnp.max(s, axis=-1, keepdims=True), s_self)
            e = jnp.exp(s - m)
            e_self = jnp.exp(s_self - m)
            den = jnp.sum(e, axis=-1, keepdims=True) + e_self
            o = jnp.dot(e.astype(BF16), vc, preferred_element_type=F32) + e_self * v_new
            outs.append(o / den)
            lses.append(m + jnp.log(den))
        mm = functools.reduce(jnp.maximum, lses)
        ws = [jnp.exp(l - mm) for l in lses]
        num = sum(wg * o for wg, o in zip(ws, outs))
        o_ref[:, sl] = (num / sum(ws)).astype(o_ref.dtype)


def dilated_decode(proj, kv_new, k_cache, v_cache):
    bt, rows, nq = proj.shape
    lc = k_cache.shape[1]
    width = DIL_HEADS * HEAD_DIM
    views, specs = [], []
    for cache in (k_cache, v_cache):
        for win, dil in DIL_GROUPS:
            assert lc % win == 0 and win % dil == 0
            w = win // dil
            views.append(cache.reshape(bt, lc // dil, dil, DIL_HEADS, HEAD_DIM))
            specs.append(pl.BlockSpec((None, w, None, DIL_HEADS, HEAD_DIM),
                                      lambda b, last=lc // win - 1: (b, last, 0, 0, 0)))
    return pl.pallas_call(
        _dil_decode_kernel,
        grid=(bt,),
        in_specs=[pl.BlockSpec((None, rows, nq), lambda b: (b, 0, 0)),
                  pl.BlockSpec((None, rows, 2 * width), lambda b: (b, 0, 0))] + specs,
        out_specs=pl.BlockSpec((None, rows, width), lambda b: (b, 0, 0)),
        out_shape=jax.ShapeDtypeStruct((bt, rows, width), BF16),
        compiler_params=_cparams("parallel"),
        name="dilated_decode",
    )(proj, kv_new, *views)


def kernel(x_prompt, x_sample, state_conv, state_ssm, cache_win_k, cache_win_v, cache_mem_k, cache_mem_v,
           mem_prompt, norm_mix_g, norm_mlp_g, norm_mem_g, w_mem_k, w_mem_v, mem_q_norm_g, mem_k_norm_g,
           w_up, w_down, w_in_a, conv_w, conv_b, dt_bias, a_log, d_skip, gate_norm_g, w_out_a,
           w_in_b, q_norm_g, w_out_b, kv_norm_g, w_k_shared, w_v_shared, k_norm_g):
    depth = w_up.shape[0]
    n_a = w_in_a.shape[0]
    d_model = x_prompt.shape[-1]
    n_heads = dt_bias.shape[1]
    d_inner = n_heads * SSM_HEAD_DIM
    conv_dim = conv_w.shape[2]
    mem_w = MEM_HEADS * HEAD_DIM
    kv_w = DIL_HEADS * HEAD_DIM
    dil_q_w = len(DIL_GROUPS) * kv_w

    s1, s2, s3 = d_inner, d_inner + conv_dim, d_inner + conv_dim + n_heads
    assert s1 % COL_TILE == 0 and s2 % COL_TILE == 0
    w_in_a_b = w_in_a.astype(BF16)
    w_a_tail = jnp.concatenate([w_in_a_b[..., s2:s3], jnp.zeros((n_a, d_model, COL_TILE - n_heads), BF16),
                                w_in_a_b[..., s3:]], axis=-1)
    tiles_a = ([(0, s1 + c) for c in range(0, conv_dim, COL_TILE)] + [(1, 0)]
               + [(0, c) for c in range(0, d_inner, COL_TILE)]
               + [(1, COL_TILE + c) for c in range(0, mem_w, COL_TILE)])
    col_xbc, col_dt, col_z = 0, conv_dim, conv_dim + COL_TILE
    col_qm_a = col_z + d_inner
    kinds_a = ["plain"] * (col_qm_a // COL_TILE) + ["norm"] * (mem_w // COL_TILE)
    kinds_b = ["rope"] * (dil_q_w // COL_TILE) + ["norm"] * (mem_w // COL_TILE)
    kinds_kv = ["rope"] * (kv_w // COL_TILE) + ["plain"] * (kv_w // COL_TILE)
    kinds_mkv = ["norm"] * (mem_w // COL_TILE) + ["plain"] * (mem_w // COL_TILE)
    w_b = w_in_b.astype(BF16)
    w_kv = jnp.concatenate([w_k_shared, w_v_shared], axis=-1).astype(BF16)[None]
    w_mkv = jnp.concatenate([w_mem_k, w_mem_v], axis=-1).astype(BF16)
    w_out_a_b, w_out_b_b = w_out_a.astype(BF16), w_out_b.astype(BF16)
    assert n_a >= 1
    w_up_b, w_down_b = [None] * depth, [None] * depth

    def gain_row(width, pieces):
        parts, at = [], 0
        for start, g, reps in pieces:
            parts += [jnp.ones((start - at,), F32), jnp.tile(g, reps)]
            at = start + reps * HEAD_DIM
        parts.append(jnp.ones((width - at,), F32))
        return jnp.concatenate(parts).reshape(1, width)

    def run(x, pos_rows, conv_prev, ssm_prev, mem_kv, k_past, v_past, t_real, round_mlp_weights):
        bt, t, _ = x.shape
        m = bt * t
        x2 = x.reshape(m, d_model)
        cos, sin = rope_tables(pos_rows)
        conv_new, ssm_new = [], []

        def run_mlp(x2, l):
            if round_mlp_weights and l + 1 < depth:
                x2, (w_up_b[l + 1], w_down_b[l + 1]) = mlp(x2, norm_mlp_g[l], w_up_b[l], w_down_b[l],
                                                           cast=(w_up[l + 1], w_down[l + 1]))
                return x2
            return mlp(x2, norm_mlp_g[l], w_up_b[l], w_down_b[l])

        for l in range(n_a):
            first = round_mlp_weights and l == 0
            hg = gain_row(col_qm_a + mem_w, [(col_qm_a, mem_q_norm_g[l], MEM_HEADS)])
            proj = norm_proj(x2, norm_mix_g[l], [w_in_a_b, w_a_tail], l, hg, kinds_a, tile_src=tiles_a,
                             cast=(w_up[0],) if first else ())
            if first:
                proj, (w_up_b[0],) = proj
            proj3 = proj.reshape(bt, t, -1)
            tp = -(-t // SSD_CHUNK) * SSD_CHUNK
            proj_ssd = proj3 if tp == t else jnp.pad(proj3, ((0, 0), (0, tp - t), (0, 0)))
            ssd = ssd_mixer(proj_ssd, col_xbc, col_dt, col_z, t_real, conv_w[l], conv_b[l],
                            dt_bias[l], a_log[l], d_skip[l], gate_norm_g[l],
                            conv_prev[l], ssm_prev[l].reshape(bt, SSM_GROUPS, -1, SSM_D_STATE),
                            cast=(w_down[0],) if first else ())
            if first:
                ssd, (w_down_b[0],) = ssd
            y, c_new, h_new = ssd
            y = y[:, :t].reshape(m, d_inner)
            mo = mem_attention(proj3, col_qm_a, *mem_kv(l)).reshape(m, mem_w)
            x2 = out_proj(x2, y, mo, w_out_a_b, l)
            x2 = run_mlp(x2, l)
            conv_new.append(c_new)
            ssm_new.append(h_new.reshape(bt, n_heads, SSM_HEAD_DIM, SSM_D_STATE))
        hg = gain_row(2 * kv_w, [(0, k_norm_g, DIL_HEADS)])
        kv3 = norm_proj(x2, kv_norm_g, w_kv, 0, hg, kinds_kv, cos, sin).reshape(bt, t, 2 * kv_w)
        for l in range(n_a, depth):
            j = l - n_a
            hg = gain_row(dil_q_w + mem_w, [(0, q_norm_g[j], dil_q_w // HEAD_DIM),
                                            (dil_q_w, mem_q_norm_g[l], MEM_HEADS)])
            proj3 = norm_proj(x2, norm_mix_g[l], w_b, j, hg, kinds_b, cos, sin).reshape(bt, t, -1)
            if k_past is None:
                att = dilated_attention(proj3, kv3)
            else:
                att = dilated_decode(proj3, kv3, k_past, v_past)
            mo = mem_attention(proj3, dil_q_w, *mem_kv(l)).reshape(m, mem_w)
            x2 = out_proj(x2, att.reshape(m, kv_w), mo, w_out_b_b, j)
            x2 = run_mlp(x2, l)
        return x2.reshape(bt, t, d_model), jnp.stack(conv_new), jnp.stack(ssm_new), kv3

    bp, t_p, _ = x_prompt.shape
    mlen = mem_prompt.shape[1]
    mem2 = mem_prompt.reshape(bp * mlen, d_model)
    mkv_p = []
    for l in range(depth):
        hg = gain_row(2 * mem_w, [(0, mem_k_norm_g[l], MEM_HEADS)])
        mkv_p.append(norm_proj(mem2, norm_mem_g[l], w_mkv, l, hg, kinds_mkv).reshape(bp, mlen, 2 * mem_w))
    conv0 = jnp.zeros((n_a, bp, CONV_W - 1, conv_dim), F32)
    ssm0 = jnp.zeros((n_a, bp, n_heads, SSM_HEAD_DIM, SSM_D_STATE), F32)
    y_p, conv_p, ssm_p, kv_p = run(x_prompt, jnp.arange(t_p, dtype=jnp.int32), conv0, ssm0,
                                   lambda l: (mkv_p[l], mkv_p[l], (0, mem_w)), None, None, t_p, True)
    mkv_all = jnp.stack(mkv_p)
    mem_k_p = mkv_all[..., :mem_w].reshape(depth, bp, mlen, MEM_HEADS, HEAD_DIM)
    mem_v_p = mkv_all[..., mem_w:].reshape(depth, bp, mlen, MEM_HEADS, HEAD_DIM)
    keep = min(max(w for w, _ in DIL_GROUPS), t_p)
    win_k_p = kv_p[:, t_p - keep:, :kv_w].reshape(bp, keep, DIL_HEADS, HEAD_DIM)
    win_v_p = kv_p[:, t_p - keep:, kv_w:].reshape(bp, keep, DIL_HEADS, HEAD_DIM)

    bs, t_s, _ = x_sample.shape
    assert t_s == 1
    xs = jnp.pad(x_sample, ((0, 0), (0, SAMPLE_ROWS - t_s), (0, 0)))
    pos_s = jnp.full((bs * SAMPLE_ROWS,), PAST_LEN, jnp.int32)
    y_s, conv_s, ssm_s, kv_s = run(xs, pos_s, state_conv, state_ssm,
                                   lambda l: (cache_mem_k, cache_mem_v, l), cache_win_k, cache_win_v, t_s, False)
    y_s = y_s[:, :t_s]
    k_s = kv_s[:, :t_s, :kv_w].reshape(bs, t_s, DIL_HEADS, HEAD_DIM)
    v_s = kv_s[:, :t_s, kv_w:].reshape(bs, t_s, DIL_HEADS, HEAD_DIM)

    return (y_p, y_s, conv_p, ssm_p, win_k_p, win_v_p, mem_k_p, mem_v_p, conv_s, ssm_s, k_s, v_s)
```

```python
import functools

import jax
import jax.numpy as jnp
from jax import lax
from jax.experimental import pallas as pl
from jax.experimental.pallas import tpu as pltpu

F32 = jnp.float32
BF16 = jnp.bfloat16

HEAD_DIM = 128
SSM_HEAD_DIM = 64
SSM_GROUPS = 4
SSM_D_STATE = 128
CONV_W = 4
SSD_CHUNK = 128
MEM_HEADS = 4
DIL_GROUPS = ((128, 1), (512, 4), (2048, 16))
DIL_HEADS = 4
PAST_LEN = 16384
ROPE_THETA = 10000.0
EPS = 1e-6
LOG2_E = 1.4426950408889634
LN_2 = 0.6931471805599453

LANES = 128
SUBLANES = 8
VMEM_LIMIT = 48 * 1024 * 1024
COL_TILE = 512
ROW_TILE = 512
PROJ_OUT_BLOCK_BYTES = 5 * 1024 * 1024
SAMPLE_ROWS = SUBLANES
DIL_UNROLL = 16

NT_DIMS = (((1,), (1,)), ((), ()))
RESIDENT = pl.Buffered(1)


def _cparams(*sem):
    return pltpu.CompilerParams(dimension_semantics=sem, vmem_limit_bytes=VMEM_LIMIT)


def _sigmoid(x):
    return 1.0 / (1.0 + jnp.exp(-x))


def _rms_scale(x):
    return lax.rsqrt(jnp.mean(x * x, axis=-1, keepdims=True) + EPS)


class _SideCast:
    def __init__(self, mats, steps, step_of):
        self.n = len(mats)
        self.shapes = [a.shape[1:] for a, _ in mats]
        self.views, self.in_specs, self.out_specs, self.out_shape = [], [], [], []
        for a, layer in mats:
            n_layers, r, c = a.shape
            assert r % (steps * 2 * SUBLANES) == 0
            band = r // steps
            self.views.append(a.reshape(n_layers, steps, band, c))
            self.in_specs.append(pl.BlockSpec((None, None, band, c),
                                              lambda *ids, layer=layer: (layer, step_of(*ids), 0, 0)))
            self.out_specs.append(pl.BlockSpec((None, band, c), lambda *ids: (step_of(*ids), 0, 0)))
            self.out_shape.append(jax.ShapeDtypeStruct((steps, band, c), BF16))

    def wrap(self, body, n_in, n_out):
        n = self.n
        if not n:
            return body

        def kern(*refs):
            ins, srcs = refs[:n_in], refs[n_in:n_in + n]
            outs, dsts = refs[n_in + n:n_in + n + n_out], refs[n_in + n + n_out:n_in + 2 * n + n_out]
            for s, d in zip(srcs, dsts):
                d[...] = s[...].astype(BF16)
            body(*ins, *outs, *refs[n_in + 2 * n + n_out:])
        return kern

    def split(self, outs, n_out):
        main = outs[0] if n_out == 1 else tuple(outs[:n_out])
        if not self.n:
            return main
        return main, [o.reshape(s) for o, s in zip(outs[n_out:], self.shapes)]


def _round_kernel(src_ref, dst_ref):
    dst_ref[...] = src_ref[...].astype(BF16)


def round_weight(w, n_cols):
    n_layers, r, _ = w.shape
    band = min(r, ROW_TILE // 2)
    assert n_cols % LANES == 0 and r % band == 0
    return pl.pallas_call(
        _round_kernel,
        grid=(n_layers, r // band),
        in_specs=[pl.BlockSpec((None, band, n_cols), lambda l, i: (l, i, 0))],
        out_specs=pl.BlockSpec((None, band, n_cols), lambda l, i: (l, i, 0)),
        out_shape=jax.ShapeDtypeStruct((n_layers, r, n_cols), BF16),
        compiler_params=_cparams("parallel", "parallel"),
        name="round_weight",
    )(w)


def _rope_table_kernel(pos_ref, inv_ref, cos_ref, sin_ref):
    ang = pos_ref[...] * inv_ref[...]
    lane = lax.broadcasted_iota(jnp.int32, ang.shape, 1)
    cos_ref[...] = jnp.cos(ang)
    s = jnp.sin(ang)
    sin_ref[...] = jnp.where(lane < HEAD_DIM // 2, -s, s)


def rope_tables(pos):
    r = pos.shape[0]
    half = HEAD_DIM // 2
    inv = ROPE_THETA ** (-jnp.arange(half, dtype=F32) * (2.0 / HEAD_DIM))
    inv = jnp.concatenate([inv, inv])[None, :]
    posf = jnp.broadcast_to(pos.astype(F32)[:, None], (r, HEAD_DIM))
    tr = min(r, ROW_TILE)
    return pl.pallas_call(
        _rope_table_kernel,
        grid=(r // tr,),
        in_specs=[pl.BlockSpec((tr, HEAD_DIM), lambda i: (i, 0)),
                  pl.BlockSpec((1, HEAD_DIM), lambda i: (0, 0))],
        out_specs=[pl.BlockSpec((tr, HEAD_DIM), lambda i: (i, 0))] * 2,
        out_shape=[jax.ShapeDtypeStruct((r, HEAD_DIM), F32)] * 2,
        compiler_params=_cparams("parallel"),
        name="rope_tables",
    )(posf, inv)


def _proj_kernel(*refs, kinds, tile_src, n_w):
    x_ref, g_ref = refs[:2]
    w_refs = refs[2:2 + n_w]
    hg_ref = refs[2 + n_w]
    if "rope" in kinds:
        cos_ref, sin_ref = refs[3 + n_w:5 + n_w]
    o_ref, xn_ref = refs[-2:]
    x = x_ref[...]
    xn_ref[...] = (x * _rms_scale(x) * g_ref[...]).astype(BF16)
    for t, kind in enumerate(kinds):
        which, col = tile_src[t]
        cols = slice(t * COL_TILE, (t + 1) * COL_TILE)
        y = jnp.dot(xn_ref[...], w_refs[which][:, col:col + COL_TILE], preferred_element_type=F32)
        if kind == "plain":
            o_ref[:, cols] = y
            continue
        for j in range(COL_TILE // HEAD_DIM):
            sl = slice(j * HEAD_DIM, (j + 1) * HEAD_DIM)
            osl = slice(t * COL_TILE + j * HEAD_DIM, t * COL_TILE + (j + 1) * HEAD_DIM)
            yh = y[:, sl]
            yh = yh * _rms_scale(yh) * hg_ref[:, osl]
            if kind == "rope":
                yh = yh * cos_ref[...] + pltpu.roll(yh, HEAD_DIM // 2, axis=1) * sin_ref[...]
            o_ref[:, osl] = yh


def norm_proj(x, g, ws, layer, head_gain, kinds, cos=None, sin=None, cast=(), tile_src=None):
    m, d = x.shape
    ws = list(ws) if isinstance(ws, (list, tuple)) else [ws]
    if tile_src is None:
        tile_src = [(0, t * COL_TILE) for t in range(len(kinds))]
    n_cols = COL_TILE * len(kinds)
    assert len(tile_src) == len(kinds) and head_gain.shape == (1, n_cols)
    tm = min(m, ROW_TILE)
    while tm * n_cols * 4 > PROJ_OUT_BLOCK_BYTES and tm % (2 * SUBLANES) == 0:
        tm //= 2
    assert m % tm == 0
    in_specs = [pl.BlockSpec((tm, d), lambda i: (i, 0)),
                pl.BlockSpec((1, d), lambda i: (0, 0))]
    for k, w in enumerate(ws):
        used = max(col + COL_TILE for which, col in tile_src if which == k)
        assert used <= w.shape[2] and (used % LANES == 0 or used == w.shape[2])
        in_specs.append(pl.BlockSpec((None, d, used), lambda i: (layer, 0, 0), pipeline_mode=RESIDENT))
    in_specs.append(pl.BlockSpec((1, n_cols), lambda i: (0, 0)))
    args = [x, g.reshape(1, d), *ws, head_gain]
    if "rope" in kinds:
        assert cos.shape[0] % tm == 0
        tab_blocks = cos.shape[0] // tm
        in_specs += [pl.BlockSpec((tm, HEAD_DIM), lambda i: (i % tab_blocks, 0))] * 2
        args += [cos, sin]
    side = _SideCast(cast, m // tm, lambda i: i)
    outs = pl.pallas_call(
        side.wrap(functools.partial(_proj_kernel, kinds=tuple(kinds), tile_src=tuple(tile_src), n_w=len(ws)),
                  len(args), 1),
        grid=(m // tm,),
        in_specs=in_specs + side.in_specs,
        out_specs=[pl.BlockSpec((tm, n_cols), lambda i: (i, 0))] + side.out_specs,
        out_shape=[jax.ShapeDtypeStruct((m, n_cols), F32)] + side.out_shape,
        scratch_shapes=[pltpu.VMEM((tm, d), BF16)],
        compiler_params=_cparams("parallel"),
        name="norm_proj",
    )(*args, *side.views)
    return side.split(outs, 1)


def _out_proj_kernel(x_ref, a_ref, b_ref, wa_ref, wb_ref, o_ref):
    acc = jnp.dot(a_ref[...], wa_ref[...], preferred_element_type=F32)
    acc = acc + jnp.dot(b_ref[...], wb_ref[...], preferred_element_type=F32)
    o_ref[...] = x_ref[...] + acc


def out_proj(x, a, b, w, layer):
    m, d = x.shape
    tm = min(m, ROW_TILE)
    ka, kb = a.shape[1], b.shape[1]
    assert w.shape[1] == ka + kb and ka % kb == 0
    return pl.pallas_call(
        _out_proj_kernel,
        grid=(m // tm,),
        in_specs=[pl.BlockSpec((tm, d), lambda i: (i, 0)),
                  pl.BlockSpec((tm, ka), lambda i: (i, 0)),
                  pl.BlockSpec((tm, kb), lambda i: (i, 0)),
                  pl.BlockSpec((None, ka, d), lambda i: (layer, 0, 0), pipeline_mode=RESIDENT),
                  pl.BlockSpec((None, kb, d), lambda i: (layer, ka // kb, 0), pipeline_mode=RESIDENT)],
        out_specs=pl.BlockSpec((tm, d), lambda i: (i, 0)),
        out_shape=jax.ShapeDtypeStruct((m, d), F32),
        compiler_params=_cparams("parallel"),
        name="out_proj",
    )(x, a, b, w, w)


def _mlp_kernel(x_ref, g_ref, wu_ref, wd_ref, o_ref, xn_ref):
    f = pl.program_id(1)

    @pl.when(f == 0)
    def _():
        x = x_ref[...]
        xn_ref[...] = (x * _rms_scale(x) * g_ref[...]).astype(BF16)
        o_ref[...] = x

    h = jnp.dot(xn_ref[...], wu_ref[...], preferred_element_type=F32)
    h = jnp.maximum(h, 0.0)
    h = (h * h).astype(BF16)
    o_ref[...] += jnp.dot(h, wd_ref[...], preferred_element_type=F32)


def mlp(x, g, wu, wd, tf=1024, cast=()):
    m, d = x.shape
    ff = wu.shape[1]
    tm = min(m, ROW_TILE)
    nf = ff // tf
    side = _SideCast(cast, (m // tm) * nf, lambda i, j: i * nf + j)
    outs = pl.pallas_call(
        side.wrap(_mlp_kernel, 4, 1),
        grid=(m // tm, nf),
        in_specs=[pl.BlockSpec((tm, d), lambda i, j: (i, 0)),
                  pl.BlockSpec((1, d), lambda i, j: (0, 0)),
                  pl.BlockSpec((d, tf), lambda i, j: (0, j)),
                  pl.BlockSpec((tf, d), lambda i, j: (j, 0))] + side.in_specs,
        out_specs=[pl.BlockSpec((tm, d), lambda i, j: (i, 0))] + side.out_specs,
        out_shape=[jax.ShapeDtypeStruct((m, d), F32)] + side.out_shape,
        scratch_shapes=[pltpu.VMEM((tm, d), BF16)],
        compiler_params=_cparams("parallel", "arbitrary"),
        name="mlp",
    )(x, g.reshape(1, d), wu, wd, *side.views)
    return side.split(outs, 1)


def _split3(x):
    hi = x.astype(BF16)
    r1 = x - hi.astype(F32)
    mid = r1.astype(BF16)
    lo = (r1 - mid.astype(F32)).astype(BF16)
    return hi, mid, lo


def _ssd_kernel(xbc_ref, dt_ref, z_ref, cw_ref, cb_ref, dtb_ref, alog_ref, dsk_ref, gg_ref, tri_ref, expand_ref,
                conv0_ref, h0_ref, y_ref, convo_ref, ho_ref,
                ext_ref, ht_ref, yd_ref, st_ref, *, t_real, n_heads, d_inner):
    L = SSD_CHUNK
    P = SSM_HEAD_DIM
    N = SSM_D_STATE
    G = SSM_GROUPS
    hpg = n_heads // G
    gw = hpg * P
    hist = CONV_W - 1
    base = SUBLANES - hist
    c = pl.program_id(1)
    nc = pl.num_programs(1)

    @pl.when(c == 0)
    def _():
        ext_ref[base:SUBLANES, :] = conv0_ref[...]
        for g in range(G):
            ht_ref[g] = h0_ref[g].T

    ext_ref[SUBLANES:SUBLANES + L, :] = xbc_ref[...]
    ext = ext_ref[...]
    acc = cb_ref[...] + ext[SUBLANES:, :] * cw_ref[hist:CONV_W, :]
    for k in range(1, CONV_W):
        acc = acc + pltpu.roll(ext, k, axis=0)[SUBLANES:, :] * cw_ref[hist - k:CONV_W - k, :]
    xc = acc * _sigmoid(acc)

    @pl.when(c == nc - 1)
    def _():
        n_in_last = t_real - ((t_real - 1) // L) * L
        convo_ref[...] = ext_ref[base + n_in_last:base + n_in_last + hist, :]

    ext_ref[base:SUBLANES, :] = ext_ref[base + L:SUBLANES + L, :]

    dt_raw = dt_ref[...] + dtb_ref[...]
    dt = jnp.maximum(dt_raw, 0.0) + jnp.log1p(jnp.exp(-jnp.abs(dt_raw)))
    row = lax.broadcasted_iota(jnp.int32, (L, LANES), 0) + c * L
    dt = jnp.where(row < t_real, dt, 0.0)
    da = dt * (-jnp.exp(alog_ref[...]))
    li = lax.broadcasted_iota(jnp.int32, (L, L), 0)
    si = lax.broadcasted_iota(jnp.int32, (L, L), 1)
    causal = li >= si
    a_cs = sum(jnp.dot(tri_ref[...], part, preferred_element_type=F32) for part in _split3(da))
    a_cs_t = a_cs.T
    dt_t = dt.T
    w_t = dt_t * jnp.exp(a_cs_t[:, L - 1:L] - a_cs_t)
    e_exp = sum(jnp.dot(part, expand_ref[...], preferred_element_type=F32) for part in _split3(jnp.exp(a_cs)))
    xb = xc[:, :d_inner].astype(BF16)
    low_half = lax.broadcasted_iota(jnp.int32, (L, 2 * P), 1) < P

    for g in range(G):
        b_g = xc[:, d_inner + g * N:d_inner + (g + 1) * N]
        c_g = xc[:, d_inner + G * N + g * N:d_inner + G * N + (g + 1) * N].astype(BF16)
        cbm = lax.dot_general(c_g, b_g.astype(BF16), NT_DIMS, preferred_element_type=F32)
        b_t = b_g.T
        y_off = jnp.dot(c_g, ht_ref[g].astype(BF16), preferred_element_type=F32)
        for kp in range(hpg // 2):
            h0 = g * hpg + 2 * kp
            x_pair = xb[:, h0 * P:(h0 + 2) * P]
            yd, st = [], []
            for h in (h0, h0 + 1):
                seg = a_cs[:, h:h + 1] - a_cs_t[h:h + 1, :]
                dec = jnp.exp(jnp.where(causal, seg, -jnp.inf))
                mh = (cbm * dec * dt_t[h:h + 1, :]).astype(BF16)
                yd.append(jnp.dot(mh, x_pair, preferred_element_type=F32))
                btw = (b_t * w_t[h:h + 1, :]).astype(BF16)
                st.append(jnp.dot(btw, x_pair, preferred_element_type=F32))
            yd_ref[:, 2 * kp * P:(2 * kp + 2) * P] = jnp.where(low_half, yd[0], yd[1])
            st_ref[:, 2 * kp * P:(2 * kp + 2) * P] = jnp.where(low_half, st[0], st[1])
        gs = slice(g * gw, (g + 1) * gw)
        e_g = e_exp[:, gs]
        y_g = yd_ref[...] + y_off * e_g + dsk_ref[:, gs] * xc[:, gs]
        ht_ref[g] = ht_ref[g] * e_g[L - 1:L, :] + st_ref[...]
        z_g = z_ref[:, gs]
        gated = y_g * (z_g * _sigmoid(z_g))
        y_ref[:, gs] = (gated * _rms_scale(gated) * gg_ref[:, gs]).astype(y_ref.dtype)

    @pl.when(c == nc - 1)
    def _():
        for g in range(G):
            ho_ref[g] = ht_ref[g].T


def ssd_mixer(proj, col_xbc, col_dt, col_z, t_real, conv_w, conv_b, dt_bias, a_log, d_skip, gate_g,
              conv_prev, h_prev, cast=()):
    bt, tp, _ = proj.shape
    conv_dim = conv_w.shape[1]
    n_heads = dt_bias.shape[0]
    d_inner = n_heads * SSM_HEAD_DIM
    gw = d_inner // SSM_GROUPS
    L = SSD_CHUNK
    assert tp % L == 0 and (tp - t_real) < L
    assert col_xbc % conv_dim == 0 and col_dt % LANES == 0 and col_z % d_inner == 0
    pad = LANES - n_heads
    dtb = jnp.pad(dt_bias, (0, pad)).reshape(1, LANES)
    alog = jnp.pad(a_log, (0, pad)).reshape(1, LANES)
    dsk = jnp.repeat(d_skip, SSM_HEAD_DIM).reshape(1, d_inner)
    tri = jnp.tril(jnp.ones((L, L), BF16))
    expand = jnp.repeat(jnp.eye(LANES, dtype=BF16)[:, :n_heads], SSM_HEAD_DIM, axis=1)
    kern = functools.partial(_ssd_kernel, t_real=t_real, n_heads=n_heads, d_inner=d_inner)
    vec = lambda n: pl.BlockSpec((1, n), lambda b, c: (0, 0))
    nc = tp // L
    side = _SideCast(cast, bt * nc, lambda b, c: b * nc + c)
    outs = pl.pallas_call(
        side.wrap(kern, 13, 3),
        grid=(bt, nc),
        in_specs=[pl.BlockSpec((None, L, conv_dim), lambda b, c: (b, c, col_xbc // conv_dim)),
                  pl.BlockSpec((None, L, LANES), lambda b, c: (b, c, col_dt // LANES)),
                  pl.BlockSpec((None, L, d_inner), lambda b, c: (b, c, col_z // d_inner)),
                  pl.BlockSpec((CONV_W, conv_dim), lambda b, c: (0, 0)),
                  vec(conv_dim), vec(LANES), vec(LANES), vec(d_inner), vec(d_inner),
                  pl.BlockSpec((L, L), lambda b, c: (0, 0)),
                  pl.BlockSpec((LANES, d_inner), lambda b, c: (0, 0)),
                  pl.BlockSpec((None, CONV_W - 1, conv_dim), lambda b, c: (b, 0, 0)),
                  pl.BlockSpec((None, SSM_GROUPS, gw, SSM_D_STATE), lambda b, c: (b, 0, 0, 0))] + side.in_specs,
        out_specs=[pl.BlockSpec((None, L, d_inner), lambda b, c: (b, c, 0)),
                   pl.BlockSpec((None, CONV_W - 1, conv_dim), lambda b, c: (b, 0, 0)),
                   pl.BlockSpec((None, SSM_GROUPS, gw, SSM_D_STATE), lambda b, c: (b, 0, 0, 0))] + side.out_specs,
        out_shape=[jax.ShapeDtypeStruct((bt, tp, d_inner), BF16),
                   jax.ShapeDtypeStruct((bt, CONV_W - 1, conv_dim), F32),
                   jax.ShapeDtypeStruct((bt, SSM_GROUPS, gw, SSM_D_STATE), F32)] + side.out_shape,
        scratch_shapes=[pltpu.VMEM((SUBLANES + L, conv_dim), F32),
                        pltpu.VMEM((SSM_GROUPS, SSM_D_STATE, gw), F32),
                        pltpu.VMEM((L, gw), F32),
                        pltpu.VMEM((SSM_D_STATE, gw), F32)],
        compiler_params=_cparams("parallel", "arbitrary"),
        name="ssd_mixer",
    )(proj, proj, proj, conv_w, conv_b.reshape(1, conv_dim), dtb, alog, dsk, gate_g.reshape(1, d_inner),
      tri, expand, conv_prev, h_prev, *side.views)
    return side.split(outs, 3)


def _mem_attn_kernel(q_ref, k_ref, v_ref, o_ref):
    scale = HEAD_DIM ** -0.5
    headed = len(k_ref.shape) == 3
    for h in range(MEM_HEADS):
        sl = slice(h * HEAD_DIM, (h + 1) * HEAD_DIM)
        q = q_ref[:, sl].astype(BF16)
        k = (k_ref[:, h, :] if headed else k_ref[:, sl]).astype(BF16)
        v = (v_ref[:, h, :] if headed else v_ref[:, sl]).astype(BF16)
        s = lax.dot_general(q, k, NT_DIMS, preferred_element_type=F32) * scale
        e = jnp.exp(s - jnp.max(s, axis=-1, keepdims=True))
        den = jnp.sum(e, axis=-1, keepdims=True)
        o = jnp.dot(e.astype(BF16), v, preferred_element_type=F32) / den
        o_ref[:, sl] = o.astype(o_ref.dtype)


def mem_attention(proj, col_q, k, v, kv_at):
    bt, t, _ = proj.shape
    width = MEM_HEADS * HEAD_DIM
    tq = min(t, ROW_TILE)
    if k.ndim == 5:
        mlen = k.shape[2]
        kv_specs = [pl.BlockSpec((None, None, mlen, MEM_HEADS, HEAD_DIM), lambda b, i: (kv_at, b, 0, 0, 0))] * 2
    else:
        mlen = k.shape[1]
        kv_specs = [pl.BlockSpec((None, mlen, width), lambda b, i, c=c: (b, 0, c // width)) for c in kv_at]
    return pl.pallas_call(
        _mem_attn_kernel,
        grid=(bt, t // tq),
        in_specs=[pl.BlockSpec((None, tq, width), lambda b, i: (b, i, col_q // width))] + kv_specs,
        out_specs=pl.BlockSpec((None, tq, width), lambda b, i: (b, i, 0)),
        out_shape=jax.ShapeDtypeStruct((bt, t, width), BF16),
        compiler_params=_cparams("parallel", "parallel"),
        name="mem_attention",
    )(proj, k, v)


def _run_units(first, count, unit, unroll):
    trips = count // unroll
    if trips == 1:
        trips = 0
    if trips:
        def trip(it, carry):
            for k in range(unroll):
                unit(first + it * unroll + k)
            return carry
        lax.fori_loop(0, trips, trip, 0)
    for k in range(trips * unroll, count):
        unit(first + k)


def _dil_attn_kernel(*refs, groups, rb, unroll):
    ng = len(groups)
    q_refs = refs[:ng]
    kp_ref, kc_ref, vp_ref, vc_ref, o_ref = refs[ng:ng + 5]
    og = refs[ng + 5:2 * ng + 5]
    lg = refs[2 * ng + 5:3 * ng + 5]
    i = pl.program_id(1)
    scale = HEAD_DIM ** -0.5

    for gi, (win, d) in enumerate(groups):
        w = win // d
        dw = d * w
        qi = lax.broadcasted_iota(jnp.int32, (w, 2 * w), 0)
        kj = lax.broadcasted_iota(jnp.int32, (w, 2 * w), 1)
        band = (kj >= qi) & (kj <= qi + w)
        band_first = band & (kj >= jnp.where(i > 0, 0, w))

        def rows(start, size, d=d, w=w):
            if d == 1:
                return pl.ds(pl.multiple_of(start, w), size)
            return pl.ds(start, size, stride=d)

        def attend(rows_q, kk, vv, mask, gi=gi, w=w):
            q = (q_refs[gi][rows_q, :] * (scale * LOG2_E)).astype(BF16)
            s = lax.dot_general(q, kk.astype(BF16), NT_DIMS, preferred_element_type=F32)
            s = jnp.where(mask, s, -jnp.inf)
            m = jnp.max(s, axis=-1, keepdims=True)
            e = jnp.exp2(s - m)
            den = jnp.sum(e, axis=-1, keepdims=True)
            o = jnp.dot(e.astype(BF16), vv.astype(BF16), preferred_element_type=F32) / den
            og[gi][rows_q, :] = o
            lg[gi][rows_q, :] = jnp.broadcast_to(m * LN_2 + jnp.log(den), (w, HEAD_DIM))

        def first_unit(r, rows=rows, attend=attend, w=w, dw=dw, mask=band_first):
            rows_q, rows_p = rows(r, w), rows(rb - dw + r, w)
            kk = jnp.concatenate([kp_ref[rows_p, :], kc_ref[rows_q, :]], axis=0)
            vv = jnp.concatenate([vp_ref[rows_p, :], vc_ref[rows_q, :]], axis=0)
            attend(rows_q, kk, vv, mask)

        def later_unit(u, rows=rows, attend=attend, d=d, w=w, dw=dw, mask=band):
            sb = u // d
            start_q = sb * dw + (u - sb * d)
            rows_k = rows(start_q - dw, 2 * w)
            attend(rows(start_q, w), kc_ref[rows_k, :], vc_ref[rows_k, :], mask)

        _run_units(0, d, first_unit, unroll)
        _run_units(d, rb // w - d, later_unit, unroll)

    ls = [r[...] for r in lg]
    mm = functools.reduce(jnp.maximum, ls)
    ws = [jnp.exp(l - mm) for l in ls]
    num = sum(wg * r[...] for wg, r in zip(ws, og))
    o_ref[...] = (num / sum(ws)).astype(o_ref.dtype)


def dilated_attention(proj, kv):
    bt, t, _ = proj.shape
    ng = len(DIL_GROUPS)
    rb = min(t, max(win for win, _ in DIL_GROUPS))
    for win, d in DIL_GROUPS:
        assert win % d == 0 and rb % win == 0
    assert t % rb == 0
    blk = lambda f: pl.BlockSpec((None, rb, HEAD_DIM), f)
    prev = lambda i: jnp.maximum(i - 1, 0)
    q_specs = [blk(lambda b, i, h, g=g: (b, i, g * DIL_HEADS + h)) for g in range(ng)]
    kv_specs = [blk(lambda b, i, h: (b, prev(i), h)), blk(lambda b, i, h: (b, i, h)),
                blk(lambda b, i, h: (b, prev(i), DIL_HEADS + h)), blk(lambda b, i, h: (b, i, DIL_HEADS + h))]
    return pl.pallas_call(
        functools.partial(_dil_attn_kernel, groups=DIL_GROUPS, rb=rb, unroll=DIL_UNROLL),
        grid=(bt, t // rb, DIL_HEADS),
        in_specs=q_specs + kv_specs,
        out_specs=blk(lambda b, i, h: (b, i, h)),
        out_shape=jax.ShapeDtypeStruct((bt, t, DIL_HEADS * HEAD_DIM), BF16),
        scratch_shapes=[pltpu.VMEM((rb, HEAD_DIM), F32)] * (2 * ng),
        compiler_params=_cparams("parallel", "parallel", "parallel"),
        name="dilated_attention",
    )(*([proj] * ng), kv, kv, kv, kv)


def _dil_decode_kernel(*refs):
    ng = len(DIL_GROUPS)
    q_ref, kv_ref = refs[:2]
    k_refs, v_refs = refs[2:2 + ng], refs[2 + ng:2 + 2 * ng]
    o_ref = refs[-1]
    width = DIL_HEADS * HEAD_DIM
    scale = HEAD_DIM ** -0.5
    for h in range(DIL_HEADS):
        sl = slice(h * HEAD_DIM, (h + 1) * HEAD_DIM)
        k_new = kv_ref[:, h * HEAD_DIM:(h + 1) * HEAD_DIM]
        v_new = kv_ref[:, width + h * HEAD_DIM:width + (h + 1) * HEAD_DIM]
        outs, lses = [], []
        for gi in range(ng):
            kc = k_refs[gi][:, h, :].astype(BF16)
            vc = v_refs[gi][:, h, :].astype(BF16)
            q = q_ref[:, gi * width + h * HEAD_DIM:gi * width + (h + 1) * HEAD_DIM]
            s = lax.dot_general(q.astype(BF16), kc, NT_DIMS, preferred_element_type=F32) * scale
            s_self = jnp.sum(q * k_new, axis=-1, keepdims=True) * scale
            m = jnp.maximum(jnp.max(s, axis=-1, keepdims=True), s_self)
            e = jnp.exp(s - m)
            e_self = jnp.exp(s_self - m)
            den = jnp.sum(e, axis=-1, keepdims=True) + e_self
            o = jnp.dot(e.astype(BF16), vc, preferred_element_type=F32) + e_self * v_new
            outs.append(o / den)
            lses.append(m + jnp.log(den))
        mm = functools.reduce(jnp.maximum, lses)
        ws = [jnp.exp(l - mm) for l in lses]
        num = sum(wg * o for wg, o in zip(ws, outs))
        o_ref[:, sl] = (num / sum(ws)).astype(o_ref.dtype)


def dilated_decode(proj, kv_new, k_cache, v_cache):
    bt, rows, nq = proj.shape
    lc = k_cache.shape[1]
    width = DIL_HEADS * HEAD_DIM
    views, specs = [], []
    for cache in (k_cache, v_cache):
        for win, dil in DIL_GROUPS:
            assert lc % win == 0 and win % dil == 0
            w = win // dil
            views.append(cache.reshape(bt, lc // dil, dil, DIL_HEADS, HEAD_DIM))
            specs.append(pl.BlockSpec((None, w, None, DIL_HEADS, HEAD_DIM),
                                      lambda b, last=lc // win - 1: (b, last, 0, 0, 0)))
    return pl.pallas_call(
        _dil_decode_kernel,
        grid=(bt,),
        in_specs=[pl.BlockSpec((None, rows, nq), lambda b: (b, 0, 0)),
                  pl.BlockSpec((None, rows, 2 * width), lambda b: (b, 0, 0))] + specs,
        out_specs=pl.BlockSpec((None, rows, width), lambda b: (b, 0, 0)),
        out_shape=jax.ShapeDtypeStruct((bt, rows, width), BF16),
        compiler_params=_cparams("parallel"),
        name="dilated_decode",
    )(proj, kv_new, *views)


def kernel(x_prompt, x_sample, state_conv, state_ssm, cache_win_k, cache_win_v, cache_mem_k, cache_mem_v,
           mem_prompt, norm_mix_g, norm_mlp_g, norm_mem_g, w_mem_k, w_mem_v, mem_q_norm_g, mem_k_norm_g,
           w_up, w_down, w_in_a, conv_w, conv_b, dt_bias, a_log, d_skip, gate_norm_g, w_out_a,
           w_in_b, q_norm_g, w_out_b, kv_norm_g, w_k_shared, w_v_shared, k_norm_g):
    depth = w_up.shape[0]
    n_a = w_in_a.shape[0]
    d_model = x_prompt.shape[-1]
    n_heads = dt_bias.shape[1]
    d_inner = n_heads * SSM_HEAD_DIM
    conv_dim = conv_w.shape[2]
    mem_w = MEM_HEADS * HEAD_DIM
    kv_w = DIL_HEADS * HEAD_DIM
    dil_q_w = len(DIL_GROUPS) * kv_w

    s1, s2, s3 = d_inner, d_inner + conv_dim, d_inner + conv_dim + n_heads
    assert s1 % COL_TILE == 0 and s2 % COL_TILE == 0
    w_a_main = round_weight(w_in_a, s2)
    tail = lax.optimization_barrier(w_in_a[..., s2:]).astype(BF16)
    w_a_tail = jnp.concatenate([tail[..., :n_heads], jnp.zeros((n_a, d_model, COL_TILE - n_heads), BF16),
                                tail[..., n_heads:]], axis=-1)
    tiles_a = ([(0, s1 + c) for c in range(0, conv_dim, COL_TILE)] + [(1, 0)]
               + [(0, c) for c in range(0, d_inner, COL_TILE)]
               + [(1, COL_TILE + c) for c in range(0, mem_w, COL_TILE)])
    col_xbc, col_dt, col_z = 0, conv_dim, conv_dim + COL_TILE
    col_qm_a = col_z + d_inner
    kinds_a = ["plain"] * (col_qm_a // COL_TILE) + ["norm"] * (mem_w // COL_TILE)
    kinds_b = ["rope"] * (dil_q_w // COL_TILE) + ["norm"] * (mem_w // COL_TILE)
    kinds_kv = ["rope"] * (kv_w // COL_TILE) + ["plain"] * (kv_w // COL_TILE)
    kinds_mkv = ["norm"] * (mem_w // COL_TILE) + ["plain"] * (mem_w // COL_TILE)
    w_b = w_in_b.astype(BF16)
    w_kv = jnp.concatenate([w_k_shared, w_v_shared], axis=-1).astype(BF16)[None]
    w_mkv = jnp.concatenate([w_mem_k, w_mem_v], axis=-1).astype(BF16)
    w_out_a_b, w_out_b_b = w_out_a.astype(BF16), w_out_b.astype(BF16)
    assert n_a >= 1
    w_up_b, w_down_b = [None] * depth, [None] * depth

    def gain_row(width, pieces):
        parts, at = [], 0
        for start, g, reps in pieces:
            parts += [jnp.ones((start - at,), F32), jnp.tile(g, reps)]
            at = start + reps * HEAD_DIM
        parts.append(jnp.ones((width - at,), F32))
        return jnp.concatenate(parts).reshape(1, width)

    def run(x, pos_rows, conv_prev, ssm_prev, mem_kv, k_past, v_past, t_real, round_mlp_weights):
        bt, t, _ = x.shape
        m = bt * t
        x2 = x.reshape(m, d_model)
        cos, sin = rope_tables(pos_rows)
        conv_new, ssm_new = [], []

        def run_mlp(x2, l):
            if round_mlp_weights and l + 1 < depth:
                x2, (w_up_b[l + 1], w_down_b[l + 1]) = mlp(x2, norm_mlp_g[l], w_up_b[l], w_down_b[l],
                                                           cast=((w_up, l + 1), (w_down, l + 1)))
                return x2
            return mlp(x2, norm_mlp_g[l], w_up_b[l], w_down_b[l])

        for l in range(n_a):
            first = round_mlp_weights and l == 0
            hg = gain_row(col_qm_a + mem_w, [(col_qm_a, mem_q_norm_g[l], MEM_HEADS)])
            proj = norm_proj(x2, norm_mix_g[l], [w_a_main, w_a_tail], l, hg, kinds_a, tile_src=tiles_a,
                             cast=((w_up, 0),) if first else ())
            if first:
                proj, (w_up_b[0],) = proj
            proj3 = proj.reshape(bt, t, -1)
            tp = -(-t // SSD_CHUNK) * SSD_CHUNK
            proj_ssd = proj3 if tp == t else jnp.pad(proj3, ((0, 0), (0, tp - t), (0, 0)))
            ssd = ssd_mixer(proj_ssd, col_xbc, col_dt, col_z, t_real, conv_w[l], conv_b[l],
                            dt_bias[l], a_log[l], d_skip[l], gate_norm_g[l],
                            conv_prev[l], ssm_prev[l].reshape(bt, SSM_GROUPS, -1, SSM_D_STATE),
                            cast=((w_down, 0),) if first else ())
            if first:
                ssd, (w_down_b[0],) = ssd
            y, c_new, h_new = ssd
            y = y[:, :t].reshape(m, d_inner)
            mo = mem_attention(proj3, col_qm_a, *mem_kv(l)).reshape(m, mem_w)
            x2 = out_proj(x2, y, mo, w_out_a_b, l)
            x2 = run_mlp(x2, l)
            conv_new.append(c_new)
            ssm_new.append(h_new.reshape(bt, n_heads, SSM_HEAD_DIM, SSM_D_STATE))
        hg = gain_row(2 * kv_w, [(0, k_norm_g, DIL_HEADS)])
        kv3 = norm_proj(x2, kv_norm_g, w_kv, 0, hg, kinds_kv, cos, sin).reshape(bt, t, 2 * kv_w)
        for l in range(n_a, depth):
            j = l - n_a
            hg = gain_row(dil_q_w + mem_w, [(0, q_norm_g[j], dil_q_w // HEAD_DIM),
                                            (dil_q_w, mem_q_norm_g[l], MEM_HEADS)])
            proj3 = norm_proj(x2, norm_mix_g[l], w_b, j, hg, kinds_b, cos, sin).reshape(bt, t, -1)
            if k_past is None:
                att = dilated_attention(proj3, kv3)
            else:
                att = dilated_decode(proj3, kv3, k_past, v_past)
            mo = mem_attention(proj3, dil_q_w, *mem_kv(l)).reshape(m, mem_w)
            x2 = out_proj(x2, att.reshape(m, kv_w), mo, w_out_b_b, j)
            x2 = run_mlp(x2, l)
        return x2.reshape(bt, t, d_model), jnp.stack(conv_new), jnp.stack(ssm_new), kv3

    bp, t_p, _ = x_prompt.shape
    mlen = mem_prompt.shape[1]
    mem2 = mem_prompt.reshape(bp * mlen, d_model)
    mkv_p = []
    for l in range(depth):
        hg = gain_row(2 * mem_w, [(0, mem_k_norm_g[l], MEM_HEADS)])
        mkv_p.append(norm_proj(mem2, norm_mem_g[l], w_mkv, l, hg, kinds_mkv).reshape(bp, mlen, 2 * mem_w))
    conv0 = jnp.zeros((n_a, bp, CONV_W - 1, conv_dim), F32)
    ssm0 = jnp.zeros((n_a, bp, n_heads, SSM_HEAD_DIM, SSM_D_STATE), F32)
    y_p, conv_p, ssm_p, kv_p = run(x_prompt, jnp.arange(t_p, dtype=jnp.int32), conv0, ssm0,
                                   lambda l: (mkv_p[l], mkv_p[l], (0, mem_w)), None, None, t_p, True)
    mkv_all = jnp.stack(mkv_p)
    mem_k_p = mkv_all[..., :mem_w].reshape(depth, bp, mlen, MEM_HEADS, HEAD_DIM)
    mem_v_p = mkv_all[..., mem_w:].reshape(depth, bp, mlen, MEM_HEADS, HEAD_DIM)
    keep = min(max(w for w, _ in DIL_GROUPS), t_p)
    win_k_p = kv_p[:, t_p - keep:, :kv_w].reshape(bp, keep, DIL_HEADS, HEAD_DIM)
    win_v_p = kv_p[:, t_p - keep:, kv_w:].reshape(bp, keep, DIL_HEADS, HEAD_DIM)

    bs, t_s, _ = x_sample.shape
    assert t_s == 1
    xs = jnp.pad(x_sample, ((0, 0), (0, SAMPLE_ROWS - t_s), (0, 0)))
    pos_s = jnp.full((bs * SAMPLE_ROWS,), PAST_LEN, jnp.int32)
    y_s, conv_s, ssm_s, kv_s = run(xs, pos_s, state_conv, state_ssm,
                                   lambda l: (cache_mem_k, cache_mem_v, l), cache_win_k, cache_win_v, t_s, False)
    y_s = y_s[:, :t_s]
    k_s = kv_s[:, :t_s, :kv_w].reshape(bs, t_s, DIL_HEADS, HEAD_DIM)
    v_s = kv_s[:, :t_s, kv_w:].reshape(bs, t_s, DIL_HEADS, HEAD_DIM)

    return (y_p, y_s, conv_p, ssm_p, win_k_p, win_v_p, mem_k_p, mem_v_p, conv_s, ssm_s, k_s, v_s)
```

```python
import functools

import jax
import jax.numpy as jnp
from jax import lax
from jax.experimental import pallas as pl
from jax.experimental.pallas import tpu as pltpu

F32 = jnp.float32
BF16 = jnp.bfloat16

HEAD_DIM = 128
SSM_HEAD_DIM = 64
SSM_GROUPS = 4
SSM_D_STATE = 128
CONV_W = 4
SSD_CHUNK = 128
MEM_HEADS = 4
DIL_GROUPS = ((128, 1), (512, 4), (2048, 16))
DIL_HEADS = 4
PAST_LEN = 16384
ROPE_THETA = 10000.0
EPS = 1e-6
LOG2_E = 1.4426950408889634
LN_2 = 0.6931471805599453

LANES = 128
SUBLANES = 8
VMEM_LIMIT = 48 * 1024 * 1024
COL_TILE = 512
ROW_TILE = 512
PROJ_OUT_BLOCK_BYTES = 5 * 1024 * 1024
SAMPLE_ROWS = SUBLANES
DIL_UNROLL = 16

NT_DIMS = (((1,), (1,)), ((), ()))
RESIDENT = pl.Buffered(1)


def _cparams(*sem):
    return pltpu.CompilerParams(dimension_semantics=sem, vmem_limit_bytes=VMEM_LIMIT)


def _sigmoid(x):
    return 1.0 / (1.0 + jnp.exp(-x))


def _rms_scale(x):
    return lax.rsqrt(jnp.mean(x * x, axis=-1, keepdims=True) + EPS)


class _SideCast:
    def __init__(self, mats, steps, step_of):
        self.n = len(mats)
        self.shapes = [a.shape[1:] for a, _ in mats]
        self.views, self.in_specs, self.out_specs, self.out_shape = [], [], [], []
        for a, layer in mats:
            n_layers, r, c = a.shape
            assert r % (steps * 2 * SUBLANES) == 0
            band = r // steps
            self.views.append(a.reshape(n_layers, steps, band, c))
            self.in_specs.append(pl.BlockSpec((None, None, band, c),
                                              lambda *ids, layer=layer: (layer, step_of(*ids), 0, 0)))
            self.out_specs.append(pl.BlockSpec((None, band, c), lambda *ids: (step_of(*ids), 0, 0)))
            self.out_shape.append(jax.ShapeDtypeStruct((steps, band, c), BF16))

    def wrap(self, body, n_in, n_out):
        n = self.n
        if not n:
            return body

        def kern(*refs):
            ins, srcs = refs[:n_in], refs[n_in:n_in + n]
            outs, dsts = refs[n_in + n:n_in + n + n_out], refs[n_in + n + n_out:n_in + 2 * n + n_out]
            for s, d in zip(srcs, dsts):
                d[...] = s[...].astype(BF16)
            body(*ins, *outs, *refs[n_in + 2 * n + n_out:])
        return kern

    def split(self, outs, n_out):
        main = outs[0] if n_out == 1 else tuple(outs[:n_out])
        if not self.n:
            return main
        return main, [o.reshape(s) for o, s in zip(outs[n_out:], self.shapes)]


def _round_kernel(src_ref, dst_ref):
    dst_ref[...] = src_ref[...].astype(BF16)


def round_weight(w, n_rows):
    n_layers, _, c = w.shape
    band = ROW_TILE
    assert n_rows % band == 0
    return pl.pallas_call(
        _round_kernel,
        grid=(n_layers, n_rows // band),
        in_specs=[pl.BlockSpec((None, band, c), lambda l, i: (l, i, 0))],
        out_specs=pl.BlockSpec((None, band, c), lambda l, i: (l, i, 0)),
        out_shape=jax.ShapeDtypeStruct((n_layers, n_rows, c), BF16),
        compiler_params=_cparams("parallel", "parallel"),
        name="round_weight",
    )(w)


def _rope_table_kernel(pos_ref, inv_ref, cos_ref, sin_ref):
    ang = pos_ref[...] * inv_ref[...]
    lane = lax.broadcasted_iota(jnp.int32, ang.shape, 1)
    cos_ref[...] = jnp.cos(ang)
    s = jnp.sin(ang)
    sin_ref[...] = jnp.where(lane < HEAD_DIM // 2, -s, s)


def rope_tables(pos):
    r = pos.shape[0]
    half = HEAD_DIM // 2
    inv = ROPE_THETA ** (-jnp.arange(half, dtype=F32) * (2.0 / HEAD_DIM))
    inv = jnp.concatenate([inv, inv])[None, :]
    posf = jnp.broadcast_to(pos.astype(F32)[:, None], (r, HEAD_DIM))
    tr = min(r, ROW_TILE)
    return pl.pallas_call(
        _rope_table_kernel,
        grid=(r // tr,),
        in_specs=[pl.BlockSpec((tr, HEAD_DIM), lambda i: (i, 0)),
                  pl.BlockSpec((1, HEAD_DIM), lambda i: (0, 0))],
        out_specs=[pl.BlockSpec((tr, HEAD_DIM), lambda i: (i, 0))] * 2,
        out_shape=[jax.ShapeDtypeStruct((r, HEAD_DIM), F32)] * 2,
        compiler_params=_cparams("parallel"),
        name="rope_tables",
    )(posf, inv)


def _proj_kernel(*refs, kinds, tile_src, n_w, transposed):
    x_ref, g_ref = refs[:2]
    w_refs = refs[2:2 + n_w]
    hg_ref = refs[2 + n_w]
    if "rope" in kinds:
        cos_ref, sin_ref = refs[3 + n_w:5 + n_w]
    o_ref, xn_ref = refs[-2:]
    x = x_ref[...]
    xn_ref[...] = (x * _rms_scale(x) * g_ref[...]).astype(BF16)
    for t, kind in enumerate(kinds):
        which, col = tile_src[t]
        cols = slice(t * COL_TILE, (t + 1) * COL_TILE)
        if transposed:
            y = lax.dot_general(xn_ref[...], w_refs[which][col:col + COL_TILE, :], NT_DIMS,
                                preferred_element_type=F32)
        else:
            y = jnp.dot(xn_ref[...], w_refs[which][:, col:col + COL_TILE], preferred_element_type=F32)
        if kind == "plain":
            o_ref[:, cols] = y
            continue
        for j in range(COL_TILE // HEAD_DIM):
            sl = slice(j * HEAD_DIM, (j + 1) * HEAD_DIM)
            osl = slice(t * COL_TILE + j * HEAD_DIM, t * COL_TILE + (j + 1) * HEAD_DIM)
            yh = y[:, sl]
            yh = yh * _rms_scale(yh) * hg_ref[:, osl]
            if kind == "rope":
                yh = yh * cos_ref[...] + pltpu.roll(yh, HEAD_DIM // 2, axis=1) * sin_ref[...]
            o_ref[:, osl] = yh


def norm_proj(x, g, ws, layer, head_gain, kinds, cos=None, sin=None, cast=(), tile_src=None, transposed=False):
    m, d = x.shape
    ws = list(ws) if isinstance(ws, (list, tuple)) else [ws]
    if tile_src is None:
        tile_src = [(0, t * COL_TILE) for t in range(len(kinds))]
    n_cols = COL_TILE * len(kinds)
    assert len(tile_src) == len(kinds) and head_gain.shape == (1, n_cols)
    tm = min(m, ROW_TILE)
    while tm * n_cols * 4 > PROJ_OUT_BLOCK_BYTES and tm % (2 * SUBLANES) == 0:
        tm //= 2
    assert m % tm == 0
    in_specs = [pl.BlockSpec((tm, d), lambda i: (i, 0)),
                pl.BlockSpec((1, d), lambda i: (0, 0))]
    for k, w in enumerate(ws):
        used = max(col + COL_TILE for which, col in tile_src if which == k)
        if transposed:
            assert used <= w.shape[1] and w.shape[2] == d
            in_specs.append(pl.BlockSpec((None, used, d), lambda i: (layer, 0, 0), pipeline_mode=RESIDENT))
            continue
        assert used <= w.shape[2] and (used % LANES == 0 or used == w.shape[2])
        in_specs.append(pl.BlockSpec((None, d, used), lambda i: (layer, 0, 0), pipeline_mode=RESIDENT))
    in_specs.append(pl.BlockSpec((1, n_cols), lambda i: (0, 0)))
    args = [x, g.reshape(1, d), *ws, head_gain]
    if "rope" in kinds:
        assert cos.shape[0] % tm == 0
        tab_blocks = cos.shape[0] // tm
        in_specs += [pl.BlockSpec((tm, HEAD_DIM), lambda i: (i % tab_blocks, 0))] * 2
        args += [cos, sin]
    side = _SideCast(cast, m // tm, lambda i: i)
    outs = pl.pallas_call(
        side.wrap(functools.partial(_proj_kernel, kinds=tuple(kinds), tile_src=tuple(tile_src), n_w=len(ws),
                                    transposed=transposed),
                  len(args), 1),
        grid=(m // tm,),
        in_specs=in_specs + side.in_specs,
        out_specs=[pl.BlockSpec((tm, n_cols), lambda i: (i, 0))] + side.out_specs,
        out_shape=[jax.ShapeDtypeStruct((m, n_cols), F32)] + side.out_shape,
        scratch_shapes=[pltpu.VMEM((tm, d), BF16)],
        compiler_params=_cparams("parallel"),
        name="norm_proj",
    )(*args, *side.views)
    return side.split(outs, 1)


def _out_proj_kernel(x_ref, a_ref, b_ref, wa_ref, wb_ref, o_ref):
    acc = jnp.dot(a_ref[...], wa_ref[...], preferred_element_type=F32)
    acc = acc + jnp.dot(b_ref[...], wb_ref[...], preferred_element_type=F32)
    o_ref[...] = x_ref[...] + acc


def out_proj(x, a, b, w, layer):
    m, d = x.shape
    tm = min(m, ROW_TILE)
    ka, kb = a.shape[1], b.shape[1]
    assert w.shape[1] == ka + kb and ka % kb == 0
    return pl.pallas_call(
        _out_proj_kernel,
        grid=(m // tm,),
        in_specs=[pl.BlockSpec((tm, d), lambda i: (i, 0)),
                  pl.BlockSpec((tm, ka), lambda i: (i, 0)),
                  pl.BlockSpec((tm, kb), lambda i: (i, 0)),
                  pl.BlockSpec((None, ka, d), lambda i: (layer, 0, 0), pipeline_mode=RESIDENT),
                  pl.BlockSpec((None, kb, d), lambda i: (layer, ka // kb, 0), pipeline_mode=RESIDENT)],
        out_specs=pl.BlockSpec((tm, d), lambda i: (i, 0)),
        out_shape=jax.ShapeDtypeStruct((m, d), F32),
        compiler_params=_cparams("parallel"),
        name="out_proj",
    )(x, a, b, w, w)


def _mlp_kernel(x_ref, g_ref, wu_ref, wd_ref, o_ref, xn_ref):
    f = pl.program_id(1)

    @pl.when(f == 0)
    def _():
        x = x_ref[...]
        xn_ref[...] = (x * _rms_scale(x) * g_ref[...]).astype(BF16)
        o_ref[...] = x

    h = jnp.dot(xn_ref[...], wu_ref[...], preferred_element_type=F32)
    h = jnp.maximum(h, 0.0)
    h = (h * h).astype(BF16)
    o_ref[...] += jnp.dot(h, wd_ref[...], preferred_element_type=F32)


def mlp(x, g, wu, wd, tf=1024, cast=()):
    m, d = x.shape
    ff = wu.shape[1]
    tm = min(m, ROW_TILE)
    nf = ff // tf
    side = _SideCast(cast, (m // tm) * nf, lambda i, j: i * nf + j)
    outs = pl.pallas_call(
        side.wrap(_mlp_kernel, 4, 1),
        grid=(m // tm, nf),
        in_specs=[pl.BlockSpec((tm, d), lambda i, j: (i, 0)),
                  pl.BlockSpec((1, d), lambda i, j: (0, 0)),
                  pl.BlockSpec((d, tf), lambda i, j: (0, j)),
                  pl.BlockSpec((tf, d), lambda i, j: (j, 0))] + side.in_specs,
        out_specs=[pl.BlockSpec((tm, d), lambda i, j: (i, 0))] + side.out_specs,
        out_shape=[jax.ShapeDtypeStruct((m, d), F32)] + side.out_shape,
        scratch_shapes=[pltpu.VMEM((tm, d), BF16)],
        compiler_params=_cparams("parallel", "arbitrary"),
        name="mlp",
    )(x, g.reshape(1, d), wu, wd, *side.views)
    return side.split(outs, 1)


def _split3(x):
    hi = x.astype(BF16)
    r1 = x - hi.astype(F32)
    mid = r1.astype(BF16)
    lo = (r1 - mid.astype(F32)).astype(BF16)
    return hi, mid, lo


def _ssd_kernel(xbc_ref, dt_ref, z_ref, cw_ref, cb_ref, dtb_ref, alog_ref, dsk_ref, gg_ref, tri_ref, expand_ref,
                conv0_ref, h0_ref, y_ref, convo_ref, ho_ref,
                ext_ref, ht_ref, yd_ref, st_ref, *, t_real, n_heads, d_inner):
    L = SSD_CHUNK
    P = SSM_HEAD_DIM
    N = SSM_D_STATE
    G = SSM_GROUPS
    hpg = n_heads // G
    gw = hpg * P
    hist = CONV_W - 1
    base = SUBLANES - hist
    c = pl.program_id(1)
    nc = pl.num_programs(1)

    @pl.when(c == 0)
    def _():
        ext_ref[base:SUBLANES, :] = conv0_ref[...]
        for g in range(G):
            ht_ref[g] = h0_ref[g].T

    ext_ref[SUBLANES:SUBLANES + L, :] = xbc_ref[...]
    ext = ext_ref[...]
    acc = cb_ref[...] + ext[SUBLANES:, :] * cw_ref[hist:CONV_W, :]
    for k in range(1, CONV_W):
        acc = acc + pltpu.roll(ext, k, axis=0)[SUBLANES:, :] * cw_ref[hist - k:CONV_W - k, :]
    xc = acc * _sigmoid(acc)

    @pl.when(c == nc - 1)
    def _():
        n_in_last = t_real - ((t_real - 1) // L) * L
        convo_ref[...] = ext_ref[base + n_in_last:base + n_in_last + hist, :]

    ext_ref[base:SUBLANES, :] = ext_ref[base + L:SUBLANES + L, :]

    dt_raw = dt_ref[...] + dtb_ref[...]
    dt = jnp.maximum(dt_raw, 0.0) + jnp.log1p(jnp.exp(-jnp.abs(dt_raw)))
    row = lax.broadcasted_iota(jnp.int32, (L, LANES), 0) + c * L
    dt = jnp.where(row < t_real, dt, 0.0)
    da = dt * (-jnp.exp(alog_ref[...]))
    li = lax.broadcasted_iota(jnp.int32, (L, L), 0)
    si = lax.broadcasted_iota(jnp.int32, (L, L), 1)
    causal = li >= si
    a_cs = sum(jnp.dot(tri_ref[...], part, preferred_element_type=F32) for part in _split3(da))
    a_cs_t = a_cs.T
    dt_t = dt.T
    w_t = dt_t * jnp.exp(a_cs_t[:, L - 1:L] - a_cs_t)
    e_exp = sum(jnp.dot(part, expand_ref[...], preferred_element_type=F32) for part in _split3(jnp.exp(a_cs)))
    xb = xc[:, :d_inner].astype(BF16)
    low_half = lax.broadcasted_iota(jnp.int32, (L, 2 * P), 1) < P

    for g in range(G):
        b_g = xc[:, d_inner + g * N:d_inner + (g + 1) * N]
        c_g = xc[:, d_inner + G * N + g * N:d_inner + G * N + (g + 1) * N].astype(BF16)
        cbm = lax.dot_general(c_g, b_g.astype(BF16), NT_DIMS, preferred_element_type=F32)
        b_t = b_g.T
        y_off = jnp.dot(c_g, ht_ref[g].astype(BF16), preferred_element_type=F32)
        for kp in range(hpg // 2):
            h0 = g * hpg + 2 * kp
            x_pair = xb[:, h0 * P:(h0 + 2) * P]
            yd, st = [], []
            for h in (h0, h0 + 1):
                seg = a_cs[:, h:h + 1] - a_cs_t[h:h + 1, :]
                dec = jnp.exp(jnp.where(causal, seg, -jnp.inf))
                mh = (cbm * dec * dt_t[h:h + 1, :]).astype(BF16)
                yd.append(jnp.dot(mh, x_pair, preferred_element_type=F32))
                btw = (b_t * w_t[h:h + 1, :]).astype(BF16)
                st.append(jnp.dot(btw, x_pair, preferred_element_type=F32))
            yd_ref[:, 2 * kp * P:(2 * kp + 2) * P] = jnp.where(low_half, yd[0], yd[1])
            st_ref[:, 2 * kp * P:(2 * kp + 2) * P] = jnp.where(low_half, st[0], st[1])
        gs = slice(g * gw, (g + 1) * gw)
        e_g = e_exp[:, gs]
        y_g = yd_ref[...] + y_off * e_g + dsk_ref[:, gs] * xc[:, gs]
        ht_ref[g] = ht_ref[g] * e_g[L - 1:L, :] + st_ref[...]
        z_g = z_ref[:, gs]
        gated = y_g * (z_g * _sigmoid(z_g))
        y_ref[:, gs] = (gated * _rms_scale(gated) * gg_ref[:, gs]).astype(y_ref.dtype)

    @pl.when(c == nc - 1)
    def _():
        for g in range(G):
            ho_ref[g] = ht_ref[g].T


def ssd_mixer(proj, col_xbc, col_dt, col_z, t_real, conv_w, conv_b, dt_bias, a_log, d_skip, gate_g,
              conv_prev, h_prev, cast=()):
    bt, tp, _ = proj.shape
    conv_dim = conv_w.shape[1]
    n_heads = dt_bias.shape[0]
    d_inner = n_heads * SSM_HEAD_DIM
    gw = d_inner // SSM_GROUPS
    L = SSD_CHUNK
    assert tp % L == 0 and (tp - t_real) < L
    assert col_xbc % conv_dim == 0 and col_dt % LANES == 0 and col_z % d_inner == 0
    pad = LANES - n_heads
    dtb = jnp.pad(dt_bias, (0, pad)).reshape(1, LANES)
    alog = jnp.pad(a_log, (0, pad)).reshape(1, LANES)
    dsk = jnp.repeat(d_skip, SSM_HEAD_DIM).reshape(1, d_inner)
    tri = jnp.tril(jnp.ones((L, L), BF16))
    expand = jnp.repeat(jnp.eye(LANES, dtype=BF16)[:, :n_heads], SSM_HEAD_DIM, axis=1)
    kern = functools.partial(_ssd_kernel, t_real=t_real, n_heads=n_heads, d_inner=d_inner)
    vec = lambda n: pl.BlockSpec((1, n), lambda b, c: (0, 0))
    nc = tp // L
    side = _SideCast(cast, bt * nc, lambda b, c: b * nc + c)
    outs = pl.pallas_call(
        side.wrap(kern, 13, 3),
        grid=(bt, nc),
        in_specs=[pl.BlockSpec((None, L, conv_dim), lambda b, c: (b, c, col_xbc // conv_dim)),
                  pl.BlockSpec((None, L, LANES), lambda b, c: (b, c, col_dt // LANES)),
                  pl.BlockSpec((None, L, d_inner), lambda b, c: (b, c, col_z // d_inner)),
                  pl.BlockSpec((CONV_W, conv_dim), lambda b, c: (0, 0)),
                  vec(conv_dim), vec(LANES), vec(LANES), vec(d_inner), vec(d_inner),
                  pl.BlockSpec((L, L), lambda b, c: (0, 0)),
                  pl.BlockSpec((LANES, d_inner), lambda b, c: (0, 0)),
                  pl.BlockSpec((None, CONV_W - 1, conv_dim), lambda b, c: (b, 0, 0)),
                  pl.BlockSpec((None, SSM_GROUPS, gw, SSM_D_STATE), lambda b, c: (b, 0, 0, 0))] + side.in_specs,
        out_specs=[pl.BlockSpec((None, L, d_inner), lambda b, c: (b, c, 0)),
                   pl.BlockSpec((None, CONV_W - 1, conv_dim), lambda b, c: (b, 0, 0)),
                   pl.BlockSpec((None, SSM_GROUPS, gw, SSM_D_STATE), lambda b, c: (b, 0, 0, 0))] + side.out_specs,
        out_shape=[jax.ShapeDtypeStruct((bt, tp, d_inner), BF16),
                   jax.ShapeDtypeStruct((bt, CONV_W - 1, conv_dim), F32),
                   jax.ShapeDtypeStruct((bt, SSM_GROUPS, gw, SSM_D_STATE), F32)] + side.out_shape,
        scratch_shapes=[pltpu.VMEM((SUBLANES + L, conv_dim), F32),
                        pltpu.VMEM((SSM_GROUPS, SSM_D_STATE, gw), F32),
                        pltpu.VMEM((L, gw), F32),
                        pltpu.VMEM((SSM_D_STATE, gw), F32)],
        compiler_params=_cparams("parallel", "arbitrary"),
        name="ssd_mixer",
    )(proj, proj, proj, conv_w, conv_b.reshape(1, conv_dim), dtb, alog, dsk, gate_g.reshape(1, d_inner),
      tri, expand, conv_prev, h_prev, *side.views)
    return side.split(outs, 3)


def _mem_attn_kernel(q_ref, k_ref, v_ref, o_ref):
    scale = HEAD_DIM ** -0.5
    headed = len(k_ref.shape) == 3
    for h in range(MEM_HEADS):
        sl = slice(h * HEAD_DIM, (h + 1) * HEAD_DIM)
        q = q_ref[:, sl].astype(BF16)
        k = (k_ref[:, h, :] if headed else k_ref[:, sl]).astype(BF16)
        v = (v_ref[:, h, :] if headed else v_ref[:, sl]).astype(BF16)
        s = lax.dot_general(q, k, NT_DIMS, preferred_element_type=F32) * scale
        e = jnp.exp(s - jnp.max(s, axis=-1, keepdims=True))
        den = jnp.sum(e, axis=-1, keepdims=True)
        o = jnp.dot(e.astype(BF16), v, preferred_element_type=F32) / den
        o_ref[:, sl] = o.astype(o_ref.dtype)


def mem_attention(proj, col_q, k, v, kv_at):
    bt, t, _ = proj.shape
    width = MEM_HEADS * HEAD_DIM
    tq = min(t, ROW_TILE)
    if k.ndim == 5:
        mlen = k.shape[2]
        kv_specs = [pl.BlockSpec((None, None, mlen, MEM_HEADS, HEAD_DIM), lambda b, i: (kv_at, b, 0, 0, 0))] * 2
    else:
        mlen = k.shape[1]
        kv_specs = [pl.BlockSpec((None, mlen, width), lambda b, i, c=c: (b, 0, c // width)) for c in kv_at]
    return pl.pallas_call(
        _mem_attn_kernel,
        grid=(bt, t // tq),
        in_specs=[pl.BlockSpec((None, tq, width), lambda b, i: (b, i, col_q // width))] + kv_specs,
        out_specs=pl.BlockSpec((None, tq, width), lambda b, i: (b, i, 0)),
        out_shape=jax.ShapeDtypeStruct((bt, t, width), BF16),
        compiler_params=_cparams("parallel", "parallel"),
        name="mem_attention",
    )(proj, k, v)


def _run_units(first, count, unit, unroll):
    trips = count // unroll
    if trips == 1:
        trips = 0
    if trips:
        def trip(it, carry):
            for k in range(unroll):
                unit(first + it * unroll + k)
            return carry
        lax.fori_loop(0, trips, trip, 0)
    for k in range(trips * unroll, count):
        unit(first + k)


def _dil_attn_kernel(*refs, groups, rb, unroll):
    ng = len(groups)
    q_refs = refs[:ng]
    kp_ref, kc_ref, vp_ref, vc_ref, o_ref = refs[ng:ng + 5]
    og = refs[ng + 5:2 * ng + 5]
    lg = refs[2 * ng + 5:3 * ng + 5]
    i = pl.program_id(1)
    scale = HEAD_DIM ** -0.5

    for gi, (win, d) in enumerate(groups):
        w = win // d
        dw = d * w
        qi = lax.broadcasted_iota(jnp.int32, (w, 2 * w), 0)
        kj = lax.broadcasted_iota(jnp.int32, (w, 2 * w), 1)
        band = (kj >= qi) & (kj <= qi + w)
        band_first = band & (kj >= jnp.where(i > 0, 0, w))

        def rows(start, size, d=d, w=w):
            if d == 1:
                return pl.ds(pl.multiple_of(start, w), size)
            return pl.ds(start, size, stride=d)

        def attend(rows_q, kk, vv, mask, gi=gi, w=w):
            q = (q_refs[gi][rows_q, :] * (scale * LOG2_E)).astype(BF16)
            s = lax.dot_general(q, kk.astype(BF16), NT_DIMS, preferred_element_type=F32)
            s = jnp.where(mask, s, -jnp.inf)
            m = jnp.max(s, axis=-1, keepdims=True)
            e = jnp.exp2(s - m)
            den = jnp.sum(e, axis=-1, keepdims=True)
            o = jnp.dot(e.astype(BF16), vv.astype(BF16), preferred_element_type=F32) / den
            og[gi][rows_q, :] = o
            lg[gi][rows_q, :] = jnp.broadcast_to(m * LN_2 + jnp.log(den), (w, HEAD_DIM))

        def first_unit(r, rows=rows, attend=attend, w=w, dw=dw, mask=band_first):
            rows_q, rows_p = rows(r, w), rows(rb - dw + r, w)
            kk = jnp.concatenate([kp_ref[rows_p, :], kc_ref[rows_q, :]], axis=0)
            vv = jnp.concatenate([vp_ref[rows_p, :], vc_ref[rows_q, :]], axis=0)
            attend(rows_q, kk, vv, mask)

        def later_unit(u, rows=rows, attend=attend, d=d, w=w, dw=dw, mask=band):
            sb = u // d
            start_q = sb * dw + (u - sb * d)
            rows_k = rows(start_q - dw, 2 * w)
            attend(rows(start_q, w), kc_ref[rows_k, :], vc_ref[rows_k, :], mask)

        _run_units(0, d, first_unit, unroll)
        _run_units(d, rb // w - d, later_unit, unroll)

    ls = [r[...] for r in lg]
    mm = functools.reduce(jnp.maximum, ls)
    ws = [jnp.exp(l - mm) for l in ls]
    num = sum(wg * r[...] for wg, r in zip(ws, og))
    o_ref[...] = (num / sum(ws)).astype(o_ref.dtype)


def dilated_attention(proj, kv):
    bt, t, _ = proj.shape
    ng = len(DIL_GROUPS)
    rb = min(t, max(win for win, _ in DIL_GROUPS))
    for win, d in DIL_GROUPS:
        assert win % d == 0 and rb % win == 0
    assert t % rb == 0
    blk = lambda f: pl.BlockSpec((None, rb, HEAD_DIM), f)
    prev = lambda i: jnp.maximum(i - 1, 0)
    q_specs = [blk(lambda b, i, h, g=g: (b, i, g * DIL_HEADS + h)) for g in range(ng)]
    kv_specs = [blk(lambda b, i, h: (b, prev(i), h)), blk(lambda b, i, h: (b, i, h)),
                blk(lambda b, i, h: (b, prev(i), DIL_HEADS + h)), blk(lambda b, i, h: (b, i, DIL_HEADS + h))]
    return pl.pallas_call(
        functools.partial(_dil_attn_kernel, groups=DIL_GROUPS, rb=rb, unroll=DIL_UNROLL),
        grid=(bt, t // rb, DIL_HEADS),
        in_specs=q_specs + kv_specs,
        out_specs=blk(lambda b, i, h: (b, i, h)),
        out_shape=jax.ShapeDtypeStruct((bt, t, DIL_HEADS * HEAD_DIM), BF16),
        scratch_shapes=[pltpu.VMEM((rb, HEAD_DIM), F32)] * (2 * ng),
        compiler_params=_cparams("parallel", "parallel", "parallel"),
        name="dilated_attention",
    )(*([proj] * ng), kv, kv, kv, kv)


def _dil_decode_kernel(*refs):
    ng = len(DIL_GROUPS)
    q_ref, kv_ref = refs[:2]
    k_refs, v_refs = refs[2:2 + ng], refs[2 + ng:2 + 2 * ng]
    o_ref = refs[-1]
    width = DIL_HEADS * HEAD_DIM
    scale = HEAD_DIM ** -0.5
    for h in range(DIL_HEADS):
        sl = slice(h * HEAD_DIM, (h + 1) * HEAD_DIM)
        k_new = kv_ref[:, h * HEAD_DIM:(h + 1) * HEAD_DIM]
        v_new = kv_ref[:, width + h * HEAD_DIM:width + (h + 1) * HEAD_DIM]
        outs, lses = [], []
        for gi in range(ng):
            kc = k_refs[gi][:, h, :].astype(BF16)
            vc = v_refs[gi][:, h, :].astype(BF16)
            q = q_ref[:, gi * width + h * HEAD_DIM:gi * width + (h + 1) * HEAD_DIM]
            s = lax.dot_general(q.astype(BF16), kc, NT_DIMS, preferred_element_type=F32) * scale
            s_self = jnp.sum(q * k_new, axis=-1, keepdims=True) * scale
            m = jnp.maximum(jnp.max(s, axis=-1, keepdims=True), s_self)
            e = jnp.exp(s - m)
            e_self = jnp.exp(s_self - m)
            den = jnp.sum(e, axis=-1, keepdims=True) + e_self
            o = jnp.dot(e.astype(BF16), vc, preferred_element_type=F32) + e_self * v_new
            outs.append(o / den)
            lses.append(m + jnp.log(den))
        mm = functools.reduce(jnp.maximum, lses)
        ws = [jnp.exp(l - mm) for l in lses]
        num = sum(wg * o for wg, o in zip(ws, outs))
        o_ref[:, sl] = (num / sum(ws)).astype(o_ref.dtype)


def dilated_decode(proj, kv_new, k_cache, v_cache):
    bt, rows, nq = proj.shape
    lc = k_cache.shape[1]
    width = DIL_HEADS * HEAD_DIM
    views, specs = [], []
    for cache in (k_cache, v_cache):
        for win, dil in DIL_GROUPS:
            assert lc % win == 0 and win % dil == 0
            w = win // dil
            views.append(cache.reshape(bt, lc // dil, dil, DIL_HEADS, HEAD_DIM))
            specs.append(pl.BlockSpec((None, w, None, DIL_HEADS, HEAD_DIM),
                                      lambda b, last=lc // win - 1: (b, last, 0, 0, 0)))
    return pl.pallas_call(
        _dil_decode_kernel,
        grid=(bt,),
        in_specs=[pl.BlockSpec((None, rows, nq), lambda b: (b, 0, 0)),
                  pl.BlockSpec((None, rows, 2 * width), lambda b: (b, 0, 0))] + specs,
        out_specs=pl.BlockSpec((None, rows, width), lambda b: (b, 0, 0)),
        out_shape=jax.ShapeDtypeStruct((bt, rows, width), BF16),
        compiler_params=_cparams("parallel"),
        name="dilated_decode",
    )(proj, kv_new, *views)


def kernel(x_prompt, x_sample, state_conv, state_ssm, cache_win_k, cache_win_v, cache_mem_k, cache_mem_v,
           mem_prompt, norm_mix_g, norm_mlp_g, norm_mem_g, w_mem_k, w_mem_v, mem_q_norm_g, mem_k_norm_g,
           w_up, w_down, w_in_a, conv_w, conv_b, dt_bias, a_log, d_skip, gate_norm_g, w_out_a,
           w_in_b, q_norm_g, w_out_b, kv_norm_g, w_k_shared, w_v_shared, k_norm_g):
    depth = w_up.shape[0]
    n_a = w_in_a.shape[0]
    d_model = x_prompt.shape[-1]
    n_heads = dt_bias.shape[1]
    d_inner = n_heads * SSM_HEAD_DIM
    conv_dim = conv_w.shape[2]
    mem_w = MEM_HEADS * HEAD_DIM
    kv_w = DIL_HEADS * HEAD_DIM
    dil_q_w = len(DIL_GROUPS) * kv_w

    s1, s2, s3 = d_inner, d_inner + conv_dim, d_inner + conv_dim + n_heads
    assert s1 % COL_TILE == 0 and s2 % COL_TILE == 0
    w_in_a_t = jnp.swapaxes(w_in_a, 1, 2)
    w_a_main = round_weight(w_in_a_t, s2)
    tail = lax.optimization_barrier(w_in_a_t[:, s2:]).astype(BF16)
    w_a_tail = jnp.concatenate([tail[:, :n_heads], jnp.zeros((n_a, COL_TILE - n_heads, d_model), BF16),
                                tail[:, n_heads:]], axis=1)
    tiles_a = ([(0, s1 + c) for c in range(0, conv_dim, COL_TILE)] + [(1, 0)]
               + [(0, c) for c in range(0, d_inner, COL_TILE)]
               + [(1, COL_TILE + c) for c in range(0, mem_w, COL_TILE)])
    col_xbc, col_dt, col_z = 0, conv_dim, conv_dim + COL_TILE
    col_qm_a = col_z + d_inner
    kinds_a = ["plain"] * (col_qm_a // COL_TILE) + ["norm"] * (mem_w // COL_TILE)
    kinds_b = ["rope"] * (dil_q_w // COL_TILE) + ["norm"] * (mem_w // COL_TILE)
    kinds_kv = ["rope"] * (kv_w // COL_TILE) + ["plain"] * (kv_w // COL_TILE)
    kinds_mkv = ["norm"] * (mem_w // COL_TILE) + ["plain"] * (mem_w // COL_TILE)
    w_b = w_in_b.astype(BF16)
    w_kv = jnp.concatenate([w_k_shared, w_v_shared], axis=-1).astype(BF16)[None]
    w_mkv = jnp.concatenate([w_mem_k, w_mem_v], axis=-1).astype(BF16)
    w_out_a_b, w_out_b_b = w_out_a.astype(BF16), w_out_b.astype(BF16)
    assert n_a >= 1
    w_up_b, w_down_b = [None] * depth, [None] * depth

    def gain_row(width, pieces):
        parts, at = [], 0
        for start, g, reps in pieces:
            parts += [jnp.ones((start - at,), F32), jnp.tile(g, reps)]
            at = start + reps * HEAD_DIM
        parts.append(jnp.ones((width - at,), F32))
        return jnp.concatenate(parts).reshape(1, width)

    def run(x, pos_rows, conv_prev, ssm_prev, mem_kv, k_past, v_past, t_real, round_mlp_weights):
        bt, t, _ = x.shape
        m = bt * t
        x2 = x.reshape(m, d_model)
        cos, sin = rope_tables(pos_rows)
        conv_new, ssm_new = [], []

        def run_mlp(x2, l):
            if round_mlp_weights and l + 1 < depth:
                x2, (w_up_b[l + 1], w_down_b[l + 1]) = mlp(x2, norm_mlp_g[l], w_up_b[l], w_down_b[l],
                                                           cast=((w_up, l + 1), (w_down, l + 1)))
                return x2
            return mlp(x2, norm_mlp_g[l], w_up_b[l], w_down_b[l])

        for l in range(n_a):
            first = round_mlp_weights and l == 0
            hg = gain_row(col_qm_a + mem_w, [(col_qm_a, mem_q_norm_g[l], MEM_HEADS)])
            proj = norm_proj(x2, norm_mix_g[l], [w_a_main, w_a_tail], l, hg, kinds_a, tile_src=tiles_a, transposed=True,
                             cast=((w_up, 0),) if first else ())
            if first:
                proj, (w_up_b[0],) = proj
            proj3 = proj.reshape(bt, t, -1)
            tp = -(-t // SSD_CHUNK) * SSD_CHUNK
            proj_ssd = proj3 if tp == t else jnp.pad(proj3, ((0, 0), (0, tp - t), (0, 0)))
            ssd = ssd_mixer(proj_ssd, col_xbc, col_dt, col_z, t_real, conv_w[l], conv_b[l],
                            dt_bias[l], a_log[l], d_skip[l], gate_norm_g[l],
                            conv_prev[l], ssm_prev[l].reshape(bt, SSM_GROUPS, -1, SSM_D_STATE),
                            cast=((w_down, 0),) if first else ())
            if first:
                ssd, (w_down_b[0],) = ssd
            y, c_new, h_new = ssd
            y = y[:, :t].reshape(m, d_inner)
            mo = mem_attention(proj3, col_qm_a, *mem_kv(l)).reshape(m, mem_w)
            x2 = out_proj(x2, y, mo, w_out_a_b, l)
            x2 = run_mlp(x2, l)
            conv_new.append(c_new)
            ssm_new.append(h_new.reshape(bt, n_heads, SSM_HEAD_DIM, SSM_D_STATE))
        hg = gain_row(2 * kv_w, [(0, k_norm_g, DIL_HEADS)])
        kv3 = norm_proj(x2, kv_norm_g, w_kv, 0, hg, kinds_kv, cos, sin).reshape(bt, t, 2 * kv_w)
        for l in range(n_a, depth):
            j = l - n_a
            hg = gain_row(dil_q_w + mem_w, [(0, q_norm_g[j], dil_q_w // HEAD_DIM),
                                            (dil_q_w, mem_q_norm_g[l], MEM_HEADS)])
            proj3 = norm_proj(x2, norm_mix_g[l], w_b, j, hg, kinds_b, cos, sin).reshape(bt, t, -1)
            if k_past is None:
                att = dilated_attention(proj3, kv3)
            else:
                att = dilated_decode(proj3, kv3, k_past, v_past)
            mo = mem_attention(proj3, dil_q_w, *mem_kv(l)).reshape(m, mem_w)
            x2 = out_proj(x2, att.reshape(m, kv_w), mo, w_out_b_b, j)
            x2 = run_mlp(x2, l)
        return x2.reshape(bt, t, d_model), jnp.stack(conv_new), jnp.stack(ssm_new), kv3

    bp, t_p, _ = x_prompt.shape
    mlen = mem_prompt.shape[1]
    mem2 = mem_prompt.reshape(bp * mlen, d_model)
    mkv_p = []
    for l in range(depth):
        hg = gain_row(2 * mem_w, [(0, mem_k_norm_g[l], MEM_HEADS)])
        mkv_p.append(norm_proj(mem2, norm_mem_g[l], w_mkv, l, hg, kinds_mkv).reshape(bp, mlen, 2 * mem_w))
    conv0 = jnp.zeros((n_a, bp, CONV_W - 1, conv_dim), F32)
    ssm0 = jnp.zeros((n_a, bp, n_heads, SSM_HEAD_DIM, SSM_D_STATE), F32)
    y_p, conv_p, ssm_p, kv_p = run(x_prompt, jnp.arange(t_p, dtype=jnp.int32), conv0, ssm0,
                                   lambda l: (mkv_p[l], mkv_p[l], (0, mem_w)), None, None, t_p, True)
    mkv_all = jnp.stack(mkv_p)
    mem_k_p = mkv_all[..., :mem_w].reshape(depth, bp, mlen, MEM_HEADS, HEAD_DIM)
    mem_v_p = mkv_all[..., mem_w:].reshape(depth, bp, mlen, MEM_HEADS, HEAD_DIM)
    keep = min(max(w for w, _ in DIL_GROUPS), t_p)
    win_k_p = kv_p[:, t_p - keep:, :kv_w].reshape(bp, keep, DIL_HEADS, HEAD_DIM)
    win_v_p = kv_p[:, t_p - keep:, kv_w:].reshape(bp, keep, DIL_HEADS, HEAD_DIM)

    bs, t_s, _ = x_sample.shape
    assert t_s == 1
    xs = jnp.pad(x_sample, ((0, 0), (0, SAMPLE_ROWS - t_s), (0, 0)))
    pos_s = jnp.full((bs * SAMPLE_ROWS,), PAST_LEN, jnp.int32)
    y_s, conv_s, ssm_s, kv_s = run(xs, pos_s, state_conv, state_ssm,
                                   lambda l: (cache_mem_k, cache_mem_v, l), cache_win_k, cache_win_v, t_s, False)
    y_s = y_s[:, :t_s]
    k_s = kv_s[:, :t_s, :kv_w].reshape(bs, t_s, DIL_HEADS, HEAD_DIM)
    v_s = kv_s[:, :t_s, kv_w:].reshape(bs, t_s, DIL_HEADS, HEAD_DIM)

    return (y_p, y_s, conv_p, ssm_p, win_k_p, win_v_p, mem_k_p, mem_v_p, conv_s, ssm_s, k_s, v_s)
```

```python
import functools

import jax
import jax.numpy as jnp
from jax import lax
from jax.experimental import pallas as pl
from jax.experimental.pallas import tpu as pltpu

F32 = jnp.float32
BF16 = jnp.bfloat16

HEAD_DIM = 128
SSM_HEAD_DIM = 64
SSM_GROUPS = 4
SSM_D_STATE = 128
CONV_W = 4
SSD_CHUNK = 128
MEM_HEADS = 4
DIL_GROUPS = ((128, 1), (512, 4), (2048, 16))
DIL_HEADS = 4
PAST_LEN = 16384
ROPE_THETA = 10000.0
EPS = 1e-6
LOG2_E = 1.4426950408889634
LN_2 = 0.6931471805599453

LANES = 128
SUBLANES = 8
VMEM_LIMIT = 48 * 1024 * 1024
COL_TILE = 512
ROW_TILE = 512
PROJ_OUT_BLOCK_BYTES = 5 * 1024 * 1024
SAMPLE_ROWS = SUBLANES
DIL_UNROLL = 16

NT_DIMS = (((1,), (1,)), ((), ()))
RESIDENT = pl.Buffered(1)


def _cparams(*sem):
    return pltpu.CompilerParams(dimension_semantics=sem, vmem_limit_bytes=VMEM_LIMIT)


def _sigmoid(x):
    return 1.0 / (1.0 + jnp.exp(-x))


def _rms_scale(x):
    return lax.rsqrt(jnp.mean(x * x, axis=-1, keepdims=True) + EPS)


class _SideCast:
    def __init__(self, mats, steps, step_of):
        self.n = len(mats)
        self.shapes = [a.shape[1:] for a, _ in mats]
        self.views, self.in_specs, self.out_specs, self.out_shape = [], [], [], []
        for a, layer in mats:
            n_layers, r, c = a.shape
            assert r % (steps * 2 * SUBLANES) == 0
            band = r // steps
            self.views.append(a.reshape(n_layers, steps, band, c))
            self.in_specs.append(pl.BlockSpec((None, None, band, c),
                                              lambda *ids, layer=layer: (layer, step_of(*ids), 0, 0)))
            self.out_specs.append(pl.BlockSpec((None, band, c), lambda *ids: (step_of(*ids), 0, 0)))
            self.out_shape.append(jax.ShapeDtypeStruct((steps, band, c), BF16))

    def wrap(self, body, n_in, n_out):
        n = self.n
        if not n:
            return body

        def kern(*refs):
            ins, srcs = refs[:n_in], refs[n_in:n_in + n]
            outs, dsts = refs[n_in + n:n_in + n + n_out], refs[n_in + n + n_out:n_in + 2 * n + n_out]
            for s, d in zip(srcs, dsts):
                d[...] = s[...].astype(BF16)
            body(*ins, *outs, *refs[n_in + 2 * n + n_out:])
        return kern

    def split(self, outs, n_out):
        main = outs[0] if n_out == 1 else tuple(outs[:n_out])
        if not self.n:
            return main
        return main, [o.reshape(s) for o, s in zip(outs[n_out:], self.shapes)]


def _round_kernel(src_ref, dst_ref):
    dst_ref[...] = src_ref[...].astype(BF16)


def round_weight(w, n_rows):
    n_layers, _, c = w.shape
    band = ROW_TILE
    assert n_rows % band == 0
    return pl.pallas_call(
        _round_kernel,
        grid=(n_layers, n_rows // band),
        in_specs=[pl.BlockSpec((None, band, c), lambda l, i: (l, i, 0))],
        out_specs=pl.BlockSpec((None, band, c), lambda l, i: (l, i, 0)),
        out_shape=jax.ShapeDtypeStruct((n_layers, n_rows, c), BF16),
        compiler_params=_cparams("parallel", "parallel"),
        name="round_weight",
    )(w)


def _rope_table_kernel(pos_ref, inv_ref, cos_ref, sin_ref):
    ang = pos_ref[...] * inv_ref[...]
    lane = lax.broadcasted_iota(jnp.int32, ang.shape, 1)
    cos_ref[...] = jnp.cos(ang)
    s = jnp.sin(ang)
    sin_ref[...] = jnp.where(lane < HEAD_DIM // 2, -s, s)


def rope_tables(pos):
    r = pos.shape[0]
    half = HEAD_DIM // 2
    inv = ROPE_THETA ** (-jnp.arange(half, dtype=F32) * (2.0 / HEAD_DIM))
    inv = jnp.concatenate([inv, inv])[None, :]
    posf = jnp.broadcast_to(pos.astype(F32)[:, None], (r, HEAD_DIM))
    tr = min(r, ROW_TILE)
    return pl.pallas_call(
        _rope_table_kernel,
        grid=(r // tr,),
        in_specs=[pl.BlockSpec((tr, HEAD_DIM), lambda i: (i, 0)),
                  pl.BlockSpec((1, HEAD_DIM), lambda i: (0, 0))],
        out_specs=[pl.BlockSpec((tr, HEAD_DIM), lambda i: (i, 0))] * 2,
        out_shape=[jax.ShapeDtypeStruct((r, HEAD_DIM), F32)] * 2,
        compiler_params=_cparams("parallel"),
        name="rope_tables",
    )(posf, inv)


def _proj_kernel(*refs, kinds, tile_src, n_w, transposed):
    x_ref, g_ref = refs[:2]
    w_refs = refs[2:2 + n_w]
    hg_ref = refs[2 + n_w]
    if "rope" in kinds:
        cos_ref, sin_ref = refs[3 + n_w:5 + n_w]
    o_ref, xn_ref = refs[-2:]
    x = x_ref[...]
    xn_ref[...] = (x * _rms_scale(x) * g_ref[...]).astype(BF16)
    for t, kind in enumerate(kinds):
        which, col = tile_src[t]
        cols = slice(t * COL_TILE, (t + 1) * COL_TILE)
        if transposed:
            y = lax.dot_general(xn_ref[...], w_refs[which][col:col + COL_TILE, :], NT_DIMS,
                                preferred_element_type=F32)
        else:
            y = jnp.dot(xn_ref[...], w_refs[which][:, col:col + COL_TILE], preferred_element_type=F32)
        if kind == "plain":
            o_ref[:, cols] = y
            continue
        for j in range(COL_TILE // HEAD_DIM):
            sl = slice(j * HEAD_DIM, (j + 1) * HEAD_DIM)
            osl = slice(t * COL_TILE + j * HEAD_DIM, t * COL_TILE + (j + 1) * HEAD_DIM)
            yh = y[:, sl]
            yh = yh * _rms_scale(yh) * hg_ref[:, osl]
            if kind == "rope":
                yh = yh * cos_ref[...] + pltpu.roll(yh, HEAD_DIM // 2, axis=1) * sin_ref[...]
            o_ref[:, osl] = yh


def norm_proj(x, g, ws, layer, head_gain, kinds, cos=None, sin=None, cast=(), tile_src=None, transposed=False):
    m, d = x.shape
    ws = list(ws) if isinstance(ws, (list, tuple)) else [ws]
    if tile_src is None:
        tile_src = [(0, t * COL_TILE) for t in range(len(kinds))]
    n_cols = COL_TILE * len(kinds)
    assert len(tile_src) == len(kinds) and head_gain.shape == (1, n_cols)
    tm = min(m, ROW_TILE)
    while tm * n_cols * 4 > PROJ_OUT_BLOCK_BYTES and tm % (2 * SUBLANES) == 0:
        tm //= 2
    assert m % tm == 0
    in_specs = [pl.BlockSpec((tm, d), lambda i: (i, 0)),
                pl.BlockSpec((1, d), lambda i: (0, 0))]
    for k, w in enumerate(ws):
        used = max(col + COL_TILE for which, col in tile_src if which == k)
        if transposed:
            assert used <= w.shape[1] and w.shape[2] == d
            in_specs.append(pl.BlockSpec((None, used, d), lambda i: (layer, 0, 0), pipeline_mode=RESIDENT))
            continue
        assert used <= w.shape[2] and (used % LANES == 0 or used == w.shape[2])
        in_specs.append(pl.BlockSpec((None, d, used), lambda i: (layer, 0, 0), pipeline_mode=RESIDENT))
    in_specs.append(pl.BlockSpec((1, n_cols), lambda i: (0, 0)))
    args = [x, g.reshape(1, d), *ws, head_gain]
    if "rope" in kinds:
        assert cos.shape[0] % tm == 0
        tab_blocks = cos.shape[0] // tm
        in_specs += [pl.BlockSpec((tm, HEAD_DIM), lambda i: (i % tab_blocks, 0))] * 2
        args += [cos, sin]
    side = _SideCast(cast, m // tm, lambda i: i)
    outs = pl.pallas_call(
        side.wrap(functools.partial(_proj_kernel, kinds=tuple(kinds), tile_src=tuple(tile_src), n_w=len(ws),
                                    transposed=transposed),
                  len(args), 1),
        grid=(m // tm,),
        in_specs=in_specs + side.in_specs,
        out_specs=[pl.BlockSpec((tm, n_cols), lambda i: (i, 0))] + side.out_specs,
        out_shape=[jax.ShapeDtypeStruct((m, n_cols), F32)] + side.out_shape,
        scratch_shapes=[pltpu.VMEM((tm, d), BF16)],
        compiler_params=_cparams("parallel"),
        name="norm_proj",
    )(*args, *side.views)
    return side.split(outs, 1)


def _out_proj_kernel(x_ref, a_ref, b_ref, wa_ref, wb_ref, o_ref):
    acc = jnp.dot(a_ref[...], wa_ref[...], preferred_element_type=F32)
    acc = acc + jnp.dot(b_ref[...], wb_ref[...], preferred_element_type=F32)
    o_ref[...] = x_ref[...] + acc


def out_proj(x, a, b, w, layer):
    m, d = x.shape
    tm = min(m, ROW_TILE)
    ka, kb = a.shape[1], b.shape[1]
    assert w.shape[1] == ka + kb and ka % kb == 0
    return pl.pallas_call(
        _out_proj_kernel,
        grid=(m // tm,),
        in_specs=[pl.BlockSpec((tm, d), lambda i: (i, 0)),
                  pl.BlockSpec((tm, ka), lambda i: (i, 0)),
                  pl.BlockSpec((tm, kb), lambda i: (i, 0)),
                  pl.BlockSpec((None, ka, d), lambda i: (layer, 0, 0), pipeline_mode=RESIDENT),
                  pl.BlockSpec((None, kb, d), lambda i: (layer, ka // kb, 0), pipeline_mode=RESIDENT)],
        out_specs=pl.BlockSpec((tm, d), lambda i: (i, 0)),
        out_shape=jax.ShapeDtypeStruct((m, d), F32),
        compiler_params=_cparams("parallel"),
        name="out_proj",
    )(x, a, b, w, w)


def _mlp_kernel(*refs, has_extra):
    if has_extra:
        x_ref, g_ref, wu_ref, wd_ref, xs_ref, o_ref, os_ref, xn_ref, xns_ref = refs
    else:
        x_ref, g_ref, wu_ref, wd_ref, o_ref, xn_ref = refs
    i, f = pl.program_id(0), pl.program_id(1)

    def update(x_ref, xn_ref, o_ref):
        @pl.when(f == 0)
        def _():
            x = x_ref[...]
            xn_ref[...] = (x * _rms_scale(x) * g_ref[...]).astype(BF16)
            o_ref[...] = x

        h = jnp.dot(xn_ref[...], wu_ref[...], preferred_element_type=F32)
        h = jnp.maximum(h, 0.0)
        h = (h * h).astype(BF16)
        o_ref[...] += jnp.dot(h, wd_ref[...], preferred_element_type=F32)

    update(x_ref, xn_ref, o_ref)
    if has_extra:
        pl.when(i == 0)(functools.partial(update, xs_ref, xns_ref, os_ref))


def mlp(x, g, wu, wd, tf=1024, cast=(), extra=None):
    m, d = x.shape
    ff = wu.shape[1]
    tm = min(m, ROW_TILE)
    nf = ff // tf
    has_extra = extra is not None
    side = _SideCast(cast, (m // tm) * nf, lambda i, j: i * nf + j)
    in_specs = [pl.BlockSpec((tm, d), lambda i, j: (i, 0)),
                pl.BlockSpec((1, d), lambda i, j: (0, 0)),
                pl.BlockSpec((d, tf), lambda i, j: (0, j)),
                pl.BlockSpec((tf, d), lambda i, j: (j, 0))]
    out_specs = [pl.BlockSpec((tm, d), lambda i, j: (i, 0))]
    out_shape = [jax.ShapeDtypeStruct((m, d), F32)]
    scratch = [pltpu.VMEM((tm, d), BF16)]
    args = [x, g.reshape(1, d), wu, wd]
    if has_extra:
        ms = extra.shape[0]
        whole = pl.BlockSpec((ms, d), lambda i, j: (0, 0))
        in_specs.append(whole)
        out_specs.append(whole)
        out_shape.append(jax.ShapeDtypeStruct((ms, d), F32))
        scratch.append(pltpu.VMEM((ms, d), BF16))
        args.append(extra)
    n_out = len(out_specs)
    outs = pl.pallas_call(
        side.wrap(functools.partial(_mlp_kernel, has_extra=has_extra), len(args), n_out),
        grid=(m // tm, nf),
        in_specs=in_specs + side.in_specs,
        out_specs=out_specs + side.out_specs,
        out_shape=out_shape + side.out_shape,
        scratch_shapes=scratch,
        compiler_params=_cparams("arbitrary", "arbitrary"),
        name="mlp",
    )(*args, *side.views)
    casts = [o.reshape(shp) for o, shp in zip(outs[n_out:], side.shapes)]
    return outs[0], (outs[1] if has_extra else None), casts


def _split3(x):
    hi = x.astype(BF16)
    r1 = x - hi.astype(F32)
    mid = r1.astype(BF16)
    lo = (r1 - mid.astype(F32)).astype(BF16)
    return hi, mid, lo


def _ssd_kernel(xbc_ref, dt_ref, z_ref, cw_ref, cb_ref, dtb_ref, alog_ref, dsk_ref, gg_ref, tri_ref, expand_ref,
                conv0_ref, h0_ref, y_ref, convo_ref, ho_ref,
                ext_ref, ht_ref, yd_ref, st_ref, *, t_real, n_heads, d_inner):
    L = SSD_CHUNK
    P = SSM_HEAD_DIM
    N = SSM_D_STATE
    G = SSM_GROUPS
    hpg = n_heads // G
    gw = hpg * P
    hist = CONV_W - 1
    base = SUBLANES - hist
    c = pl.program_id(1)
    nc = pl.num_programs(1)

    @pl.when(c == 0)
    def _():
        ext_ref[base:SUBLANES, :] = conv0_ref[...]
        for g in range(G):
            ht_ref[g] = h0_ref[g].T

    ext_ref[SUBLANES:SUBLANES + L, :] = xbc_ref[...]
    ext = ext_ref[...]
    acc = cb_ref[...] + ext[SUBLANES:, :] * cw_ref[hist:CONV_W, :]
    for k in range(1, CONV_W):
        acc = acc + pltpu.roll(ext, k, axis=0)[SUBLANES:, :] * cw_ref[hist - k:CONV_W - k, :]
    xc = acc * _sigmoid(acc)

    @pl.when(c == nc - 1)
    def _():
        n_in_last = t_real - ((t_real - 1) // L) * L
        convo_ref[...] = ext_ref[base + n_in_last:base + n_in_last + hist, :]

    ext_ref[base:SUBLANES, :] = ext_ref[base + L:SUBLANES + L, :]

    dt_raw = dt_ref[...] + dtb_ref[...]
    dt = jnp.maximum(dt_raw, 0.0) + jnp.log1p(jnp.exp(-jnp.abs(dt_raw)))
    row = lax.broadcasted_iota(jnp.int32, (L, LANES), 0) + c * L
    dt = jnp.where(row < t_real, dt, 0.0)
    da = dt * (-jnp.exp(alog_ref[...]))
    li = lax.broadcasted_iota(jnp.int32, (L, L), 0)
    si = lax.broadcasted_iota(jnp.int32, (L, L), 1)
    causal = li >= si
    a_cs = sum(jnp.dot(tri_ref[...], part, preferred_element_type=F32) for part in _split3(da))
    a_cs_t = a_cs.T
    dt_t = dt.T
    w_t = dt_t * jnp.exp(a_cs_t[:, L - 1:L] - a_cs_t)
    e_exp = sum(jnp.dot(part, expand_ref[...], preferred_element_type=F32) for part in _split3(jnp.exp(a_cs)))
    xb = xc[:, :d_inner].astype(BF16)
    low_half = lax.broadcasted_iota(jnp.int32, (L, 2 * P), 1) < P

    for g in range(G):
        b_g = xc[:, d_inner + g * N:d_inner + (g + 1) * N]
        c_g = xc[:, d_inner + G * N + g * N:d_inner + G * N + (g + 1) * N].astype(BF16)
        cbm = lax.dot_general(c_g, b_g.astype(BF16), NT_DIMS, preferred_element_type=F32)
        b_t = b_g.T
        y_off = jnp.dot(c_g, ht_ref[g].astype(BF16), preferred_element_type=F32)
        for kp in range(hpg // 2):
            h0 = g * hpg + 2 * kp
            x_pair = xb[:, h0 * P:(h0 + 2) * P]
            yd, st = [], []
            for h in (h0, h0 + 1):
                seg = a_cs[:, h:h + 1] - a_cs_t[h:h + 1, :]
                dec = jnp.exp(jnp.where(causal, seg, -jnp.inf))
                mh = (cbm * dec * dt_t[h:h + 1, :]).astype(BF16)
                yd.append(jnp.dot(mh, x_pair, preferred_element_type=F32))
                btw = (b_t * w_t[h:h + 1, :]).astype(BF16)
                st.append(jnp.dot(btw, x_pair, preferred_element_type=F32))
            yd_ref[:, 2 * kp * P:(2 * kp + 2) * P] = jnp.where(low_half, yd[0], yd[1])
            st_ref[:, 2 * kp * P:(2 * kp + 2) * P] = jnp.where(low_half, st[0], st[1])
        gs = slice(g * gw, (g + 1) * gw)
        e_g = e_exp[:, gs]
        y_g = yd_ref[...] + y_off * e_g + dsk_ref[:, gs] * xc[:, gs]
        ht_ref[g] = ht_ref[g] * e_g[L - 1:L, :] + st_ref[...]
        z_g = z_ref[:, gs]
        gated = y_g * (z_g * _sigmoid(z_g))
        y_ref[:, gs] = (gated * _rms_scale(gated) * gg_ref[:, gs]).astype(y_ref.dtype)

    @pl.when(c == nc - 1)
    def _():
        for g in range(G):
            ho_ref[g] = ht_ref[g].T


def ssd_mixer(proj, col_xbc, col_dt, col_z, t_real, conv_w, conv_b, dt_bias, a_log, d_skip, gate_g,
              conv_prev, h_prev, cast=()):
    bt, tp, _ = proj.shape
    conv_dim = conv_w.shape[1]
    n_heads = dt_bias.shape[0]
    d_inner = n_heads * SSM_HEAD_DIM
    gw = d_inner // SSM_GROUPS
    L = SSD_CHUNK
    assert tp % L == 0 and (tp - t_real) < L
    assert col_xbc % conv_dim == 0 and col_dt % LANES == 0 and col_z % d_inner == 0
    pad = LANES - n_heads
    dtb = jnp.pad(dt_bias, (0, pad)).reshape(1, LANES)
    alog = jnp.pad(a_log, (0, pad)).reshape(1, LANES)
    dsk = jnp.repeat(d_skip, SSM_HEAD_DIM).reshape(1, d_inner)
    tri = jnp.tril(jnp.ones((L, L), BF16))
    expand = jnp.repeat(jnp.eye(LANES, dtype=BF16)[:, :n_heads], SSM_HEAD_DIM, axis=1)
    kern = functools.partial(_ssd_kernel, t_real=t_real, n_heads=n_heads, d_inner=d_inner)
    vec = lambda n: pl.BlockSpec((1, n), lambda b, c: (0, 0))
    nc = tp // L
    side = _SideCast(cast, bt * nc, lambda b, c: b * nc + c)
    outs = pl.pallas_call(
        side.wrap(kern, 13, 3),
        grid=(bt, nc),
        in_specs=[pl.BlockSpec((None, L, conv_dim), lambda b, c: (b, c, col_xbc // conv_dim)),
                  pl.BlockSpec((None, L, LANES), lambda b, c: (b, c, col_dt // LANES)),
                  pl.BlockSpec((None, L, d_inner), lambda b, c: (b, c, col_z // d_inner)),
                  pl.BlockSpec((CONV_W, conv_dim), lambda b, c: (0, 0)),
                  vec(conv_dim), vec(LANES), vec(LANES), vec(d_inner), vec(d_inner),
                  pl.BlockSpec((L, L), lambda b, c: (0, 0)),
                  pl.BlockSpec((LANES, d_inner), lambda b, c: (0, 0)),
                  pl.BlockSpec((None, CONV_W - 1, conv_dim), lambda b, c: (b, 0, 0)),
                  pl.BlockSpec((None, SSM_GROUPS, gw, SSM_D_STATE), lambda b, c: (b, 0, 0, 0))] + side.in_specs,
        out_specs=[pl.BlockSpec((None, L, d_inner), lambda b, c: (b, c, 0)),
                   pl.BlockSpec((None, CONV_W - 1, conv_dim), lambda b, c: (b, 0, 0)),
                   pl.BlockSpec((None, SSM_GROUPS, gw, SSM_D_STATE), lambda b, c: (b, 0, 0, 0))] + side.out_specs,
        out_shape=[jax.ShapeDtypeStruct((bt, tp, d_inner), BF16),
                   jax.ShapeDtypeStruct((bt, CONV_W - 1, conv_dim), F32),
                   jax.ShapeDtypeStruct((bt, SSM_GROUPS, gw, SSM_D_STATE), F32)] + side.out_shape,
        scratch_shapes=[pltpu.VMEM((SUBLANES + L, conv_dim), F32),
                        pltpu.VMEM((SSM_GROUPS, SSM_D_STATE, gw), F32),
                        pltpu.VMEM((L, gw), F32),
                        pltpu.VMEM((SSM_D_STATE, gw), F32)],
        compiler_params=_cparams("parallel", "arbitrary"),
        name="ssd_mixer",
    )(proj, proj, proj, conv_w, conv_b.reshape(1, conv_dim), dtb, alog, dsk, gate_g.reshape(1, d_inner),
      tri, expand, conv_prev, h_prev, *side.views)
    return side.split(outs, 3)


def _mem_attn_kernel(q_ref, k_ref, v_ref, o_ref):
    scale = HEAD_DIM ** -0.5
    headed = len(k_ref.shape) == 3
    for h in range(MEM_HEADS):
        sl = slice(h * HEAD_DIM, (h + 1) * HEAD_DIM)
        q = q_ref[:, sl].astype(BF16)
        k = (k_ref[:, h, :] if headed else k_ref[:, sl]).astype(BF16)
        v = (v_ref[:, h, :] if headed else v_ref[:, sl]).astype(BF16)
        s = lax.dot_general(q, k, NT_DIMS, preferred_element_type=F32) * scale
        e = jnp.exp(s - jnp.max(s, axis=-1, keepdims=True))
        den = jnp.sum(e, axis=-1, keepdims=True)
        o = jnp.dot(e.astype(BF16), v, preferred_element_type=F32) / den
        o_ref[:, sl] = o.astype(o_ref.dtype)


def mem_attention(proj, col_q, k, v, kv_at):
    bt, t, _ = proj.shape
    width = MEM_HEADS * HEAD_DIM
    tq = min(t, ROW_TILE)
    if k.ndim == 5:
        mlen = k.shape[2]
        kv_specs = [pl.BlockSpec((None, None, mlen, MEM_HEADS, HEAD_DIM), lambda b, i: (kv_at, b, 0, 0, 0))] * 2
    else:
        mlen = k.shape[1]
        kv_specs = [pl.BlockSpec((None, mlen, width), lambda b, i, c=c: (b, 0, c // width)) for c in kv_at]
    return pl.pallas_call(
        _mem_attn_kernel,
        grid=(bt, t // tq),
        in_specs=[pl.BlockSpec((None, tq, width), lambda b, i: (b, i, col_q // width))] + kv_specs,
        out_specs=pl.BlockSpec((None, tq, width), lambda b, i: (b, i, 0)),
        out_shape=jax.ShapeDtypeStruct((bt, t, width), BF16),
        compiler_params=_cparams("parallel", "parallel"),
        name="mem_attention",
    )(proj, k, v)


def _run_units(first, count, unit, unroll):
    trips = count // unroll
    if trips == 1:
        trips = 0
    if trips:
        def trip(it, carry):
            for k in range(unroll):
                unit(first + it * unroll + k)
            return carry
        lax.fori_loop(0, trips, trip, 0)
    for k in range(trips * unroll, count):
        unit(first + k)


def _dil_attn_kernel(*refs, groups, rb, unroll):
    ng = len(groups)
    q_refs = refs[:ng]
    kp_ref, kc_ref, vp_ref, vc_ref, o_ref = refs[ng:ng + 5]
    og = refs[ng + 5:2 * ng + 5]
    lg = refs[2 * ng + 5:3 * ng + 5]
    i = pl.program_id(1)
    scale = HEAD_DIM ** -0.5

    for gi, (win, d) in enumerate(groups):
        w = win // d
        dw = d * w
        qi = lax.broadcasted_iota(jnp.int32, (w, 2 * w), 0)
        kj = lax.broadcasted_iota(jnp.int32, (w, 2 * w), 1)
        band = (kj >= qi) & (kj <= qi + w)
        band_first = band & (kj >= jnp.where(i > 0, 0, w))

        def rows(start, size, d=d, w=w):
            if d == 1:
                return pl.ds(pl.multiple_of(start, w), size)
            return pl.ds(start, size, stride=d)

        def attend(rows_q, kk, vv, mask, gi=gi, w=w):
            q = (q_refs[gi][rows_q, :] * (scale * LOG2_E)).astype(BF16)
            s = lax.dot_general(q, kk.astype(BF16), NT_DIMS, preferred_element_type=F32)
            s = jnp.where(mask, s, -jnp.inf)
            m = jnp.max(s, axis=-1, keepdims=True)
            e = jnp.exp2(s - m)
            den = jnp.sum(e, axis=-1, keepdims=True)
            o = jnp.dot(e.astype(BF16), vv.astype(BF16), preferred_element_type=F32) / den
            og[gi][rows_q, :] = o
            lg[gi][rows_q, :] = jnp.broadcast_to(m * LN_2 + jnp.log(den), (w, HEAD_DIM))

        def first_unit(r, rows=rows, attend=attend, w=w, dw=dw, mask=band_first):
            rows_q, rows_p = rows(r, w), rows(rb - dw + r, w)
            kk = jnp.concatenate([kp_ref[rows_p, :], kc_ref[rows_q, :]], axis=0)
            vv = jnp.concatenate([vp_ref[rows_p, :], vc_ref[rows_q, :]], axis=0)
            attend(rows_q, kk, vv, mask)

        def later_unit(u, rows=rows, attend=attend, d=d, w=w, dw=dw, mask=band):
            sb = u // d
            start_q = sb * dw + (u - sb * d)
            rows_k = rows(start_q - dw, 2 * w)
            attend(rows(start_q, w), kc_ref[rows_k, :], vc_ref[rows_k, :], mask)

        _run_units(0, d, first_unit, unroll)
        _run_units(d, rb // w - d, later_unit, unroll)

    ls = [r[...] for r in lg]
    mm = functools.reduce(jnp.maximum, ls)
    ws = [jnp.exp(l - mm) for l in ls]
    num = sum(wg * r[...] for wg, r in zip(ws, og))
    o_ref[...] = (num / sum(ws)).astype(o_ref.dtype)


def dilated_attention(proj, kv):
    bt, t, _ = proj.shape
    ng = len(DIL_GROUPS)
    rb = min(t, max(win for win, _ in DIL_GROUPS))
    for win, d in DIL_GROUPS:
        assert win % d == 0 and rb % win == 0
    assert t % rb == 0
    blk = lambda f: pl.BlockSpec((None, rb, HEAD_DIM), f)
    prev = lambda i: jnp.maximum(i - 1, 0)
    q_specs = [blk(lambda b, i, h, g=g: (b, i, g * DIL_HEADS + h)) for g in range(ng)]
    kv_specs = [blk(lambda b, i, h: (b, prev(i), h)), blk(lambda b, i, h: (b, i, h)),
                blk(lambda b, i, h: (b, prev(i), DIL_HEADS + h)), blk(lambda b, i, h: (b, i, DIL_HEADS + h))]
    return pl.pallas_call(
        functools.partial(_dil_attn_kernel, groups=DIL_GROUPS, rb=rb, unroll=DIL_UNROLL),
        grid=(bt, t // rb, DIL_HEADS),
        in_specs=q_specs + kv_specs,
        out_specs=blk(lambda b, i, h: (b, i, h)),
        out_shape=jax.ShapeDtypeStruct((bt, t, DIL_HEADS * HEAD_DIM), BF16),
        scratch_shapes=[pltpu.VMEM((rb, HEAD_DIM), F32)] * (2 * ng),
        compiler_params=_cparams("parallel", "parallel", "parallel"),
        name="dilated_attention",
    )(*([proj] * ng), kv, kv, kv, kv)


def _dil_decode_kernel(*refs):
    ng = len(DIL_GROUPS)
    q_ref, kv_ref = refs[:2]
    k_refs, v_refs = refs[2:2 + ng], refs[2 + ng:2 + 2 * ng]
    o_ref = refs[-1]
    width = DIL_HEADS * HEAD_DIM
    scale = HEAD_DIM ** -0.5
    for h in range(DIL_HEADS):
        sl = slice(h * HEAD_DIM, (h + 1) * HEAD_DIM)
        k_new = kv_ref[:, h * HEAD_DIM:(h + 1) * HEAD_DIM]
        v_new = kv_ref[:, width + h * HEAD_DIM:width + (h + 1) * HEAD_DIM]
        outs, lses = [], []
        for gi in range(ng):
            kc = k_refs[gi][:, h, :].astype(BF16)
            vc = v_refs[gi][:, h, :].astype(BF16)
            q = q_ref[:, gi * width + h * HEAD_DIM:gi * width + (h + 1) * HEAD_DIM]
            s = lax.dot_general(q.astype(BF16), kc, NT_DIMS, preferred_element_type=F32) * scale
            s_self = jnp.sum(q * k_new, axis=-1, keepdims=True) * scale
            m = jnp.maximum(jnp.max(s, axis=-1, keepdims=True), s_self)
            e = jnp.exp(s - m)
            e_self = jnp.exp(s_self - m)
            den = jnp.sum(e, axis=-1, keepdims=True) + e_self
            o = jnp.dot(e.astype(BF16), vc, preferred_element_type=F32) + e_self * v_new
            outs.append(o / den)
            lses.append(m + jnp.log(den))
        mm = functools.reduce(jnp.maximum, lses)
        ws = [jnp.exp(l - mm) for l in lses]
        num = sum(wg * o for wg, o in zip(ws, outs))
        o_ref[:, sl] = (num / sum(ws)).astype(o_ref.dtype)


def dilated_decode(proj, kv_new, k_cache, v_cache):
    bt, rows, nq = proj.shape
    lc = k_cache.shape[1]
    width = DIL_HEADS * HEAD_DIM
    views, specs = [], []
    for cache in (k_cache, v_cache):
        for win, dil in DIL_GROUPS:
            assert lc % win == 0 and win % dil == 0
            w = win // dil
            views.append(cache.reshape(bt, lc // dil, dil, DIL_HEADS, HEAD_DIM))
            specs.append(pl.BlockSpec((None, w, None, DIL_HEADS, HEAD_DIM),
                                      lambda b, last=lc // win - 1: (b, last, 0, 0, 0)))
    return pl.pallas_call(
        _dil_decode_kernel,
        grid=(bt,),
        in_specs=[pl.BlockSpec((None, rows, nq), lambda b: (b, 0, 0)),
                  pl.BlockSpec((None, rows, 2 * width), lambda b: (b, 0, 0))] + specs,
        out_specs=pl.BlockSpec((None, rows, width), lambda b: (b, 0, 0)),
        out_shape=jax.ShapeDtypeStruct((bt, rows, width), BF16),
        compiler_params=_cparams("parallel"),
        name="dilated_decode",
    )(proj, kv_new, *views)


def kernel(x_prompt, x_sample, state_conv, state_ssm, cache_win_k, cache_win_v, cache_mem_k, cache_mem_v,
           mem_prompt, norm_mix_g, norm_mlp_g, norm_mem_g, w_mem_k, w_mem_v, mem_q_norm_g, mem_k_norm_g,
           w_up, w_down, w_in_a, conv_w, conv_b, dt_bias, a_log, d_skip, gate_norm_g, w_out_a,
           w_in_b, q_norm_g, w_out_b, kv_norm_g, w_k_shared, w_v_shared, k_norm_g):
    depth = w_up.shape[0]
    n_a = w_in_a.shape[0]
    d_model = x_prompt.shape[-1]
    n_heads = dt_bias.shape[1]
    d_inner = n_heads * SSM_HEAD_DIM
    conv_dim = conv_w.shape[2]
    mem_w = MEM_HEADS * HEAD_DIM
    kv_w = DIL_HEADS * HEAD_DIM
    dil_q_w = len(DIL_GROUPS) * kv_w

    s1, s2, s3 = d_inner, d_inner + conv_dim, d_inner + conv_dim + n_heads
    assert s1 % COL_TILE == 0 and s2 % COL_TILE == 0
    w_in_a_t = jnp.swapaxes(w_in_a, 1, 2)
    w_a_main = round_weight(w_in_a_t, s2)
    tail = lax.optimization_barrier(w_in_a_t[:, s2:]).astype(BF16)
    w_a_tail = jnp.concatenate([tail[:, :n_heads], jnp.zeros((n_a, COL_TILE - n_heads, d_model), BF16),
                                tail[:, n_heads:]], axis=1)
    tiles_a = ([(0, s1 + c) for c in range(0, conv_dim, COL_TILE)] + [(1, 0)]
               + [(0, c) for c in range(0, d_inner, COL_TILE)]
               + [(1, COL_TILE + c) for c in range(0, mem_w, COL_TILE)])
    col_xbc, col_dt, col_z = 0, conv_dim, conv_dim + COL_TILE
    col_qm_a = col_z + d_inner
    kinds_a = ["plain"] * (col_qm_a // COL_TILE) + ["norm"] * (mem_w // COL_TILE)
    kinds_b = ["rope"] * (dil_q_w // COL_TILE) + ["norm"] * (mem_w // COL_TILE)
    kinds_kv = ["rope"] * (kv_w // COL_TILE) + ["plain"] * (kv_w // COL_TILE)
    kinds_mkv = ["norm"] * (mem_w // COL_TILE) + ["plain"] * (mem_w // COL_TILE)
    w_b = w_in_b.astype(BF16)
    w_kv = jnp.concatenate([w_k_shared, w_v_shared], axis=-1).astype(BF16)[None]
    w_mkv = jnp.concatenate([w_mem_k, w_mem_v], axis=-1).astype(BF16)
    w_out_a_b, w_out_b_b = w_out_a.astype(BF16), w_out_b.astype(BF16)
    assert n_a >= 1
    w_up_b, w_down_b = [None] * depth, [None] * depth

    def gain_row(width, pieces):
        parts, at = [], 0
        for start, g, reps in pieces:
            parts += [jnp.ones((start - at,), F32), jnp.tile(g, reps)]
            at = start + reps * HEAD_DIM
        parts.append(jnp.ones((width - at,), F32))
        return jnp.concatenate(parts).reshape(1, width)

    class Group:
        def __init__(self, x, pos_rows, conv_prev, ssm_prev, mem_kv, k_past, v_past, t_real):
            self.bt, self.t, _ = x.shape
            self.m = self.bt * self.t
            self.x2 = x.reshape(self.m, d_model)
            self.cos, self.sin = rope_tables(pos_rows)
            self.conv_prev, self.ssm_prev, self.mem_kv = conv_prev, ssm_prev, mem_kv
            self.k_past, self.v_past, self.t_real = k_past, v_past, t_real
            self.conv_new, self.ssm_new, self.kv3 = [], [], None

    def mixer_a(gr, l, round_mlp_weights):
        bt, t, m = gr.bt, gr.t, gr.m
        hg = gain_row(col_qm_a + mem_w, [(col_qm_a, mem_q_norm_g[l], MEM_HEADS)])
        proj = norm_proj(gr.x2, norm_mix_g[l], [w_a_main, w_a_tail], l, hg, kinds_a, tile_src=tiles_a,
                         transposed=True, cast=((w_up, 0),) if round_mlp_weights else ())
        if round_mlp_weights:
            proj, (w_up_b[0],) = proj
        proj3 = proj.reshape(bt, t, -1)
        tp = -(-t // SSD_CHUNK) * SSD_CHUNK
        proj_ssd = proj3 if tp == t else jnp.pad(proj3, ((0, 0), (0, tp - t), (0, 0)))
        ssd = ssd_mixer(proj_ssd, col_xbc, col_dt, col_z, gr.t_real, conv_w[l], conv_b[l],
                        dt_bias[l], a_log[l], d_skip[l], gate_norm_g[l],
                        gr.conv_prev[l], gr.ssm_prev[l].reshape(bt, SSM_GROUPS, -1, SSM_D_STATE),
                        cast=((w_down, 0),) if round_mlp_weights else ())
        if round_mlp_weights:
            ssd, (w_down_b[0],) = ssd
        y, c_new, h_new = ssd
        y = y[:, :t].reshape(m, d_inner)
        mo = mem_attention(proj3, col_qm_a, *gr.mem_kv(l)).reshape(m, mem_w)
        gr.x2 = out_proj(gr.x2, y, mo, w_out_a_b, l)
        gr.conv_new.append(c_new)
        gr.ssm_new.append(h_new.reshape(bt, n_heads, SSM_HEAD_DIM, SSM_D_STATE))

    def shared_kv(gr):
        hg = gain_row(2 * kv_w, [(0, k_norm_g, DIL_HEADS)])
        gr.kv3 = norm_proj(gr.x2, kv_norm_g, w_kv, 0, hg, kinds_kv, gr.cos, gr.sin).reshape(gr.bt, gr.t, 2 * kv_w)

    def mixer_b(gr, l):
        j = l - n_a
        hg = gain_row(dil_q_w + mem_w, [(0, q_norm_g[j], dil_q_w // HEAD_DIM),
                                        (dil_q_w, mem_q_norm_g[l], MEM_HEADS)])
        proj3 = norm_proj(gr.x2, norm_mix_g[l], w_b, j, hg, kinds_b, gr.cos, gr.sin).reshape(gr.bt, gr.t, -1)
        if gr.k_past is None:
            att = dilated_attention(proj3, gr.kv3)
        else:
            att = dilated_decode(proj3, gr.kv3, gr.k_past, gr.v_past)
        mo = mem_attention(proj3, dil_q_w, *gr.mem_kv(l)).reshape(gr.m, mem_w)
        gr.x2 = out_proj(gr.x2, att.reshape(gr.m, kv_w), mo, w_out_b_b, j)

    bp, t_p, _ = x_prompt.shape
    mlen = mem_prompt.shape[1]
    mem2 = mem_prompt.reshape(bp * mlen, d_model)
    mkv_p = []
    for l in range(depth):
        hg = gain_row(2 * mem_w, [(0, mem_k_norm_g[l], MEM_HEADS)])
        mkv_p.append(norm_proj(mem2, norm_mem_g[l], w_mkv, l, hg, kinds_mkv).reshape(bp, mlen, 2 * mem_w))
    conv0 = jnp.zeros((n_a, bp, CONV_W - 1, conv_dim), F32)
    ssm0 = jnp.zeros((n_a, bp, n_heads, SSM_HEAD_DIM, SSM_D_STATE), F32)
    prompt = Group(x_prompt, jnp.arange(t_p, dtype=jnp.int32), conv0, ssm0,
                   lambda l: (mkv_p[l], mkv_p[l], (0, mem_w)), None, None, t_p)

    bs, t_s, _ = x_sample.shape
    assert t_s == 1
    xs = jnp.pad(x_sample, ((0, 0), (0, SAMPLE_ROWS - t_s), (0, 0)))
    pos_s = jnp.full((bs * SAMPLE_ROWS,), PAST_LEN, jnp.int32)
    sample = Group(xs, pos_s, state_conv, state_ssm, lambda l: (cache_mem_k, cache_mem_v, l),
                   cache_win_k, cache_win_v, t_s)

    for l in range(depth):
        for gr in (prompt, sample):
            if l < n_a:
                mixer_a(gr, l, round_mlp_weights=(gr is prompt and l == 0))
            else:
                if l == n_a:
                    shared_kv(gr)
                mixer_b(gr, l)
        nxt = ((w_up, l + 1), (w_down, l + 1)) if l + 1 < depth else ()
        prompt.x2, sample.x2, rounded = mlp(prompt.x2, norm_mlp_g[l], w_up_b[l], w_down_b[l],
                                            cast=nxt, extra=sample.x2)
        if nxt:
            w_up_b[l + 1], w_down_b[l + 1] = rounded

    y_p = prompt.x2.reshape(bp, t_p, d_model)
    conv_p, ssm_p, kv_p = jnp.stack(prompt.conv_new), jnp.stack(prompt.ssm_new), prompt.kv3
    mkv_all = jnp.stack(mkv_p)
    mem_k_p = mkv_all[..., :mem_w].reshape(depth, bp, mlen, MEM_HEADS, HEAD_DIM)
    mem_v_p = mkv_all[..., mem_w:].reshape(depth, bp, mlen, MEM_HEADS, HEAD_DIM)
    keep = min(max(w for w, _ in DIL_GROUPS), t_p)
    win_k_p = kv_p[:, t_p - keep:, :kv_w].reshape(bp, keep, DIL_HEADS, HEAD_DIM)
    win_v_p = kv_p[:, t_p - keep:, kv_w:].reshape(bp, keep, DIL_HEADS, HEAD_DIM)

    y_s = sample.x2.reshape(bs, SAMPLE_ROWS, d_model)
    conv_s, ssm_s, kv_s = jnp.stack(sample.conv_new), jnp.stack(sample.ssm_new), sample.kv3
    y_s = y_s[:, :t_s]
    k_s = kv_s[:, :t_s, :kv_w].reshape(bs, t_s, DIL_HEADS, HEAD_DIM)
    v_s = kv_s[:, :t_s, kv_w:].reshape(bs, t_s, DIL_HEADS, HEAD_DIM)

    return (y_p, y_s, conv_p, ssm_p, win_k_p, win_v_p, mem_k_p, mem_v_p, conv_s, ssm_s, k_s, v_s)
```

```python
import functools

import jax
import jax.numpy as jnp
from jax import lax
from jax.experimental import pallas as pl
from jax.experimental.pallas import tpu as pltpu

F32 = jnp.float32
BF16 = jnp.bfloat16

HEAD_DIM = 128
SSM_HEAD_DIM = 64
SSM_GROUPS = 4
SSM_D_STATE = 128
CONV_W = 4
SSD_CHUNK = 128
MEM_HEADS = 4
DIL_GROUPS = ((128, 1), (512, 4), (2048, 16))
DIL_HEADS = 4
PAST_LEN = 16384
ROPE_THETA = 10000.0
EPS = 1e-6
LOG2_E = 1.4426950408889634
LN_2 = 0.6931471805599453

LANES = 128
SUBLANES = 8
VMEM_LIMIT = 48 * 1024 * 1024
COL_TILE = 512
ROW_TILE = 512
PROJ_OUT_BLOCK_BYTES = 5 * 1024 * 1024
SAMPLE_ROWS = SUBLANES
DIL_UNROLL = 16

NT_DIMS = (((1,), (1,)), ((), ()))
RESIDENT = pl.Buffered(1)


def _cparams(*sem):
    return pltpu.CompilerParams(dimension_semantics=sem, vmem_limit_bytes=VMEM_LIMIT)


def _sigmoid(x):
    return 1.0 / (1.0 + jnp.exp(-x))


def _rms_scale(x):
    return lax.rsqrt(jnp.mean(x * x, axis=-1, keepdims=True) + EPS)


class _SideCast:
    def __init__(self, mats, steps, step_of):
        self.n = len(mats)
        self.shapes = [a.shape[1:] for a, _ in mats]
        self.views, self.in_specs, self.out_specs, self.out_shape = [], [], [], []
        for a, layer in mats:
            n_layers, r, c = a.shape
            assert r % (steps * 2 * SUBLANES) == 0
            band = r // steps
            self.views.append(a.reshape(n_layers, steps, band, c))
            self.in_specs.append(pl.BlockSpec((None, None, band, c),
                                              lambda *ids, layer=layer: (layer, step_of(*ids), 0, 0)))
            self.out_specs.append(pl.BlockSpec((None, band, c), lambda *ids: (step_of(*ids), 0, 0)))
            self.out_shape.append(jax.ShapeDtypeStruct((steps, band, c), BF16))

    def wrap(self, body, n_in, n_out):
        n = self.n
        if not n:
            return body

        def kern(*refs):
            ins, srcs = refs[:n_in], refs[n_in:n_in + n]
            outs, dsts = refs[n_in + n:n_in + n + n_out], refs[n_in + n + n_out:n_in + 2 * n + n_out]
            for s, d in zip(srcs, dsts):
                d[...] = s[...].astype(BF16)
            body(*ins, *outs, *refs[n_in + 2 * n + n_out:])
        return kern

    def split(self, outs, n_out):
        main = outs[0] if n_out == 1 else tuple(outs[:n_out])
        if not self.n:
            return main
        return main, [o.reshape(s) for o, s in zip(outs[n_out:], self.shapes)]


def _round_kernel(src_ref, dst_ref):
    dst_ref[...] = src_ref[...].astype(BF16)


def round_weight(w, n_rows):
    n_layers, _, c = w.shape
    band = ROW_TILE
    assert n_rows % band == 0
    return pl.pallas_call(
        _round_kernel,
        grid=(n_layers, n_rows // band),
        in_specs=[pl.BlockSpec((None, band, c), lambda l, i: (l, i, 0))],
        out_specs=pl.BlockSpec((None, band, c), lambda l, i: (l, i, 0)),
        out_shape=jax.ShapeDtypeStruct((n_layers, n_rows, c), BF16),
        compiler_params=_cparams("parallel", "parallel"),
        name="round_weight",
    )(w)


def _rope_table_kernel(pos_ref, inv_ref, cos_ref, sin_ref):
    ang = pos_ref[...] * inv_ref[...]
    lane = lax.broadcasted_iota(jnp.int32, ang.shape, 1)
    cos_ref[...] = jnp.cos(ang)
    s = jnp.sin(ang)
    sin_ref[...] = jnp.where(lane < HEAD_DIM // 2, -s, s)


def rope_tables(pos):
    r = pos.shape[0]
    half = HEAD_DIM // 2
    inv = ROPE_THETA ** (-jnp.arange(half, dtype=F32) * (2.0 / HEAD_DIM))
    inv = jnp.concatenate([inv, inv])[None, :]
    posf = jnp.broadcast_to(pos.astype(F32)[:, None], (r, HEAD_DIM))
    tr = min(r, ROW_TILE)
    return pl.pallas_call(
        _rope_table_kernel,
        grid=(r // tr,),
        in_specs=[pl.BlockSpec((tr, HEAD_DIM), lambda i: (i, 0)),
                  pl.BlockSpec((1, HEAD_DIM), lambda i: (0, 0))],
        out_specs=[pl.BlockSpec((tr, HEAD_DIM), lambda i: (i, 0))] * 2,
        out_shape=[jax.ShapeDtypeStruct((r, HEAD_DIM), F32)] * 2,
        compiler_params=_cparams("parallel"),
        name="rope_tables",
    )(posf, inv)


def _proj_kernel(*refs, kinds, tile_src, n_w, transposed, has_extra):
    rope = "rope" in kinds
    refs = list(refs)
    x_ref, g_ref = refs[:2]
    w_refs = refs[2:2 + n_w]
    hg_ref = refs[2 + n_w]
    at = 3 + n_w
    if rope:
        cos_ref, sin_ref = refs[at:at + 2]
        at += 2
    if has_extra:
        xs_ref = refs[at]
        at += 1
        if rope:
            cos_s_ref, sin_s_ref = refs[at:at + 2]
            at += 2
        o_ref, os_ref, xn_ref = refs[at:at + 3]
    else:
        o_ref, xn_ref = refs[at:at + 2]
    tm = x_ref.shape[0]

    def normed(x):
        return (x * _rms_scale(x) * g_ref[...]).astype(BF16)

    def tiles(xn, cos, sin, store):
        for t, kind in enumerate(kinds):
            which, col = tile_src[t]
            if transposed:
                y = lax.dot_general(xn, w_refs[which][col:col + COL_TILE, :], NT_DIMS, preferred_element_type=F32)
            else:
                y = jnp.dot(xn, w_refs[which][:, col:col + COL_TILE], preferred_element_type=F32)
            if kind == "plain":
                store(slice(t * COL_TILE, (t + 1) * COL_TILE), y)
                continue
            for j in range(COL_TILE // HEAD_DIM):
                osl = slice(t * COL_TILE + j * HEAD_DIM, t * COL_TILE + (j + 1) * HEAD_DIM)
                yh = y[:, j * HEAD_DIM:(j + 1) * HEAD_DIM]
                yh = yh * _rms_scale(yh) * hg_ref[:, osl]
                if kind == "rope":
                    yh = yh * cos + pltpu.roll(yh, HEAD_DIM // 2, axis=1) * sin
                store(osl, yh)

    def store_main(cols, y):
        o_ref[:, cols] = y

    def store_both(cols, y):
        o_ref[:, cols] = y[:tm]
        os_ref[:, cols] = y[tm:]

    xn_ref[:tm, :] = normed(x_ref[...])
    cos = cos_ref[...] if rope else None
    sin = sin_ref[...] if rope else None
    if not has_extra:
        tiles(xn_ref[...], cos, sin, store_main)
        return

    @pl.when(pl.program_id(0) == 0)
    def _():
        xn_ref[tm:, :] = normed(xs_ref[...])
        cos_all = jnp.concatenate([cos, cos_s_ref[...]], axis=0) if rope else None
        sin_all = jnp.concatenate([sin, sin_s_ref[...]], axis=0) if rope else None
        tiles(xn_ref[...], cos_all, sin_all, store_both)

    @pl.when(pl.program_id(0) != 0)
    def _():
        tiles(xn_ref[:tm, :], cos, sin, store_main)


def norm_proj(x, g, ws, layer, head_gain, kinds, cos=None, sin=None, cast=(), tile_src=None, transposed=False,
              extra=None):
    m, d = x.shape
    ws = list(ws) if isinstance(ws, (list, tuple)) else [ws]
    if tile_src is None:
        tile_src = [(0, t * COL_TILE) for t in range(len(kinds))]
    n_cols = COL_TILE * len(kinds)
    assert len(tile_src) == len(kinds) and head_gain.shape == (1, n_cols)
    rope = "rope" in kinds
    has_extra = extra is not None
    ms = extra[0].shape[0] if has_extra else 0
    tm = min(m, ROW_TILE)
    while tm * n_cols * 4 > PROJ_OUT_BLOCK_BYTES and tm % (2 * SUBLANES) == 0:
        tm //= 2
    assert m % tm == 0
    in_specs = [pl.BlockSpec((tm, d), lambda i: (i, 0)),
                pl.BlockSpec((1, d), lambda i: (0, 0))]
    for k, w in enumerate(ws):
        used = max(col + COL_TILE for which, col in tile_src if which == k)
        if transposed:
            assert used <= w.shape[1] and w.shape[2] == d
            in_specs.append(pl.BlockSpec((None, used, d), lambda i: (layer, 0, 0), pipeline_mode=RESIDENT))
            continue
        assert used <= w.shape[2] and (used % LANES == 0 or used == w.shape[2])
        in_specs.append(pl.BlockSpec((None, d, used), lambda i: (layer, 0, 0), pipeline_mode=RESIDENT))
    in_specs.append(pl.BlockSpec((1, n_cols), lambda i: (0, 0)))
    args = [x, g.reshape(1, d), *ws, head_gain]
    if rope:
        assert cos.shape[0] % tm == 0
        tab_blocks = cos.shape[0] // tm
        in_specs += [pl.BlockSpec((tm, HEAD_DIM), lambda i: (i % tab_blocks, 0))] * 2
        args += [cos, sin]
    out_specs = [pl.BlockSpec((tm, n_cols), lambda i: (i, 0))]
    out_shape = [jax.ShapeDtypeStruct((m, n_cols), F32)]
    if has_extra:
        whole = lambda a: pl.BlockSpec(a.shape, lambda i: (0, 0))
        xs, cos_s, sin_s = extra
        extras = [xs] + ([cos_s, sin_s] if rope else [])
        in_specs += [whole(a) for a in extras]
        args += extras
        out_specs.append(pl.BlockSpec((ms, n_cols), lambda i: (0, 0)))
        out_shape.append(jax.ShapeDtypeStruct((ms, n_cols), F32))
    n_out = len(out_specs)
    side = _SideCast(cast, m // tm, lambda i: i)
    outs = pl.pallas_call(
        side.wrap(functools.partial(_proj_kernel, kinds=tuple(kinds), tile_src=tuple(tile_src), n_w=len(ws),
                                    transposed=transposed, has_extra=has_extra),
                  len(args), n_out),
        grid=(m // tm,),
        in_specs=in_specs + side.in_specs,
        out_specs=out_specs + side.out_specs,
        out_shape=out_shape + side.out_shape,
        scratch_shapes=[pltpu.VMEM((tm + ms, d), BF16)],
        compiler_params=_cparams("arbitrary"),
        name="norm_proj",
    )(*args, *side.views)
    casts = [o.reshape(shp) for o, shp in zip(outs[n_out:], side.shapes)]
    return outs[0], (outs[1] if has_extra else None), casts


def _out_proj_kernel(*refs, has_extra):
    if has_extra:
        x_ref, a_ref, b_ref, wa_ref, wb_ref, xs_ref, as_ref, bs_ref, o_ref, os_ref = refs
    else:
        x_ref, a_ref, b_ref, wa_ref, wb_ref, o_ref = refs
    tm = x_ref.shape[0]

    def mixed(a, b):
        acc = jnp.dot(a, wa_ref[...], preferred_element_type=F32)
        return acc + jnp.dot(b, wb_ref[...], preferred_element_type=F32)

    if not has_extra:
        o_ref[...] = x_ref[...] + mixed(a_ref[...], b_ref[...])
        return

    @pl.when(pl.program_id(0) == 0)
    def _():
        acc = mixed(jnp.concatenate([a_ref[...], as_ref[...]], axis=0),
                    jnp.concatenate([b_ref[...], bs_ref[...]], axis=0))
        o_ref[...] = x_ref[...] + acc[:tm]
        os_ref[...] = xs_ref[...] + acc[tm:]

    @pl.when(pl.program_id(0) != 0)
    def _():
        o_ref[...] = x_ref[...] + mixed(a_ref[...], b_ref[...])


def out_proj(x, a, b, w, layer, extra=None):
    m, d = x.shape
    tm = min(m, ROW_TILE)
    ka, kb = a.shape[1], b.shape[1]
    assert w.shape[1] == ka + kb and ka % kb == 0
    has_extra = extra is not None
    in_specs = [pl.BlockSpec((tm, d), lambda i: (i, 0)),
                pl.BlockSpec((tm, ka), lambda i: (i, 0)),
                pl.BlockSpec((tm, kb), lambda i: (i, 0)),
                pl.BlockSpec((None, ka, d), lambda i: (layer, 0, 0), pipeline_mode=RESIDENT),
                pl.BlockSpec((None, kb, d), lambda i: (layer, ka // kb, 0), pipeline_mode=RESIDENT)]
    out_specs = [pl.BlockSpec((tm, d), lambda i: (i, 0))]
    out_shape = [jax.ShapeDtypeStruct((m, d), F32)]
    args = [x, a, b, w, w]
    if has_extra:
        in_specs += [pl.BlockSpec(e.shape, lambda i: (0, 0)) for e in extra]
        args += list(extra)
        out_specs.append(pl.BlockSpec(extra[0].shape, lambda i: (0, 0)))
        out_shape.append(jax.ShapeDtypeStruct(extra[0].shape, F32))
    outs = pl.pallas_call(
        functools.partial(_out_proj_kernel, has_extra=has_extra),
        grid=(m // tm,),
        in_specs=in_specs,
        out_specs=out_specs,
        out_shape=out_shape,
        compiler_params=_cparams("arbitrary"),
        name="out_proj",
    )(*args)
    return outs[0], (outs[1] if has_extra else None)


def _mlp_kernel(*refs, has_extra):
    if has_extra:
        x_ref, g_ref, wu_ref, wd_ref, xs_ref, o_ref, os_ref, xn_ref = refs
    else:
        x_ref, g_ref, wu_ref, wd_ref, o_ref, xn_ref = refs
    i, f = pl.program_id(0), pl.program_id(1)
    tm = x_ref.shape[0]

    def normed(x):
        return (x * _rms_scale(x) * g_ref[...]).astype(BF16)

    def ffn(xn):
        h = jnp.dot(xn, wu_ref[...], preferred_element_type=F32)
        h = jnp.maximum(h, 0.0)
        h = (h * h).astype(BF16)
        return jnp.dot(h, wd_ref[...], preferred_element_type=F32)

    @pl.when(f == 0)
    def _():
        x = x_ref[...]
        xn_ref[:tm, :] = normed(x)
        o_ref[...] = x

    if not has_extra:
        o_ref[...] += ffn(xn_ref[...])
        return

    @pl.when(i == 0)
    def _():
        @pl.when(f == 0)
        def _():
            xs = xs_ref[...]
            xn_ref[tm:, :] = normed(xs)
            os_ref[...] = xs

        upd = ffn(xn_ref[...])
        o_ref[...] += upd[:tm]
        os_ref[...] += upd[tm:]

    @pl.when(i != 0)
    def _():
        o_ref[...] += ffn(xn_ref[:tm, :])


def mlp(x, g, wu, wd, tf=1024, cast=(), extra=None):
    m, d = x.shape
    ff = wu.shape[1]
    tm = min(m, ROW_TILE)
    nf = ff // tf
    has_extra = extra is not None
    side = _SideCast(cast, (m // tm) * nf, lambda i, j: i * nf + j)
    in_specs = [pl.BlockSpec((tm, d), lambda i, j: (i, 0)),
                pl.BlockSpec((1, d), lambda i, j: (0, 0)),
                pl.BlockSpec((d, tf), lambda i, j: (0, j)),
                pl.BlockSpec((tf, d), lambda i, j: (j, 0))]
    out_specs = [pl.BlockSpec((tm, d), lambda i, j: (i, 0))]
    out_shape = [jax.ShapeDtypeStruct((m, d), F32)]
    args = [x, g.reshape(1, d), wu, wd]
    ms = extra.shape[0] if has_extra else 0
    if has_extra:
        whole = pl.BlockSpec((ms, d), lambda i, j: (0, 0))
        in_specs.append(whole)
        out_specs.append(whole)
        out_shape.append(jax.ShapeDtypeStruct((ms, d), F32))
        args.append(extra)
    scratch = [pltpu.VMEM((tm + ms, d), BF16)]
    n_out = len(out_specs)
    outs = pl.pallas_call(
        side.wrap(functools.partial(_mlp_kernel, has_extra=has_extra), len(args), n_out),
        grid=(m // tm, nf),
        in_specs=in_specs + side.in_specs,
        out_specs=out_specs + side.out_specs,
        out_shape=out_shape + side.out_shape,
        scratch_shapes=scratch,
        compiler_params=_cparams("arbitrary", "arbitrary"),
        name="mlp",
    )(*args, *side.views)
    casts = [o.reshape(shp) for o, shp in zip(outs[n_out:], side.shapes)]
    return outs[0], (outs[1] if has_extra else None), casts


def _split3(x):
    hi = x.astype(BF16)
    r1 = x - hi.astype(F32)
    mid = r1.astype(BF16)
    lo = (r1 - mid.astype(F32)).astype(BF16)
    return hi, mid, lo


def _ssd_kernel(xbc_ref, dt_ref, z_ref, cw_ref, cb_ref, dtb_ref, alog_ref, dsk_ref, gg_ref, tri_ref, expand_ref,
                conv0_ref, h0_ref, y_ref, convo_ref, ho_ref,
                ext_ref, ht_ref, yd_ref, st_ref, *, t_real, n_heads, d_inner):
    L = SSD_CHUNK
    P = SSM_HEAD_DIM
    N = SSM_D_STATE
    G = SSM_GROUPS
    hpg = n_heads // G
    gw = hpg * P
    hist = CONV_W - 1
    base = SUBLANES - hist
    c = pl.program_id(1)
    nc = pl.num_programs(1)

    @pl.when(c == 0)
    def _():
        ext_ref[base:SUBLANES, :] = conv0_ref[...]
        for g in range(G):
            ht_ref[g] = h0_ref[g].T

    ext_ref[SUBLANES:SUBLANES + L, :] = xbc_ref[...]
    ext = ext_ref[...]
    acc = cb_ref[...] + ext[SUBLANES:, :] * cw_ref[hist:CONV_W, :]
    for k in range(1, CONV_W):
        acc = acc + pltpu.roll(ext, k, axis=0)[SUBLANES:, :] * cw_ref[hist - k:CONV_W - k, :]
    xc = acc * _sigmoid(acc)

    @pl.when(c == nc - 1)
    def _():
        n_in_last = t_real - ((t_real - 1) // L) * L
        convo_ref[...] = ext_ref[base + n_in_last:base + n_in_last + hist, :]

    ext_ref[base:SUBLANES, :] = ext_ref[base + L:SUBLANES + L, :]

    dt_raw = dt_ref[...] + dtb_ref[...]
    dt = jnp.maximum(dt_raw, 0.0) + jnp.log1p(jnp.exp(-jnp.abs(dt_raw)))
    row = lax.broadcasted_iota(jnp.int32, (L, LANES), 0) + c * L
    dt = jnp.where(row < t_real, dt, 0.0)
    da = dt * (-jnp.exp(alog_ref[...]))
    li = lax.broadcasted_iota(jnp.int32, (L, L), 0)
    si = lax.broadcasted_iota(jnp.int32, (L, L), 1)
    causal = li >= si
    a_cs = sum(jnp.dot(tri_ref[...], part, preferred_element_type=F32) for part in _split3(da))
    a_cs_t = a_cs.T
    dt_t = dt.T
    w_t = dt_t * jnp.exp(a_cs_t[:, L - 1:L] - a_cs_t)
    e_exp = sum(jnp.dot(part, expand_ref[...], preferred_element_type=F32) for part in _split3(jnp.exp(a_cs)))
    xb = xc[:, :d_inner].astype(BF16)
    low_half = lax.broadcasted_iota(jnp.int32, (L, 2 * P), 1) < P

    for g in range(G):
        b_g = xc[:, d_inner + g * N:d_inner + (g + 1) * N]
        c_g = xc[:, d_inner + G * N + g * N:d_inner + G * N + (g + 1) * N].astype(BF16)
        cbm = lax.dot_general(c_g, b_g.astype(BF16), NT_DIMS, preferred_element_type=F32)
        b_t = b_g.T
        y_off = jnp.dot(c_g, ht_ref[g].astype(BF16), preferred_element_type=F32)
        for kp in range(hpg // 2):
            h0 = g * hpg + 2 * kp
            x_pair = xb[:, h0 * P:(h0 + 2) * P]
            yd, st = [], []
            for h in (h0, h0 + 1):
                seg = a_cs[:, h:h + 1] - a_cs_t[h:h + 1, :]
                dec = jnp.exp(jnp.where(causal, seg, -jnp.inf))
                mh = (cbm * dec * dt_t[h:h + 1, :]).astype(BF16)
                yd.append(jnp.dot(mh, x_pair, preferred_element_type=F32))
                btw = (b_t * w_t[h:h + 1, :]).astype(BF16)
                st.append(jnp.dot(btw, x_pair, preferred_element_type=F32))
            yd_ref[:, 2 * kp * P:(2 * kp + 2) * P] = jnp.where(low_half, yd[0], yd[1])
            st_ref[:, 2 * kp * P:(2 * kp + 2) * P] = jnp.where(low_half, st[0], st[1])
        gs = slice(g * gw, (g + 1) * gw)
        e_g = e_exp[:, gs]
        y_g = yd_ref[...] + y_off * e_g + dsk_ref[:, gs] * xc[:, gs]
        ht_ref[g] = ht_ref[g] * e_g[L - 1:L, :] + st_ref[...]
        z_g = z_ref[:, gs]
        gated = y_g * (z_g * _sigmoid(z_g))
        y_ref[:, gs] = (gated * _rms_scale(gated) * gg_ref[:, gs]).astype(y_ref.dtype)

    @pl.when(c == nc - 1)
    def _():
        for g in range(G):
            ho_ref[g] = ht_ref[g].T


def ssd_mixer(proj, col_xbc, col_dt, col_z, t_real, conv_w, conv_b, dt_bias, a_log, d_skip, gate_g,
              conv_prev, h_prev, cast=()):
    bt, tp, _ = proj.shape
    conv_dim = conv_w.shape[1]
    n_heads = dt_bias.shape[0]
    d_inner = n_heads * SSM_HEAD_DIM
    gw = d_inner // SSM_GROUPS
    L = SSD_CHUNK
    assert tp % L == 0 and (tp - t_real) < L
    assert col_xbc % conv_dim == 0 and col_dt % LANES == 0 and col_z % d_inner == 0
    pad = LANES - n_heads
    dtb = jnp.pad(dt_bias, (0, pad)).reshape(1, LANES)
    alog = jnp.pad(a_log, (0, pad)).reshape(1, LANES)
    dsk = jnp.repeat(d_skip, SSM_HEAD_DIM).reshape(1, d_inner)
    tri = jnp.tril(jnp.ones((L, L), BF16))
    expand = jnp.repeat(jnp.eye(LANES, dtype=BF16)[:, :n_heads], SSM_HEAD_DIM, axis=1)
    kern = functools.partial(_ssd_kernel, t_real=t_real, n_heads=n_heads, d_inner=d_inner)
    vec = lambda n: pl.BlockSpec((1, n), lambda b, c: (0, 0))
    nc = tp // L
    side = _SideCast(cast, bt * nc, lambda b, c: b * nc + c)
    outs = pl.pallas_call(
        side.wrap(kern, 13, 3),
        grid=(bt, nc),
        in_specs=[pl.BlockSpec((None, L, conv_dim), lambda b, c: (b, c, col_xbc // conv_dim)),
                  pl.BlockSpec((None, L, LANES), lambda b, c: (b, c, col_dt // LANES)),
                  pl.BlockSpec((None, L, d_inner), lambda b, c: (b, c, col_z // d_inner)),
                  pl.BlockSpec((CONV_W, conv_dim), lambda b, c: (0, 0)),
                  vec(conv_dim), vec(LANES), vec(LANES), vec(d_inner), vec(d_inner),
                  pl.BlockSpec((L, L), lambda b, c: (0, 0)),
                  pl.BlockSpec((LANES, d_inner), lambda b, c: (0, 0)),
                  pl.BlockSpec((None, CONV_W - 1, conv_dim), lambda b, c: (b, 0, 0)),
                  pl.BlockSpec((None, SSM_GROUPS, gw, SSM_D_STATE), lambda b, c: (b, 0, 0, 0))] + side.in_specs,
        out_specs=[pl.BlockSpec((None, L, d_inner), lambda b, c: (b, c, 0)),
                   pl.BlockSpec((None, CONV_W - 1, conv_dim), lambda b, c: (b, 0, 0)),
                   pl.BlockSpec((None, SSM_GROUPS, gw, SSM_D_STATE), lambda b, c: (b, 0, 0, 0))] + side.out_specs,
        out_shape=[jax.ShapeDtypeStruct((bt, tp, d_inner), BF16),
                   jax.ShapeDtypeStruct((bt, CONV_W - 1, conv_dim), F32),
                   jax.ShapeDtypeStruct((bt, SSM_GROUPS, gw, SSM_D_STATE), F32)] + side.out_shape,
        scratch_shapes=[pltpu.VMEM((SUBLANES + L, conv_dim), F32),
                        pltpu.VMEM((SSM_GROUPS, SSM_D_STATE, gw), F32),
                        pltpu.VMEM((L, gw), F32),
                        pltpu.VMEM((SSM_D_STATE, gw), F32)],
        compiler_params=_cparams("parallel", "arbitrary"),
        name="ssd_mixer",
    )(proj, proj, proj, conv_w, conv_b.reshape(1, conv_dim), dtb, alog, dsk, gate_g.reshape(1, d_inner),
      tri, expand, conv_prev, h_prev, *side.views)
    return side.split(outs, 3)


def _mem_attn_kernel(q_ref, k_ref, v_ref, o_ref):
    scale = HEAD_DIM ** -0.5
    headed = len(k_ref.shape) == 3
    for h in range(MEM_HEADS):
        sl = slice(h * HEAD_DIM, (h + 1) * HEAD_DIM)
        q = q_ref[:, sl].astype(BF16)
        k = (k_ref[:, h, :] if headed else k_ref[:, sl]).astype(BF16)
        v = (v_ref[:, h, :] if headed else v_ref[:, sl]).astype(BF16)
        s = lax.dot_general(q, k, NT_DIMS, preferred_element_type=F32) * scale
        e = jnp.exp(s - jnp.max(s, axis=-1, keepdims=True))
        den = jnp.sum(e, axis=-1, keepdims=True)
        o = jnp.dot(e.astype(BF16), v, preferred_element_type=F32) / den
        o_ref[:, sl] = o.astype(o_ref.dtype)


def mem_attention(proj, col_q, k, v, kv_at):
    bt, t, _ = proj.shape
    width = MEM_HEADS * HEAD_DIM
    tq = min(t, ROW_TILE)
    if k.ndim == 5:
        mlen = k.shape[2]
        kv_specs = [pl.BlockSpec((None, None, mlen, MEM_HEADS, HEAD_DIM), lambda b, i: (kv_at, b, 0, 0, 0))] * 2
    else:
        mlen = k.shape[1]
        kv_specs = [pl.BlockSpec((None, mlen, width), lambda b, i, c=c: (b, 0, c // width)) for c in kv_at]
    return pl.pallas_call(
        _mem_attn_kernel,
        grid=(bt, t // tq),
        in_specs=[pl.BlockSpec((None, tq, width), lambda b, i: (b, i, col_q // width))] + kv_specs,
        out_specs=pl.BlockSpec((None, tq, width), lambda b, i: (b, i, 0)),
        out_shape=jax.ShapeDtypeStruct((bt, t, width), BF16),
        compiler_params=_cparams("parallel", "parallel"),
        name="mem_attention",
    )(proj, k, v)


def _run_units(first, count, unit, unroll):
    trips = count // unroll
    if trips == 1:
        trips = 0
    if trips:
        def trip(it, carry):
            for k in range(unroll):
                unit(first + it * unroll + k)
            return carry
        lax.fori_loop(0, trips, trip, 0)
    for k in range(trips * unroll, count):
        unit(first + k)


def _dil_attn_kernel(*refs, groups, rb, unroll):
    ng = len(groups)
    q_refs = refs[:ng]
    kp_ref, kc_ref, vp_ref, vc_ref, o_ref = refs[ng:ng + 5]
    og = refs[ng + 5:2 * ng + 5]
    lg = refs[2 * ng + 5:3 * ng + 5]
    i = pl.program_id(1)
    scale = HEAD_DIM ** -0.5

    for gi, (win, d) in enumerate(groups):
        w = win // d
        dw = d * w
        qi = lax.broadcasted_iota(jnp.int32, (w, 2 * w), 0)
        kj = lax.broadcasted_iota(jnp.int32, (w, 2 * w), 1)
        band = (kj >= qi) & (kj <= qi + w)
        band_first = band & (kj >= jnp.where(i > 0, 0, w))

        def rows(start, size, d=d, w=w):
            if d == 1:
                return pl.ds(pl.multiple_of(start, w), size)
            return pl.ds(start, size, stride=d)

        def attend(rows_q, kk, vv, mask, gi=gi, w=w):
            q = (q_refs[gi][rows_q, :] * (scale * LOG2_E)).astype(BF16)
            s = lax.dot_general(q, kk.astype(BF16), NT_DIMS, preferred_element_type=F32)
            s = jnp.where(mask, s, -jnp.inf)
            m = jnp.max(s, axis=-1, keepdims=True)
            e = jnp.exp2(s - m)
            den = jnp.sum(e, axis=-1, keepdims=True)
            o = jnp.dot(e.astype(BF16), vv.astype(BF16), preferred_element_type=F32) / den
            og[gi][rows_q, :] = o
            lg[gi][rows_q, :] = jnp.broadcast_to(m * LN_2 + jnp.log(den), (w, HEAD_DIM))

        def first_unit(r, rows=rows, attend=attend, w=w, dw=dw, mask=band_first):
            rows_q, rows_p = rows(r, w), rows(rb - dw + r, w)
            kk = jnp.concatenate([kp_ref[rows_p, :], kc_ref[rows_q, :]], axis=0)
            vv = jnp.concatenate([vp_ref[rows_p, :], vc_ref[rows_q, :]], axis=0)
            attend(rows_q, kk, vv, mask)

        def later_unit(u, rows=rows, attend=attend, d=d, w=w, dw=dw, mask=band):
            sb = u // d
            start_q = sb * dw + (u - sb * d)
            rows_k = rows(start_q - dw, 2 * w)
            attend(rows(start_q, w), kc_ref[rows_k, :], vc_ref[rows_k, :], mask)

        _run_units(0, d, first_unit, unroll)
        _run_units(d, rb // w - d, later_unit, unroll)

    ls = [r[...] for r in lg]
    mm = functools.reduce(jnp.maximum, ls)
    ws = [jnp.exp(l - mm) for l in ls]
    num = sum(wg * r[...] for wg, r in zip(ws, og))
    o_ref[...] = (num / sum(ws)).astype(o_ref.dtype)


def dilated_attention(proj, kv):
    bt, t, _ = proj.shape
    ng = len(DIL_GROUPS)
    rb = min(t, max(win for win, _ in DIL_GROUPS))
    for win, d in DIL_GROUPS:
        assert win % d == 0 and rb % win == 0
    assert t % rb == 0
    blk = lambda f: pl.BlockSpec((None, rb, HEAD_DIM), f)
    prev = lambda i: jnp.maximum(i - 1, 0)
    q_specs = [blk(lambda b, i, h, g=g: (b, i, g * DIL_HEADS + h)) for g in range(ng)]
    kv_specs = [blk(lambda b, i, h: (b, prev(i), h)), blk(lambda b, i, h: (b, i, h)),
                blk(lambda b, i, h: (b, prev(i), DIL_HEADS + h)), blk(lambda b, i, h: (b, i, DIL_HEADS + h))]
    return pl.pallas_call(
        functools.partial(_dil_attn_kernel, groups=DIL_GROUPS, rb=rb, unroll=DIL_UNROLL),
        grid=(bt, t // rb, DIL_HEADS),
        in_specs=q_specs + kv_specs,
        out_specs=blk(lambda b, i, h: (b, i, h)),
        out_shape=jax.ShapeDtypeStruct((bt, t, DIL_HEADS * HEAD_DIM), BF16),
        scratch_shapes=[pltpu.VMEM((rb, HEAD_DIM), F32)] * (2 * ng),
        compiler_params=_cparams("parallel", "parallel", "parallel"),
        name="dilated_attention",
    )(*([proj] * ng), kv, kv, kv, kv)


def _dil_decode_kernel(*refs):
    ng = len(DIL_GROUPS)
    q_ref, kv_ref = refs[:2]
    k_refs, v_refs = refs[2:2 + ng], refs[2 + ng:2 + 2 * ng]
    o_ref = refs[-1]
    width = DIL_HEADS * HEAD_DIM
    scale = HEAD_DIM ** -0.5
    for h in range(DIL_HEADS):
        sl = slice(h * HEAD_DIM, (h + 1) * HEAD_DIM)
        k_new = kv_ref[:, h * HEAD_DIM:(h + 1) * HEAD_DIM]
        v_new = kv_ref[:, width + h * HEAD_DIM:width + (h + 1) * HEAD_DIM]
        outs, lses = [], []
        for gi in range(ng):
            kc = k_refs[gi][:, h, :].astype(BF16)
            vc = v_refs[gi][:, h, :].astype(BF16)
            q = q_ref[:, gi * width + h * HEAD_DIM:gi * width + (h + 1) * HEAD_DIM]
            s = lax.dot_general(q.astype(BF16), kc, NT_DIMS, preferred_element_type=F32) * scale
            s_self = jnp.sum(q * k_new, axis=-1, keepdims=True) * scale
            m = jnp.maximum(jnp.max(s, axis=-1, keepdims=True), s_self)
            e = jnp.exp(s - m)
            e_self = jnp.exp(s_self - m)
            den = jnp.sum(e, axis=-1, keepdims=True) + e_self
            o = jnp.dot(e.astype(BF16), vc, preferred_element_type=F32) + e_self * v_new
            outs.append(o / den)
            lses.append(m + jnp.log(den))
        mm = functools.reduce(jnp.maximum, lses)
        ws = [jnp.exp(l - mm) for l in lses]
        num = sum(wg * o for wg, o in zip(ws, outs))
        o_ref[:, sl] = (num / sum(ws)).astype(o_ref.dtype)


def dilated_decode(proj, kv_new, k_cache, v_cache):
    bt, rows, nq = proj.shape
    lc = k_cache.shape[1]
    width = DIL_HEADS * HEAD_DIM
    views, specs = [], []
    for cache in (k_cache, v_cache):
        for win, dil in DIL_GROUPS:
            assert lc % win == 0 and win % dil == 0
            w = win // dil
            views.append(cache.reshape(bt, lc // dil, dil, DIL_HEADS, HEAD_DIM))
            specs.append(pl.BlockSpec((None, w, None, DIL_HEADS, HEAD_DIM),
                                      lambda b, last=lc // win - 1: (b, last, 0, 0, 0)))
    return pl.pallas_call(
        _dil_decode_kernel,
        grid=(bt,),
        in_specs=[pl.BlockSpec((None, rows, nq), lambda b: (b, 0, 0)),
                  pl.BlockSpec((None, rows, 2 * width), lambda b: (b, 0, 0))] + specs,
        out_specs=pl.BlockSpec((None, rows, width), lambda b: (b, 0, 0)),
        out_shape=jax.ShapeDtypeStruct((bt, rows, width), BF16),
        compiler_params=_cparams("parallel"),
        name="dilated_decode",
    )(proj, kv_new, *views)


def kernel(x_prompt, x_sample, state_conv, state_ssm, cache_win_k, cache_win_v, cache_mem_k, cache_mem_v,
           mem_prompt, norm_mix_g, norm_mlp_g, norm_mem_g, w_mem_k, w_mem_v, mem_q_norm_g, mem_k_norm_g,
           w_up, w_down, w_in_a, conv_w, conv_b, dt_bias, a_log, d_skip, gate_norm_g, w_out_a,
           w_in_b, q_norm_g, w_out_b, kv_norm_g, w_k_shared, w_v_shared, k_norm_g):
    depth = w_up.shape[0]
    n_a = w_in_a.shape[0]
    d_model = x_prompt.shape[-1]
    n_heads = dt_bias.shape[1]
    d_inner = n_heads * SSM_HEAD_DIM
    conv_dim = conv_w.shape[2]
    mem_w = MEM_HEADS * HEAD_DIM
    kv_w = DIL_HEADS * HEAD_DIM
    dil_q_w = len(DIL_GROUPS) * kv_w

    s1, s2, s3 = d_inner, d_inner + conv_dim, d_inner + conv_dim + n_heads
    assert s1 % COL_TILE == 0 and s2 % COL_TILE == 0
    w_in_a_t = jnp.swapaxes(w_in_a, 1, 2)
    w_a_main = round_weight(w_in_a_t, s2)
    tail = lax.optimization_barrier(w_in_a_t[:, s2:]).astype(BF16)
    w_a_tail = jnp.concatenate([tail[:, :n_heads], jnp.zeros((n_a, COL_TILE - n_heads, d_model), BF16),
                                tail[:, n_heads:]], axis=1)
    tiles_a = ([(0, s1 + c) for c in range(0, conv_dim, COL_TILE)] + [(1, 0)]
               + [(0, c) for c in range(0, d_inner, COL_TILE)]
               + [(1, COL_TILE + c) for c in range(0, mem_w, COL_TILE)])
    col_xbc, col_dt, col_z = 0, conv_dim, conv_dim + COL_TILE
    col_qm_a = col_z + d_inner
    kinds_a = ["plain"] * (col_qm_a // COL_TILE) + ["norm"] * (mem_w // COL_TILE)
    kinds_b = ["rope"] * (dil_q_w // COL_TILE) + ["norm"] * (mem_w // COL_TILE)
    kinds_kv = ["rope"] * (kv_w // COL_TILE) + ["plain"] * (kv_w // COL_TILE)
    kinds_mkv = ["norm"] * (mem_w // COL_TILE) + ["plain"] * (mem_w // COL_TILE)
    w_b = w_in_b.astype(BF16)
    w_kv = jnp.concatenate([w_k_shared, w_v_shared], axis=-1).astype(BF16)[None]
    w_mkv = jnp.concatenate([w_mem_k, w_mem_v], axis=-1).astype(BF16)
    w_out_a_b, w_out_b_b = w_out_a.astype(BF16), w_out_b.astype(BF16)
    assert n_a >= 1
    w_up_b, w_down_b = [None] * depth, [None] * depth

    def gain_row(width, pieces):
        parts, at = [], 0
        for start, g, reps in pieces:
            parts += [jnp.ones((start - at,), F32), jnp.tile(g, reps)]
            at = start + reps * HEAD_DIM
        parts.append(jnp.ones((width - at,), F32))
        return jnp.concatenate(parts).reshape(1, width)

    class Group:
        def __init__(self, x, pos_rows, conv_prev, ssm_prev, mem_kv, k_past, v_past, t_real):
            self.bt, self.t, _ = x.shape
            self.m = self.bt * self.t
            self.x2 = x.reshape(self.m, d_model)
            self.cos, self.sin = rope_tables(pos_rows)
            self.conv_prev, self.ssm_prev, self.mem_kv = conv_prev, ssm_prev, mem_kv
            self.k_past, self.v_past, self.t_real = k_past, v_past, t_real
            self.conv_new, self.ssm_new, self.kv3 = [], [], None

    def in_proj_a(prompt, sample, l):
        hg = gain_row(col_qm_a + mem_w, [(col_qm_a, mem_q_norm_g[l], MEM_HEADS)])
        proj_p, proj_s, rounded = norm_proj(prompt.x2, norm_mix_g[l], [w_a_main, w_a_tail], l, hg, kinds_a,
                                            tile_src=tiles_a, transposed=True, extra=(sample.x2, None, None),
                                            cast=((w_up, 0),) if l == 0 else ())
        if l == 0:
            (w_up_b[0],) = rounded
        return proj_p, proj_s

    def mix_a(gr, proj, l, round_mlp_weights):
        bt, t, m = gr.bt, gr.t, gr.m
        proj3 = proj.reshape(bt, t, -1)
        tp = -(-t // SSD_CHUNK) * SSD_CHUNK
        proj_ssd = proj3 if tp == t else jnp.pad(proj3, ((0, 0), (0, tp - t), (0, 0)))
        ssd = ssd_mixer(proj_ssd, col_xbc, col_dt, col_z, gr.t_real, conv_w[l], conv_b[l],
                        dt_bias[l], a_log[l], d_skip[l], gate_norm_g[l],
                        gr.conv_prev[l], gr.ssm_prev[l].reshape(bt, SSM_GROUPS, -1, SSM_D_STATE),
                        cast=((w_down, 0),) if round_mlp_weights else ())
        if round_mlp_weights:
            ssd, (w_down_b[0],) = ssd
        y, c_new, h_new = ssd
        gr.conv_new.append(c_new)
        gr.ssm_new.append(h_new.reshape(bt, n_heads, SSM_HEAD_DIM, SSM_D_STATE))
        mo = mem_attention(proj3, col_qm_a, *gr.mem_kv(l)).reshape(m, mem_w)
        return y[:, :t].reshape(m, d_inner), mo

    def shared_kv(prompt, sample):
        hg = gain_row(2 * kv_w, [(0, k_norm_g, DIL_HEADS)])
        kv_p, kv_s, _ = norm_proj(prompt.x2, kv_norm_g, w_kv, 0, hg, kinds_kv, prompt.cos, prompt.sin,
                                  extra=(sample.x2, sample.cos, sample.sin))
        prompt.kv3 = kv_p.reshape(prompt.bt, prompt.t, 2 * kv_w)
        sample.kv3 = kv_s.reshape(sample.bt, sample.t, 2 * kv_w)

    def in_proj_b(prompt, sample, l):
        j = l - n_a
        hg = gain_row(dil_q_w + mem_w, [(0, q_norm_g[j], dil_q_w // HEAD_DIM),
                                        (dil_q_w, mem_q_norm_g[l], MEM_HEADS)])
        proj_p, proj_s, _ = norm_proj(prompt.x2, norm_mix_g[l], w_b, j, hg, kinds_b, prompt.cos, prompt.sin,
                                      extra=(sample.x2, sample.cos, sample.sin))
        return proj_p, proj_s

    def mix_b(gr, proj, l):
        proj3 = proj.reshape(gr.bt, gr.t, -1)
        if gr.k_past is None:
            att = dilated_attention(proj3, gr.kv3)
        else:
            att = dilated_decode(proj3, gr.kv3, gr.k_past, gr.v_past)
        mo = mem_attention(proj3, dil_q_w, *gr.mem_kv(l)).reshape(gr.m, mem_w)
        return att.reshape(gr.m, kv_w), mo

    bp, t_p, _ = x_prompt.shape
    mlen = mem_prompt.shape[1]
    mem2 = mem_prompt.reshape(bp * mlen, d_model)
    mkv_p = []
    for l in range(depth):
        hg = gain_row(2 * mem_w, [(0, mem_k_norm_g[l], MEM_HEADS)])
        mkv_p.append(norm_proj(mem2, norm_mem_g[l], w_mkv, l, hg, kinds_mkv)[0].reshape(bp, mlen, 2 * mem_w))
    conv0 = jnp.zeros((n_a, bp, CONV_W - 1, conv_dim), F32)
    ssm0 = jnp.zeros((n_a, bp, n_heads, SSM_HEAD_DIM, SSM_D_STATE), F32)
    prompt = Group(x_prompt, jnp.arange(t_p, dtype=jnp.int32), conv0, ssm0,
                   lambda l: (mkv_p[l], mkv_p[l], (0, mem_w)), None, None, t_p)

    bs, t_s, _ = x_sample.shape
    assert t_s == 1
    xs = jnp.pad(x_sample, ((0, 0), (0, SAMPLE_ROWS - t_s), (0, 0)))
    pos_s = jnp.full((bs * SAMPLE_ROWS,), PAST_LEN, jnp.int32)
    sample = Group(xs, pos_s, state_conv, state_ssm, lambda l: (cache_mem_k, cache_mem_v, l),
                   cache_win_k, cache_win_v, t_s)

    for l in range(depth):
        if l < n_a:
            proj_p, proj_s = in_proj_a(prompt, sample, l)
            a_p, b_p = mix_a(prompt, proj_p, l, round_mlp_weights=(l == 0))
            a_s, b_s = mix_a(sample, proj_s, l, round_mlp_weights=False)
            w_out, lw = w_out_a_b, l
        else:
            if l == n_a:
                shared_kv(prompt, sample)
            proj_p, proj_s = in_proj_b(prompt, sample, l)
            a_p, b_p = mix_b(prompt, proj_p, l)
            a_s, b_s = mix_b(sample, proj_s, l)
            w_out, lw = w_out_b_b, l - n_a
        prompt.x2, sample.x2 = out_proj(prompt.x2, a_p, b_p, w_out, lw, extra=(sample.x2, a_s, b_s))
        nxt = ((w_up, l + 1), (w_down, l + 1)) if l + 1 < depth else ()
        prompt.x2, sample.x2, rounded = mlp(prompt.x2, norm_mlp_g[l], w_up_b[l], w_down_b[l],
                                            cast=nxt, extra=sample.x2)
        if nxt:
            w_up_b[l + 1], w_down_b[l + 1] = rounded

    y_p = prompt.x2.reshape(bp, t_p, d_model)
    conv_p, ssm_p, kv_p = jnp.stack(prompt.conv_new), jnp.stack(prompt.ssm_new), prompt.kv3
    mkv_all = jnp.stack(mkv_p)
    mem_k_p = mkv_all[..., :mem_w].reshape(depth, bp, mlen, MEM_HEADS, HEAD_DIM)
    mem_v_p = mkv_all[..., mem_w:].reshape(depth, bp, mlen, MEM_HEADS, HEAD_DIM)
    keep = min(max(w for w, _ in DIL_GROUPS), t_p)
    win_k_p = kv_p[:, t_p - keep:, :kv_w].reshape(bp, keep, DIL_HEADS, HEAD_DIM)
    win_v_p = kv_p[:, t_p - keep:, kv_w:].reshape(bp, keep, DIL_HEADS, HEAD_DIM)

    y_s = sample.x2.reshape(bs, SAMPLE_ROWS, d_model)
    conv_s, ssm_s, kv_s = jnp.stack(sample.conv_new), jnp.stack(sample.ssm_new), sample.kv3
    y_s = y_s[:, :t_s]
    k_s = kv_s[:, :t_s, :kv_w].reshape(bs, t_s, DIL_HEADS, HEAD_DIM)
    v_s = kv_s[:, :t_s, kv_w:].reshape(bs, t_s, DIL_HEADS, HEAD_DIM)

    return (y_p, y_s, conv_p, ssm_p, win_k_p, win_v_p, mem_k_p, mem_v_p, conv_s, ssm_s, k_s, v_s)
```

```python
import functools

import jax
import jax.numpy as jnp
from jax import lax
from jax.experimental import pallas as pl
from jax.experimental.pallas import tpu as pltpu

F32 = jnp.float32
BF16 = jnp.bfloat16

HEAD_DIM = 128
SSM_HEAD_DIM = 64
SSM_GROUPS = 4
SSM_D_STATE = 128
CONV_W = 4
SSD_CHUNK = 128
MEM_HEADS = 4
DIL_GROUPS = ((128, 1), (512, 4), (2048, 16))
DIL_HEADS = 4
PAST_LEN = 16384
ROPE_THETA = 10000.0
EPS = 1e-6
LOG2_E = 1.4426950408889634
LN_2 = 0.6931471805599453

LANES = 128
SUBLANES = 8
VMEM_LIMIT = 48 * 1024 * 1024
COL_TILE = 512
ROW_TILE = 512
PROJ_OUT_BLOCK_BYTES = 5 * 1024 * 1024
SAMPLE_ROWS = SUBLANES
DIL_UNROLL = 16

NT_DIMS = (((1,), (1,)), ((), ()))
RESIDENT = pl.Buffered(1)


def _cparams(*sem):
    return pltpu.CompilerParams(dimension_semantics=sem, vmem_limit_bytes=VMEM_LIMIT)


def _sigmoid(x):
    return 1.0 / (1.0 + jnp.exp(-x))


def _rms_scale(x):
    return lax.rsqrt(jnp.mean(x * x, axis=-1, keepdims=True) + EPS)


class _SideCast:
    def __init__(self, mats, steps, step_of):
        self.n = len(mats)
        self.shapes = [a.shape[1:] for a, _ in mats]
        self.views, self.in_specs, self.out_specs, self.out_shape = [], [], [], []
        for a, layer in mats:
            n_layers, r, c = a.shape
            assert r % (steps * 2 * SUBLANES) == 0
            band = r // steps
            self.views.append(a.reshape(n_layers, steps, band, c))
            self.in_specs.append(pl.BlockSpec((None, None, band, c),
                                              lambda *ids, layer=layer: (layer, step_of(*ids), 0, 0)))
            self.out_specs.append(pl.BlockSpec((None, band, c), lambda *ids: (step_of(*ids), 0, 0)))
            self.out_shape.append(jax.ShapeDtypeStruct((steps, band, c), BF16))

    def wrap(self, body, n_in, n_out):
        n = self.n
        if not n:
            return body

        def kern(*refs):
            ins, srcs = refs[:n_in], refs[n_in:n_in + n]
            outs, dsts = refs[n_in + n:n_in + n + n_out], refs[n_in + n + n_out:n_in + 2 * n + n_out]
            for s, d in zip(srcs, dsts):
                d[...] = s[...].astype(BF16)
            body(*ins, *outs, *refs[n_in + 2 * n + n_out:])
        return kern

    def split(self, outs, n_out):
        main = outs[0] if n_out == 1 else tuple(outs[:n_out])
        if not self.n:
            return main
        return main, [o.reshape(s) for o, s in zip(outs[n_out:], self.shapes)]


def _round_kernel(src_ref, dst_ref):
    dst_ref[...] = src_ref[...].astype(BF16)


def round_weight(w, n_rows):
    n_layers, _, c = w.shape
    band = ROW_TILE
    assert n_rows % band == 0
    return pl.pallas_call(
        _round_kernel,
        grid=(n_layers, n_rows // band),
        in_specs=[pl.BlockSpec((None, band, c), lambda l, i: (l, i, 0))],
        out_specs=pl.BlockSpec((None, band, c), lambda l, i: (l, i, 0)),
        out_shape=jax.ShapeDtypeStruct((n_layers, n_rows, c), BF16),
        compiler_params=_cparams("parallel", "parallel"),
        name="round_weight",
    )(w)


def _rope_table_kernel(pos_ref, inv_ref, cos_ref, sin_ref):
    ang = pos_ref[...] * inv_ref[...]
    lane = lax.broadcasted_iota(jnp.int32, ang.shape, 1)
    cos_ref[...] = jnp.cos(ang)
    s = jnp.sin(ang)
    sin_ref[...] = jnp.where(lane < HEAD_DIM // 2, -s, s)


def rope_tables(pos):
    r = pos.shape[0]
    half = HEAD_DIM // 2
    inv = ROPE_THETA ** (-jnp.arange(half, dtype=F32) * (2.0 / HEAD_DIM))
    inv = jnp.concatenate([inv, inv])[None, :]
    posf = jnp.broadcast_to(pos.astype(F32)[:, None], (r, HEAD_DIM))
    tr = min(r, ROW_TILE)
    return pl.pallas_call(
        _rope_table_kernel,
        grid=(r // tr,),
        in_specs=[pl.BlockSpec((tr, HEAD_DIM), lambda i: (i, 0)),
                  pl.BlockSpec((1, HEAD_DIM), lambda i: (0, 0))],
        out_specs=[pl.BlockSpec((tr, HEAD_DIM), lambda i: (i, 0))] * 2,
        out_shape=[jax.ShapeDtypeStruct((r, HEAD_DIM), F32)] * 2,
        compiler_params=_cparams("parallel"),
        name="rope_tables",
    )(posf, inv)


def _proj_kernel(*refs, kinds, tile_src, n_w, transposed, has_extra):
    rope = "rope" in kinds
    refs = list(refs)
    x_ref, g_ref = refs[:2]
    w_refs = refs[2:2 + n_w]
    hg_ref = refs[2 + n_w]
    at = 3 + n_w
    if rope:
        cos_ref, sin_ref = refs[at:at + 2]
        at += 2
    if has_extra:
        xs_ref = refs[at]
        at += 1
        if rope:
            cos_s_ref, sin_s_ref = refs[at:at + 2]
            at += 2
        o_ref, os_ref, xn_ref = refs[at:at + 3]
    else:
        o_ref, xn_ref = refs[at:at + 2]
    tm = x_ref.shape[0]

    def normed(x):
        return (x * _rms_scale(x) * g_ref[...]).astype(BF16)

    def tiles(xn, cos, sin, store):
        for t, kind in enumerate(kinds):
            which, col, width = tile_src[t]
            at_col = sum(w for _, _, w in tile_src[:t])
            if transposed:
                y = lax.dot_general(xn, w_refs[which][col:col + width, :], NT_DIMS, preferred_element_type=F32)
            else:
                y = jnp.dot(xn, w_refs[which][:, col:col + width], preferred_element_type=F32)
            if kind == "plain":
                store(slice(at_col, at_col + width), y)
                continue
            for j in range(width // HEAD_DIM):
                osl = slice(at_col + j * HEAD_DIM, at_col + (j + 1) * HEAD_DIM)
                yh = y[:, j * HEAD_DIM:(j + 1) * HEAD_DIM]
                yh = yh * _rms_scale(yh) * hg_ref[:, osl]
                if kind == "rope":
                    yh = yh * cos + pltpu.roll(yh, HEAD_DIM // 2, axis=1) * sin
                store(osl, yh)

    def store_main(cols, y):
        o_ref[:, cols] = y

    def store_both(cols, y):
        o_ref[:, cols] = y[:tm]
        os_ref[:, cols] = y[tm:]

    xn_ref[:tm, :] = normed(x_ref[...])
    cos = cos_ref[...] if rope else None
    sin = sin_ref[...] if rope else None
    if not has_extra:
        tiles(xn_ref[...], cos, sin, store_main)
        return

    @pl.when(pl.program_id(0) == 0)
    def _():
        xn_ref[tm:, :] = normed(xs_ref[...])
        cos_all = jnp.concatenate([cos, cos_s_ref[...]], axis=0) if rope else None
        sin_all = jnp.concatenate([sin, sin_s_ref[...]], axis=0) if rope else None
        tiles(xn_ref[...], cos_all, sin_all, store_both)

    @pl.when(pl.program_id(0) != 0)
    def _():
        tiles(xn_ref[:tm, :], cos, sin, store_main)


def norm_proj(x, g, ws, layer, head_gain, kinds, cos=None, sin=None, cast=(), tile_src=None, transposed=False,
              extra=None):
    m, d = x.shape
    ws = list(ws) if isinstance(ws, (list, tuple)) else [ws]
    if tile_src is None:
        tile_src = [(0, t * COL_TILE, COL_TILE) for t in range(len(kinds))]
    n_cols = sum(width for _, _, width in tile_src)
    assert len(tile_src) == len(kinds) and head_gain.shape == (1, n_cols)
    assert all(width % HEAD_DIM == 0 for _, _, width in tile_src)
    rope = "rope" in kinds
    has_extra = extra is not None
    ms = extra[0].shape[0] if has_extra else 0
    tm = min(m, ROW_TILE)
    while tm * n_cols * 4 > PROJ_OUT_BLOCK_BYTES and tm % (2 * SUBLANES) == 0:
        tm //= 2
    assert m % tm == 0
    in_specs = [pl.BlockSpec((tm, d), lambda i: (i, 0)),
                pl.BlockSpec((1, d), lambda i: (0, 0))]
    for k, w in enumerate(ws):
        used = max(col + width for which, col, width in tile_src if which == k)
        if transposed:
            assert used <= w.shape[1] and w.shape[2] == d
            in_specs.append(pl.BlockSpec((None, used, d), lambda i: (layer, 0, 0), pipeline_mode=RESIDENT))
            continue
        assert used <= w.shape[2] and (used % LANES == 0 or used == w.shape[2])
        in_specs.append(pl.BlockSpec((None, d, used), lambda i: (layer, 0, 0), pipeline_mode=RESIDENT))
    in_specs.append(pl.BlockSpec((1, n_cols), lambda i: (0, 0)))
    args = [x, g.reshape(1, d), *ws, head_gain]
    if rope:
        assert cos.shape[0] % tm == 0
        tab_blocks = cos.shape[0] // tm
        in_specs += [pl.BlockSpec((tm, HEAD_DIM), lambda i: (i % tab_blocks, 0))] * 2
        args += [cos, sin]
    out_specs = [pl.BlockSpec((tm, n_cols), lambda i: (i, 0))]
    out_shape = [jax.ShapeDtypeStruct((m, n_cols), F32)]
    if has_extra:
        whole = lambda a: pl.BlockSpec(a.shape, lambda i: (0, 0))
        xs, cos_s, sin_s = extra
        extras = [xs] + ([cos_s, sin_s] if rope else [])
        in_specs += [whole(a) for a in extras]
        args += extras
        out_specs.append(pl.BlockSpec((ms, n_cols), lambda i: (0, 0)))
        out_shape.append(jax.ShapeDtypeStruct((ms, n_cols), F32))
    n_out = len(out_specs)
    side = _SideCast(cast, m // tm, lambda i: i)
    outs = pl.pallas_call(
        side.wrap(functools.partial(_proj_kernel, kinds=tuple(kinds), tile_src=tuple(tile_src), n_w=len(ws),
                                    transposed=transposed, has_extra=has_extra),
                  len(args), n_out),
        grid=(m // tm,),
        in_specs=in_specs + side.in_specs,
        out_specs=out_specs + side.out_specs,
        out_shape=out_shape + side.out_shape,
        scratch_shapes=[pltpu.VMEM((tm + ms, d), BF16)],
        compiler_params=_cparams("arbitrary"),
        name="norm_proj",
    )(*args, *side.views)
    casts = [o.reshape(shp) for o, shp in zip(outs[n_out:], side.shapes)]
    return outs[0], (outs[1] if has_extra else None), casts


def _out_proj_kernel(*refs, has_extra):
    if has_extra:
        x_ref, a_ref, b_ref, wa_ref, wb_ref, xs_ref, as_ref, bs_ref, o_ref, os_ref = refs
    else:
        x_ref, a_ref, b_ref, wa_ref, wb_ref, o_ref = refs
    tm = x_ref.shape[0]

    def mixed(a, b):
        acc = jnp.dot(a, wa_ref[...], preferred_element_type=F32)
        return acc + jnp.dot(b, wb_ref[...], preferred_element_type=F32)

    if not has_extra:
        o_ref[...] = x_ref[...] + mixed(a_ref[...], b_ref[...])
        return

    @pl.when(pl.program_id(0) == 0)
    def _():
        acc = mixed(jnp.concatenate([a_ref[...], as_ref[...]], axis=0),
                    jnp.concatenate([b_ref[...], bs_ref[...]], axis=0))
        o_ref[...] = x_ref[...] + acc[:tm]
        os_ref[...] = xs_ref[...] + acc[tm:]

    @pl.when(pl.program_id(0) != 0)
    def _():
        o_ref[...] = x_ref[...] + mixed(a_ref[...], b_ref[...])


def out_proj(x, a, b, w, layer, extra=None):
    m, d = x.shape
    tm = min(m, ROW_TILE)
    ka, kb = a.shape[1], b.shape[1]
    assert w.shape[1] == ka + kb and ka % kb == 0
    has_extra = extra is not None
    in_specs = [pl.BlockSpec((tm, d), lambda i: (i, 0)),
                pl.BlockSpec((tm, ka), lambda i: (i, 0)),
                pl.BlockSpec((tm, kb), lambda i: (i, 0)),
                pl.BlockSpec((None, ka, d), lambda i: (layer, 0, 0), pipeline_mode=RESIDENT),
                pl.BlockSpec((None, kb, d), lambda i: (layer, ka // kb, 0), pipeline_mode=RESIDENT)]
    out_specs = [pl.BlockSpec((tm, d), lambda i: (i, 0))]
    out_shape = [jax.ShapeDtypeStruct((m, d), F32)]
    args = [x, a, b, w, w]
    if has_extra:
        in_specs += [pl.BlockSpec(e.shape, lambda i: (0, 0)) for e in extra]
        args += list(extra)
        out_specs.append(pl.BlockSpec(extra[0].shape, lambda i: (0, 0)))
        out_shape.append(jax.ShapeDtypeStruct(extra[0].shape, F32))
    outs = pl.pallas_call(
        functools.partial(_out_proj_kernel, has_extra=has_extra),
        grid=(m // tm,),
        in_specs=in_specs,
        out_specs=out_specs,
        out_shape=out_shape,
        compiler_params=_cparams("arbitrary"),
        name="out_proj",
    )(*args)
    return outs[0], (outs[1] if has_extra else None)


def _mlp_kernel(*refs, has_extra):
    if has_extra:
        x_ref, g_ref, wu_ref, wd_ref, xs_ref, o_ref, os_ref, xn_ref = refs
    else:
        x_ref, g_ref, wu_ref, wd_ref, o_ref, xn_ref = refs
    i, f = pl.program_id(0), pl.program_id(1)
    tm = x_ref.shape[0]

    def normed(x):
        return (x * _rms_scale(x) * g_ref[...]).astype(BF16)

    def ffn(xn):
        h = jnp.dot(xn, wu_ref[...], preferred_element_type=F32)
        h = jnp.maximum(h, 0.0)
        h = (h * h).astype(BF16)
        return jnp.dot(h, wd_ref[...], preferred_element_type=F32)

    @pl.when(f == 0)
    def _():
        x = x_ref[...]
        xn_ref[:tm, :] = normed(x)
        o_ref[...] = x

    if not has_extra:
        o_ref[...] += ffn(xn_ref[...])
        return

    @pl.when(i == 0)
    def _():
        @pl.when(f == 0)
        def _():
            xs = xs_ref[...]
            xn_ref[tm:, :] = normed(xs)
            os_ref[...] = xs

        upd = ffn(xn_ref[...])
        o_ref[...] += upd[:tm]
        os_ref[...] += upd[tm:]

    @pl.when(i != 0)
    def _():
        o_ref[...] += ffn(xn_ref[:tm, :])


def mlp(x, g, wu, wd, tf=1024, cast=(), extra=None):
    m, d = x.shape
    ff = wu.shape[1]
    tm = min(m, ROW_TILE)
    nf = ff // tf
    has_extra = extra is not None
    side = _SideCast(cast, (m // tm) * nf, lambda i, j: i * nf + j)
    in_specs = [pl.BlockSpec((tm, d), lambda i, j: (i, 0)),
                pl.BlockSpec((1, d), lambda i, j: (0, 0)),
                pl.BlockSpec((d, tf), lambda i, j: (0, j)),
                pl.BlockSpec((tf, d), lambda i, j: (j, 0))]
    out_specs = [pl.BlockSpec((tm, d), lambda i, j: (i, 0))]
    out_shape = [jax.ShapeDtypeStruct((m, d), F32)]
    args = [x, g.reshape(1, d), wu, wd]
    ms = extra.shape[0] if has_extra else 0
    if has_extra:
        whole = pl.BlockSpec((ms, d), lambda i, j: (0, 0))
        in_specs.append(whole)
        out_specs.append(whole)
        out_shape.append(jax.ShapeDtypeStruct((ms, d), F32))
        args.append(extra)
    scratch = [pltpu.VMEM((tm + ms, d), BF16)]
    n_out = len(out_specs)
    outs = pl.pallas_call(
        side.wrap(functools.partial(_mlp_kernel, has_extra=has_extra), len(args), n_out),
        grid=(m // tm, nf),
        in_specs=in_specs + side.in_specs,
        out_specs=out_specs + side.out_specs,
        out_shape=out_shape + side.out_shape,
        scratch_shapes=scratch,
        compiler_params=_cparams("arbitrary", "arbitrary"),
        name="mlp",
    )(*args, *side.views)
    casts = [o.reshape(shp) for o, shp in zip(outs[n_out:], side.shapes)]
    return outs[0], (outs[1] if has_extra else None), casts


def _split3(x):
    hi = x.astype(BF16)
    r1 = x - hi.astype(F32)
    mid = r1.astype(BF16)
    lo = (r1 - mid.astype(F32)).astype(BF16)
    return hi, mid, lo


def _ssd_kernel(xbc_ref, dt_ref, z_ref, cw_ref, cb_ref, dtb_ref, alog_ref, dsk_ref, gg_ref, tri_ref, expand_ref,
                conv0_ref, h0_ref, y_ref, convo_ref, ho_ref,
                ext_ref, ht_ref, yd_ref, st_ref, *, t_real, n_heads, d_inner):
    L = SSD_CHUNK
    P = SSM_HEAD_DIM
    N = SSM_D_STATE
    G = SSM_GROUPS
    hpg = n_heads // G
    gw = hpg * P
    hist = CONV_W - 1
    base = SUBLANES - hist
    c = pl.program_id(1)
    nc = pl.num_programs(1)

    @pl.when(c == 0)
    def _():
        ext_ref[base:SUBLANES, :] = conv0_ref[...]
        for g in range(G):
            ht_ref[g] = h0_ref[g].T

    ext_ref[SUBLANES:SUBLANES + L, :] = xbc_ref[...]
    ext = ext_ref[...]
    acc = cb_ref[...] + ext[SUBLANES:, :] * cw_ref[hist:CONV_W, :]
    for k in range(1, CONV_W):
        acc = acc + pltpu.roll(ext, k, axis=0)[SUBLANES:, :] * cw_ref[hist - k:CONV_W - k, :]
    xc = acc * _sigmoid(acc)

    @pl.when(c == nc - 1)
    def _():
        n_in_last = t_real - ((t_real - 1) // L) * L
        convo_ref[...] = ext_ref[base + n_in_last:base + n_in_last + hist, :]

    ext_ref[base:SUBLANES, :] = ext_ref[base + L:SUBLANES + L, :]

    dt_raw = dt_ref[...] + dtb_ref[...]
    dt = jnp.maximum(dt_raw, 0.0) + jnp.log1p(jnp.exp(-jnp.abs(dt_raw)))
    row = lax.broadcasted_iota(jnp.int32, (L, LANES), 0) + c * L
    dt = jnp.where(row < t_real, dt, 0.0)
    da = dt * (-jnp.exp(alog_ref[...]))
    li = lax.broadcasted_iota(jnp.int32, (L, L), 0)
    si = lax.broadcasted_iota(jnp.int32, (L, L), 1)
    causal = li >= si
    a_cs = sum(jnp.dot(tri_ref[...], part, preferred_element_type=F32) for part in _split3(da))
    a_cs_t = a_cs.T
    dt_t = dt.T
    w_t = dt_t * jnp.exp(a_cs_t[:, L - 1:L] - a_cs_t)
    e_exp = sum(jnp.dot(part, expand_ref[...], preferred_element_type=F32) for part in _split3(jnp.exp(a_cs)))
    xb = xc[:, :d_inner].astype(BF16)
    low_half = lax.broadcasted_iota(jnp.int32, (L, 2 * P), 1) < P

    for g in range(G):
        b_g = xc[:, d_inner + g * N:d_inner + (g + 1) * N]
        c_g = xc[:, d_inner + G * N + g * N:d_inner + G * N + (g + 1) * N].astype(BF16)
        cbm = lax.dot_general(c_g, b_g.astype(BF16), NT_DIMS, preferred_element_type=F32)
        b_t = b_g.T
        y_off = jnp.dot(c_g, ht_ref[g].astype(BF16), preferred_element_type=F32)
        for kp in range(hpg // 2):
            h0 = g * hpg + 2 * kp
            x_pair = xb[:, h0 * P:(h0 + 2) * P]
            yd, st = [], []
            for h in (h0, h0 + 1):
                seg = a_cs[:, h:h + 1] - a_cs_t[h:h + 1, :]
                dec = jnp.exp(jnp.where(causal, seg, -jnp.inf))
                mh = (cbm * dec * dt_t[h:h + 1, :]).astype(BF16)
                yd.append(jnp.dot(mh, x_pair, preferred_element_type=F32))
                btw = (b_t * w_t[h:h + 1, :]).astype(BF16)
                st.append(jnp.dot(btw, x_pair, preferred_element_type=F32))
            yd_ref[:, 2 * kp * P:(2 * kp + 2) * P] = jnp.where(low_half, yd[0], yd[1])
            st_ref[:, 2 * kp * P:(2 * kp + 2) * P] = jnp.where(low_half, st[0], st[1])
        gs = slice(g * gw, (g + 1) * gw)
        e_g = e_exp[:, gs]
        y_g = yd_ref[...] + y_off * e_g + dsk_ref[:, gs] * xc[:, gs]
        ht_ref[g] = ht_ref[g] * e_g[L - 1:L, :] + st_ref[...]
        z_g = z_ref[:, gs]
        gated = y_g * (z_g * _sigmoid(z_g))
        y_ref[:, gs] = (gated * _rms_scale(gated) * gg_ref[:, gs]).astype(y_ref.dtype)

    @pl.when(c == nc - 1)
    def _():
        for g in range(G):
            ho_ref[g] = ht_ref[g].T


def ssd_mixer(proj, col_xbc, col_dt, col_z, t_real, conv_w, conv_b, dt_bias, a_log, d_skip, gate_g,
              conv_prev, h_prev, cast=()):
    bt, tp, _ = proj.shape
    conv_dim = conv_w.shape[1]
    n_heads = dt_bias.shape[0]
    d_inner = n_heads * SSM_HEAD_DIM
    gw = d_inner // SSM_GROUPS
    L = SSD_CHUNK
    assert tp % L == 0 and (tp - t_real) < L
    assert col_xbc % conv_dim == 0 and col_dt % LANES == 0 and col_z % d_inner == 0
    pad = LANES - n_heads
    dtb = jnp.pad(dt_bias, (0, pad)).reshape(1, LANES)
    alog = jnp.pad(a_log, (0, pad)).reshape(1, LANES)
    dsk = jnp.repeat(d_skip, SSM_HEAD_DIM).reshape(1, d_inner)
    tri = jnp.tril(jnp.ones((L, L), BF16))
    expand = jnp.repeat(jnp.eye(LANES, dtype=BF16)[:, :n_heads], SSM_HEAD_DIM, axis=1)
    kern = functools.partial(_ssd_kernel, t_real=t_real, n_heads=n_heads, d_inner=d_inner)
    vec = lambda n: pl.BlockSpec((1, n), lambda b, c: (0, 0))
    nc = tp // L
    side = _SideCast(cast, bt * nc, lambda b, c: b * nc + c)
    outs = pl.pallas_call(
        side.wrap(kern, 13, 3),
        grid=(bt, nc),
        in_specs=[pl.BlockSpec((None, L, conv_dim), lambda b, c: (b, c, col_xbc // conv_dim)),
                  pl.BlockSpec((None, L, LANES), lambda b, c: (b, c, col_dt // LANES)),
                  pl.BlockSpec((None, L, d_inner), lambda b, c: (b, c, col_z // d_inner)),
                  pl.BlockSpec((CONV_W, conv_dim), lambda b, c: (0, 0)),
                  vec(conv_dim), vec(LANES), vec(LANES), vec(d_inner), vec(d_inner),
                  pl.BlockSpec((L, L), lambda b, c: (0, 0)),
                  pl.BlockSpec((LANES, d_inner), lambda b, c: (0, 0)),
                  pl.BlockSpec((None, CONV_W - 1, conv_dim), lambda b, c: (b, 0, 0)),
                  pl.BlockSpec((None, SSM_GROUPS, gw, SSM_D_STATE), lambda b, c: (b, 0, 0, 0))] + side.in_specs,
        out_specs=[pl.BlockSpec((None, L, d_inner), lambda b, c: (b, c, 0)),
                   pl.BlockSpec((None, CONV_W - 1, conv_dim), lambda b, c: (b, 0, 0)),
                   pl.BlockSpec((None, SSM_GROUPS, gw, SSM_D_STATE), lambda b, c: (b, 0, 0, 0))] + side.out_specs,
        out_shape=[jax.ShapeDtypeStruct((bt, tp, d_inner), BF16),
                   jax.ShapeDtypeStruct((bt, CONV_W - 1, conv_dim), F32),
                   jax.ShapeDtypeStruct((bt, SSM_GROUPS, gw, SSM_D_STATE), F32)] + side.out_shape,
        scratch_shapes=[pltpu.VMEM((SUBLANES + L, conv_dim), F32),
                        pltpu.VMEM((SSM_GROUPS, SSM_D_STATE, gw), F32),
                        pltpu.VMEM((L, gw), F32),
                        pltpu.VMEM((SSM_D_STATE, gw), F32)],
        compiler_params=_cparams("parallel", "arbitrary"),
        name="ssd_mixer",
    )(proj, proj, proj, conv_w, conv_b.reshape(1, conv_dim), dtb, alog, dsk, gate_g.reshape(1, d_inner),
      tri, expand, conv_prev, h_prev, *side.views)
    return side.split(outs, 3)


def _mem_attn_kernel(q_ref, k_ref, v_ref, o_ref):
    scale = HEAD_DIM ** -0.5
    headed = len(k_ref.shape) == 3
    for h in range(MEM_HEADS):
        sl = slice(h * HEAD_DIM, (h + 1) * HEAD_DIM)
        q = (q_ref[:, sl] * (scale * LOG2_E)).astype(BF16)
        k = (k_ref[:, h, :] if headed else k_ref[:, sl]).astype(BF16)
        v = (v_ref[:, h, :] if headed else v_ref[:, sl]).astype(BF16)
        s = lax.dot_general(q, k, NT_DIMS, preferred_element_type=F32)
        e = jnp.exp2(s - jnp.max(s, axis=-1, keepdims=True))
        den = jnp.sum(e, axis=-1, keepdims=True)
        o = jnp.dot(e.astype(BF16), v, preferred_element_type=F32) / den
        o_ref[:, sl] = o.astype(o_ref.dtype)


def mem_attention(proj, col_q, k, v, kv_at):
    bt, t, _ = proj.shape
    width = MEM_HEADS * HEAD_DIM
    tq = min(t, ROW_TILE)
    if k.ndim == 5:
        mlen = k.shape[2]
        kv_specs = [pl.BlockSpec((None, None, mlen, MEM_HEADS, HEAD_DIM), lambda b, i: (kv_at, b, 0, 0, 0))] * 2
    else:
        mlen = k.shape[1]
        kv_specs = [pl.BlockSpec((None, mlen, width), lambda b, i, c=c: (b, 0, c // width)) for c in kv_at]
    return pl.pallas_call(
        _mem_attn_kernel,
        grid=(bt, t // tq),
        in_specs=[pl.BlockSpec((None, tq, width), lambda b, i: (b, i, col_q // width))] + kv_specs,
        out_specs=pl.BlockSpec((None, tq, width), lambda b, i: (b, i, 0)),
        out_shape=jax.ShapeDtypeStruct((bt, t, width), BF16),
        compiler_params=_cparams("parallel", "parallel"),
        name="mem_attention",
    )(proj, k, v)


def _run_units(first, count, unit, unroll):
    trips = count // unroll
    if trips == 1:
        trips = 0
    if trips:
        def trip(it, carry):
            for k in range(unroll):
                unit(first + it * unroll + k)
            return carry
        lax.fori_loop(0, trips, trip, 0)
    for k in range(trips * unroll, count):
        unit(first + k)


def _dil_attn_kernel(*refs, groups, rb, unroll):
    ng = len(groups)
    q_refs = refs[:ng]
    kp_ref, kc_ref, vp_ref, vc_ref, o_ref = refs[ng:ng + 5]
    og = refs[ng + 5:2 * ng + 5]
    lg = refs[2 * ng + 5:3 * ng + 5]
    i = pl.program_id(1)
    scale = HEAD_DIM ** -0.5

    for gi, (win, d) in enumerate(groups):
        w = win // d
        dw = d * w
        qi = lax.broadcasted_iota(jnp.int32, (w, 2 * w), 0)
        kj = lax.broadcasted_iota(jnp.int32, (w, 2 * w), 1)
        band = (kj >= qi) & (kj <= qi + w)
        band_first = band & (kj >= jnp.where(i > 0, 0, w))

        def rows(start, size, d=d, w=w):
            if d == 1:
                return pl.ds(pl.multiple_of(start, w), size)
            return pl.ds(start, size, stride=d)

        def attend(rows_q, kk, vv, mask, gi=gi, w=w):
            q = (q_refs[gi][rows_q, :] * (scale * LOG2_E)).astype(BF16)
            s = lax.dot_general(q, kk.astype(BF16), NT_DIMS, preferred_element_type=F32)
            s = jnp.where(mask, s, -jnp.inf)
            m = jnp.max(s, axis=-1, keepdims=True)
            e = jnp.exp2(s - m)
            den = jnp.sum(e, axis=-1, keepdims=True)
            o = jnp.dot(e.astype(BF16), vv.astype(BF16), preferred_element_type=F32) / den
            og[gi][rows_q, :] = o
            lg[gi][rows_q, :] = jnp.broadcast_to(m * LN_2 + jnp.log(den), (w, HEAD_DIM))

        def first_unit(r, rows=rows, attend=attend, w=w, dw=dw, mask=band_first):
            rows_q, rows_p = rows(r, w), rows(rb - dw + r, w)
            kk = jnp.concatenate([kp_ref[rows_p, :], kc_ref[rows_q, :]], axis=0)
            vv = jnp.concatenate([vp_ref[rows_p, :], vc_ref[rows_q, :]], axis=0)
            attend(rows_q, kk, vv, mask)

        def later_unit(u, rows=rows, attend=attend, d=d, w=w, dw=dw, mask=band):
            sb = u // d
            start_q = sb * dw + (u - sb * d)
            rows_k = rows(start_q - dw, 2 * w)
            attend(rows(start_q, w), kc_ref[rows_k, :], vc_ref[rows_k, :], mask)

        _run_units(0, d, first_unit, unroll)
        _run_units(d, rb // w - d, later_unit, unroll)

    ls = [r[...] for r in lg]
    mm = functools.reduce(jnp.maximum, ls)
    ws = [jnp.exp(l - mm) for l in ls]
    num = sum(wg * r[...] for wg, r in zip(ws, og))
    o_ref[...] = (num / sum(ws)).astype(o_ref.dtype)


def dilated_attention(proj, kv):
    bt, t, _ = proj.shape
    ng = len(DIL_GROUPS)
    rb = min(t, max(win for win, _ in DIL_GROUPS))
    for win, d in DIL_GROUPS:
        assert win % d == 0 and rb % win == 0
    assert t % rb == 0
    blk = lambda f: pl.BlockSpec((None, rb, HEAD_DIM), f)
    prev = lambda i: jnp.maximum(i - 1, 0)
    q_specs = [blk(lambda b, i, h, g=g: (b, i, g * DIL_HEADS + h)) for g in range(ng)]
    kv_specs = [blk(lambda b, i, h: (b, prev(i), h)), blk(lambda b, i, h: (b, i, h)),
                blk(lambda b, i, h: (b, prev(i), DIL_HEADS + h)), blk(lambda b, i, h: (b, i, DIL_HEADS + h))]
    return pl.pallas_call(
        functools.partial(_dil_attn_kernel, groups=DIL_GROUPS, rb=rb, unroll=DIL_UNROLL),
        grid=(bt, t // rb, DIL_HEADS),
        in_specs=q_specs + kv_specs,
        out_specs=blk(lambda b, i, h: (b, i, h)),
        out_shape=jax.ShapeDtypeStruct((bt, t, DIL_HEADS * HEAD_DIM), BF16),
        scratch_shapes=[pltpu.VMEM((rb, HEAD_DIM), F32)] * (2 * ng),
        compiler_params=_cparams("parallel", "parallel", "parallel"),
        name="dilated_attention",
    )(*([proj] * ng), kv, kv, kv, kv)


def _dil_decode_kernel(*refs):
    ng = len(DIL_GROUPS)
    q_ref, kv_ref = refs[:2]
    k_refs, v_refs = refs[2:2 + ng], refs[2 + ng:2 + 2 * ng]
    o_ref = refs[-1]
    width = DIL_HEADS * HEAD_DIM
    scale = HEAD_DIM ** -0.5
    for h in range(DIL_HEADS):
        sl = slice(h * HEAD_DIM, (h + 1) * HEAD_DIM)
        k_new = kv_ref[:, h * HEAD_DIM:(h + 1) * HEAD_DIM]
        v_new = kv_ref[:, width + h * HEAD_DIM:width + (h + 1) * HEAD_DIM]
        outs, lses = [], []
        for gi in range(ng):
            kc = k_refs[gi][:, h, :].astype(BF16)
            vc = v_refs[gi][:, h, :].astype(BF16)
            q = q_ref[:, gi * width + h * HEAD_DIM:gi * width + (h + 1) * HEAD_DIM]
            s = lax.dot_general(q.astype(BF16), kc, NT_DIMS, preferred_element_type=F32) * scale
            s_self = jnp.sum(q * k_new, axis=-1, keepdims=True) * scale
            m = jnp.maximum(jnp.max(s, axis=-1, keepdims=True), s_self)
            e = jnp.exp(s - m)
            e_self = jnp.exp(s_self - m)
            den = jnp.sum(e, axis=-1, keepdims=True) + e_self
            o = jnp.dot(e.astype(BF16), vc, preferred_element_type=F32) + e_self * v_new
            outs.append(o / den)
            lses.append(m + jnp.log(den))
        mm = functools.reduce(jnp.maximum, lses)
        ws = [jnp.exp(l - mm) for l in lses]
        num = sum(wg * o for wg, o in zip(ws, outs))
        o_ref[:, sl] = (num / sum(ws)).astype(o_ref.dtype)


def dilated_decode(proj, kv_new, k_cache, v_cache):
    bt, rows, nq = proj.shape
    lc = k_cache.shape[1]
    width = DIL_HEADS * HEAD_DIM
    views, specs = [], []
    for cache in (k_cache, v_cache):
        for win, dil in DIL_GROUPS:
            assert lc % win == 0 and win % dil == 0
            w = win // dil
            views.append(cache.reshape(bt, lc // dil, dil, DIL_HEADS, HEAD_DIM))
            specs.append(pl.BlockSpec((None, w, None, DIL_HEADS, HEAD_DIM),
                                      lambda b, last=lc // win - 1: (b, last, 0, 0, 0)))
    return pl.pallas_call(
        _dil_decode_kernel,
        grid=(bt,),
        in_specs=[pl.BlockSpec((None, rows, nq), lambda b: (b, 0, 0)),
                  pl.BlockSpec((None, rows, 2 * width), lambda b: (b, 0, 0))] + specs,
        out_specs=pl.BlockSpec((None, rows, width), lambda b: (b, 0, 0)),
        out_shape=jax.ShapeDtypeStruct((bt, rows, width), BF16),
        compiler_params=_cparams("parallel"),
        name="dilated_decode",
    )(proj, kv_new, *views)


def kernel(x_prompt, x_sample, state_conv, state_ssm, cache_win_k, cache_win_v, cache_mem_k, cache_mem_v,
           mem_prompt, norm_mix_g, norm_mlp_g, norm_mem_g, w_mem_k, w_mem_v, mem_q_norm_g, mem_k_norm_g,
           w_up, w_down, w_in_a, conv_w, conv_b, dt_bias, a_log, d_skip, gate_norm_g, w_out_a,
           w_in_b, q_norm_g, w_out_b, kv_norm_g, w_k_shared, w_v_shared, k_norm_g):
    depth = w_up.shape[0]
    n_a = w_in_a.shape[0]
    d_model = x_prompt.shape[-1]
    n_heads = dt_bias.shape[1]
    d_inner = n_heads * SSM_HEAD_DIM
    conv_dim = conv_w.shape[2]
    mem_w = MEM_HEADS * HEAD_DIM
    kv_w = DIL_HEADS * HEAD_DIM
    dil_q_w = len(DIL_GROUPS) * kv_w

    s1, s2, s3 = d_inner, d_inner + conv_dim, d_inner + conv_dim + n_heads
    assert s1 % COL_TILE == 0 and s2 % COL_TILE == 0
    w_in_a_t = jnp.swapaxes(w_in_a, 1, 2)
    w_a_main = round_weight(w_in_a_t, s2)
    tail = lax.optimization_barrier(w_in_a_t[:, s2:]).astype(BF16)
    w_a_tail = jnp.concatenate([tail[:, :n_heads], jnp.zeros((n_a, COL_TILE - n_heads, d_model), BF16),
                                tail[:, n_heads:]], axis=1)
    tiles_a = ([(0, s1 + c, COL_TILE) for c in range(0, conv_dim, COL_TILE)]
               + [(1, COL_TILE + c, COL_TILE) for c in range(0, mem_w, COL_TILE)]
               + [(0, c, COL_TILE) for c in range(0, d_inner, COL_TILE)] + [(1, 0, LANES)])
    kinds_a = (["plain"] * (conv_dim // COL_TILE) + ["norm"] * (mem_w // COL_TILE)
               + ["plain"] * (d_inner // COL_TILE + 1))
    col_xbc, col_qm_a, col_z = 0, conv_dim, conv_dim + mem_w
    col_dt = col_z + d_inner
    a_cols = col_dt + LANES
    kinds_b = ["rope"] * (dil_q_w // COL_TILE) + ["norm"] * (mem_w // COL_TILE)
    kinds_kv = ["rope"] * (kv_w // COL_TILE) + ["plain"] * (kv_w // COL_TILE)
    kinds_mkv = ["norm"] * (mem_w // COL_TILE) + ["plain"] * (mem_w // COL_TILE)
    w_b = w_in_b.astype(BF16)
    w_kv = jnp.concatenate([w_k_shared, w_v_shared], axis=-1).astype(BF16)[None]
    w_mkv = jnp.concatenate([w_mem_k, w_mem_v], axis=-1).astype(BF16)
    w_out_a_b, w_out_b_b = w_out_a.astype(BF16), w_out_b.astype(BF16)
    assert n_a >= 1
    w_up_b, w_down_b = [None] * depth, [None] * depth

    def gain_row(width, pieces):
        parts, at = [], 0
        for start, g, reps in pieces:
            parts += [jnp.ones((start - at,), F32), jnp.tile(g, reps)]
            at = start + reps * HEAD_DIM
        parts.append(jnp.ones((width - at,), F32))
        return jnp.concatenate(parts).reshape(1, width)

    class Group:
        def __init__(self, x, pos_rows, conv_prev, ssm_prev, mem_kv, k_past, v_past, t_real):
            self.bt, self.t, _ = x.shape
            self.m = self.bt * self.t
            self.x2 = x.reshape(self.m, d_model)
            self.cos, self.sin = rope_tables(pos_rows)
            self.conv_prev, self.ssm_prev, self.mem_kv = conv_prev, ssm_prev, mem_kv
            self.k_past, self.v_past, self.t_real = k_past, v_past, t_real
            self.conv_new, self.ssm_new, self.kv3 = [], [], None

    def in_proj_a(prompt, sample, l):
        hg = gain_row(a_cols, [(col_qm_a, mem_q_norm_g[l], MEM_HEADS)])
        proj_p, proj_s, rounded = norm_proj(prompt.x2, norm_mix_g[l], [w_a_main, w_a_tail], l, hg, kinds_a,
                                            tile_src=tiles_a, transposed=True, extra=(sample.x2, None, None),
                                            cast=((w_up, 0),) if l == 0 else ())
        if l == 0:
            (w_up_b[0],) = rounded
        return proj_p, proj_s

    def mix_a(gr, proj, l, round_mlp_weights):
        bt, t, m = gr.bt, gr.t, gr.m
        proj3 = proj.reshape(bt, t, -1)
        tp = -(-t // SSD_CHUNK) * SSD_CHUNK
        proj_ssd = proj3 if tp == t else jnp.pad(proj3, ((0, 0), (0, tp - t), (0, 0)))
        ssd = ssd_mixer(proj_ssd, col_xbc, col_dt, col_z, gr.t_real, conv_w[l], conv_b[l],
                        dt_bias[l], a_log[l], d_skip[l], gate_norm_g[l],
                        gr.conv_prev[l], gr.ssm_prev[l].reshape(bt, SSM_GROUPS, -1, SSM_D_STATE),
                        cast=((w_down, 0),) if round_mlp_weights else ())
        if round_mlp_weights:
            ssd, (w_down_b[0],) = ssd
        y, c_new, h_new = ssd
        gr.conv_new.append(c_new)
        gr.ssm_new.append(h_new.reshape(bt, n_heads, SSM_HEAD_DIM, SSM_D_STATE))
        mo = mem_attention(proj3, col_qm_a, *gr.mem_kv(l)).reshape(m, mem_w)
        return y[:, :t].reshape(m, d_inner), mo

    def shared_kv(prompt, sample):
        hg = gain_row(2 * kv_w, [(0, k_norm_g, DIL_HEADS)])
        kv_p, kv_s, _ = norm_proj(prompt.x2, kv_norm_g, w_kv, 0, hg, kinds_kv, prompt.cos, prompt.sin,
                                  extra=(sample.x2, sample.cos, sample.sin))
        prompt.kv3 = kv_p.reshape(prompt.bt, prompt.t, 2 * kv_w)
        sample.kv3 = kv_s.reshape(sample.bt, sample.t, 2 * kv_w)

    def in_proj_b(prompt, sample, l):
        j = l - n_a
        hg = gain_row(dil_q_w + mem_w, [(0, q_norm_g[j], dil_q_w // HEAD_DIM),
                                        (dil_q_w, mem_q_norm_g[l], MEM_HEADS)])
        proj_p, proj_s, _ = norm_proj(prompt.x2, norm_mix_g[l], w_b, j, hg, kinds_b, prompt.cos, prompt.sin,
                                      extra=(sample.x2, sample.cos, sample.sin))
        return proj_p, proj_s

    def mix_b(gr, proj, l):
        proj3 = proj.reshape(gr.bt, gr.t, -1)
        if gr.k_past is None:
            att = dilated_attention(proj3, gr.kv3)
        else:
            att = dilated_decode(proj3, gr.kv3, gr.k_past, gr.v_past)
        mo = mem_attention(proj3, dil_q_w, *gr.mem_kv(l)).reshape(gr.m, mem_w)
        return att.reshape(gr.m, kv_w), mo

    bp, t_p, _ = x_prompt.shape
    mlen = mem_prompt.shape[1]
    mem2 = mem_prompt.reshape(bp * mlen, d_model)
    mkv_p = []
    for l in range(depth):
        hg = gain_row(2 * mem_w, [(0, mem_k_norm_g[l], MEM_HEADS)])
        mkv_p.append(norm_proj(mem2, norm_mem_g[l], w_mkv, l, hg, kinds_mkv)[0].reshape(bp, mlen, 2 * mem_w))
    conv0 = jnp.zeros((n_a, bp, CONV_W - 1, conv_dim), F32)
    ssm0 = jnp.zeros((n_a, bp, n_heads, SSM_HEAD_DIM, SSM_D_STATE), F32)
    prompt = Group(x_prompt, jnp.arange(t_p, dtype=jnp.int32), conv0, ssm0,
                   lambda l: (mkv_p[l], mkv_p[l], (0, mem_w)), None, None, t_p)

    bs, t_s, _ = x_sample.shape
    assert t_s == 1
    xs = jnp.pad(x_sample, ((0, 0), (0, SAMPLE_ROWS - t_s), (0, 0)))
    pos_s = jnp.full((bs * SAMPLE_ROWS,), PAST_LEN, jnp.int32)
    sample = Group(xs, pos_s, state_conv, state_ssm, lambda l: (cache_mem_k, cache_mem_v, l),
                   cache_win_k, cache_win_v, t_s)

    for l in range(depth):
        if l < n_a:
            proj_p, proj_s = in_proj_a(prompt, sample, l)
            a_p, b_p = mix_a(prompt, proj_p, l, round_mlp_weights=(l == 0))
            a_s, b_s = mix_a(sample, proj_s, l, round_mlp_weights=False)
            w_out, lw = w_out_a_b, l
        else:
            if l == n_a:
                shared_kv(prompt, sample)
            proj_p, proj_s = in_proj_b(prompt, sample, l)
            a_p, b_p = mix_b(prompt, proj_p, l)
            a_s, b_s = mix_b(sample, proj_s, l)
            w_out, lw = w_out_b_b, l - n_a
        prompt.x2, sample.x2 = out_proj(prompt.x2, a_p, b_p, w_out, lw, extra=(sample.x2, a_s, b_s))
        nxt = ((w_up, l + 1), (w_down, l + 1)) if l + 1 < depth else ()
        prompt.x2, sample.x2, rounded = mlp(prompt.x2, norm_mlp_g[l], w_up_b[l], w_down_b[l],
                                            cast=nxt, extra=sample.x2)
        if nxt:
            w_up_b[l + 1], w_down_b[l + 1] = rounded

    y_p = prompt.x2.reshape(bp, t_p, d_model)
    conv_p, ssm_p, kv_p = jnp.stack(prompt.conv_new), jnp.stack(prompt.ssm_new), prompt.kv3
    mkv_all = jnp.stack(mkv_p)
    mem_k_p = mkv_all[..., :mem_w].reshape(depth, bp, mlen, MEM_HEADS, HEAD_DIM)
    mem_v_p = mkv_all[..., mem_w:].reshape(depth, bp, mlen, MEM_HEADS, HEAD_DIM)
    keep = min(max(w for w, _ in DIL_GROUPS), t_p)
    win_k_p = kv_p[:, t_p - keep:, :kv_w].reshape(bp, keep, DIL_HEADS, HEAD_DIM)
    win_v_p = kv_p[:, t_p - keep:, kv_w:].reshape(bp, keep, DIL_HEADS, HEAD_DIM)

    y_s = sample.x2.reshape(bs, SAMPLE_ROWS, d_model)
    conv_s, ssm_s, kv_s = jnp.stack(sample.conv_new), jnp.stack(sample.ssm_new), sample.kv3
    y_s = y_s[:, :t_s]
    k_s = kv_s[:, :t_s, :kv_w].reshape(bs, t_s, DIL_HEADS, HEAD_DIM)
    v_s = kv_s[:, :t_s, kv_w:].reshape(bs, t_s, DIL_HEADS, HEAD_DIM)

    return (y_p, y_s, conv_p, ssm_p, win_k_p, win_v_p, mem_k_p, mem_v_p, conv_s, ssm_s, k_s, v_s)
```

```python
import functools

import jax
import jax.numpy as jnp
from jax import lax
from jax.experimental import pallas as pl
from jax.experimental.pallas import tpu as pltpu

F32 = jnp.float32
BF16 = jnp.bfloat16

HEAD_DIM = 128
SSM_HEAD_DIM = 64
SSM_GROUPS = 4
SSM_D_STATE = 128
CONV_W = 4
SSD_CHUNK = 128
MEM_HEADS = 4
DIL_GROUPS = ((128, 1), (512, 4), (2048, 16))
DIL_HEADS = 4
PAST_LEN = 16384
ROPE_THETA = 10000.0
EPS = 1e-6
LOG2_E = 1.4426950408889634
LN_2 = 0.6931471805599453

LANES = 128
SUBLANES = 8
VMEM_LIMIT = 48 * 1024 * 1024
COL_TILE = 512
ROW_TILE = 512
PROJ_OUT_BLOCK_BYTES = 5 * 1024 * 1024
SAMPLE_ROWS = SUBLANES
DIL_UNROLL = 16

NT_DIMS = (((1,), (1,)), ((), ()))
RESIDENT = pl.Buffered(1)


def _cparams(*sem):
    return pltpu.CompilerParams(dimension_semantics=sem, vmem_limit_bytes=VMEM_LIMIT)


def _sigmoid(x):
    return 1.0 / (1.0 + jnp.exp(-x))


def _rms_scale(x):
    return lax.rsqrt(jnp.mean(x * x, axis=-1, keepdims=True) + EPS)


class _SideCast:
    def __init__(self, mats, steps, step_of):
        self.n = len(mats)
        self.shapes = [a.shape[1:] for a, _ in mats]
        self.views, self.in_specs, self.out_specs, self.out_shape = [], [], [], []
        for a, layer in mats:
            n_layers, r, c = a.shape
            assert r % (steps * 2 * SUBLANES) == 0
            band = r // steps
            self.views.append(a.reshape(n_layers, steps, band, c))
            self.in_specs.append(pl.BlockSpec((None, None, band, c),
                                              lambda *ids, layer=layer: (layer, step_of(*ids), 0, 0)))
            self.out_specs.append(pl.BlockSpec((None, band, c), lambda *ids: (step_of(*ids), 0, 0)))
            self.out_shape.append(jax.ShapeDtypeStruct((steps, band, c), BF16))

    def wrap(self, body, n_in, n_out):
        n = self.n
        if not n:
            return body

        def kern(*refs):
            ins, srcs = refs[:n_in], refs[n_in:n_in + n]
            outs, dsts = refs[n_in + n:n_in + n + n_out], refs[n_in + n + n_out:n_in + 2 * n + n_out]
            for s, d in zip(srcs, dsts):
                d[...] = s[...].astype(BF16)
            body(*ins, *outs, *refs[n_in + 2 * n + n_out:])
        return kern

    def split(self, outs, n_out):
        main = outs[0] if n_out == 1 else tuple(outs[:n_out])
        if not self.n:
            return main
        return main, [o.reshape(s) for o, s in zip(outs[n_out:], self.shapes)]


def _round_kernel(src_ref, dst_ref):
    dst_ref[...] = src_ref[...].astype(BF16)


def round_weight(w, n_rows):
    n_layers, _, c = w.shape
    band = ROW_TILE
    assert n_rows % band == 0
    return pl.pallas_call(
        _round_kernel,
        grid=(n_layers, n_rows // band),
        in_specs=[pl.BlockSpec((None, band, c), lambda l, i: (l, i, 0))],
        out_specs=pl.BlockSpec((None, band, c), lambda l, i: (l, i, 0)),
        out_shape=jax.ShapeDtypeStruct((n_layers, n_rows, c), BF16),
        compiler_params=_cparams("parallel", "parallel"),
        name="round_weight",
    )(w)


def _rope_table_kernel(pos_ref, inv_ref, cos_ref, sin_ref):
    ang = pos_ref[...] * inv_ref[...]
    lane = lax.broadcasted_iota(jnp.int32, ang.shape, 1)
    cos_ref[...] = jnp.cos(ang)
    s = jnp.sin(ang)
    sin_ref[...] = jnp.where(lane < HEAD_DIM // 2, -s, s)


def rope_tables(pos):
    r = pos.shape[0]
    half = HEAD_DIM // 2
    inv = ROPE_THETA ** (-jnp.arange(half, dtype=F32) * (2.0 / HEAD_DIM))
    inv = jnp.concatenate([inv, inv])[None, :]
    posf = jnp.broadcast_to(pos.astype(F32)[:, None], (r, HEAD_DIM))
    tr = min(r, ROW_TILE)
    return pl.pallas_call(
        _rope_table_kernel,
        grid=(r // tr,),
        in_specs=[pl.BlockSpec((tr, HEAD_DIM), lambda i: (i, 0)),
                  pl.BlockSpec((1, HEAD_DIM), lambda i: (0, 0))],
        out_specs=[pl.BlockSpec((tr, HEAD_DIM), lambda i: (i, 0))] * 2,
        out_shape=[jax.ShapeDtypeStruct((r, HEAD_DIM), F32)] * 2,
        compiler_params=_cparams("parallel"),
        name="rope_tables",
    )(posf, inv)


def _proj_kernel(*refs, kinds, tile_src, n_w, transposed, has_extra):
    rope = "rope" in kinds
    refs = list(refs)
    x_ref, g_ref = refs[:2]
    w_refs = refs[2:2 + n_w]
    hg_ref = refs[2 + n_w]
    at = 3 + n_w
    if rope:
        cos_ref, sin_ref = refs[at:at + 2]
        at += 2
    if has_extra:
        xs_ref = refs[at]
        at += 1
        if rope:
            cos_s_ref, sin_s_ref = refs[at:at + 2]
            at += 2
        o_ref, os_ref, xn_ref = refs[at:at + 3]
    else:
        o_ref, xn_ref = refs[at:at + 2]
    tm = x_ref.shape[0]

    def normed(x):
        return (x * _rms_scale(x) * g_ref[...]).astype(BF16)

    def tiles(xn, cos, sin, store):
        for t, kind in enumerate(kinds):
            which, col, width = tile_src[t]
            at_col = sum(w for _, _, w in tile_src[:t])
            if transposed:
                y = lax.dot_general(xn, w_refs[which][col:col + width, :], NT_DIMS, preferred_element_type=F32)
            else:
                y = jnp.dot(xn, w_refs[which][:, col:col + width], preferred_element_type=F32)
            if kind == "plain":
                store(slice(at_col, at_col + width), y)
                continue
            for j in range(width // HEAD_DIM):
                osl = slice(at_col + j * HEAD_DIM, at_col + (j + 1) * HEAD_DIM)
                yh = y[:, j * HEAD_DIM:(j + 1) * HEAD_DIM]
                yh = yh * _rms_scale(yh) * hg_ref[:, osl]
                if kind == "rope":
                    yh = yh * cos + pltpu.roll(yh, HEAD_DIM // 2, axis=1) * sin
                store(osl, yh)

    def store_main(cols, y):
        o_ref[:, cols] = y

    def store_both(cols, y):
        o_ref[:, cols] = y[:tm]
        os_ref[:, cols] = y[tm:]

    xn_ref[:tm, :] = normed(x_ref[...])
    cos = cos_ref[...] if rope else None
    sin = sin_ref[...] if rope else None
    if not has_extra:
        tiles(xn_ref[...], cos, sin, store_main)
        return

    @pl.when(pl.program_id(0) == 0)
    def _():
        xn_ref[tm:, :] = normed(xs_ref[...])
        cos_all = jnp.concatenate([cos, cos_s_ref[...]], axis=0) if rope else None
        sin_all = jnp.concatenate([sin, sin_s_ref[...]], axis=0) if rope else None
        tiles(xn_ref[...], cos_all, sin_all, store_both)

    @pl.when(pl.program_id(0) != 0)
    def _():
        tiles(xn_ref[:tm, :], cos, sin, store_main)


def norm_proj(x, g, ws, layer, head_gain, kinds, cos=None, sin=None, cast=(), tile_src=None, transposed=False,
              extra=None):
    m, d = x.shape
    ws = list(ws) if isinstance(ws, (list, tuple)) else [ws]
    if tile_src is None:
        tile_src = [(0, t * COL_TILE, COL_TILE) for t in range(len(kinds))]
    n_cols = sum(width for _, _, width in tile_src)
    assert len(tile_src) == len(kinds) and head_gain.shape == (1, n_cols)
    assert all(width % HEAD_DIM == 0 for _, _, width in tile_src)
    rope = "rope" in kinds
    has_extra = extra is not None
    ms = extra[0].shape[0] if has_extra else 0
    tm = min(m, ROW_TILE)
    while tm * n_cols * 4 > PROJ_OUT_BLOCK_BYTES and tm % (2 * SUBLANES) == 0:
        tm //= 2
    assert m % tm == 0
    in_specs = [pl.BlockSpec((tm, d), lambda i: (i, 0)),
                pl.BlockSpec((1, d), lambda i: (0, 0))]
    for k, w in enumerate(ws):
        used = max(col + width for which, col, width in tile_src if which == k)
        if transposed:
            assert used <= w.shape[1] and w.shape[2] == d
            in_specs.append(pl.BlockSpec((None, used, d), lambda i: (layer, 0, 0), pipeline_mode=RESIDENT))
            continue
        assert used <= w.shape[2] and (used % LANES == 0 or used == w.shape[2])
        in_specs.append(pl.BlockSpec((None, d, used), lambda i: (layer, 0, 0), pipeline_mode=RESIDENT))
    in_specs.append(pl.BlockSpec((1, n_cols), lambda i: (0, 0)))
    args = [x, g.reshape(1, d), *ws, head_gain]
    if rope:
        assert cos.shape[0] % tm == 0
        tab_blocks = cos.shape[0] // tm
        in_specs += [pl.BlockSpec((tm, HEAD_DIM), lambda i: (i % tab_blocks, 0))] * 2
        args += [cos, sin]
    out_specs = [pl.BlockSpec((tm, n_cols), lambda i: (i, 0))]
    out_shape = [jax.ShapeDtypeStruct((m, n_cols), F32)]
    if has_extra:
        whole = lambda a: pl.BlockSpec(a.shape, lambda i: (0, 0))
        xs, cos_s, sin_s = extra
        extras = [xs] + ([cos_s, sin_s] if rope else [])
        in_specs += [whole(a) for a in extras]
        args += extras
        out_specs.append(pl.BlockSpec((ms, n_cols), lambda i: (0, 0)))
        out_shape.append(jax.ShapeDtypeStruct((ms, n_cols), F32))
    n_out = len(out_specs)
    side = _SideCast(cast, m // tm, lambda i: i)
    outs = pl.pallas_call(
        side.wrap(functools.partial(_proj_kernel, kinds=tuple(kinds), tile_src=tuple(tile_src), n_w=len(ws),
                                    transposed=transposed, has_extra=has_extra),
                  len(args), n_out),
        grid=(m // tm,),
        in_specs=in_specs + side.in_specs,
        out_specs=out_specs + side.out_specs,
        out_shape=out_shape + side.out_shape,
        scratch_shapes=[pltpu.VMEM((tm + ms, d), BF16)],
        compiler_params=_cparams("arbitrary"),
        name="norm_proj",
    )(*args, *side.views)
    casts = [o.reshape(shp) for o, shp in zip(outs[n_out:], side.shapes)]
    return outs[0], (outs[1] if has_extra else None), casts


def _out_proj_kernel(*refs, has_extra):
    if has_extra:
        x_ref, a_ref, b_ref, wa_ref, wb_ref, xs_ref, as_ref, bs_ref, o_ref, os_ref = refs
    else:
        x_ref, a_ref, b_ref, wa_ref, wb_ref, o_ref = refs
    tm = x_ref.shape[0]

    def mixed(a, b):
        acc = jnp.dot(a, wa_ref[...], preferred_element_type=F32)
        return acc + jnp.dot(b, wb_ref[...], preferred_element_type=F32)

    if not has_extra:
        o_ref[...] = x_ref[...] + mixed(a_ref[...], b_ref[...])
        return

    @pl.when(pl.program_id(0) == 0)
    def _():
        acc = mixed(jnp.concatenate([a_ref[...], as_ref[...]], axis=0),
                    jnp.concatenate([b_ref[...], bs_ref[...]], axis=0))
        o_ref[...] = x_ref[...] + acc[:tm]
        os_ref[...] = xs_ref[...] + acc[tm:]

    @pl.when(pl.program_id(0) != 0)
    def _():
        o_ref[...] = x_ref[...] + mixed(a_ref[...], b_ref[...])


def out_proj(x, a, b, w, layer, extra=None):
    m, d = x.shape
    tm = min(m, ROW_TILE)
    ka, kb = a.shape[1], b.shape[1]
    assert w.shape[1] == ka + kb and ka % kb == 0
    has_extra = extra is not None
    in_specs = [pl.BlockSpec((tm, d), lambda i: (i, 0)),
                pl.BlockSpec((tm, ka), lambda i: (i, 0)),
                pl.BlockSpec((tm, kb), lambda i: (i, 0)),
                pl.BlockSpec((None, ka, d), lambda i: (layer, 0, 0), pipeline_mode=RESIDENT),
                pl.BlockSpec((None, kb, d), lambda i: (layer, ka // kb, 0), pipeline_mode=RESIDENT)]
    out_specs = [pl.BlockSpec((tm, d), lambda i: (i, 0))]
    out_shape = [jax.ShapeDtypeStruct((m, d), F32)]
    args = [x, a, b, w, w]
    if has_extra:
        in_specs += [pl.BlockSpec(e.shape, lambda i: (0, 0)) for e in extra]
        args += list(extra)
        out_specs.append(pl.BlockSpec(extra[0].shape, lambda i: (0, 0)))
        out_shape.append(jax.ShapeDtypeStruct(extra[0].shape, F32))
    outs = pl.pallas_call(
        functools.partial(_out_proj_kernel, has_extra=has_extra),
        grid=(m // tm,),
        in_specs=in_specs,
        out_specs=out_specs,
        out_shape=out_shape,
        compiler_params=_cparams("arbitrary"),
        name="out_proj",
    )(*args)
    return outs[0], (outs[1] if has_extra else None)


def _mlp_kernel(*refs, has_extra):
    if has_extra:
        x_ref, g_ref, wu_ref, wd_ref, xs_ref, o_ref, os_ref, xn_ref = refs
    else:
        x_ref, g_ref, wu_ref, wd_ref, o_ref, xn_ref = refs
    i, f = pl.program_id(0), pl.program_id(1)
    tm = x_ref.shape[0]

    def normed(x):
        return (x * _rms_scale(x) * g_ref[...]).astype(BF16)

    def ffn(xn):
        h = jnp.dot(xn, wu_ref[...], preferred_element_type=F32)
        h = jnp.maximum(h, 0.0)
        h = (h * h).astype(BF16)
        return jnp.dot(h, wd_ref[...], preferred_element_type=F32)

    @pl.when(f == 0)
    def _():
        x = x_ref[...]
        xn_ref[:tm, :] = normed(x)
        o_ref[...] = x

    if not has_extra:
        o_ref[...] += ffn(xn_ref[...])
        return

    @pl.when(i == 0)
    def _():
        @pl.when(f == 0)
        def _():
            xs = xs_ref[...]
            xn_ref[tm:, :] = normed(xs)
            os_ref[...] = xs

        upd = ffn(xn_ref[...])
        o_ref[...] += upd[:tm]
        os_ref[...] += upd[tm:]

    @pl.when(i != 0)
    def _():
        o_ref[...] += ffn(xn_ref[:tm, :])


def mlp(x, g, wu, wd, tf=1024, cast=(), extra=None):
    m, d = x.shape
    ff = wu.shape[1]
    tm = min(m, ROW_TILE)
    nf = ff // tf
    has_extra = extra is not None
    side = _SideCast(cast, (m // tm) * nf, lambda i, j: i * nf + j)
    in_specs = [pl.BlockSpec((tm, d), lambda i, j: (i, 0)),
                pl.BlockSpec((1, d), lambda i, j: (0, 0)),
                pl.BlockSpec((d, tf), lambda i, j: (0, j)),
                pl.BlockSpec((tf, d), lambda i, j: (j, 0))]
    out_specs = [pl.BlockSpec((tm, d), lambda i, j: (i, 0))]
    out_shape = [jax.ShapeDtypeStruct((m, d), F32)]
    args = [x, g.reshape(1, d), wu, wd]
    ms = extra.shape[0] if has_extra else 0
    if has_extra:
        whole = pl.BlockSpec((ms, d), lambda i, j: (0, 0))
        in_specs.append(whole)
        out_specs.append(whole)
        out_shape.append(jax.ShapeDtypeStruct((ms, d), F32))
        args.append(extra)
    scratch = [pltpu.VMEM((tm + ms, d), BF16)]
    n_out = len(out_specs)
    outs = pl.pallas_call(
        side.wrap(functools.partial(_mlp_kernel, has_extra=has_extra), len(args), n_out),
        grid=(m // tm, nf),
        in_specs=in_specs + side.in_specs,
        out_specs=out_specs + side.out_specs,
        out_shape=out_shape + side.out_shape,
        scratch_shapes=scratch,
        compiler_params=_cparams("arbitrary", "arbitrary"),
        name="mlp",
    )(*args, *side.views)
    casts = [o.reshape(shp) for o, shp in zip(outs[n_out:], side.shapes)]
    return outs[0], (outs[1] if has_extra else None), casts


def _split3(x):
    hi = x.astype(BF16)
    r1 = x - hi.astype(F32)
    mid = r1.astype(BF16)
    lo = (r1 - mid.astype(F32)).astype(BF16)
    return hi, mid, lo


def _ssd_kernel(xbc_ref, dt_ref, z_ref, cw_ref, cb_ref, dtb_ref, alog_ref, dsk_ref, gg_ref, tri_ref, expand_ref,
                conv0_ref, h0_ref, y_ref, convo_ref, ho_ref,
                ext_ref, ht_ref, yd_ref, st_ref, *, t_real, n_heads, d_inner):
    L = SSD_CHUNK
    P = SSM_HEAD_DIM
    N = SSM_D_STATE
    G = SSM_GROUPS
    hpg = n_heads // G
    gw = hpg * P
    hist = CONV_W - 1
    base = SUBLANES - hist
    c = pl.program_id(1)
    nc = pl.num_programs(1)

    @pl.when(c == 0)
    def _():
        ext_ref[base:SUBLANES, :] = conv0_ref[...]
        for g in range(G):
            ht_ref[g] = h0_ref[g].T

    ext_ref[SUBLANES:SUBLANES + L, :] = xbc_ref[...]
    ext = ext_ref[...]
    acc = cb_ref[...] + ext[SUBLANES:, :] * cw_ref[hist:CONV_W, :]
    for k in range(1, CONV_W):
        acc = acc + pltpu.roll(ext, k, axis=0)[SUBLANES:, :] * cw_ref[hist - k:CONV_W - k, :]
    xc = acc * _sigmoid(acc)

    @pl.when(c == nc - 1)
    def _():
        n_in_last = t_real - ((t_real - 1) // L) * L
        convo_ref[...] = ext_ref[base + n_in_last:base + n_in_last + hist, :]

    ext_ref[base:SUBLANES, :] = ext_ref[base + L:SUBLANES + L, :]

    dt_raw = dt_ref[...] + dtb_ref[...]
    dt = jnp.maximum(dt_raw, 0.0) + jnp.log1p(jnp.exp(-jnp.abs(dt_raw)))
    row = lax.broadcasted_iota(jnp.int32, (L, LANES), 0) + c * L
    dt = jnp.where(row < t_real, dt, 0.0)
    da = dt * (-jnp.exp(alog_ref[...]))
    li = lax.broadcasted_iota(jnp.int32, (L, L), 0)
    si = lax.broadcasted_iota(jnp.int32, (L, L), 1)
    causal = li >= si
    a_cs = sum(jnp.dot(tri_ref[...], part, preferred_element_type=F32) for part in _split3(da))
    a_cs_t = a_cs.T
    dt_t = dt.T
    w_t = dt_t * jnp.exp(a_cs_t[:, L - 1:L] - a_cs_t)
    e_exp = sum(jnp.dot(part, expand_ref[...], preferred_element_type=F32) for part in _split3(jnp.exp(a_cs)))
    xb = xc[:, :d_inner].astype(BF16)
    low_half = lax.broadcasted_iota(jnp.int32, (L, 2 * P), 1) < P

    for g in range(G):
        b_g = xc[:, d_inner + g * N:d_inner + (g + 1) * N]
        c_g = xc[:, d_inner + G * N + g * N:d_inner + G * N + (g + 1) * N].astype(BF16)
        cbm = lax.dot_general(c_g, b_g.astype(BF16), NT_DIMS, preferred_element_type=F32)
        b_t = b_g.T
        y_off = jnp.dot(c_g, ht_ref[g].astype(BF16), preferred_element_type=F32)
        for kp in range(hpg // 2):
            h0 = g * hpg + 2 * kp
            x_pair = xb[:, h0 * P:(h0 + 2) * P]
            yd, st = [], []
            for h in (h0, h0 + 1):
                seg = a_cs[:, h:h + 1] - a_cs_t[h:h + 1, :]
                dec = jnp.exp(jnp.where(causal, seg, -jnp.inf))
                mh = (cbm * dec * dt_t[h:h + 1, :]).astype(BF16)
                yd.append(jnp.dot(mh, x_pair, preferred_element_type=F32))
                btw = (b_t * w_t[h:h + 1, :]).astype(BF16)
                st.append(jnp.dot(btw, x_pair, preferred_element_type=F32))
            yd_ref[:, 2 * kp * P:(2 * kp + 2) * P] = jnp.where(low_half, yd[0], yd[1])
            st_ref[:, 2 * kp * P:(2 * kp + 2) * P] = jnp.where(low_half, st[0], st[1])
        gs = slice(g * gw, (g + 1) * gw)
        e_g = e_exp[:, gs]
        y_g = yd_ref[...] + y_off * e_g + dsk_ref[:, gs] * xc[:, gs]
        ht_ref[g] = ht_ref[g] * e_g[L - 1:L, :] + st_ref[...]
        z_g = z_ref[:, gs]
        gated = y_g * (z_g * _sigmoid(z_g))
        y_ref[:, gs] = (gated * _rms_scale(gated) * gg_ref[:, gs]).astype(y_ref.dtype)

    @pl.when(c == nc - 1)
    def _():
        for g in range(G):
            ho_ref[g] = ht_ref[g].T


def ssd_mixer(proj, col_xbc, col_dt, col_z, t_real, conv_w, conv_b, dt_bias, a_log, d_skip, gate_g,
              conv_prev, h_prev, cast=()):
    bt, tp, _ = proj.shape
    conv_dim = conv_w.shape[1]
    n_heads = dt_bias.shape[0]
    d_inner = n_heads * SSM_HEAD_DIM
    gw = d_inner // SSM_GROUPS
    L = SSD_CHUNK
    assert tp % L == 0 and (tp - t_real) < L
    assert col_xbc % conv_dim == 0 and col_dt % LANES == 0 and col_z % d_inner == 0
    pad = LANES - n_heads
    dtb = jnp.pad(dt_bias, (0, pad)).reshape(1, LANES)
    alog = jnp.pad(a_log, (0, pad)).reshape(1, LANES)
    dsk = jnp.repeat(d_skip, SSM_HEAD_DIM).reshape(1, d_inner)
    tri = jnp.tril(jnp.ones((L, L), BF16))
    expand = jnp.repeat(jnp.eye(LANES, dtype=BF16)[:, :n_heads], SSM_HEAD_DIM, axis=1)
    kern = functools.partial(_ssd_kernel, t_real=t_real, n_heads=n_heads, d_inner=d_inner)
    vec = lambda n: pl.BlockSpec((1, n), lambda b, c: (0, 0))
    nc = tp // L
    side = _SideCast(cast, bt * nc, lambda b, c: b * nc + c)
    outs = pl.pallas_call(
        side.wrap(kern, 13, 3),
        grid=(bt, nc),
        in_specs=[pl.BlockSpec((None, L, conv_dim), lambda b, c: (b, c, col_xbc // conv_dim)),
                  pl.BlockSpec((None, L, LANES), lambda b, c: (b, c, col_dt // LANES)),
                  pl.BlockSpec((None, L, d_inner), lambda b, c: (b, c, col_z // d_inner)),
                  pl.BlockSpec((CONV_W, conv_dim), lambda b, c: (0, 0)),
                  vec(conv_dim), vec(LANES), vec(LANES), vec(d_inner), vec(d_inner),
                  pl.BlockSpec((L, L), lambda b, c: (0, 0)),
                  pl.BlockSpec((LANES, d_inner), lambda b, c: (0, 0)),
                  pl.BlockSpec((None, CONV_W - 1, conv_dim), lambda b, c: (b, 0, 0)),
                  pl.BlockSpec((None, SSM_GROUPS, gw, SSM_D_STATE), lambda b, c: (b, 0, 0, 0))] + side.in_specs,
        out_specs=[pl.BlockSpec((None, L, d_inner), lambda b, c: (b, c, 0)),
                   pl.BlockSpec((None, CONV_W - 1, conv_dim), lambda b, c: (b, 0, 0)),
                   pl.BlockSpec((None, SSM_GROUPS, gw, SSM_D_STATE), lambda b, c: (b, 0, 0, 0))] + side.out_specs,
        out_shape=[jax.ShapeDtypeStruct((bt, tp, d_inner), BF16),
                   jax.ShapeDtypeStruct((bt, CONV_W - 1, conv_dim), F32),
                   jax.ShapeDtypeStruct((bt, SSM_GROUPS, gw, SSM_D_STATE), F32)] + side.out_shape,
        scratch_shapes=[pltpu.VMEM((SUBLANES + L, conv_dim), F32),
                        pltpu.VMEM((SSM_GROUPS, SSM_D_STATE, gw), F32),
                        pltpu.VMEM((L, gw), F32),
                        pltpu.VMEM((SSM_D_STATE, gw), F32)],
        compiler_params=_cparams("parallel", "arbitrary"),
        name="ssd_mixer",
    )(proj, proj, proj, conv_w, conv_b.reshape(1, conv_dim), dtb, alog, dsk, gate_g.reshape(1, d_inner),
      tri, expand, conv_prev, h_prev, *side.views)
    return side.split(outs, 3)


def _mem_attn_head(q, k, v):
    q = (q * (HEAD_DIM ** -0.5 * LOG2_E)).astype(BF16)
    s = lax.dot_general(q, k.astype(BF16), NT_DIMS, preferred_element_type=F32)
    e = jnp.exp2(s - jnp.max(s, axis=-1, keepdims=True))
    den = jnp.sum(e, axis=-1, keepdims=True)
    return jnp.dot(e.astype(BF16), v.astype(BF16), preferred_element_type=F32) / den


def _mem_attn_kernel(q_ref, k_ref, v_ref, o_ref):
    for h in range(MEM_HEADS):
        sl = slice(h * HEAD_DIM, (h + 1) * HEAD_DIM)
        o_ref[:, sl] = _mem_attn_head(q_ref[:, sl], k_ref[:, sl], v_ref[:, sl]).astype(o_ref.dtype)


def _mem_attn_cached_kernel(q_ref, k_ref, v_ref, o_ref):
    for b in range(q_ref.shape[0]):
        for h in range(MEM_HEADS):
            sl = slice(h * HEAD_DIM, (h + 1) * HEAD_DIM)
            o_ref[b, :, sl] = _mem_attn_head(q_ref[b, :, sl], k_ref[b, :, h, :], v_ref[b, :, h, :]).astype(o_ref.dtype)


def mem_attention(proj, col_q, k, v, kv_at):
    bt, t, _ = proj.shape
    width = MEM_HEADS * HEAD_DIM
    if k.ndim == 5:
        mlen = k.shape[2]
        kv_spec = pl.BlockSpec((None, bt, mlen, MEM_HEADS, HEAD_DIM), lambda i: (kv_at, 0, 0, 0, 0),
                               pipeline_mode=RESIDENT)
        return pl.pallas_call(
            _mem_attn_cached_kernel,
            grid=(1,),
            in_specs=[pl.BlockSpec((bt, t, width), lambda i: (0, 0, col_q // width)), kv_spec, kv_spec],
            out_specs=pl.BlockSpec((bt, t, width), lambda i: (0, 0, 0)),
            out_shape=jax.ShapeDtypeStruct((bt, t, width), BF16),
            compiler_params=_cparams("arbitrary"),
            name="mem_attention_cached",
        )(proj, k, v)
    tq = min(t, ROW_TILE)
    mlen = k.shape[1]
    kv_specs = [pl.BlockSpec((None, mlen, width), lambda b, i, c=c: (b, 0, c // width)) for c in kv_at]
    return pl.pallas_call(
        _mem_attn_kernel,
        grid=(bt, t // tq),
        in_specs=[pl.BlockSpec((None, tq, width), lambda b, i: (b, i, col_q // width))] + kv_specs,
        out_specs=pl.BlockSpec((None, tq, width), lambda b, i: (b, i, 0)),
        out_shape=jax.ShapeDtypeStruct((bt, t, width), BF16),
        compiler_params=_cparams("parallel", "parallel"),
        name="mem_attention",
    )(proj, k, v)


def _run_units(first, count, unit, unroll):
    trips = count // unroll
    if trips == 1:
        trips = 0
    if trips:
        def trip(it, carry):
            for k in range(unroll):
                unit(first + it * unroll + k)
            return carry
        lax.fori_loop(0, trips, trip, 0)
    for k in range(trips * unroll, count):
        unit(first + k)


def _dil_attn_kernel(*refs, groups, rb, unroll):
    ng = len(groups)
    q_refs = refs[:ng]
    kp_ref, kc_ref, vp_ref, vc_ref, o_ref = refs[ng:ng + 5]
    og = refs[ng + 5:2 * ng + 5]
    lg = refs[2 * ng + 5:3 * ng + 5]
    i = pl.program_id(1)
    scale = HEAD_DIM ** -0.5

    for gi, (win, d) in enumerate(groups):
        w = win // d
        dw = d * w
        qi = lax.broadcasted_iota(jnp.int32, (w, 2 * w), 0)
        kj = lax.broadcasted_iota(jnp.int32, (w, 2 * w), 1)
        band = (kj >= qi) & (kj <= qi + w)
        band_first = band & (kj >= jnp.where(i > 0, 0, w))

        def rows(start, size, d=d, w=w):
            if d == 1:
                return pl.ds(pl.multiple_of(start, w), size)
            return pl.ds(start, size, stride=d)

        def attend(rows_q, kk, vv, mask, gi=gi, w=w):
            q = (q_refs[gi][rows_q, :] * (scale * LOG2_E)).astype(BF16)
            s = lax.dot_general(q, kk.astype(BF16), NT_DIMS, preferred_element_type=F32)
            s = jnp.where(mask, s, -jnp.inf)
            m = jnp.max(s, axis=-1, keepdims=True)
            e = jnp.exp2(s - m)
            den = jnp.sum(e, axis=-1, keepdims=True)
            o = jnp.dot(e.astype(BF16), vv.astype(BF16), preferred_element_type=F32) / den
            og[gi][rows_q, :] = o
            lg[gi][rows_q, :] = jnp.broadcast_to(m * LN_2 + jnp.log(den), (w, HEAD_DIM))

        def first_unit(r, rows=rows, attend=attend, w=w, dw=dw, mask=band_first):
            rows_q, rows_p = rows(r, w), rows(rb - dw + r, w)
            kk = jnp.concatenate([kp_ref[rows_p, :], kc_ref[rows_q, :]], axis=0)
            vv = jnp.concatenate([vp_ref[rows_p, :], vc_ref[rows_q, :]], axis=0)
            attend(rows_q, kk, vv, mask)

        def later_unit(u, rows=rows, attend=attend, d=d, w=w, dw=dw, mask=band):
            sb = u // d
            start_q = sb * dw + (u - sb * d)
            rows_k = rows(start_q - dw, 2 * w)
            attend(rows(start_q, w), kc_ref[rows_k, :], vc_ref[rows_k, :], mask)

        _run_units(0, d, first_unit, unroll)
        _run_units(d, rb // w - d, later_unit, unroll)

    ls = [r[...] for r in lg]
    mm = functools.reduce(jnp.maximum, ls)
    ws = [jnp.exp(l - mm) for l in ls]
    num = sum(wg * r[...] for wg, r in zip(ws, og))
    o_ref[...] = (num / sum(ws)).astype(o_ref.dtype)


def dilated_attention(proj, kv):
    bt, t, _ = proj.shape
    ng = len(DIL_GROUPS)
    rb = min(t, max(win for win, _ in DIL_GROUPS))
    for win, d in DIL_GROUPS:
        assert win % d == 0 and rb % win == 0
    assert t % rb == 0
    blk = lambda f: pl.BlockSpec((None, rb, HEAD_DIM), f)
    prev = lambda i: jnp.maximum(i - 1, 0)
    q_specs = [blk(lambda b, i, h, g=g: (b, i, g * DIL_HEADS + h)) for g in range(ng)]
    kv_specs = [blk(lambda b, i, h: (b, prev(i), h)), blk(lambda b, i, h: (b, i, h)),
                blk(lambda b, i, h: (b, prev(i), DIL_HEADS + h)), blk(lambda b, i, h: (b, i, DIL_HEADS + h))]
    return pl.pallas_call(
        functools.partial(_dil_attn_kernel, groups=DIL_GROUPS, rb=rb, unroll=DIL_UNROLL),
        grid=(bt, t // rb, DIL_HEADS),
        in_specs=q_specs + kv_specs,
        out_specs=blk(lambda b, i, h: (b, i, h)),
        out_shape=jax.ShapeDtypeStruct((bt, t, DIL_HEADS * HEAD_DIM), BF16),
        scratch_shapes=[pltpu.VMEM((rb, HEAD_DIM), F32)] * (2 * ng),
        compiler_params=_cparams("parallel", "parallel", "parallel"),
        name="dilated_attention",
    )(*([proj] * ng), kv, kv, kv, kv)


def _dil_decode_kernel(*refs):
    ng = len(DIL_GROUPS)
    q_ref, kv_ref = refs[:2]
    k_refs, v_refs = refs[2:2 + ng], refs[2 + ng:2 + 2 * ng]
    o_ref = refs[-1]
    width = DIL_HEADS * HEAD_DIM
    scale = HEAD_DIM ** -0.5
    for b in range(q_ref.shape[0]):
        for h in range(DIL_HEADS):
            sl = slice(h * HEAD_DIM, (h + 1) * HEAD_DIM)
            k_new = kv_ref[b, :, h * HEAD_DIM:(h + 1) * HEAD_DIM]
            v_new = kv_ref[b, :, width + h * HEAD_DIM:width + (h + 1) * HEAD_DIM]
            outs, lses = [], []
            for gi in range(ng):
                kc = k_refs[gi][b, :, h, :].astype(BF16)
                vc = v_refs[gi][b, :, h, :].astype(BF16)
                q = q_ref[b, :, gi * width + h * HEAD_DIM:gi * width + (h + 1) * HEAD_DIM]
                s = lax.dot_general(q.astype(BF16), kc, NT_DIMS, preferred_element_type=F32) * scale
                s_self = jnp.sum(q * k_new, axis=-1, keepdims=True) * scale
                m = jnp.maximum(jnp.max(s, axis=-1, keepdims=True), s_self)
                e = jnp.exp(s - m)
                e_self = jnp.exp(s_self - m)
                den = jnp.sum(e, axis=-1, keepdims=True) + e_self
                o = jnp.dot(e.astype(BF16), vc, preferred_element_type=F32) + e_self * v_new
                outs.append(o / den)
                lses.append(m + jnp.log(den))
            mm = functools.reduce(jnp.maximum, lses)
            ws = [jnp.exp(l - mm) for l in lses]
            num = sum(wg * o for wg, o in zip(ws, outs))
            o_ref[b, :, sl] = (num / sum(ws)).astype(o_ref.dtype)


def dilated_decode(proj, kv_new, k_cache, v_cache):
    bt, rows, nq = proj.shape
    lc = k_cache.shape[1]
    width = DIL_HEADS * HEAD_DIM
    views, specs = [], []
    for cache in (k_cache, v_cache):
        for win, dil in DIL_GROUPS:
            assert lc % win == 0 and win % dil == 0
            w = win // dil
            views.append(cache.reshape(bt, lc // dil, dil, DIL_HEADS, HEAD_DIM))
            specs.append(pl.BlockSpec((bt, w, None, DIL_HEADS, HEAD_DIM),
                                      lambda i, last=lc // win - 1: (0, last, 0, 0, 0), pipeline_mode=RESIDENT))
    return pl.pallas_call(
        _dil_decode_kernel,
        grid=(1,),
        in_specs=[pl.BlockSpec((bt, rows, nq), lambda i: (0, 0, 0)),
                  pl.BlockSpec((bt, rows, 2 * width), lambda i: (0, 0, 0))] + specs,
        out_specs=pl.BlockSpec((bt, rows, width), lambda i: (0, 0, 0)),
        out_shape=jax.ShapeDtypeStruct((bt, rows, width), BF16),
        compiler_params=_cparams("arbitrary"),
        name="dilated_decode",
    )(proj, kv_new, *views)


def kernel(x_prompt, x_sample, state_conv, state_ssm, cache_win_k, cache_win_v, cache_mem_k, cache_mem_v,
           mem_prompt, norm_mix_g, norm_mlp_g, norm_mem_g, w_mem_k, w_mem_v, mem_q_norm_g, mem_k_norm_g,
           w_up, w_down, w_in_a, conv_w, conv_b, dt_bias, a_log, d_skip, gate_norm_g, w_out_a,
           w_in_b, q_norm_g, w_out_b, kv_norm_g, w_k_shared, w_v_shared, k_norm_g):
    depth = w_up.shape[0]
    n_a = w_in_a.shape[0]
    d_model = x_prompt.shape[-1]
    n_heads = dt_bias.shape[1]
    d_inner = n_heads * SSM_HEAD_DIM
    conv_dim = conv_w.shape[2]
    mem_w = MEM_HEADS * HEAD_DIM
    kv_w = DIL_HEADS * HEAD_DIM
    dil_q_w = len(DIL_GROUPS) * kv_w

    s1, s2, s3 = d_inner, d_inner + conv_dim, d_inner + conv_dim + n_heads
    assert s1 % COL_TILE == 0 and s2 % COL_TILE == 0
    w_in_a_t = jnp.swapaxes(w_in_a, 1, 2)
    w_a_main = round_weight(w_in_a_t, s2)
    tail = lax.optimization_barrier(w_in_a_t[:, s2:]).astype(BF16)
    w_a_tail = jnp.concatenate([tail[:, :n_heads], jnp.zeros((n_a, COL_TILE - n_heads, d_model), BF16),
                                tail[:, n_heads:]], axis=1)
    tiles_a = ([(0, s1 + c, COL_TILE) for c in range(0, conv_dim, COL_TILE)]
               + [(1, COL_TILE + c, COL_TILE) for c in range(0, mem_w, COL_TILE)]
               + [(0, c, COL_TILE) for c in range(0, d_inner, COL_TILE)] + [(1, 0, LANES)])
    kinds_a = (["plain"] * (conv_dim // COL_TILE) + ["norm"] * (mem_w // COL_TILE)
               + ["plain"] * (d_inner // COL_TILE + 1))
    col_xbc, col_qm_a, col_z = 0, conv_dim, conv_dim + mem_w
    col_dt = col_z + d_inner
    a_cols = col_dt + LANES
    kinds_b = ["rope"] * (dil_q_w // COL_TILE) + ["norm"] * (mem_w // COL_TILE)
    kinds_kv = ["rope"] * (kv_w // COL_TILE) + ["plain"] * (kv_w // COL_TILE)
    kinds_mkv = ["norm"] * (mem_w // COL_TILE) + ["plain"] * (mem_w // COL_TILE)
    w_b = w_in_b.astype(BF16)
    w_kv = jnp.concatenate([w_k_shared, w_v_shared], axis=-1).astype(BF16)[None]
    w_mkv = jnp.concatenate([w_mem_k, w_mem_v], axis=-1).astype(BF16)
    w_out_a_b, w_out_b_b = w_out_a.astype(BF16), w_out_b.astype(BF16)
    assert n_a >= 1
    w_up_b, w_down_b = [None] * depth, [None] * depth

    def gain_row(width, pieces):
        parts, at = [], 0
        for start, g, reps in pieces:
            parts += [jnp.ones((start - at,), F32), jnp.tile(g, reps)]
            at = start + reps * HEAD_DIM
        parts.append(jnp.ones((width - at,), F32))
        return jnp.concatenate(parts).reshape(1, width)

    class Group:
        def __init__(self, x, pos_rows, conv_prev, ssm_prev, mem_kv, k_past, v_past, t_real):
            self.bt, self.t, _ = x.shape
            self.m = self.bt * self.t
            self.x2 = x.reshape(self.m, d_model)
            self.cos, self.sin = rope_tables(pos_rows)
            self.conv_prev, self.ssm_prev, self.mem_kv = conv_prev, ssm_prev, mem_kv
            self.k_past, self.v_past, self.t_real = k_past, v_past, t_real
            self.conv_new, self.ssm_new, self.kv3 = [], [], None

    def in_proj_a(prompt, sample, l):
        hg = gain_row(a_cols, [(col_qm_a, mem_q_norm_g[l], MEM_HEADS)])
        proj_p, proj_s, rounded = norm_proj(prompt.x2, norm_mix_g[l], [w_a_main, w_a_tail], l, hg, kinds_a,
                                            tile_src=tiles_a, transposed=True, extra=(sample.x2, None, None),
                                            cast=((w_up, 0),) if l == 0 else ())
        if l == 0:
            (w_up_b[0],) = rounded
        return proj_p, proj_s

    def mix_a(gr, proj, l, round_mlp_weights):
        bt, t, m = gr.bt, gr.t, gr.m
        proj3 = proj.reshape(bt, t, -1)
        tp = -(-t // SSD_CHUNK) * SSD_CHUNK
        proj_ssd = proj3 if tp == t else jnp.pad(proj3, ((0, 0), (0, tp - t), (0, 0)))
        ssd = ssd_mixer(proj_ssd, col_xbc, col_dt, col_z, gr.t_real, conv_w[l], conv_b[l],
                        dt_bias[l], a_log[l], d_skip[l], gate_norm_g[l],
                        gr.conv_prev[l], gr.ssm_prev[l].reshape(bt, SSM_GROUPS, -1, SSM_D_STATE),
                        cast=((w_down, 0),) if round_mlp_weights else ())
        if round_mlp_weights:
            ssd, (w_down_b[0],) = ssd
        y, c_new, h_new = ssd
        gr.conv_new.append(c_new)
        gr.ssm_new.append(h_new.reshape(bt, n_heads, SSM_HEAD_DIM, SSM_D_STATE))
        mo = mem_attention(proj3, col_qm_a, *gr.mem_kv(l)).reshape(m, mem_w)
        return y[:, :t].reshape(m, d_inner), mo

    def shared_kv(prompt, sample):
        hg = gain_row(2 * kv_w, [(0, k_norm_g, DIL_HEADS)])
        kv_p, kv_s, _ = norm_proj(prompt.x2, kv_norm_g, w_kv, 0, hg, kinds_kv, prompt.cos, prompt.sin,
                                  extra=(sample.x2, sample.cos, sample.sin))
        prompt.kv3 = kv_p.reshape(prompt.bt, prompt.t, 2 * kv_w)
        sample.kv3 = kv_s.reshape(sample.bt, sample.t, 2 * kv_w)

    def in_proj_b(prompt, sample, l):
        j = l - n_a
        hg = gain_row(dil_q_w + mem_w, [(0, q_norm_g[j], dil_q_w // HEAD_DIM),
                                        (dil_q_w, mem_q_norm_g[l], MEM_HEADS)])
        proj_p, proj_s, _ = norm_proj(prompt.x2, norm_mix_g[l], w_b, j, hg, kinds_b, prompt.cos, prompt.sin,
                                      extra=(sample.x2, sample.cos, sample.sin))
        return proj_p, proj_s

    def mix_b(gr, proj, l):
        proj3 = proj.reshape(gr.bt, gr.t, -1)
        if gr.k_past is None:
            att = dilated_attention(proj3, gr.kv3)
        else:
            att = dilated_decode(proj3, gr.kv3, gr.k_past, gr.v_past)
        mo = mem_attention(proj3, dil_q_w, *gr.mem_kv(l)).reshape(gr.m, mem_w)
        return att.reshape(gr.m, kv_w), mo

    bp, t_p, _ = x_prompt.shape
    mlen = mem_prompt.shape[1]
    mem2 = mem_prompt.reshape(bp * mlen, d_model)
    mkv_p = []
    for l in range(depth):
        hg = gain_row(2 * mem_w, [(0, mem_k_norm_g[l], MEM_HEADS)])
        mkv_p.append(norm_proj(mem2, norm_mem_g[l], w_mkv, l, hg, kinds_mkv)[0].reshape(bp, mlen, 2 * mem_w))
    conv0 = jnp.zeros((n_a, bp, CONV_W - 1, conv_dim), F32)
    ssm0 = jnp.zeros((n_a, bp, n_heads, SSM_HEAD_DIM, SSM_D_STATE), F32)
    prompt = Group(x_prompt, jnp.arange(t_p, dtype=jnp.int32), conv0, ssm0,
                   lambda l: (mkv_p[l], mkv_p[l], (0, mem_w)), None, None, t_p)

    bs, t_s, _ = x_sample.shape
    assert t_s == 1
    xs = jnp.pad(x_sample, ((0, 0), (0, SAMPLE_ROWS - t_s), (0, 0)))
    pos_s = jnp.full((bs * SAMPLE_ROWS,), PAST_LEN, jnp.int32)
    sample = Group(xs, pos_s, state_conv, state_ssm, lambda l: (cache_mem_k, cache_mem_v, l),
                   cache_win_k, cache_win_v, t_s)

    for l in range(depth):
        if l < n_a:
            proj_p, proj_s = in_proj_a(prompt, sample, l)
            a_p, b_p = mix_a(prompt, proj_p, l, round_mlp_weights=(l == 0))
            a_s, b_s = mix_a(sample, proj_s, l, round_mlp_weights=False)
            w_out, lw = w_out_a_b, l
        else:
            if l == n_a:
                shared_kv(prompt, sample)
            proj_p, proj_s = in_proj_b(prompt, sample, l)
            a_p, b_p = mix_b(prompt, proj_p, l)
            a_s, b_s = mix_b(sample, proj_s, l)
            w_out, lw = w_out_b_b, l - n_a
        prompt.x2, sample.x2 = out_proj(prompt.x2, a_p, b_p, w_out, lw, extra=(sample.x2, a_s, b_s))
        nxt = ((w_up, l + 1), (w_down, l + 1)) if l + 1 < depth else ()
        prompt.x2, sample.x2, rounded = mlp(prompt.x2, norm_mlp_g[l], w_up_b[l], w_down_b[l],
                                            cast=nxt, extra=sample.x2)
        if nxt:
            w_up_b[l + 1], w_down_b[l + 1] = rounded

    y_p = prompt.x2.reshape(bp, t_p, d_model)
    conv_p, ssm_p, kv_p = jnp.stack(prompt.conv_new), jnp.stack(prompt.ssm_new), prompt.kv3
    mkv_all = jnp.stack(mkv_p)
    mem_k_p = mkv_all[..., :mem_w].reshape(depth, bp, mlen, MEM_HEADS, HEAD_DIM)
    mem_v_p = mkv_all[..., mem_w:].reshape(depth, bp, mlen, MEM_HEADS, HEAD_DIM)
    keep = min(max(w for w, _ in DIL_GROUPS), t_p)
    win_k_p = kv_p[:, t_p - keep:, :kv_w].reshape(bp, keep, DIL_HEADS, HEAD_DIM)
    win_v_p = kv_p[:, t_p - keep:, kv_w:].reshape(bp, keep, DIL_HEADS, HEAD_DIM)

    y_s = sample.x2.reshape(bs, SAMPLE_ROWS, d_model)
    conv_s, ssm_s, kv_s = jnp.stack(sample.conv_new), jnp.stack(sample.ssm_new), sample.kv3
    y_s = y_s[:, :t_s]
    k_s = kv_s[:, :t_s, :kv_w].reshape(bs, t_s, DIL_HEADS, HEAD_DIM)
    v_s = kv_s[:, :t_s, kv_w:].reshape(bs, t_s, DIL_HEADS, HEAD_DIM)

    return (y_p, y_s, conv_p, ssm_p, win_k_p, win_v_p, mem_k_p, mem_v_p, conv_s, ssm_s, k_s, v_s)
```

```python
import functools

import jax
import jax.numpy as jnp
from jax import lax
from jax.experimental import pallas as pl
from jax.experimental.pallas import tpu as pltpu

F32 = jnp.float32
BF16 = jnp.bfloat16

HEAD_DIM = 128
SSM_HEAD_DIM = 64
SSM_GROUPS = 4
SSM_D_STATE = 128
CONV_W = 4
SSD_CHUNK = 128
MEM_HEADS = 4
DIL_GROUPS = ((128, 1), (512, 4), (2048, 16))
DIL_HEADS = 4
PAST_LEN = 16384
ROPE_THETA = 10000.0
EPS = 1e-6
LOG2_E = 1.4426950408889634
LN_2 = 0.6931471805599453

LANES = 128
SUBLANES = 8
VMEM_LIMIT = 48 * 1024 * 1024
MIX_MLP_VMEM_LIMIT = 56 * 1024 * 1024
COL_TILE = 512
ROW_TILE = 512
PROJ_OUT_BLOCK_BYTES = 5 * 1024 * 1024
SAMPLE_ROWS = SUBLANES
DIL_UNROLL = 16

NT_DIMS = (((1,), (1,)), ((), ()))
RESIDENT = pl.Buffered(1)


def _cparams(*sem):
    return pltpu.CompilerParams(dimension_semantics=sem, vmem_limit_bytes=VMEM_LIMIT)


def _sigmoid(x):
    return 1.0 / (1.0 + jnp.exp(-x))


def _rms_scale(x):
    return lax.rsqrt(jnp.mean(x * x, axis=-1, keepdims=True) + EPS)


class _SideCast:
    def __init__(self, mats, steps, step_of):
        self.n = len(mats)
        self.shapes = [a.shape[1:] for a, _ in mats]
        self.views, self.in_specs, self.out_specs, self.out_shape = [], [], [], []
        for a, layer in mats:
            n_layers, r, c = a.shape
            assert r % (steps * 2 * SUBLANES) == 0
            band = r // steps
            self.views.append(a.reshape(n_layers, steps, band, c))
            self.in_specs.append(pl.BlockSpec((None, None, band, c),
                                              lambda *ids, layer=layer: (layer, step_of(*ids), 0, 0)))
            self.out_specs.append(pl.BlockSpec((None, band, c), lambda *ids: (step_of(*ids), 0, 0)))
            self.out_shape.append(jax.ShapeDtypeStruct((steps, band, c), BF16))

    def wrap(self, body, n_in, n_out):
        n = self.n
        if not n:
            return body

        def kern(*refs):
            ins, srcs = refs[:n_in], refs[n_in:n_in + n]
            outs, dsts = refs[n_in + n:n_in + n + n_out], refs[n_in + n + n_out:n_in + 2 * n + n_out]
            for s, d in zip(srcs, dsts):
                d[...] = s[...].astype(BF16)
            body(*ins, *outs, *refs[n_in + 2 * n + n_out:])
        return kern

    def split(self, outs, n_out):
        main = outs[0] if n_out == 1 else tuple(outs[:n_out])
        if not self.n:
            return main
        return main, [o.reshape(s) for o, s in zip(outs[n_out:], self.shapes)]


def _round_kernel(src_ref, dst_ref):
    dst_ref[...] = src_ref[...].astype(BF16)


def round_weight(w, n_rows):
    n_layers, _, c = w.shape
    band = ROW_TILE
    assert n_rows % band == 0
    return pl.pallas_call(
        _round_kernel,
        grid=(n_layers, n_rows // band),
        in_specs=[pl.BlockSpec((None, band, c), lambda l, i: (l, i, 0))],
        out_specs=pl.BlockSpec((None, band, c), lambda l, i: (l, i, 0)),
        out_shape=jax.ShapeDtypeStruct((n_layers, n_rows, c), BF16),
        compiler_params=_cparams("parallel", "parallel"),
        name="round_weight",
    )(w)


def _rope_table_kernel(pos_ref, inv_ref, cos_ref, sin_ref):
    ang = pos_ref[...] * inv_ref[...]
    lane = lax.broadcasted_iota(jnp.int32, ang.shape, 1)
    cos_ref[...] = jnp.cos(ang)
    s = jnp.sin(ang)
    sin_ref[...] = jnp.where(lane < HEAD_DIM // 2, -s, s)


def rope_tables(pos):
    r = pos.shape[0]
    half = HEAD_DIM // 2
    inv = ROPE_THETA ** (-jnp.arange(half, dtype=F32) * (2.0 / HEAD_DIM))
    inv = jnp.concatenate([inv, inv])[None, :]
    posf = jnp.broadcast_to(pos.astype(F32)[:, None], (r, HEAD_DIM))
    tr = min(r, ROW_TILE)
    return pl.pallas_call(
        _rope_table_kernel,
        grid=(r // tr,),
        in_specs=[pl.BlockSpec((tr, HEAD_DIM), lambda i: (i, 0)),
                  pl.BlockSpec((1, HEAD_DIM), lambda i: (0, 0))],
        out_specs=[pl.BlockSpec((tr, HEAD_DIM), lambda i: (i, 0))] * 2,
        out_shape=[jax.ShapeDtypeStruct((r, HEAD_DIM), F32)] * 2,
        compiler_params=_cparams("parallel"),
        name="rope_tables",
    )(posf, inv)


def _proj_kernel(*refs, kinds, tile_src, n_w, transposed, has_extra):
    rope = "rope" in kinds
    refs = list(refs)
    x_ref, g_ref = refs[:2]
    w_refs = refs[2:2 + n_w]
    hg_ref = refs[2 + n_w]
    at = 3 + n_w
    if rope:
        cos_ref, sin_ref = refs[at:at + 2]
        at += 2
    if has_extra:
        xs_ref = refs[at]
        at += 1
        if rope:
            cos_s_ref, sin_s_ref = refs[at:at + 2]
            at += 2
        o_ref, os_ref, xn_ref = refs[at:at + 3]
    else:
        o_ref, xn_ref = refs[at:at + 2]
    tm = x_ref.shape[0]

    def normed(x):
        return (x * _rms_scale(x) * g_ref[...]).astype(BF16)

    def tiles(xn, cos, sin, store):
        for t, kind in enumerate(kinds):
            which, col, width = tile_src[t]
            at_col = sum(w for _, _, w in tile_src[:t])
            if transposed:
                y = lax.dot_general(xn, w_refs[which][col:col + width, :], NT_DIMS, preferred_element_type=F32)
            else:
                y = jnp.dot(xn, w_refs[which][:, col:col + width], preferred_element_type=F32)
            if kind == "plain":
                store(slice(at_col, at_col + width), y)
                continue
            for j in range(width // HEAD_DIM):
                osl = slice(at_col + j * HEAD_DIM, at_col + (j + 1) * HEAD_DIM)
                yh = y[:, j * HEAD_DIM:(j + 1) * HEAD_DIM]
                yh = yh * _rms_scale(yh) * hg_ref[:, osl]
                if kind == "rope":
                    yh = yh * cos + pltpu.roll(yh, HEAD_DIM // 2, axis=1) * sin
                store(osl, yh)

    def store_main(cols, y):
        o_ref[:, cols] = y

    def store_both(cols, y):
        o_ref[:, cols] = y[:tm]
        os_ref[:, cols] = y[tm:]

    xn_ref[:tm, :] = normed(x_ref[...])
    cos = cos_ref[...] if rope else None
    sin = sin_ref[...] if rope else None
    if not has_extra:
        tiles(xn_ref[...], cos, sin, store_main)
        return

    @pl.when(pl.program_id(0) == 0)
    def _():
        xn_ref[tm:, :] = normed(xs_ref[...])
        cos_all = jnp.concatenate([cos, cos_s_ref[...]], axis=0) if rope else None
        sin_all = jnp.concatenate([sin, sin_s_ref[...]], axis=0) if rope else None
        tiles(xn_ref[...], cos_all, sin_all, store_both)

    @pl.when(pl.program_id(0) != 0)
    def _():
        tiles(xn_ref[:tm, :], cos, sin, store_main)


def norm_proj(x, g, ws, layer, head_gain, kinds, cos=None, sin=None, cast=(), tile_src=None, transposed=False,
              extra=None):
    m, d = x.shape
    ws = list(ws) if isinstance(ws, (list, tuple)) else [ws]
    if tile_src is None:
        tile_src = [(0, t * COL_TILE, COL_TILE) for t in range(len(kinds))]
    n_cols = sum(width for _, _, width in tile_src)
    assert len(tile_src) == len(kinds) and head_gain.shape == (1, n_cols)
    assert all(width % HEAD_DIM == 0 for _, _, width in tile_src)
    rope = "rope" in kinds
    has_extra = extra is not None
    ms = extra[0].shape[0] if has_extra else 0
    tm = min(m, ROW_TILE)
    while tm * n_cols * 4 > PROJ_OUT_BLOCK_BYTES and tm % (2 * SUBLANES) == 0:
        tm //= 2
    assert m % tm == 0
    in_specs = [pl.BlockSpec((tm, d), lambda i: (i, 0)),
                pl.BlockSpec((1, d), lambda i: (0, 0))]
    for k, w in enumerate(ws):
        used = max(col + width for which, col, width in tile_src if which == k)
        if transposed:
            assert used <= w.shape[1] and w.shape[2] == d
            in_specs.append(pl.BlockSpec((None, used, d), lambda i: (layer, 0, 0), pipeline_mode=RESIDENT))
            continue
        assert used <= w.shape[2] and (used % LANES == 0 or used == w.shape[2])
        in_specs.append(pl.BlockSpec((None, d, used), lambda i: (layer, 0, 0), pipeline_mode=RESIDENT))
    in_specs.append(pl.BlockSpec((1, n_cols), lambda i: (0, 0)))
    args = [x, g.reshape(1, d), *ws, head_gain]
    if rope:
        assert cos.shape[0] % tm == 0
        tab_blocks = cos.shape[0] // tm
        in_specs += [pl.BlockSpec((tm, HEAD_DIM), lambda i: (i % tab_blocks, 0))] * 2
        args += [cos, sin]
    out_specs = [pl.BlockSpec((tm, n_cols), lambda i: (i, 0))]
    out_shape = [jax.ShapeDtypeStruct((m, n_cols), F32)]
    if has_extra:
        whole = lambda a: pl.BlockSpec(a.shape, lambda i: (0, 0))
        xs, cos_s, sin_s = extra
        extras = [xs] + ([cos_s, sin_s] if rope else [])
        in_specs += [whole(a) for a in extras]
        args += extras
        out_specs.append(pl.BlockSpec((ms, n_cols), lambda i: (0, 0)))
        out_shape.append(jax.ShapeDtypeStruct((ms, n_cols), F32))
    n_out = len(out_specs)
    side = _SideCast(cast, m // tm, lambda i: i)
    outs = pl.pallas_call(
        side.wrap(functools.partial(_proj_kernel, kinds=tuple(kinds), tile_src=tuple(tile_src), n_w=len(ws),
                                    transposed=transposed, has_extra=has_extra),
                  len(args), n_out),
        grid=(m // tm,),
        in_specs=in_specs + side.in_specs,
        out_specs=out_specs + side.out_specs,
        out_shape=out_shape + side.out_shape,
        scratch_shapes=[pltpu.VMEM((tm + ms, d), BF16)],
        compiler_params=_cparams("arbitrary"),
        name="norm_proj",
    )(*args, *side.views)
    casts = [o.reshape(shp) for o, shp in zip(outs[n_out:], side.shapes)]
    return outs[0], (outs[1] if has_extra else None), casts


def _mix_mlp_kernel(*refs, has_extra):
    if has_extra:
        (x_ref, g_ref, wu_ref, wd_ref, a_ref, b_ref, wa_ref, wb_ref, xs_ref, as_ref, bs_ref,
         o_ref, os_ref, xn_ref) = refs
    else:
        x_ref, g_ref, wu_ref, wd_ref, a_ref, b_ref, wa_ref, wb_ref, o_ref, xn_ref = refs
    i, f = pl.program_id(0), pl.program_id(1)
    tm = x_ref.shape[0]

    def normed(x):
        return (x * _rms_scale(x) * g_ref[...]).astype(BF16)

    def mixed(a, b):
        acc = jnp.dot(a, wa_ref[...], preferred_element_type=F32)
        return acc + jnp.dot(b, wb_ref[...], preferred_element_type=F32)

    def ffn(xn):
        h = jnp.dot(xn, wu_ref[...], preferred_element_type=F32)
        h = jnp.maximum(h, 0.0)
        h = (h * h).astype(BF16)
        return jnp.dot(h, wd_ref[...], preferred_element_type=F32)

    def start_main():
        x1 = x_ref[...] + mixed(a_ref[...], b_ref[...])
        xn_ref[:tm, :] = normed(x1)
        o_ref[...] = x1

    def start_both():
        acc = mixed(jnp.concatenate([a_ref[...], as_ref[...]], axis=0),
                    jnp.concatenate([b_ref[...], bs_ref[...]], axis=0))
        x1 = x_ref[...] + acc[:tm]
        xs1 = xs_ref[...] + acc[tm:]
        xn_ref[:tm, :] = normed(x1)
        xn_ref[tm:, :] = normed(xs1)
        o_ref[...] = x1
        os_ref[...] = xs1

    if not has_extra:
        pl.when(f == 0)(start_main)
        o_ref[...] += ffn(xn_ref[...])
        return

    @pl.when(i == 0)
    def _():
        pl.when(f == 0)(start_both)
        upd = ffn(xn_ref[...])
        o_ref[...] += upd[:tm]
        os_ref[...] += upd[tm:]

    @pl.when(i != 0)
    def _():
        pl.when(f == 0)(start_main)
        o_ref[...] += ffn(xn_ref[:tm, :])


def mix_mlp(x, a, b, w_out, layer, g, wu, wd, tf=1024, cast=(), extra=None):
    m, d = x.shape
    ff = wu.shape[1]
    tm = min(m, ROW_TILE)
    nf = ff // tf
    ka, kb = a.shape[1], b.shape[1]
    assert w_out.shape[1] == ka + kb and ka % kb == 0
    has_extra = extra is not None
    side = _SideCast(cast, (m // tm) * nf, lambda i, j: i * nf + j)
    in_specs = [pl.BlockSpec((tm, d), lambda i, j: (i, 0)),
                pl.BlockSpec((1, d), lambda i, j: (0, 0)),
                pl.BlockSpec((d, tf), lambda i, j: (0, j)),
                pl.BlockSpec((tf, d), lambda i, j: (j, 0)),
                pl.BlockSpec((tm, ka), lambda i, j: (i, 0)),
                pl.BlockSpec((tm, kb), lambda i, j: (i, 0)),
                pl.BlockSpec((None, ka, d), lambda i, j: (layer, 0, 0), pipeline_mode=RESIDENT),
                pl.BlockSpec((None, kb, d), lambda i, j: (layer, ka // kb, 0), pipeline_mode=RESIDENT)]
    out_specs = [pl.BlockSpec((tm, d), lambda i, j: (i, 0))]
    out_shape = [jax.ShapeDtypeStruct((m, d), F32)]
    args = [x, g.reshape(1, d), wu, wd, a, b, w_out, w_out]
    ms = extra[0].shape[0] if has_extra else 0
    if has_extra:
        in_specs += [pl.BlockSpec(e.shape, lambda i, j: (0, 0)) for e in extra]
        args += list(extra)
        out_specs.append(pl.BlockSpec((ms, d), lambda i, j: (0, 0)))
        out_shape.append(jax.ShapeDtypeStruct((ms, d), F32))
    n_out = len(out_specs)
    outs = pl.pallas_call(
        side.wrap(functools.partial(_mix_mlp_kernel, has_extra=has_extra), len(args), n_out),
        grid=(m // tm, nf),
        in_specs=in_specs + side.in_specs,
        out_specs=out_specs + side.out_specs,
        out_shape=out_shape + side.out_shape,
        scratch_shapes=[pltpu.VMEM((tm + ms, d), BF16)],
        compiler_params=pltpu.CompilerParams(dimension_semantics=("arbitrary", "arbitrary"),
                                             vmem_limit_bytes=MIX_MLP_VMEM_LIMIT),
        name="mix_mlp",
    )(*args, *side.views)
    casts = [o.reshape(shp) for o, shp in zip(outs[n_out:], side.shapes)]
    return outs[0], (outs[1] if has_extra else None), casts


def _split3(x):
    hi = x.astype(BF16)
    r1 = x - hi.astype(F32)
    mid = r1.astype(BF16)
    lo = (r1 - mid.astype(F32)).astype(BF16)
    return hi, mid, lo


def _ssd_kernel(xbc_ref, dt_ref, z_ref, cw_ref, cb_ref, dtb_ref, alog_ref, dsk_ref, gg_ref, tri_ref, expand_ref,
                conv0_ref, h0_ref, y_ref, convo_ref, ho_ref,
                ext_ref, ht_ref, yd_ref, st_ref, *, t_real, n_heads, d_inner):
    L = SSD_CHUNK
    P = SSM_HEAD_DIM
    N = SSM_D_STATE
    G = SSM_GROUPS
    hpg = n_heads // G
    gw = hpg * P
    hist = CONV_W - 1
    base = SUBLANES - hist
    c = pl.program_id(1)
    nc = pl.num_programs(1)

    @pl.when(c == 0)
    def _():
        ext_ref[base:SUBLANES, :] = conv0_ref[...]
        for g in range(G):
            ht_ref[g] = h0_ref[g].T

    ext_ref[SUBLANES:SUBLANES + L, :] = xbc_ref[...]
    ext = ext_ref[...]
    acc = cb_ref[...] + ext[SUBLANES:, :] * cw_ref[hist:CONV_W, :]
    for k in range(1, CONV_W):
        acc = acc + pltpu.roll(ext, k, axis=0)[SUBLANES:, :] * cw_ref[hist - k:CONV_W - k, :]
    xc = acc * _sigmoid(acc)

    @pl.when(c == nc - 1)
    def _():
        n_in_last = t_real - ((t_real - 1) // L) * L
        convo_ref[...] = ext_ref[base + n_in_last:base + n_in_last + hist, :]

    ext_ref[base:SUBLANES, :] = ext_ref[base + L:SUBLANES + L, :]

    dt_raw = dt_ref[...] + dtb_ref[...]
    dt = jnp.maximum(dt_raw, 0.0) + jnp.log1p(jnp.exp(-jnp.abs(dt_raw)))
    row = lax.broadcasted_iota(jnp.int32, (L, LANES), 0) + c * L
    dt = jnp.where(row < t_real, dt, 0.0)
    da = dt * (-jnp.exp(alog_ref[...]))
    li = lax.broadcasted_iota(jnp.int32, (L, L), 0)
    si = lax.broadcasted_iota(jnp.int32, (L, L), 1)
    causal = li >= si
    a_cs = sum(jnp.dot(tri_ref[...], part, preferred_element_type=F32) for part in _split3(da))
    a_cs_t = a_cs.T
    dt_t = dt.T
    w_t = dt_t * jnp.exp(a_cs_t[:, L - 1:L] - a_cs_t)
    e_exp = sum(jnp.dot(part, expand_ref[...], preferred_element_type=F32) for part in _split3(jnp.exp(a_cs)))
    xb = xc[:, :d_inner].astype(BF16)
    low_half = lax.broadcasted_iota(jnp.int32, (L, 2 * P), 1) < P

    for g in range(G):
        b_g = xc[:, d_inner + g * N:d_inner + (g + 1) * N]
        c_g = xc[:, d_inner + G * N + g * N:d_inner + G * N + (g + 1) * N].astype(BF16)
        cbm = lax.dot_general(c_g, b_g.astype(BF16), NT_DIMS, preferred_element_type=F32)
        b_t = b_g.T
        y_off = jnp.dot(c_g, ht_ref[g].astype(BF16), preferred_element_type=F32)
        for kp in range(hpg // 2):
            h0 = g * hpg + 2 * kp
            x_pair = xb[:, h0 * P:(h0 + 2) * P]
            yd, st = [], []
            for h in (h0, h0 + 1):
                seg = a_cs[:, h:h + 1] - a_cs_t[h:h + 1, :]
                dec = jnp.exp(jnp.where(causal, seg, -jnp.inf))
                mh = (cbm * dec * dt_t[h:h + 1, :]).astype(BF16)
                yd.append(jnp.dot(mh, x_pair, preferred_element_type=F32))
                btw = (b_t * w_t[h:h + 1, :]).astype(BF16)
                st.append(jnp.dot(btw, x_pair, preferred_element_type=F32))
            yd_ref[:, 2 * kp * P:(2 * kp + 2) * P] = jnp.where(low_half, yd[0], yd[1])
            st_ref[:, 2 * kp * P:(2 * kp + 2) * P] = jnp.where(low_half, st[0], st[1])
        gs = slice(g * gw, (g + 1) * gw)
        e_g = e_exp[:, gs]
        y_g = yd_ref[...] + y_off * e_g + dsk_ref[:, gs] * xc[:, gs]
        ht_ref[g] = ht_ref[g] * e_g[L - 1:L, :] + st_ref[...]
        z_g = z_ref[:, gs]
        gated = y_g * (z_g * _sigmoid(z_g))
        y_ref[:, gs] = (gated * _rms_scale(gated) * gg_ref[:, gs]).astype(y_ref.dtype)

    @pl.when(c == nc - 1)
    def _():
        for g in range(G):
            ho_ref[g] = ht_ref[g].T


def ssd_mixer(proj, col_xbc, col_dt, col_z, t_real, conv_w, conv_b, dt_bias, a_log, d_skip, gate_g,
              conv_prev, h_prev, cast=()):
    bt, tp, _ = proj.shape
    conv_dim = conv_w.shape[1]
    n_heads = dt_bias.shape[0]
    d_inner = n_heads * SSM_HEAD_DIM
    gw = d_inner // SSM_GROUPS
    L = SSD_CHUNK
    assert tp % L == 0 and (tp - t_real) < L
    assert col_xbc % conv_dim == 0 and col_dt % LANES == 0 and col_z % d_inner == 0
    pad = LANES - n_heads
    dtb = jnp.pad(dt_bias, (0, pad)).reshape(1, LANES)
    alog = jnp.pad(a_log, (0, pad)).reshape(1, LANES)
    dsk = jnp.repeat(d_skip, SSM_HEAD_DIM).reshape(1, d_inner)
    tri = jnp.tril(jnp.ones((L, L), BF16))
    expand = jnp.repeat(jnp.eye(LANES, dtype=BF16)[:, :n_heads], SSM_HEAD_DIM, axis=1)
    kern = functools.partial(_ssd_kernel, t_real=t_real, n_heads=n_heads, d_inner=d_inner)
    vec = lambda n: pl.BlockSpec((1, n), lambda b, c: (0, 0))
    nc = tp // L
    side = _SideCast(cast, bt * nc, lambda b, c: b * nc + c)
    outs = pl.pallas_call(
        side.wrap(kern, 13, 3),
        grid=(bt, nc),
        in_specs=[pl.BlockSpec((None, L, conv_dim), lambda b, c: (b, c, col_xbc // conv_dim)),
                  pl.BlockSpec((None, L, LANES), lambda b, c: (b, c, col_dt // LANES)),
                  pl.BlockSpec((None, L, d_inner), lambda b, c: (b, c, col_z // d_inner)),
                  pl.BlockSpec((CONV_W, conv_dim), lambda b, c: (0, 0)),
                  vec(conv_dim), vec(LANES), vec(LANES), vec(d_inner), vec(d_inner),
                  pl.BlockSpec((L, L), lambda b, c: (0, 0)),
                  pl.BlockSpec((LANES, d_inner), lambda b, c: (0, 0)),
                  pl.BlockSpec((None, CONV_W - 1, conv_dim), lambda b, c: (b, 0, 0)),
                  pl.BlockSpec((None, SSM_GROUPS, gw, SSM_D_STATE), lambda b, c: (b, 0, 0, 0))] + side.in_specs,
        out_specs=[pl.BlockSpec((None, L, d_inner), lambda b, c: (b, c, 0)),
                   pl.BlockSpec((None, CONV_W - 1, conv_dim), lambda b, c: (b, 0, 0)),
                   pl.BlockSpec((None, SSM_GROUPS, gw, SSM_D_STATE), lambda b, c: (b, 0, 0, 0))] + side.out_specs,
        out_shape=[jax.ShapeDtypeStruct((bt, tp, d_inner), BF16),
                   jax.ShapeDtypeStruct((bt, CONV_W - 1, conv_dim), F32),
                   jax.ShapeDtypeStruct((bt, SSM_GROUPS, gw, SSM_D_STATE), F32)] + side.out_shape,
        scratch_shapes=[pltpu.VMEM((SUBLANES + L, conv_dim), F32),
                        pltpu.VMEM((SSM_GROUPS, SSM_D_STATE, gw), F32),
                        pltpu.VMEM((L, gw), F32),
                        pltpu.VMEM((SSM_D_STATE, gw), F32)],
        compiler_params=_cparams("parallel", "arbitrary"),
        name="ssd_mixer",
    )(proj, proj, proj, conv_w, conv_b.reshape(1, conv_dim), dtb, alog, dsk, gate_g.reshape(1, d_inner),
      tri, expand, conv_prev, h_prev, *side.views)
    return side.split(outs, 3)


def _mem_attn_kernel(q_ref, k_ref, v_ref, o_ref):
    scale = HEAD_DIM ** -0.5
    headed = len(k_ref.shape) == 3
    for h in range(MEM_HEADS):
        sl = slice(h * HEAD_DIM, (h + 1) * HEAD_DIM)
        q = (q_ref[:, sl] * (scale * LOG2_E)).astype(BF16)
        k = (k_ref[:, h, :] if headed else k_ref[:, sl]).astype(BF16)
        v = (v_ref[:, h, :] if headed else v_ref[:, sl]).astype(BF16)
        s = lax.dot_general(q, k, NT_DIMS, preferred_element_type=F32)
        e = jnp.exp2(s - jnp.max(s, axis=-1, keepdims=True))
        den = jnp.sum(e, axis=-1, keepdims=True)
        o = jnp.dot(e.astype(BF16), v, preferred_element_type=F32) / den
        o_ref[:, sl] = o.astype(o_ref.dtype)


def mem_attention(proj, col_q, k, v, kv_at):
    bt, t, _ = proj.shape
    width = MEM_HEADS * HEAD_DIM
    tq = min(t, ROW_TILE)
    if k.ndim == 5:
        mlen = k.shape[2]
        kv_specs = [pl.BlockSpec((None, None, mlen, MEM_HEADS, HEAD_DIM), lambda b, i: (kv_at, b, 0, 0, 0))] * 2
    else:
        mlen = k.shape[1]
        kv_specs = [pl.BlockSpec((None, mlen, width), lambda b, i, c=c: (b, 0, c // width)) for c in kv_at]
    return pl.pallas_call(
        _mem_attn_kernel,
        grid=(bt, t // tq),
        in_specs=[pl.BlockSpec((None, tq, width), lambda b, i: (b, i, col_q // width))] + kv_specs,
        out_specs=pl.BlockSpec((None, tq, width), lambda b, i: (b, i, 0)),
        out_shape=jax.ShapeDtypeStruct((bt, t, width), BF16),
        compiler_params=_cparams("parallel", "parallel"),
        name="mem_attention",
    )(proj, k, v)


def _run_units(first, count, unit, unroll):
    trips = count // unroll
    if trips == 1:
        trips = 0
    if trips:
        def trip(it, carry):
            for k in range(unroll):
                unit(first + it * unroll + k)
            return carry
        lax.fori_loop(0, trips, trip, 0)
    for k in range(trips * unroll, count):
        unit(first + k)


def _dil_attn_kernel(*refs, groups, rb, unroll):
    ng = len(groups)
    q_refs = refs[:ng]
    kp_ref, kc_ref, vp_ref, vc_ref, o_ref = refs[ng:ng + 5]
    og = refs[ng + 5:2 * ng + 5]
    lg = refs[2 * ng + 5:3 * ng + 5]
    i = pl.program_id(1)
    scale = HEAD_DIM ** -0.5

    for gi, (win, d) in enumerate(groups):
        w = win // d
        dw = d * w
        qi = lax.broadcasted_iota(jnp.int32, (w, 2 * w), 0)
        kj = lax.broadcasted_iota(jnp.int32, (w, 2 * w), 1)
        band = (kj >= qi) & (kj <= qi + w)
        band_first = band & (kj >= jnp.where(i > 0, 0, w))

        def rows(start, size, d=d, w=w):
            if d == 1:
                return pl.ds(pl.multiple_of(start, w), size)
            return pl.ds(start, size, stride=d)

        def attend(rows_q, kk, vv, mask, gi=gi, w=w):
            q = (q_refs[gi][rows_q, :] * (scale * LOG2_E)).astype(BF16)
            s = lax.dot_general(q, kk.astype(BF16), NT_DIMS, preferred_element_type=F32)
            s = jnp.where(mask, s, -jnp.inf)
            m = jnp.max(s, axis=-1, keepdims=True)
            e = jnp.exp2(s - m)
            den = jnp.sum(e, axis=-1, keepdims=True)
            o = jnp.dot(e.astype(BF16), vv.astype(BF16), preferred_element_type=F32) / den
            og[gi][rows_q, :] = o
            lg[gi][rows_q, :] = jnp.broadcast_to(m * LN_2 + jnp.log(den), (w, HEAD_DIM))

        def first_unit(r, rows=rows, attend=attend, w=w, dw=dw, mask=band_first):
            rows_q, rows_p = rows(r, w), rows(rb - dw + r, w)
            kk = jnp.concatenate([kp_ref[rows_p, :], kc_ref[rows_q, :]], axis=0)
            vv = jnp.concatenate([vp_ref[rows_p, :], vc_ref[rows_q, :]], axis=0)
            attend(rows_q, kk, vv, mask)

        def later_unit(u, rows=rows, attend=attend, d=d, w=w, dw=dw, mask=band):
            sb = u // d
            start_q = sb * dw + (u - sb * d)
            rows_k = rows(start_q - dw, 2 * w)
            attend(rows(start_q, w), kc_ref[rows_k, :], vc_ref[rows_k, :], mask)

        _run_units(0, d, first_unit, unroll)
        _run_units(d, rb // w - d, later_unit, unroll)

    ls = [r[...] for r in lg]
    mm = functools.reduce(jnp.maximum, ls)
    ws = [jnp.exp(l - mm) for l in ls]
    num = sum(wg * r[...] for wg, r in zip(ws, og))
    o_ref[...] = (num / sum(ws)).astype(o_ref.dtype)


def dilated_attention(proj, kv):
    bt, t, _ = proj.shape
    ng = len(DIL_GROUPS)
    rb = min(t, max(win for win, _ in DIL_GROUPS))
    for win, d in DIL_GROUPS:
        assert win % d == 0 and rb % win == 0
    assert t % rb == 0
    blk = lambda f: pl.BlockSpec((None, rb, HEAD_DIM), f)
    prev = lambda i: jnp.maximum(i - 1, 0)
    q_specs = [blk(lambda b, i, h, g=g: (b, i, g * DIL_HEADS + h)) for g in range(ng)]
    kv_specs = [blk(lambda b, i, h: (b, prev(i), h)), blk(lambda b, i, h: (b, i, h)),
                blk(lambda b, i, h: (b, prev(i), DIL_HEADS + h)), blk(lambda b, i, h: (b, i, DIL_HEADS + h))]
    return pl.pallas_call(
        functools.partial(_dil_attn_kernel, groups=DIL_GROUPS, rb=rb, unroll=DIL_UNROLL),
        grid=(bt, t // rb, DIL_HEADS),
        in_specs=q_specs + kv_specs,
        out_specs=blk(lambda b, i, h: (b, i, h)),
        out_shape=jax.ShapeDtypeStruct((bt, t, DIL_HEADS * HEAD_DIM), BF16),
        scratch_shapes=[pltpu.VMEM((rb, HEAD_DIM), F32)] * (2 * ng),
        compiler_params=_cparams("parallel", "parallel", "parallel"),
        name="dilated_attention",
    )(*([proj] * ng), kv, kv, kv, kv)


def _dil_decode_kernel(*refs):
    ng = len(DIL_GROUPS)
    q_ref, kv_ref = refs[:2]
    k_refs, v_refs = refs[2:2 + ng], refs[2 + ng:2 + 2 * ng]
    o_ref = refs[-1]
    width = DIL_HEADS * HEAD_DIM
    scale = HEAD_DIM ** -0.5
    for h in range(DIL_HEADS):
        sl = slice(h * HEAD_DIM, (h + 1) * HEAD_DIM)
        k_new = kv_ref[:, h * HEAD_DIM:(h + 1) * HEAD_DIM]
        v_new = kv_ref[:, width + h * HEAD_DIM:width + (h + 1) * HEAD_DIM]
        outs, lses = [], []
        for gi in range(ng):
            kc = k_refs[gi][:, h, :].astype(BF16)
            vc = v_refs[gi][:, h, :].astype(BF16)
            q = q_ref[:, gi * width + h * HEAD_DIM:gi * width + (h + 1) * HEAD_DIM]
            s = lax.dot_general(q.astype(BF16), kc, NT_DIMS, preferred_element_type=F32) * scale
            s_self = jnp.sum(q * k_new, axis=-1, keepdims=True) * scale
            m = jnp.maximum(jnp.max(s, axis=-1, keepdims=True), s_self)
            e = jnp.exp(s - m)
            e_self = jnp.exp(s_self - m)
            den = jnp.sum(e, axis=-1, keepdims=True) + e_self
            o = jnp.dot(e.astype(BF16), vc, preferred_element_type=F32) + e_self * v_new
            outs.append(o / den)
            lses.append(m + jnp.log(den))
        mm = functools.reduce(jnp.maximum, lses)
        ws = [jnp.exp(l - mm) for l in lses]
        num = sum(wg * o for wg, o in zip(ws, outs))
        o_ref[:, sl] = (num / sum(ws)).astype(o_ref.dtype)


def dilated_decode(proj, kv_new, k_cache, v_cache):
    bt, rows, nq = proj.shape
    lc = k_cache.shape[1]
    width = DIL_HEADS * HEAD_DIM
    views, specs = [], []
    for cache in (k_cache, v_cache):
        for win, dil in DIL_GROUPS:
            assert lc % win == 0 and win % dil == 0
            w = win // dil
            views.append(cache.reshape(bt, lc // dil, dil, DIL_HEADS, HEAD_DIM))
            specs.append(pl.BlockSpec((None, w, None, DIL_HEADS, HEAD_DIM),
                                      lambda b, last=lc // win - 1: (b, last, 0, 0, 0)))
    return pl.pallas_call(
        _dil_decode_kernel,
        grid=(bt,),
        in_specs=[pl.BlockSpec((None, rows, nq), lambda b: (b, 0, 0)),
                  pl.BlockSpec((None, rows, 2 * width), lambda b: (b, 0, 0))] + specs,
        out_specs=pl.BlockSpec((None, rows, width), lambda b: (b, 0, 0)),
        out_shape=jax.ShapeDtypeStruct((bt, rows, width), BF16),
        compiler_params=_cparams("parallel"),
        name="dilated_decode",
    )(proj, kv_new, *views)


def kernel(x_prompt, x_sample, state_conv, state_ssm, cache_win_k, cache_win_v, cache_mem_k, cache_mem_v,
           mem_prompt, norm_mix_g, norm_mlp_g, norm_mem_g, w_mem_k, w_mem_v, mem_q_norm_g, mem_k_norm_g,
           w_up, w_down, w_in_a, conv_w, conv_b, dt_bias, a_log, d_skip, gate_norm_g, w_out_a,
           w_in_b, q_norm_g, w_out_b, kv_norm_g, w_k_shared, w_v_shared, k_norm_g):
    depth = w_up.shape[0]
    n_a = w_in_a.shape[0]
    d_model = x_prompt.shape[-1]
    n_heads = dt_bias.shape[1]
    d_inner = n_heads * SSM_HEAD_DIM
    conv_dim = conv_w.shape[2]
    mem_w = MEM_HEADS * HEAD_DIM
    kv_w = DIL_HEADS * HEAD_DIM
    dil_q_w = len(DIL_GROUPS) * kv_w

    s1, s2, s3 = d_inner, d_inner + conv_dim, d_inner + conv_dim + n_heads
    assert s1 % COL_TILE == 0 and s2 % COL_TILE == 0
    w_in_a_t = jnp.swapaxes(w_in_a, 1, 2)
    w_a_main = round_weight(w_in_a_t, s2)
    tail = lax.optimization_barrier(w_in_a_t[:, s2:]).astype(BF16)
    w_a_tail = jnp.concatenate([tail[:, :n_heads], jnp.zeros((n_a, COL_TILE - n_heads, d_model), BF16),
                                tail[:, n_heads:]], axis=1)
    tiles_a = ([(0, s1 + c, COL_TILE) for c in range(0, conv_dim, COL_TILE)]
               + [(1, COL_TILE + c, COL_TILE) for c in range(0, mem_w, COL_TILE)]
               + [(0, c, COL_TILE) for c in range(0, d_inner, COL_TILE)] + [(1, 0, LANES)])
    kinds_a = (["plain"] * (conv_dim // COL_TILE) + ["norm"] * (mem_w // COL_TILE)
               + ["plain"] * (d_inner // COL_TILE + 1))
    col_xbc, col_qm_a, col_z = 0, conv_dim, conv_dim + mem_w
    col_dt = col_z + d_inner
    a_cols = col_dt + LANES
    kinds_b = ["rope"] * (dil_q_w // COL_TILE) + ["norm"] * (mem_w // COL_TILE)
    kinds_kv = ["rope"] * (kv_w // COL_TILE) + ["plain"] * (kv_w // COL_TILE)
    kinds_mkv = ["norm"] * (mem_w // COL_TILE) + ["plain"] * (mem_w // COL_TILE)
    w_b = w_in_b.astype(BF16)
    w_kv = jnp.concatenate([w_k_shared, w_v_shared], axis=-1).astype(BF16)[None]
    w_mkv = jnp.concatenate([w_mem_k, w_mem_v], axis=-1).astype(BF16)
    w_out_a_b, w_out_b_b = w_out_a.astype(BF16), w_out_b.astype(BF16)
    assert n_a >= 1
    w_up_b, w_down_b = [None] * depth, [None] * depth

    def gain_row(width, pieces):
        parts, at = [], 0
        for start, g, reps in pieces:
            parts += [jnp.ones((start - at,), F32), jnp.tile(g, reps)]
            at = start + reps * HEAD_DIM
        parts.append(jnp.ones((width - at,), F32))
        return jnp.concatenate(parts).reshape(1, width)

    class Group:
        def __init__(self, x, pos_rows, conv_prev, ssm_prev, mem_kv, k_past, v_past, t_real):
            self.bt, self.t, _ = x.shape
            self.m = self.bt * self.t
            self.x2 = x.reshape(self.m, d_model)
            self.cos, self.sin = rope_tables(pos_rows)
            self.conv_prev, self.ssm_prev, self.mem_kv = conv_prev, ssm_prev, mem_kv
            self.k_past, self.v_past, self.t_real = k_past, v_past, t_real
            self.conv_new, self.ssm_new, self.kv3 = [], [], None

    def in_proj_a(prompt, sample, l):
        hg = gain_row(a_cols, [(col_qm_a, mem_q_norm_g[l], MEM_HEADS)])
        proj_p, proj_s, rounded = norm_proj(prompt.x2, norm_mix_g[l], [w_a_main, w_a_tail], l, hg, kinds_a,
                                            tile_src=tiles_a, transposed=True, extra=(sample.x2, None, None),
                                            cast=((w_up, 0),) if l == 0 else ())
        if l == 0:
            (w_up_b[0],) = rounded
        return proj_p, proj_s

    def mix_a(gr, proj, l, round_mlp_weights):
        bt, t, m = gr.bt, gr.t, gr.m
        proj3 = proj.reshape(bt, t, -1)
        tp = -(-t // SSD_CHUNK) * SSD_CHUNK
        proj_ssd = proj3 if tp == t else jnp.pad(proj3, ((0, 0), (0, tp - t), (0, 0)))
        ssd = ssd_mixer(proj_ssd, col_xbc, col_dt, col_z, gr.t_real, conv_w[l], conv_b[l],
                        dt_bias[l], a_log[l], d_skip[l], gate_norm_g[l],
                        gr.conv_prev[l], gr.ssm_prev[l].reshape(bt, SSM_GROUPS, -1, SSM_D_STATE),
                        cast=((w_down, 0),) if round_mlp_weights else ())
        if round_mlp_weights:
            ssd, (w_down_b[0],) = ssd
        y, c_new, h_new = ssd
        gr.conv_new.append(c_new)
        gr.ssm_new.append(h_new.reshape(bt, n_heads, SSM_HEAD_DIM, SSM_D_STATE))
        mo = mem_attention(proj3, col_qm_a, *gr.mem_kv(l)).reshape(m, mem_w)
        return y[:, :t].reshape(m, d_inner), mo

    def shared_kv(prompt, sample):
        hg = gain_row(2 * kv_w, [(0, k_norm_g, DIL_HEADS)])
        kv_p, kv_s, _ = norm_proj(prompt.x2, kv_norm_g, w_kv, 0, hg, kinds_kv, prompt.cos, prompt.sin,
                                  extra=(sample.x2, sample.cos, sample.sin))
        prompt.kv3 = kv_p.reshape(prompt.bt, prompt.t, 2 * kv_w)
        sample.kv3 = kv_s.reshape(sample.bt, sample.t, 2 * kv_w)

    def in_proj_b(prompt, sample, l):
        j = l - n_a
        hg = gain_row(dil_q_w + mem_w, [(0, q_norm_g[j], dil_q_w // HEAD_DIM),
                                        (dil_q_w, mem_q_norm_g[l], MEM_HEADS)])
        proj_p, proj_s, _ = norm_proj(prompt.x2, norm_mix_g[l], w_b, j, hg, kinds_b, prompt.cos, prompt.sin,
                                      extra=(sample.x2, sample.cos, sample.sin))
        return proj_p, proj_s

    def mix_b(gr, proj, l):
        proj3 = proj.reshape(gr.bt, gr.t, -1)
        if gr.k_past is None:
            att = dilated_attention(proj3, gr.kv3)
        else:
            att = dilated_decode(proj3, gr.kv3, gr.k_past, gr.v_past)
        mo = mem_attention(proj3, dil_q_w, *gr.mem_kv(l)).reshape(gr.m, mem_w)
        return att.reshape(gr.m, kv_w), mo

    bp, t_p, _ = x_prompt.shape
    mlen = mem_prompt.shape[1]
    mem2 = mem_prompt.reshape(bp * mlen, d_model)
    mkv_p = []
    for l in range(depth):
        hg = gain_row(2 * mem_w, [(0, mem_k_norm_g[l], MEM_HEADS)])
        mkv_p.append(norm_proj(mem2, norm_mem_g[l], w_mkv, l, hg, kinds_mkv)[0].reshape(bp, mlen, 2 * mem_w))
    conv0 = jnp.zeros((n_a, bp, CONV_W - 1, conv_dim), F32)
    ssm0 = jnp.zeros((n_a, bp, n_heads, SSM_HEAD_DIM, SSM_D_STATE), F32)
    prompt = Group(x_prompt, jnp.arange(t_p, dtype=jnp.int32), conv0, ssm0,
                   lambda l: (mkv_p[l], mkv_p[l], (0, mem_w)), None, None, t_p)

    bs, t_s, _ = x_sample.shape
    assert t_s == 1
    xs = jnp.pad(x_sample, ((0, 0), (0, SAMPLE_ROWS - t_s), (0, 0)))
    pos_s = jnp.full((bs * SAMPLE_ROWS,), PAST_LEN, jnp.int32)
    sample = Group(xs, pos_s, state_conv, state_ssm, lambda l: (cache_mem_k, cache_mem_v, l),
                   cache_win_k, cache_win_v, t_s)

    for l in range(depth):
        if l < n_a:
            proj_p, proj_s = in_proj_a(prompt, sample, l)
            a_p, b_p = mix_a(prompt, proj_p, l, round_mlp_weights=(l == 0))
            a_s, b_s = mix_a(sample, proj_s, l, round_mlp_weights=False)
            w_out, lw = w_out_a_b, l
        else:
            if l == n_a:
                shared_kv(prompt, sample)
            proj_p, proj_s = in_proj_b(prompt, sample, l)
            a_p, b_p = mix_b(prompt, proj_p, l)
            a_s, b_s = mix_b(sample, proj_s, l)
            w_out, lw = w_out_b_b, l - n_a
        nxt = ((w_up, l + 1), (w_down, l + 1)) if l + 1 < depth else ()
        prompt.x2, sample.x2, rounded = mix_mlp(prompt.x2, a_p, b_p, w_out, lw, norm_mlp_g[l], w_up_b[l], w_down_b[l],
                                                cast=nxt, extra=(sample.x2, a_s, b_s))
        if nxt:
            w_up_b[l + 1], w_down_b[l + 1] = rounded

    y_p = prompt.x2.reshape(bp, t_p, d_model)
    conv_p, ssm_p, kv_p = jnp.stack(prompt.conv_new), jnp.stack(prompt.ssm_new), prompt.kv3
    mkv_all = jnp.stack(mkv_p)
    mem_k_p = mkv_all[..., :mem_w].reshape(depth, bp, mlen, MEM_HEADS, HEAD_DIM)
    mem_v_p = mkv_all[..., mem_w:].reshape(depth, bp, mlen, MEM_HEADS, HEAD_DIM)
    keep = min(max(w for w, _ in DIL_GROUPS), t_p)
    win_k_p = kv_p[:, t_p - keep:, :kv_w].reshape(bp, keep, DIL_HEADS, HEAD_DIM)
    win_v_p = kv_p[:, t_p - keep:, kv_w:].reshape(bp, keep, DIL_HEADS, HEAD_DIM)

    y_s = sample.x2.reshape(bs, SAMPLE_ROWS, d_model)
    conv_s, ssm_s, kv_s = jnp.stack(sample.conv_new), jnp.stack(sample.ssm_new), sample.kv3
    y_s = y_s[:, :t_s]
    k_s = kv_s[:, :t_s, :kv_w].reshape(bs, t_s, DIL_HEADS, HEAD_DIM)
    v_s = kv_s[:, :t_s, kv_w:].reshape(bs, t_s, DIL_HEADS, HEAD_DIM)

    return (y_p, y_s, conv_p, ssm_p, win_k_p, win_v_p, mem_k_p, mem_v_p, conv_s, ssm_s, k_s, v_s)
```

```python
import functools

import jax
import jax.numpy as jnp
from jax import lax
from jax.experimental import pallas as pl
from jax.experimental.pallas import tpu as pltpu

F32 = jnp.float32
BF16 = jnp.bfloat16

HEAD_DIM = 128
SSM_HEAD_DIM = 64
SSM_GROUPS = 4
SSM_D_STATE = 128
CONV_W = 4
SSD_CHUNK = 128
MEM_HEADS = 4
DIL_GROUPS = ((128, 1), (512, 4), (2048, 16))
DIL_HEADS = 4
PAST_LEN = 16384
ROPE_THETA = 10000.0
EPS = 1e-6
LOG2_E = 1.4426950408889634
LN_2 = 0.6931471805599453

LANES = 128
SUBLANES = 8
VMEM_LIMIT = 48 * 1024 * 1024
MIX_MLP_VMEM_LIMIT = 56 * 1024 * 1024
COL_TILE = 512
ROW_TILE = 512
PROJ_OUT_BLOCK_BYTES = 5 * 1024 * 1024
SAMPLE_ROWS = SUBLANES
DIL_UNROLL = 16

NT_DIMS = (((1,), (1,)), ((), ()))
RESIDENT = pl.Buffered(1)


def _cparams(*sem):
    return pltpu.CompilerParams(dimension_semantics=sem, vmem_limit_bytes=VMEM_LIMIT)


def _sigmoid(x):
    return 1.0 / (1.0 + jnp.exp(-x))


def _rms_scale(x):
    return lax.rsqrt(jnp.mean(x * x, axis=-1, keepdims=True) + EPS)


class _SideCast:
    def __init__(self, mats, steps, step_of):
        self.n = len(mats)
        self.shapes = [a.shape[1:] for a, _ in mats]
        self.views, self.in_specs, self.out_specs, self.out_shape = [], [], [], []
        for a, layer in mats:
            n_layers, r, c = a.shape
            assert r % (steps * 2 * SUBLANES) == 0
            band = r // steps
            self.views.append(a.reshape(n_layers, steps, band, c))
            self.in_specs.append(pl.BlockSpec((None, None, band, c),
                                              lambda *ids, layer=layer: (layer, step_of(*ids), 0, 0)))
            self.out_specs.append(pl.BlockSpec((None, band, c), lambda *ids: (step_of(*ids), 0, 0)))
            self.out_shape.append(jax.ShapeDtypeStruct((steps, band, c), BF16))

    def wrap(self, body, n_in, n_out):
        n = self.n
        if not n:
            return body

        def kern(*refs):
            ins, srcs = refs[:n_in], refs[n_in:n_in + n]
            outs, dsts = refs[n_in + n:n_in + n + n_out], refs[n_in + n + n_out:n_in + 2 * n + n_out]
            for s, d in zip(srcs, dsts):
                d[...] = s[...].astype(BF16)
            body(*ins, *outs, *refs[n_in + 2 * n + n_out:])
        return kern

    def split(self, outs, n_out):
        main = outs[0] if n_out == 1 else tuple(outs[:n_out])
        if not self.n:
            return main
        return main, [o.reshape(s) for o, s in zip(outs[n_out:], self.shapes)]


def _round_kernel(src_ref, dst_ref):
    dst_ref[...] = src_ref[...].astype(BF16)


def round_weight(w, n_rows):
    n_layers, _, c = w.shape
    band = ROW_TILE
    assert n_rows % band == 0
    return pl.pallas_call(
        _round_kernel,
        grid=(n_layers, n_rows // band),
        in_specs=[pl.BlockSpec((None, band, c), lambda l, i: (l, i, 0))],
        out_specs=pl.BlockSpec((None, band, c), lambda l, i: (l, i, 0)),
        out_shape=jax.ShapeDtypeStruct((n_layers, n_rows, c), BF16),
        compiler_params=_cparams("parallel", "parallel"),
        name="round_weight",
    )(w)


def _rope_table_kernel(pos_ref, inv_ref, cos_ref, sin_ref):
    ang = pos_ref[...] * inv_ref[...]
    lane = lax.broadcasted_iota(jnp.int32, ang.shape, 1)
    cos_ref[...] = jnp.cos(ang)
    s = jnp.sin(ang)
    sin_ref[...] = jnp.where(lane < HEAD_DIM // 2, -s, s)


def rope_tables(pos):
    r = pos.shape[0]
    half = HEAD_DIM // 2
    inv = ROPE_THETA ** (-jnp.arange(half, dtype=F32) * (2.0 / HEAD_DIM))
    inv = jnp.concatenate([inv, inv])[None, :]
    posf = jnp.broadcast_to(pos.astype(F32)[:, None], (r, HEAD_DIM))
    tr = min(r, ROW_TILE)
    return pl.pallas_call(
        _rope_table_kernel,
        grid=(r // tr,),
        in_specs=[pl.BlockSpec((tr, HEAD_DIM), lambda i: (i, 0)),
                  pl.BlockSpec((1, HEAD_DIM), lambda i: (0, 0))],
        out_specs=[pl.BlockSpec((tr, HEAD_DIM), lambda i: (i, 0))] * 2,
        out_shape=[jax.ShapeDtypeStruct((r, HEAD_DIM), F32)] * 2,
        compiler_params=_cparams("parallel"),
        name="rope_tables",
    )(posf, inv)


def _proj_kernel(*refs, kinds, tile_src, n_w, transposed, has_extra):
    rope = "rope" in kinds
    refs = list(refs)
    x_ref, g_ref = refs[:2]
    w_refs = refs[2:2 + n_w]
    hg_ref = refs[2 + n_w]
    at = 3 + n_w
    if rope:
        cos_ref, sin_ref = refs[at:at + 2]
        at += 2
    if has_extra:
        xs_ref = refs[at]
        at += 1
        if rope:
            cos_s_ref, sin_s_ref = refs[at:at + 2]
            at += 2
        o_ref, os_ref, xn_ref = refs[at:at + 3]
    else:
        o_ref, xn_ref = refs[at:at + 2]
    tm = x_ref.shape[0]

    def normed(x):
        return (x * _rms_scale(x) * g_ref[...]).astype(BF16)

    def tiles(xn, cos, sin, store):
        for t, kind in enumerate(kinds):
            which, col, width = tile_src[t]
            at_col = sum(w for _, _, w in tile_src[:t])
            if transposed:
                y = lax.dot_general(xn, w_refs[which][col:col + width, :], NT_DIMS, preferred_element_type=F32)
            else:
                y = jnp.dot(xn, w_refs[which][:, col:col + width], preferred_element_type=F32)
            if kind == "plain":
                store(slice(at_col, at_col + width), y)
                continue
            for j in range(width // HEAD_DIM):
                osl = slice(at_col + j * HEAD_DIM, at_col + (j + 1) * HEAD_DIM)
                yh = y[:, j * HEAD_DIM:(j + 1) * HEAD_DIM]
                yh = yh * _rms_scale(yh) * hg_ref[:, osl]
                if kind == "rope":
                    yh = yh * cos + pltpu.roll(yh, HEAD_DIM // 2, axis=1) * sin
                store(osl, yh)

    def store_main(cols, y):
        o_ref[:, cols] = y

    def store_both(cols, y):
        o_ref[:, cols] = y[:tm]
        os_ref[:, cols] = y[tm:]

    xn_ref[:tm, :] = normed(x_ref[...])
    cos = cos_ref[...] if rope else None
    sin = sin_ref[...] if rope else None
    if not has_extra:
        tiles(xn_ref[...], cos, sin, store_main)
        return

    @pl.when(pl.program_id(0) == 0)
    def _():
        xn_ref[tm:, :] = normed(xs_ref[...])
        cos_all = jnp.concatenate([cos, cos_s_ref[...]], axis=0) if rope else None
        sin_all = jnp.concatenate([sin, sin_s_ref[...]], axis=0) if rope else None
        tiles(xn_ref[...], cos_all, sin_all, store_both)

    @pl.when(pl.program_id(0) != 0)
    def _():
        tiles(xn_ref[:tm, :], cos, sin, store_main)


def norm_proj(x, g, ws, layer, head_gain, kinds, cos=None, sin=None, cast=(), tile_src=None, transposed=False,
              extra=None):
    m, d = x.shape
    ws = list(ws) if isinstance(ws, (list, tuple)) else [ws]
    if tile_src is None:
        tile_src = [(0, t * COL_TILE, COL_TILE) for t in range(len(kinds))]
    n_cols = sum(width for _, _, width in tile_src)
    assert len(tile_src) == len(kinds) and head_gain.shape == (1, n_cols)
    assert all(width % HEAD_DIM == 0 for _, _, width in tile_src)
    rope = "rope" in kinds
    has_extra = extra is not None
    ms = extra[0].shape[0] if has_extra else 0
    tm = min(m, ROW_TILE)
    while tm * n_cols * 4 > PROJ_OUT_BLOCK_BYTES and tm % (2 * SUBLANES) == 0:
        tm //= 2
    assert m % tm == 0
    in_specs = [pl.BlockSpec((tm, d), lambda i: (i, 0)),
                pl.BlockSpec((1, d), lambda i: (0, 0))]
    for k, w in enumerate(ws):
        used = max(col + width for which, col, width in tile_src if which == k)
        if transposed:
            assert used <= w.shape[1] and w.shape[2] == d
            in_specs.append(pl.BlockSpec((None, used, d), lambda i: (layer, 0, 0), pipeline_mode=RESIDENT))
            continue
        assert used <= w.shape[2] and (used % LANES == 0 or used == w.shape[2])
        in_specs.append(pl.BlockSpec((None, d, used), lambda i: (layer, 0, 0), pipeline_mode=RESIDENT))
    in_specs.append(pl.BlockSpec((1, n_cols), lambda i: (0, 0)))
    args = [x, g.reshape(1, d), *ws, head_gain]
    if rope:
        assert cos.shape[0] % tm == 0
        tab_blocks = cos.shape[0] // tm
        in_specs += [pl.BlockSpec((tm, HEAD_DIM), lambda i: (i % tab_blocks, 0))] * 2
        args += [cos, sin]
    out_specs = [pl.BlockSpec((tm, n_cols), lambda i: (i, 0))]
    out_shape = [jax.ShapeDtypeStruct((m, n_cols), F32)]
    if has_extra:
        whole = lambda a: pl.BlockSpec(a.shape, lambda i: (0, 0))
        xs, cos_s, sin_s = extra
        extras = [xs] + ([cos_s, sin_s] if rope else [])
        in_specs += [whole(a) for a in extras]
        args += extras
        out_specs.append(pl.BlockSpec((ms, n_cols), lambda i: (0, 0)))
        out_shape.append(jax.ShapeDtypeStruct((ms, n_cols), F32))
    n_out = len(out_specs)
    side = _SideCast(cast, m // tm, lambda i: i)
    outs = pl.pallas_call(
        side.wrap(functools.partial(_proj_kernel, kinds=tuple(kinds), tile_src=tuple(tile_src), n_w=len(ws),
                                    transposed=transposed, has_extra=has_extra),
                  len(args), n_out),
        grid=(m // tm,),
        in_specs=in_specs + side.in_specs,
        out_specs=out_specs + side.out_specs,
        out_shape=out_shape + side.out_shape,
        scratch_shapes=[pltpu.VMEM((tm + ms, d), BF16)],
        compiler_params=_cparams("arbitrary"),
        name="norm_proj",
    )(*args, *side.views)
    casts = [o.reshape(shp) for o, shp in zip(outs[n_out:], side.shapes)]
    return outs[0], (outs[1] if has_extra else None), casts


def _mix_mlp_kernel(*refs, has_extra):
    if has_extra:
        (x_ref, g_ref, wu_ref, wd_ref, a_ref, b_ref, wa_ref, wb_ref, xs_ref, as_ref, bs_ref,
         o_ref, os_ref, xn_ref) = refs
    else:
        x_ref, g_ref, wu_ref, wd_ref, a_ref, b_ref, wa_ref, wb_ref, o_ref, xn_ref = refs
    i, f = pl.program_id(0), pl.program_id(1)
    tm = x_ref.shape[0]

    def normed(x):
        return (x * _rms_scale(x) * g_ref[...]).astype(BF16)

    def mixed(a, b):
        acc = jnp.dot(a, wa_ref[...], preferred_element_type=F32)
        return acc + jnp.dot(b, wb_ref[...], preferred_element_type=F32)

    def ffn(xn):
        h = jnp.dot(xn, wu_ref[...], preferred_element_type=F32)
        h = jnp.maximum(h, 0.0)
        h = (h * h).astype(BF16)
        return jnp.dot(h, wd_ref[...], preferred_element_type=F32)

    def start_main():
        x1 = x_ref[...] + mixed(a_ref[...], b_ref[...])
        xn_ref[:tm, :] = normed(x1)
        o_ref[...] = x1

    def start_both():
        acc = mixed(jnp.concatenate([a_ref[...], as_ref[...]], axis=0),
                    jnp.concatenate([b_ref[...], bs_ref[...]], axis=0))
        x1 = x_ref[...] + acc[:tm]
        xs1 = xs_ref[...] + acc[tm:]
        xn_ref[:tm, :] = normed(x1)
        xn_ref[tm:, :] = normed(xs1)
        o_ref[...] = x1
        os_ref[...] = xs1

    if not has_extra:
        pl.when(f == 0)(start_main)
        o_ref[...] += ffn(xn_ref[...])
        return

    @pl.when(i == 0)
    def _():
        pl.when(f == 0)(start_both)
        upd = ffn(xn_ref[...])
        o_ref[...] += upd[:tm]
        os_ref[...] += upd[tm:]

    @pl.when(i != 0)
    def _():
        pl.when(f == 0)(start_main)
        o_ref[...] += ffn(xn_ref[:tm, :])


def mix_mlp(x, a, b, w_out, layer, g, wu, wd, tf=1024, cast=(), extra=None):
    m, d = x.shape
    ff = wu.shape[1]
    tm = min(m, ROW_TILE)
    nf = ff // tf
    ka, kb = a.shape[1], b.shape[1]
    assert w_out.shape[1] == ka + kb and ka % kb == 0
    has_extra = extra is not None
    side = _SideCast(cast, (m // tm) * nf, lambda i, j: i * nf + j)
    in_specs = [pl.BlockSpec((tm, d), lambda i, j: (i, 0)),
                pl.BlockSpec((1, d), lambda i, j: (0, 0)),
                pl.BlockSpec((d, tf), lambda i, j: (0, j)),
                pl.BlockSpec((tf, d), lambda i, j: (j, 0)),
                pl.BlockSpec((tm, ka), lambda i, j: (i, 0)),
                pl.BlockSpec((tm, kb), lambda i, j: (i, 0)),
                pl.BlockSpec((None, ka, d), lambda i, j: (layer, 0, 0), pipeline_mode=RESIDENT),
                pl.BlockSpec((None, kb, d), lambda i, j: (layer, ka // kb, 0), pipeline_mode=RESIDENT)]
    out_specs = [pl.BlockSpec((tm, d), lambda i, j: (i, 0))]
    out_shape = [jax.ShapeDtypeStruct((m, d), F32)]
    args = [x, g.reshape(1, d), wu, wd, a, b, w_out, w_out]
    ms = extra[0].shape[0] if has_extra else 0
    if has_extra:
        in_specs += [pl.BlockSpec(e.shape, lambda i, j: (0, 0)) for e in extra]
        args += list(extra)
        out_specs.append(pl.BlockSpec((ms, d), lambda i, j: (0, 0)))
        out_shape.append(jax.ShapeDtypeStruct((ms, d), F32))
    n_out = len(out_specs)
    outs = pl.pallas_call(
        side.wrap(functools.partial(_mix_mlp_kernel, has_extra=has_extra), len(args), n_out),
        grid=(m // tm, nf),
        in_specs=in_specs + side.in_specs,
        out_specs=out_specs + side.out_specs,
        out_shape=out_shape + side.out_shape,
        scratch_shapes=[pltpu.VMEM((tm + ms, d), BF16)],
        compiler_params=pltpu.CompilerParams(dimension_semantics=("arbitrary", "arbitrary"),
                                             vmem_limit_bytes=MIX_MLP_VMEM_LIMIT),
        name="mix_mlp",
    )(*args, *side.views)
    casts = [o.reshape(shp) for o, shp in zip(outs[n_out:], side.shapes)]
    return outs[0], (outs[1] if has_extra else None), casts


def _split3(x):
    hi = x.astype(BF16)
    r1 = x - hi.astype(F32)
    mid = r1.astype(BF16)
    lo = (r1 - mid.astype(F32)).astype(BF16)
    return hi, mid, lo


def _ssd_kernel(xbc_ref, dt_ref, z_ref, cw_ref, cb_ref, dtb_ref, alog_ref, dsk_ref, gg_ref, tri_ref, expand_ref,
                conv0_ref, h0_ref, y_ref, convo_ref, ho_ref,
                ext_ref, ht_ref, yd_ref, st_ref, *, t_real, n_heads, d_inner):
    L = SSD_CHUNK
    P = SSM_HEAD_DIM
    N = SSM_D_STATE
    G = SSM_GROUPS
    hpg = n_heads // G
    gw = hpg * P
    hist = CONV_W - 1
    base = SUBLANES - hist
    c = pl.program_id(1)
    nc = pl.num_programs(1)

    @pl.when(c == 0)
    def _():
        ext_ref[base:SUBLANES, :] = conv0_ref[...]
        for g in range(G):
            ht_ref[g] = h0_ref[g].T

    ext_ref[SUBLANES:SUBLANES + L, :] = xbc_ref[...]
    ext = ext_ref[...]
    acc = cb_ref[...] + ext[SUBLANES:, :] * cw_ref[hist:CONV_W, :]
    for k in range(1, CONV_W):
        acc = acc + pltpu.roll(ext, k, axis=0)[SUBLANES:, :] * cw_ref[hist - k:CONV_W - k, :]
    xc = acc * _sigmoid(acc)

    @pl.when(c == nc - 1)
    def _():
        n_in_last = t_real - ((t_real - 1) // L) * L
        convo_ref[...] = ext_ref[base + n_in_last:base + n_in_last + hist, :]

    ext_ref[base:SUBLANES, :] = ext_ref[base + L:SUBLANES + L, :]

    dt_raw = dt_ref[...] + dtb_ref[...]
    dt = jnp.maximum(dt_raw, 0.0) + jnp.log1p(jnp.exp(-jnp.abs(dt_raw)))
    row = lax.broadcasted_iota(jnp.int32, (L, LANES), 0) + c * L
    dt = jnp.where(row < t_real, dt, 0.0)
    da = dt * (-jnp.exp(alog_ref[...]))
    li = lax.broadcasted_iota(jnp.int32, (L, L), 0)
    si = lax.broadcasted_iota(jnp.int32, (L, L), 1)
    causal = li >= si
    a_cs = sum(jnp.dot(tri_ref[...], part, preferred_element_type=F32) for part in _split3(da))
    a_cs_t = a_cs.T
    dt_t = dt.T
    w_t = dt_t * jnp.exp(a_cs_t[:, L - 1:L] - a_cs_t)
    e_exp = sum(jnp.dot(part, expand_ref[...], preferred_element_type=F32) for part in _split3(jnp.exp(a_cs)))
    xb = xc[:, :d_inner].astype(BF16)
    low_half = lax.broadcasted_iota(jnp.int32, (L, 2 * P), 1) < P

    for g in range(G):
        b_g = xc[:, d_inner + g * N:d_inner + (g + 1) * N]
        c_g = xc[:, d_inner + G * N + g * N:d_inner + G * N + (g + 1) * N].astype(BF16)
        cbm = lax.dot_general(c_g, b_g.astype(BF16), NT_DIMS, preferred_element_type=F32)
        b_t = b_g.T
        y_off = jnp.dot(c_g, ht_ref[g].astype(BF16), preferred_element_type=F32)
        for kp in range(hpg // 2):
            h0 = g * hpg + 2 * kp
            x_pair = xb[:, h0 * P:(h0 + 2) * P]
            yd, st = [], []
            for h in (h0, h0 + 1):
                seg = a_cs[:, h:h + 1] - a_cs_t[h:h + 1, :]
                dec = jnp.exp(jnp.where(causal, seg, -jnp.inf))
                mh = (cbm * dec * dt_t[h:h + 1, :]).astype(BF16)
                yd.append(jnp.dot(mh, x_pair, preferred_element_type=F32))
                btw = (b_t * w_t[h:h + 1, :]).astype(BF16)
                st.append(jnp.dot(btw, x_pair, preferred_element_type=F32))
            yd_ref[:, 2 * kp * P:(2 * kp + 2) * P] = jnp.where(low_half, yd[0], yd[1])
            st_ref[:, 2 * kp * P:(2 * kp + 2) * P] = jnp.where(low_half, st[0], st[1])
        gs = slice(g * gw, (g + 1) * gw)
        e_g = e_exp[:, gs]
        y_g = yd_ref[...] + y_off * e_g + dsk_ref[:, gs] * xc[:, gs]
        ht_ref[g] = ht_ref[g] * e_g[L - 1:L, :] + st_ref[...]
        z_g = z_ref[:, gs]
        gated = y_g * (z_g * _sigmoid(z_g))
        y_ref[:, gs] = (gated * _rms_scale(gated) * gg_ref[:, gs]).astype(y_ref.dtype)

    @pl.when(c == nc - 1)
    def _():
        for g in range(G):
            ho_ref[g] = ht_ref[g].T


def ssd_mixer(proj, col_xbc, col_dt, col_z, t_real, conv_w, conv_b, dt_bias, a_log, d_skip, gate_g,
              conv_prev, h_prev, cast=()):
    bt, tp, _ = proj.shape
    conv_dim = conv_w.shape[1]
    n_heads = dt_bias.shape[0]
    d_inner = n_heads * SSM_HEAD_DIM
    gw = d_inner // SSM_GROUPS
    L = SSD_CHUNK
    assert tp % L == 0 and (tp - t_real) < L
    assert col_xbc % conv_dim == 0 and col_dt % LANES == 0 and col_z % d_inner == 0
    pad = LANES - n_heads
    dtb = jnp.pad(dt_bias, (0, pad)).reshape(1, LANES)
    alog = jnp.pad(a_log, (0, pad)).reshape(1, LANES)
    dsk = jnp.repeat(d_skip, SSM_HEAD_DIM).reshape(1, d_inner)
    tri = jnp.tril(jnp.ones((L, L), BF16))
    expand = jnp.repeat(jnp.eye(LANES, dtype=BF16)[:, :n_heads], SSM_HEAD_DIM, axis=1)
    kern = functools.partial(_ssd_kernel, t_real=t_real, n_heads=n_heads, d_inner=d_inner)
    vec = lambda n: pl.BlockSpec((1, n), lambda b, c: (0, 0))
    nc = tp // L
    side = _SideCast(cast, bt * nc, lambda b, c: b * nc + c)
    outs = pl.pallas_call(
        side.wrap(kern, 13, 3),
        grid=(bt, nc),
        in_specs=[pl.BlockSpec((None, L, conv_dim), lambda b, c: (b, c, col_xbc // conv_dim)),
                  pl.BlockSpec((None, L, LANES), lambda b, c: (b, c, col_dt // LANES)),
                  pl.BlockSpec((None, L, d_inner), lambda b, c: (b, c, col_z // d_inner)),
                  pl.BlockSpec((CONV_W, conv_dim), lambda b, c: (0, 0)),
                  vec(conv_dim), vec(LANES), vec(LANES), vec(d_inner), vec(d_inner),
                  pl.BlockSpec((L, L), lambda b, c: (0, 0)),
                  pl.BlockSpec((LANES, d_inner), lambda b, c: (0, 0)),
                  pl.BlockSpec((None, CONV_W - 1, conv_dim), lambda b, c: (b, 0, 0)),
                  pl.BlockSpec((None, SSM_GROUPS, gw, SSM_D_STATE), lambda b, c: (b, 0, 0, 0))] + side.in_specs,
        out_specs=[pl.BlockSpec((None, L, d_inner), lambda b, c: (b, c, 0)),
                   pl.BlockSpec((None, CONV_W - 1, conv_dim), lambda b, c: (b, 0, 0)),
                   pl.BlockSpec((None, SSM_GROUPS, gw, SSM_D_STATE), lambda b, c: (b, 0, 0, 0))] + side.out_specs,
        out_shape=[jax.ShapeDtypeStruct((bt, tp, d_inner), BF16),
                   jax.ShapeDtypeStruct((bt, CONV_W - 1, conv_dim), F32),
                   jax.ShapeDtypeStruct((bt, SSM_GROUPS, gw, SSM_D_STATE), F32)] + side.out_shape,
        scratch_shapes=[pltpu.VMEM((SUBLANES + L, conv_dim), F32),
                        pltpu.VMEM((SSM_GROUPS, SSM_D_STATE, gw), F32),
                        pltpu.VMEM((L, gw), F32),
                        pltpu.VMEM((SSM_D_STATE, gw), F32)],
        compiler_params=_cparams("parallel", "arbitrary"),
        name="ssd_mixer",
    )(proj, proj, proj, conv_w, conv_b.reshape(1, conv_dim), dtb, alog, dsk, gate_g.reshape(1, d_inner),
      tri, expand, conv_prev, h_prev, *side.views)
    return side.split(outs, 3)


def _ssd_step_kernel(xbc_ref, dt_ref, z_ref, cw_ref, cb_ref, dtb_ref, alog_ref, dsk_ref, gg_ref, expand_ref,
                     conv0_ref, h0_ref, y_ref, convo_ref, ho_ref, *, n_heads, d_inner):
    P, N, G = SSM_HEAD_DIM, SSM_D_STATE, SSM_GROUPS
    gw = n_heads // G * P
    hist = CONV_W - 1
    acc = cb_ref[...] + xbc_ref[...] * cw_ref[hist:CONV_W, :]
    for k in range(hist):
        acc = acc + conv0_ref[k:k + 1, :] * cw_ref[k:k + 1, :]
    xc = acc * _sigmoid(acc)
    convo_ref[0:hist - 1, :] = conv0_ref[1:hist, :]
    convo_ref[hist - 1:hist, :] = xbc_ref[0:1, :]

    dt_raw = dt_ref[...] + dtb_ref[...]
    dt = jnp.maximum(dt_raw, 0.0) + jnp.log1p(jnp.exp(-jnp.abs(dt_raw)))
    decay = jnp.exp(dt * (-jnp.exp(alog_ref[...])))
    expand = lambda v: sum(jnp.dot(part, expand_ref[...], preferred_element_type=F32) for part in _split3(v))
    dt_x = expand(dt) * xc[:, :d_inner]
    decay_c = expand(decay)

    for g in range(G):
        gs = slice(g * gw, (g + 1) * gw)
        b_col = xc[:, d_inner + g * N:d_inner + (g + 1) * N].T[:, 0:1]
        c_g = xc[:, d_inner + G * N + g * N:d_inner + G * N + (g + 1) * N].astype(BF16)
        h_t = h0_ref[g].T * decay_c[0:1, gs] + b_col * dt_x[0:1, gs]
        ho_ref[g] = h_t.T
        y_g = jnp.dot(c_g, h_t.astype(BF16), preferred_element_type=F32) + dsk_ref[:, gs] * xc[:, gs]
        z_g = z_ref[:, gs]
        gated = y_g * (z_g * _sigmoid(z_g))
        y_ref[:, gs] = (gated * _rms_scale(gated) * gg_ref[:, gs]).astype(y_ref.dtype)


def ssd_step(proj, col_xbc, col_dt, col_z, conv_w, conv_b, dt_bias, a_log, d_skip, gate_g, conv_prev, h_prev):
    bt, rows, _ = proj.shape
    conv_dim = conv_w.shape[1]
    n_heads = dt_bias.shape[0]
    d_inner = n_heads * SSM_HEAD_DIM
    gw = d_inner // SSM_GROUPS
    assert col_xbc % conv_dim == 0 and col_dt % LANES == 0 and col_z % d_inner == 0
    pad = LANES - n_heads
    dtb = jnp.pad(dt_bias, (0, pad)).reshape(1, LANES)
    alog = jnp.pad(a_log, (0, pad)).reshape(1, LANES)
    dsk = jnp.repeat(d_skip, SSM_HEAD_DIM).reshape(1, d_inner)
    expand = jnp.repeat(jnp.eye(LANES, dtype=BF16)[:, :n_heads], SSM_HEAD_DIM, axis=1)
    vec = lambda n: pl.BlockSpec((1, n), lambda b: (0, 0))
    state = pl.BlockSpec((None, SSM_GROUPS, gw, SSM_D_STATE), lambda b: (b, 0, 0, 0))
    hist_rows = pl.BlockSpec((None, CONV_W - 1, conv_dim), lambda b: (b, 0, 0))
    return pl.pallas_call(
        functools.partial(_ssd_step_kernel, n_heads=n_heads, d_inner=d_inner),
        grid=(bt,),
        in_specs=[pl.BlockSpec((None, rows, conv_dim), lambda b: (b, 0, col_xbc // conv_dim)),
                  pl.BlockSpec((None, rows, LANES), lambda b: (b, 0, col_dt // LANES)),
                  pl.BlockSpec((None, rows, d_inner), lambda b: (b, 0, col_z // d_inner)),
                  pl.BlockSpec((CONV_W, conv_dim), lambda b: (0, 0)),
                  vec(conv_dim), vec(LANES), vec(LANES), vec(d_inner), vec(d_inner),
                  pl.BlockSpec((LANES, d_inner), lambda b: (0, 0)),
                  hist_rows, state],
        out_specs=[pl.BlockSpec((None, rows, d_inner), lambda b: (b, 0, 0)), hist_rows, state],
        out_shape=[jax.ShapeDtypeStruct((bt, rows, d_inner), BF16),
                   jax.ShapeDtypeStruct((bt, CONV_W - 1, conv_dim), F32),
                   jax.ShapeDtypeStruct((bt, SSM_GROUPS, gw, SSM_D_STATE), F32)],
        compiler_params=_cparams("parallel"),
        name="ssd_step",
    )(proj, proj, proj, conv_w, conv_b.reshape(1, conv_dim), dtb, alog, dsk, gate_g.reshape(1, d_inner),
      expand, conv_prev, h_prev)


def _mem_attn_kernel(q_ref, k_ref, v_ref, o_ref):
    scale = HEAD_DIM ** -0.5
    headed = len(k_ref.shape) == 3
    for h in range(MEM_HEADS):
        sl = slice(h * HEAD_DIM, (h + 1) * HEAD_DIM)
        q = (q_ref[:, sl] * (scale * LOG2_E)).astype(BF16)
        k = (k_ref[:, h, :] if headed else k_ref[:, sl]).astype(BF16)
        v = (v_ref[:, h, :] if headed else v_ref[:, sl]).astype(BF16)
        s = lax.dot_general(q, k, NT_DIMS, preferred_element_type=F32)
        e = jnp.exp2(s - jnp.max(s, axis=-1, keepdims=True))
        den = jnp.sum(e, axis=-1, keepdims=True)
        o = jnp.dot(e.astype(BF16), v, preferred_element_type=F32) / den
        o_ref[:, sl] = o.astype(o_ref.dtype)


def mem_attention(proj, col_q, k, v, kv_at):
    bt, t, _ = proj.shape
    width = MEM_HEADS * HEAD_DIM
    tq = min(t, ROW_TILE)
    if k.ndim == 5:
        mlen = k.shape[2]
        kv_specs = [pl.BlockSpec((None, None, mlen, MEM_HEADS, HEAD_DIM), lambda b, i: (kv_at, b, 0, 0, 0))] * 2
    else:
        mlen = k.shape[1]
        kv_specs = [pl.BlockSpec((None, mlen, width), lambda b, i, c=c: (b, 0, c // width)) for c in kv_at]
    return pl.pallas_call(
        _mem_attn_kernel,
        grid=(bt, t // tq),
        in_specs=[pl.BlockSpec((None, tq, width), lambda b, i: (b, i, col_q // width))] + kv_specs,
        out_specs=pl.BlockSpec((None, tq, width), lambda b, i: (b, i, 0)),
        out_shape=jax.ShapeDtypeStruct((bt, t, width), BF16),
        compiler_params=_cparams("parallel", "parallel"),
        name="mem_attention",
    )(proj, k, v)


def _run_units(first, count, unit, unroll):
    trips = count // unroll
    if trips == 1:
        trips = 0
    if trips:
        def trip(it, carry):
            for k in range(unroll):
                unit(first + it * unroll + k)
            return carry
        lax.fori_loop(0, trips, trip, 0)
    for k in range(trips * unroll, count):
        unit(first + k)


def _dil_attn_kernel(*refs, groups, rb, unroll):
    ng = len(groups)
    q_refs = refs[:ng]
    kp_ref, kc_ref, vp_ref, vc_ref, o_ref = refs[ng:ng + 5]
    og = refs[ng + 5:2 * ng + 5]
    lg = refs[2 * ng + 5:3 * ng + 5]
    i = pl.program_id(1)
    scale = HEAD_DIM ** -0.5

    for gi, (win, d) in enumerate(groups):
        w = win // d
        dw = d * w
        qi = lax.broadcasted_iota(jnp.int32, (w, 2 * w), 0)
        kj = lax.broadcasted_iota(jnp.int32, (w, 2 * w), 1)
        band = (kj >= qi) & (kj <= qi + w)
        band_first = band & (kj >= jnp.where(i > 0, 0, w))

        def rows(start, size, d=d, w=w):
            if d == 1:
                return pl.ds(pl.multiple_of(start, w), size)
            return pl.ds(start, size, stride=d)

        def attend(rows_q, kk, vv, mask, gi=gi, w=w):
            q = (q_refs[gi][rows_q, :] * (scale * LOG2_E)).astype(BF16)
            s = lax.dot_general(q, kk.astype(BF16), NT_DIMS, preferred_element_type=F32)
            s = jnp.where(mask, s, -jnp.inf)
            m = jnp.max(s, axis=-1, keepdims=True)
            e = jnp.exp2(s - m)
            den = jnp.sum(e, axis=-1, keepdims=True)
            o = jnp.dot(e.astype(BF16), vv.astype(BF16), preferred_element_type=F32) / den
            og[gi][rows_q, :] = o
            lg[gi][rows_q, :] = jnp.broadcast_to(m * LN_2 + jnp.log(den), (w, HEAD_DIM))

        def first_unit(r, rows=rows, attend=attend, w=w, dw=dw, mask=band_first):
            rows_q, rows_p = rows(r, w), rows(rb - dw + r, w)
            kk = jnp.concatenate([kp_ref[rows_p, :], kc_ref[rows_q, :]], axis=0)
            vv = jnp.concatenate([vp_ref[rows_p, :], vc_ref[rows_q, :]], axis=0)
            attend(rows_q, kk, vv, mask)

        def later_unit(u, rows=rows, attend=attend, d=d, w=w, dw=dw, mask=band):
            sb = u // d
            start_q = sb * dw + (u - sb * d)
            rows_k = rows(start_q - dw, 2 * w)
            attend(rows(start_q, w), kc_ref[rows_k, :], vc_ref[rows_k, :], mask)

        _run_units(0, d, first_unit, unroll)
        _run_units(d, rb // w - d, later_unit, unroll)

    ls = [r[...] for r in lg]
    mm = functools.reduce(jnp.maximum, ls)
    ws = [jnp.exp(l - mm) for l in ls]
    num = sum(wg * r[...] for wg, r in zip(ws, og))
    o_ref[...] = (num / sum(ws)).astype(o_ref.dtype)


def dilated_attention(proj, kv):
    bt, t, _ = proj.shape
    ng = len(DIL_GROUPS)
    rb = min(t, max(win for win, _ in DIL_GROUPS))
    for win, d in DIL_GROUPS:
        assert win % d == 0 and rb % win == 0
    assert t % rb == 0
    blk = lambda f: pl.BlockSpec((None, rb, HEAD_DIM), f)
    prev = lambda i: jnp.maximum(i - 1, 0)
    q_specs = [blk(lambda b, i, h, g=g: (b, i, g * DIL_HEADS + h)) for g in range(ng)]
    kv_specs = [blk(lambda b, i, h: (b, prev(i), h)), blk(lambda b, i, h: (b, i, h)),
                blk(lambda b, i, h: (b, prev(i), DIL_HEADS + h)), blk(lambda b, i, h: (b, i, DIL_HEADS + h))]
    return pl.pallas_call(
        functools.partial(_dil_attn_kernel, groups=DIL_GROUPS, rb=rb, unroll=DIL_UNROLL),
        grid=(bt, t // rb, DIL_HEADS),
        in_specs=q_specs + kv_specs,
        out_specs=blk(lambda b, i, h: (b, i, h)),
        out_shape=jax.ShapeDtypeStruct((bt, t, DIL_HEADS * HEAD_DIM), BF16),
        scratch_shapes=[pltpu.VMEM((rb, HEAD_DIM), F32)] * (2 * ng),
        compiler_params=_cparams("parallel", "parallel", "parallel"),
        name="dilated_attention",
    )(*([proj] * ng), kv, kv, kv, kv)


def _dil_decode_kernel(*refs):
    ng = len(DIL_GROUPS)
    q_ref, kv_ref = refs[:2]
    k_refs, v_refs = refs[2:2 + ng], refs[2 + ng:2 + 2 * ng]
    o_ref = refs[-1]
    width = DIL_HEADS * HEAD_DIM
    scale = HEAD_DIM ** -0.5
    for h in range(DIL_HEADS):
        sl = slice(h * HEAD_DIM, (h + 1) * HEAD_DIM)
        k_new = kv_ref[:, h * HEAD_DIM:(h + 1) * HEAD_DIM]
        v_new = kv_ref[:, width + h * HEAD_DIM:width + (h + 1) * HEAD_DIM]
        outs, lses = [], []
        for gi in range(ng):
            kc = k_refs[gi][:, h, :].astype(BF16)
            vc = v_refs[gi][:, h, :].astype(BF16)
            q = q_ref[:, gi * width + h * HEAD_DIM:gi * width + (h + 1) * HEAD_DIM]
            s = lax.dot_general(q.astype(BF16), kc, NT_DIMS, preferred_element_type=F32) * scale
            s_self = jnp.sum(q * k_new, axis=-1, keepdims=True) * scale
            m = jnp.maximum(jnp.max(s, axis=-1, keepdims=True), s_self)
            e = jnp.exp(s - m)
            e_self = jnp.exp(s_self - m)
            den = jnp.sum(e, axis=-1, keepdims=True) + e_self
            o = jnp.dot(e.astype(BF16), vc, preferred_element_type=F32) + e_self * v_new
            outs.append(o / den)
            lses.append(m + jnp.log(den))
        mm = functools.reduce(jnp.maximum, lses)
        ws = [jnp.exp(l - mm) for l in lses]
        num = sum(wg * o for wg, o in zip(ws, outs))
        o_ref[:, sl] = (num / sum(ws)).astype(o_ref.dtype)


def dilated_decode(proj, kv_new, k_cache, v_cache):
    bt, rows, nq = proj.shape
    lc = k_cache.shape[1]
    width = DIL_HEADS * HEAD_DIM
    views, specs = [], []
    for cache in (k_cache, v_cache):
        for win, dil in DIL_GROUPS:
            assert lc % win == 0 and win % dil == 0
            w = win // dil
            views.append(cache.reshape(bt, lc // dil, dil, DIL_HEADS, HEAD_DIM))
            specs.append(pl.BlockSpec((None, w, None, DIL_HEADS, HEAD_DIM),
                                      lambda b, last=lc // win - 1: (b, last, 0, 0, 0)))
    return pl.pallas_call(
        _dil_decode_kernel,
        grid=(bt,),
        in_specs=[pl.BlockSpec((None, rows, nq), lambda b: (b, 0, 0)),
                  pl.BlockSpec((None, rows, 2 * width), lambda b: (b, 0, 0))] + specs,
        out_specs=pl.BlockSpec((None, rows, width), lambda b: (b, 0, 0)),
        out_shape=jax.ShapeDtypeStruct((bt, rows, width), BF16),
        compiler_params=_cparams("parallel"),
        name="dilated_decode",
    )(proj, kv_new, *views)


def kernel(x_prompt, x_sample, state_conv, state_ssm, cache_win_k, cache_win_v, cache_mem_k, cache_mem_v,
           mem_prompt, norm_mix_g, norm_mlp_g, norm_mem_g, w_mem_k, w_mem_v, mem_q_norm_g, mem_k_norm_g,
           w_up, w_down, w_in_a, conv_w, conv_b, dt_bias, a_log, d_skip, gate_norm_g, w_out_a,
           w_in_b, q_norm_g, w_out_b, kv_norm_g, w_k_shared, w_v_shared, k_norm_g):
    depth = w_up.shape[0]
    n_a = w_in_a.shape[0]
    d_model = x_prompt.shape[-1]
    n_heads = dt_bias.shape[1]
    d_inner = n_heads * SSM_HEAD_DIM
    conv_dim = conv_w.shape[2]
    mem_w = MEM_HEADS * HEAD_DIM
    kv_w = DIL_HEADS * HEAD_DIM
    dil_q_w = len(DIL_GROUPS) * kv_w

    s1, s2, s3 = d_inner, d_inner + conv_dim, d_inner + conv_dim + n_heads
    assert s1 % COL_TILE == 0 and s2 % COL_TILE == 0
    w_in_a_t = jnp.swapaxes(w_in_a, 1, 2)
    w_a_main = round_weight(w_in_a_t, s2)
    tail = lax.optimization_barrier(w_in_a_t[:, s2:]).astype(BF16)
    w_a_tail = jnp.concatenate([tail[:, :n_heads], jnp.zeros((n_a, COL_TILE - n_heads, d_model), BF16),
                                tail[:, n_heads:]], axis=1)
    tiles_a = ([(0, s1 + c, COL_TILE) for c in range(0, conv_dim, COL_TILE)]
               + [(1, COL_TILE + c, COL_TILE) for c in range(0, mem_w, COL_TILE)]
               + [(0, c, COL_TILE) for c in range(0, d_inner, COL_TILE)] + [(1, 0, LANES)])
    kinds_a = (["plain"] * (conv_dim // COL_TILE) + ["norm"] * (mem_w // COL_TILE)
               + ["plain"] * (d_inner // COL_TILE + 1))
    col_xbc, col_qm_a, col_z = 0, conv_dim, conv_dim + mem_w
    col_dt = col_z + d_inner
    a_cols = col_dt + LANES
    kinds_b = ["rope"] * (dil_q_w // COL_TILE) + ["norm"] * (mem_w // COL_TILE)
    kinds_kv = ["rope"] * (kv_w // COL_TILE) + ["plain"] * (kv_w // COL_TILE)
    kinds_mkv = ["norm"] * (mem_w // COL_TILE) + ["plain"] * (mem_w // COL_TILE)
    w_b = w_in_b.astype(BF16)
    w_kv = jnp.concatenate([w_k_shared, w_v_shared], axis=-1).astype(BF16)[None]
    w_mkv = jnp.concatenate([w_mem_k, w_mem_v], axis=-1).astype(BF16)
    w_out_a_b, w_out_b_b = w_out_a.astype(BF16), w_out_b.astype(BF16)
    assert n_a >= 1
    w_up_b, w_down_b = [None] * depth, [None] * depth

    def gain_row(width, pieces):
        parts, at = [], 0
        for start, g, reps in pieces:
            parts += [jnp.ones((start - at,), F32), jnp.tile(g, reps)]
            at = start + reps * HEAD_DIM
        parts.append(jnp.ones((width - at,), F32))
        return jnp.concatenate(parts).reshape(1, width)

    class Group:
        def __init__(self, x, pos_rows, conv_prev, ssm_prev, mem_kv, k_past, v_past, t_real):
            self.bt, self.t, _ = x.shape
            self.m = self.bt * self.t
            self.x2 = x.reshape(self.m, d_model)
            self.cos, self.sin = rope_tables(pos_rows)
            self.conv_prev, self.ssm_prev, self.mem_kv = conv_prev, ssm_prev, mem_kv
            self.k_past, self.v_past, self.t_real = k_past, v_past, t_real
            self.conv_new, self.ssm_new, self.kv3 = [], [], None

    def in_proj_a(prompt, sample, l):
        hg = gain_row(a_cols, [(col_qm_a, mem_q_norm_g[l], MEM_HEADS)])
        proj_p, proj_s, rounded = norm_proj(prompt.x2, norm_mix_g[l], [w_a_main, w_a_tail], l, hg, kinds_a,
                                            tile_src=tiles_a, transposed=True, extra=(sample.x2, None, None),
                                            cast=((w_up, 0),) if l == 0 else ())
        if l == 0:
            (w_up_b[0],) = rounded
        return proj_p, proj_s

    def mix_a(gr, proj, l, round_mlp_weights):
        bt, t, m = gr.bt, gr.t, gr.m
        proj3 = proj.reshape(bt, t, -1)
        h_prev = gr.ssm_prev[l].reshape(bt, SSM_GROUPS, -1, SSM_D_STATE)
        ssm_params = (conv_w[l], conv_b[l], dt_bias[l], a_log[l], d_skip[l], gate_norm_g[l])
        if gr.t_real == 1:
            y, c_new, h_new = ssd_step(proj3, col_xbc, col_dt, col_z, *ssm_params, gr.conv_prev[l], h_prev)
        else:
            assert t % SSD_CHUNK == 0 and gr.t_real == t
            ssd = ssd_mixer(proj3, col_xbc, col_dt, col_z, gr.t_real, *ssm_params, gr.conv_prev[l], h_prev,
                            cast=((w_down, 0),) if round_mlp_weights else ())
            if round_mlp_weights:
                ssd, (w_down_b[0],) = ssd
            y, c_new, h_new = ssd
        gr.conv_new.append(c_new)
        gr.ssm_new.append(h_new.reshape(bt, n_heads, SSM_HEAD_DIM, SSM_D_STATE))
        mo = mem_attention(proj3, col_qm_a, *gr.mem_kv(l)).reshape(m, mem_w)
        return y.reshape(m, d_inner), mo

    def shared_kv(prompt, sample):
        hg = gain_row(2 * kv_w, [(0, k_norm_g, DIL_HEADS)])
        kv_p, kv_s, _ = norm_proj(prompt.x2, kv_norm_g, w_kv, 0, hg, kinds_kv, prompt.cos, prompt.sin,
                                  extra=(sample.x2, sample.cos, sample.sin))
        prompt.kv3 = kv_p.reshape(prompt.bt, prompt.t, 2 * kv_w)
        sample.kv3 = kv_s.reshape(sample.bt, sample.t, 2 * kv_w)

    def in_proj_b(prompt, sample, l):
        j = l - n_a
        hg = gain_row(dil_q_w + mem_w, [(0, q_norm_g[j], dil_q_w // HEAD_DIM),
                                        (dil_q_w, mem_q_norm_g[l], MEM_HEADS)])
        proj_p, proj_s, _ = norm_proj(prompt.x2, norm_mix_g[l], w_b, j, hg, kinds_b, prompt.cos, prompt.sin,
                                      extra=(sample.x2, sample.cos, sample.sin))
        return proj_p, proj_s

    def mix_b(gr, proj, l):
        proj3 = proj.reshape(gr.bt, gr.t, -1)
        if gr.k_past is None:
            att = dilated_attention(proj3, gr.kv3)
        else:
            att = dilated_decode(proj3, gr.kv3, gr.k_past, gr.v_past)
        mo = mem_attention(proj3, dil_q_w, *gr.mem_kv(l)).reshape(gr.m, mem_w)
        return att.reshape(gr.m, kv_w), mo

    bp, t_p, _ = x_prompt.shape
    mlen = mem_prompt.shape[1]
    mem2 = mem_prompt.reshape(bp * mlen, d_model)
    mkv_p = []
    for l in range(depth):
        hg = gain_row(2 * mem_w, [(0, mem_k_norm_g[l], MEM_HEADS)])
        mkv_p.append(norm_proj(mem2, norm_mem_g[l], w_mkv, l, hg, kinds_mkv)[0].reshape(bp, mlen, 2 * mem_w))
    conv0 = jnp.zeros((n_a, bp, CONV_W - 1, conv_dim), F32)
    ssm0 = jnp.zeros((n_a, bp, n_heads, SSM_HEAD_DIM, SSM_D_STATE), F32)
    prompt = Group(x_prompt, jnp.arange(t_p, dtype=jnp.int32), conv0, ssm0,
                   lambda l: (mkv_p[l], mkv_p[l], (0, mem_w)), None, None, t_p)

    bs, t_s, _ = x_sample.shape
    assert t_s == 1
    xs = jnp.pad(x_sample, ((0, 0), (0, SAMPLE_ROWS - t_s), (0, 0)))
    pos_s = jnp.full((bs * SAMPLE_ROWS,), PAST_LEN, jnp.int32)
    sample = Group(xs, pos_s, state_conv, state_ssm, lambda l: (cache_mem_k, cache_mem_v, l),
                   cache_win_k, cache_win_v, t_s)

    for l in range(depth):
        if l < n_a:
            proj_p, proj_s = in_proj_a(prompt, sample, l)
            a_p, b_p = mix_a(prompt, proj_p, l, round_mlp_weights=(l == 0))
            a_s, b_s = mix_a(sample, proj_s, l, round_mlp_weights=False)
            w_out, lw = w_out_a_b, l
        else:
            if l == n_a:
                shared_kv(prompt, sample)
            proj_p, proj_s = in_proj_b(prompt, sample, l)
            a_p, b_p = mix_b(prompt, proj_p, l)
            a_s, b_s = mix_b(sample, proj_s, l)
            w_out, lw = w_out_b_b, l - n_a
        nxt = ((w_up, l + 1), (w_down, l + 1)) if l + 1 < depth else ()
        prompt.x2, sample.x2, rounded = mix_mlp(prompt.x2, a_p, b_p, w_out, lw, norm_mlp_g[l], w_up_b[l], w_down_b[l],
                                                cast=nxt, extra=(sample.x2, a_s, b_s))
        if nxt:
            w_up_b[l + 1], w_down_b[l + 1] = rounded

    y_p = prompt.x2.reshape(bp, t_p, d_model)
    conv_p, ssm_p, kv_p = jnp.stack(prompt.conv_new), jnp.stack(prompt.ssm_new), prompt.kv3
    mkv_all = jnp.stack(mkv_p)
    mem_k_p = mkv_all[..., :mem_w].reshape(depth, bp, mlen, MEM_HEADS, HEAD_DIM)
    mem_v_p = mkv_all[..., mem_w:].reshape(depth, bp, mlen, MEM_HEADS, HEAD_DIM)
    keep = min(max(w for w, _ in DIL_GROUPS), t_p)
    win_k_p = kv_p[:, t_p - keep:, :kv_w].reshape(bp, keep, DIL_HEADS, HEAD_DIM)
    win_v_p = kv_p[:, t_p - keep:, kv_w:].reshape(bp, keep, DIL_HEADS, HEAD_DIM)

    y_s = sample.x2.reshape(bs, SAMPLE_ROWS, d_model)
    conv_s, ssm_s, kv_s = jnp.stack(sample.conv_new), jnp.stack(sample.ssm_new), sample.kv3
    y_s = y_s[:, :t_s]
    k_s = kv_s[:, :t_s, :kv_w].reshape(bs, t_s, DIL_HEADS, HEAD_DIM)
    v_s = kv_s[:, :t_s, kv_w:].reshape(bs, t_s, DIL_HEADS, HEAD_DIM)

    return (y_p, y_s, conv_p, ssm_p, win_k_p, win_v_p, mem_k_p, mem_v_p, conv_s, ssm_s, k_s, v_s)
```

```python
import functools

import jax
import jax.numpy as jnp
from jax import lax
from jax.experimental import pallas as pl
from jax.experimental.pallas import tpu as pltpu

F32 = jnp.float32
BF16 = jnp.bfloat16

HEAD_DIM = 128
SSM_HEAD_DIM = 64
SSM_GROUPS = 4
SSM_D_STATE = 128
CONV_W = 4
SSD_CHUNK = 128
MEM_HEADS = 4
DIL_GROUPS = ((128, 1), (512, 4), (2048, 16))
DIL_HEADS = 4
PAST_LEN = 16384
ROPE_THETA = 10000.0
EPS = 1e-6
LOG2_E = 1.4426950408889634
LN_2 = 0.6931471805599453

LANES = 128
SUBLANES = 8
VMEM_LIMIT = 48 * 1024 * 1024
MIX_MLP_VMEM_LIMIT = 56 * 1024 * 1024
COL_TILE = 512
ROW_TILE = 512
PROJ_OUT_BLOCK_BYTES = 5 * 1024 * 1024
SAMPLE_ROWS = SUBLANES
DIL_UNROLL = 16

NT_DIMS = (((1,), (1,)), ((), ()))
RESIDENT = pl.Buffered(1)


def _cparams(*sem):
    return pltpu.CompilerParams(dimension_semantics=sem, vmem_limit_bytes=VMEM_LIMIT)


def _sigmoid(x):
    return 1.0 / (1.0 + jnp.exp(-x))


def _rms_scale(x):
    return lax.rsqrt(jnp.mean(x * x, axis=-1, keepdims=True) + EPS)


class _SideCast:
    def __init__(self, mats, steps, step_of):
        self.n = len(mats)
        self.shapes = [a.shape[1:] for a, _ in mats]
        self.views, self.in_specs, self.out_specs, self.out_shape = [], [], [], []
        for a, layer in mats:
            n_layers, r, c = a.shape
            assert r % (steps * 2 * SUBLANES) == 0
            band = r // steps
            self.views.append(a.reshape(n_layers, steps, band, c))
            self.in_specs.append(pl.BlockSpec((None, None, band, c),
                                              lambda *ids, layer=layer: (layer, step_of(*ids), 0, 0)))
            self.out_specs.append(pl.BlockSpec((None, band, c), lambda *ids: (step_of(*ids), 0, 0)))
            self.out_shape.append(jax.ShapeDtypeStruct((steps, band, c), BF16))

    def wrap(self, body, n_in, n_out):
        n = self.n
        if not n:
            return body

        def kern(*refs):
            ins, srcs = refs[:n_in], refs[n_in:n_in + n]
            outs, dsts = refs[n_in + n:n_in + n + n_out], refs[n_in + n + n_out:n_in + 2 * n + n_out]
            for s, d in zip(srcs, dsts):
                d[...] = s[...].astype(BF16)
            body(*ins, *outs, *refs[n_in + 2 * n + n_out:])
        return kern

    def split(self, outs, n_out):
        main = outs[0] if n_out == 1 else tuple(outs[:n_out])
        if not self.n:
            return main
        return main, [o.reshape(s) for o, s in zip(outs[n_out:], self.shapes)]


def _round_kernel(src_ref, dst_ref):
    dst_ref[...] = src_ref[...].astype(BF16)


def round_weight(w, n_rows):
    n_layers, _, c = w.shape
    band = ROW_TILE
    assert n_rows % band == 0
    return pl.pallas_call(
        _round_kernel,
        grid=(n_layers, n_rows // band),
        in_specs=[pl.BlockSpec((None, band, c), lambda l, i: (l, i, 0))],
        out_specs=pl.BlockSpec((None, band, c), lambda l, i: (l, i, 0)),
        out_shape=jax.ShapeDtypeStruct((n_layers, n_rows, c), BF16),
        compiler_params=_cparams("parallel", "parallel"),
        name="round_weight",
    )(w)


def _rope_table_kernel(pos_ref, inv_ref, cos_ref, sin_ref):
    ang = pos_ref[...] * inv_ref[...]
    lane = lax.broadcasted_iota(jnp.int32, ang.shape, 1)
    cos_ref[...] = jnp.cos(ang)
    s = jnp.sin(ang)
    sin_ref[...] = jnp.where(lane < HEAD_DIM // 2, -s, s)


def rope_tables(pos):
    r = pos.shape[0]
    half = HEAD_DIM // 2
    inv = ROPE_THETA ** (-jnp.arange(half, dtype=F32) * (2.0 / HEAD_DIM))
    inv = jnp.concatenate([inv, inv])[None, :]
    posf = jnp.broadcast_to(pos.astype(F32)[:, None], (r, HEAD_DIM))
    tr = min(r, ROW_TILE)
    return pl.pallas_call(
        _rope_table_kernel,
        grid=(r // tr,),
        in_specs=[pl.BlockSpec((tr, HEAD_DIM), lambda i: (i, 0)),
                  pl.BlockSpec((1, HEAD_DIM), lambda i: (0, 0))],
        out_specs=[pl.BlockSpec((tr, HEAD_DIM), lambda i: (i, 0))] * 2,
        out_shape=[jax.ShapeDtypeStruct((r, HEAD_DIM), F32)] * 2,
        compiler_params=_cparams("parallel"),
        name="rope_tables",
    )(posf, inv)


def _proj_kernel(*refs, kinds, tile_src, n_w, transposed, has_extra):
    rope = "rope" in kinds
    refs = list(refs)
    x_ref, g_ref = refs[:2]
    w_refs = refs[2:2 + n_w]
    hg_ref = refs[2 + n_w]
    at = 3 + n_w
    if rope:
        cos_ref, sin_ref = refs[at:at + 2]
        at += 2
    if has_extra:
        xs_ref = refs[at]
        at += 1
        if rope:
            cos_s_ref, sin_s_ref = refs[at:at + 2]
            at += 2
        o_ref, os_ref, xn_ref = refs[at:at + 3]
    else:
        o_ref, xn_ref = refs[at:at + 2]
    tm = x_ref.shape[0]

    def normed(x):
        return (x * _rms_scale(x) * g_ref[...]).astype(BF16)

    def tiles(xn, cos, sin, store):
        for t, kind in enumerate(kinds):
            which, col, width = tile_src[t]
            at_col = sum(w for _, _, w in tile_src[:t])
            if transposed:
                y = lax.dot_general(xn, w_refs[which][col:col + width, :], NT_DIMS, preferred_element_type=F32)
            else:
                y = jnp.dot(xn, w_refs[which][:, col:col + width], preferred_element_type=F32)
            if kind == "plain":
                store(slice(at_col, at_col + width), y)
                continue
            for j in range(width // HEAD_DIM):
                osl = slice(at_col + j * HEAD_DIM, at_col + (j + 1) * HEAD_DIM)
                yh = y[:, j * HEAD_DIM:(j + 1) * HEAD_DIM]
                yh = yh * _rms_scale(yh) * hg_ref[:, osl]
                if kind == "rope":
                    yh = yh * cos + pltpu.roll(yh, HEAD_DIM // 2, axis=1) * sin
                store(osl, yh)

    def store_main(cols, y):
        o_ref[:, cols] = y

    def store_both(cols, y):
        o_ref[:, cols] = y[:tm]
        os_ref[:, cols] = y[tm:]

    xn_ref[:tm, :] = normed(x_ref[...])
    cos = cos_ref[...] if rope else None
    sin = sin_ref[...] if rope else None
    if not has_extra:
        tiles(xn_ref[...], cos, sin, store_main)
        return

    @pl.when(pl.program_id(0) == 0)
    def _():
        xn_ref[tm:, :] = normed(xs_ref[...])
        cos_all = jnp.concatenate([cos, cos_s_ref[...]], axis=0) if rope else None
        sin_all = jnp.concatenate([sin, sin_s_ref[...]], axis=0) if rope else None
        tiles(xn_ref[...], cos_all, sin_all, store_both)

    @pl.when(pl.program_id(0) != 0)
    def _():
        tiles(xn_ref[:tm, :], cos, sin, store_main)


def norm_proj(x, g, ws, layer, head_gain, kinds, cos=None, sin=None, cast=(), tile_src=None, transposed=False,
              extra=None):
    m, d = x.shape
    ws = list(ws) if isinstance(ws, (list, tuple)) else [ws]
    if tile_src is None:
        tile_src = [(0, t * COL_TILE, COL_TILE) for t in range(len(kinds))]
    n_cols = sum(width for _, _, width in tile_src)
    assert len(tile_src) == len(kinds) and head_gain.shape == (1, n_cols)
    assert all(width % HEAD_DIM == 0 for _, _, width in tile_src)
    rope = "rope" in kinds
    has_extra = extra is not None
    ms = extra[0].shape[0] if has_extra else 0
    tm = min(m, ROW_TILE)
    while tm * n_cols * 4 > PROJ_OUT_BLOCK_BYTES and tm % (2 * SUBLANES) == 0:
        tm //= 2
    assert m % tm == 0
    in_specs = [pl.BlockSpec((tm, d), lambda i: (i, 0)),
                pl.BlockSpec((1, d), lambda i: (0, 0))]
    for k, w in enumerate(ws):
        used = max(col + width for which, col, width in tile_src if which == k)
        if transposed:
            assert used <= w.shape[1] and w.shape[2] == d
            in_specs.append(pl.BlockSpec((None, used, d), lambda i: (layer, 0, 0), pipeline_mode=RESIDENT))
            continue
        assert used <= w.shape[2] and (used % LANES == 0 or used == w.shape[2])
        in_specs.append(pl.BlockSpec((None, d, used), lambda i: (layer, 0, 0), pipeline_mode=RESIDENT))
    in_specs.append(pl.BlockSpec((1, n_cols), lambda i: (0, 0)))
    args = [x, g.reshape(1, d), *ws, head_gain]
    if rope:
        assert cos.shape[0] % tm == 0
        tab_blocks = cos.shape[0] // tm
        in_specs += [pl.BlockSpec((tm, HEAD_DIM), lambda i: (i % tab_blocks, 0))] * 2
        args += [cos, sin]
    out_specs = [pl.BlockSpec((tm, n_cols), lambda i: (i, 0))]
    out_shape = [jax.ShapeDtypeStruct((m, n_cols), F32)]
    if has_extra:
        whole = lambda a: pl.BlockSpec(a.shape, lambda i: (0, 0))
        xs, cos_s, sin_s = extra
        extras = [xs] + ([cos_s, sin_s] if rope else [])
        in_specs += [whole(a) for a in extras]
        args += extras
        out_specs.append(pl.BlockSpec((ms, n_cols), lambda i: (0, 0)))
        out_shape.append(jax.ShapeDtypeStruct((ms, n_cols), F32))
    n_out = len(out_specs)
    side = _SideCast(cast, m // tm, lambda i: i)
    outs = pl.pallas_call(
        side.wrap(functools.partial(_proj_kernel, kinds=tuple(kinds), tile_src=tuple(tile_src), n_w=len(ws),
                                    transposed=transposed, has_extra=has_extra),
                  len(args), n_out),
        grid=(m // tm,),
        in_specs=in_specs + side.in_specs,
        out_specs=out_specs + side.out_specs,
        out_shape=out_shape + side.out_shape,
        scratch_shapes=[pltpu.VMEM((tm + ms, d), BF16)],
        compiler_params=_cparams("arbitrary"),
        name="norm_proj",
    )(*args, *side.views)
    casts = [o.reshape(shp) for o, shp in zip(outs[n_out:], side.shapes)]
    return outs[0], (outs[1] if has_extra else None), casts


def _mix_mlp_kernel(*refs, has_extra):
    if has_extra:
        (x_ref, g_ref, wu_ref, wd_ref, a_ref, b_ref, wa_ref, wb_ref, xs_ref, as_ref, bs_ref,
         o_ref, os_ref, xn_ref) = refs
    else:
        x_ref, g_ref, wu_ref, wd_ref, a_ref, b_ref, wa_ref, wb_ref, o_ref, xn_ref = refs
    i, f = pl.program_id(0), pl.program_id(1)
    tm = x_ref.shape[0]

    def normed(x):
        return (x * _rms_scale(x) * g_ref[...]).astype(BF16)

    def mixed(a, b):
        acc = jnp.dot(a, wa_ref[...], preferred_element_type=F32)
        return acc + jnp.dot(b, wb_ref[...], preferred_element_type=F32)

    def ffn(xn):
        h = jnp.dot(xn, wu_ref[...], preferred_element_type=F32)
        h = jnp.maximum(h, 0.0)
        h = (h * h).astype(BF16)
        return jnp.dot(h, wd_ref[...], preferred_element_type=F32)

    def start_main():
        x1 = x_ref[...] + mixed(a_ref[...], b_ref[...])
        xn_ref[:tm, :] = normed(x1)
        o_ref[...] = x1

    def start_both():
        acc = mixed(jnp.concatenate([a_ref[...], as_ref[...]], axis=0),
                    jnp.concatenate([b_ref[...], bs_ref[...]], axis=0))
        x1 = x_ref[...] + acc[:tm]
        xs1 = xs_ref[...] + acc[tm:]
        xn_ref[:tm, :] = normed(x1)
        xn_ref[tm:, :] = normed(xs1)
        o_ref[...] = x1
        os_ref[...] = xs1

    if not has_extra:
        pl.when(f == 0)(start_main)
        o_ref[...] += ffn(xn_ref[...])
        return

    @pl.when(i == 0)
    def _():
        pl.when(f == 0)(start_both)
        upd = ffn(xn_ref[...])
        o_ref[...] += upd[:tm]
        os_ref[...] += upd[tm:]

    @pl.when(i != 0)
    def _():
        pl.when(f == 0)(start_main)
        o_ref[...] += ffn(xn_ref[:tm, :])


def mix_mlp(x, a, b, w_out, layer, g, wu, wd, tf=1024, cast=(), extra=None):
    m, d = x.shape
    ff = wu.shape[1]
    tm = min(m, ROW_TILE)
    nf = ff // tf
    ka, kb = a.shape[1], b.shape[1]
    assert w_out.shape[1] == ka + kb and ka % kb == 0
    has_extra = extra is not None
    side = _SideCast(cast, (m // tm) * nf, lambda i, j: i * nf + j)
    in_specs = [pl.BlockSpec((tm, d), lambda i, j: (i, 0)),
                pl.BlockSpec((1, d), lambda i, j: (0, 0)),
                pl.BlockSpec((d, tf), lambda i, j: (0, j)),
                pl.BlockSpec((tf, d), lambda i, j: (j, 0)),
                pl.BlockSpec((tm, ka), lambda i, j: (i, 0)),
                pl.BlockSpec((tm, kb), lambda i, j: (i, 0)),
                pl.BlockSpec((None, ka, d), lambda i, j: (layer, 0, 0), pipeline_mode=RESIDENT),
                pl.BlockSpec((None, kb, d), lambda i, j: (layer, ka // kb, 0), pipeline_mode=RESIDENT)]
    out_specs = [pl.BlockSpec((tm, d), lambda i, j: (i, 0))]
    out_shape = [jax.ShapeDtypeStruct((m, d), F32)]
    args = [x, g.reshape(1, d), wu, wd, a, b, w_out, w_out]
    ms = extra[0].shape[0] if has_extra else 0
    if has_extra:
        in_specs += [pl.BlockSpec(e.shape, lambda i, j: (0, 0)) for e in extra]
        args += list(extra)
        out_specs.append(pl.BlockSpec((ms, d), lambda i, j: (0, 0)))
        out_shape.append(jax.ShapeDtypeStruct((ms, d), F32))
    n_out = len(out_specs)
    outs = pl.pallas_call(
        side.wrap(functools.partial(_mix_mlp_kernel, has_extra=has_extra), len(args), n_out),
        grid=(m // tm, nf),
        in_specs=in_specs + side.in_specs,
        out_specs=out_specs + side.out_specs,
        out_shape=out_shape + side.out_shape,
        scratch_shapes=[pltpu.VMEM((tm + ms, d), BF16)],
        compiler_params=pltpu.CompilerParams(dimension_semantics=("arbitrary", "arbitrary"),
                                             vmem_limit_bytes=MIX_MLP_VMEM_LIMIT),
        name="mix_mlp",
    )(*args, *side.views)
    casts = [o.reshape(shp) for o, shp in zip(outs[n_out:], side.shapes)]
    return outs[0], (outs[1] if has_extra else None), casts


def _split3(x):
    hi = x.astype(BF16)
    r1 = x - hi.astype(F32)
    mid = r1.astype(BF16)
    lo = (r1 - mid.astype(F32)).astype(BF16)
    return hi, mid, lo


def _ssd_kernel(xbc_ref, dt_ref, z_ref, cw_ref, cb_ref, dtb_ref, alog_ref, dsk_ref, gg_ref, tri_ref, expand_ref,
                conv0_ref, h0_ref, y_ref, convo_ref, ho_ref,
                ext_ref, ht_ref, yd_ref, st_ref, *, t_real, n_heads, d_inner):
    L = SSD_CHUNK
    P = SSM_HEAD_DIM
    N = SSM_D_STATE
    G = SSM_GROUPS
    hpg = n_heads // G
    gw = hpg * P
    hist = CONV_W - 1
    base = SUBLANES - hist
    c = pl.program_id(1)
    nc = pl.num_programs(1)

    @pl.when(c == 0)
    def _():
        ext_ref[base:SUBLANES, :] = conv0_ref[...]
        for g in range(G):
            ht_ref[g] = h0_ref[g].T

    ext_ref[SUBLANES:SUBLANES + L, :] = xbc_ref[...]
    ext = ext_ref[...]
    acc = cb_ref[...] + ext[SUBLANES:, :] * cw_ref[hist:CONV_W, :]
    for k in range(1, CONV_W):
        acc = acc + pltpu.roll(ext, k, axis=0)[SUBLANES:, :] * cw_ref[hist - k:CONV_W - k, :]
    xc = acc * _sigmoid(acc)

    @pl.when(c == nc - 1)
    def _():
        n_in_last = t_real - ((t_real - 1) // L) * L
        convo_ref[...] = ext_ref[base + n_in_last:base + n_in_last + hist, :]

    ext_ref[base:SUBLANES, :] = ext_ref[base + L:SUBLANES + L, :]

    dt_raw = dt_ref[...] + dtb_ref[...]
    dt = jnp.maximum(dt_raw, 0.0) + jnp.log1p(jnp.exp(-jnp.abs(dt_raw)))
    row = lax.broadcasted_iota(jnp.int32, (L, LANES), 0) + c * L
    dt = jnp.where(row < t_real, dt, 0.0)
    da = dt * (-jnp.exp(alog_ref[...]))
    li = lax.broadcasted_iota(jnp.int32, (L, L), 0)
    si = lax.broadcasted_iota(jnp.int32, (L, L), 1)
    causal = li >= si
    a_cs = sum(jnp.dot(tri_ref[...], part, preferred_element_type=F32) for part in _split3(da))
    a_cs_t = a_cs.T
    dt_t = dt.T
    w_t = dt_t * jnp.exp(a_cs_t[:, L - 1:L] - a_cs_t)
    e_exp = sum(jnp.dot(part, expand_ref[...], preferred_element_type=F32) for part in _split3(jnp.exp(a_cs)))
    xb = xc[:, :d_inner].astype(BF16)
    low_half = lax.broadcasted_iota(jnp.int32, (L, 2 * P), 1) < P

    for g in range(G):
        b_g = xc[:, d_inner + g * N:d_inner + (g + 1) * N]
        c_g = xc[:, d_inner + G * N + g * N:d_inner + G * N + (g + 1) * N].astype(BF16)
        cbm = lax.dot_general(c_g, b_g.astype(BF16), NT_DIMS, preferred_element_type=F32)
        b_t = b_g.T
        y_off = jnp.dot(c_g, ht_ref[g].astype(BF16), preferred_element_type=F32)
        for kp in range(hpg // 2):
            h0 = g * hpg + 2 * kp
            x_pair = xb[:, h0 * P:(h0 + 2) * P]
            yd, st = [], []
            for h in (h0, h0 + 1):
                seg = a_cs[:, h:h + 1] - a_cs_t[h:h + 1, :]
                dec = jnp.exp(jnp.where(causal, seg, -jnp.inf))
                mh = (cbm * dec * dt_t[h:h + 1, :]).astype(BF16)
                yd.append(jnp.dot(mh, x_pair, preferred_element_type=F32))
                btw = (b_t * w_t[h:h + 1, :]).astype(BF16)
                st.append(jnp.dot(btw, x_pair, preferred_element_type=F32))
            yd_ref[:, 2 * kp * P:(2 * kp + 2) * P] = jnp.where(low_half, yd[0], yd[1])
            st_ref[:, 2 * kp * P:(2 * kp + 2) * P] = jnp.where(low_half, st[0], st[1])
        gs = slice(g * gw, (g + 1) * gw)
        e_g = e_exp[:, gs]
        y_g = yd_ref[...] + y_off * e_g + dsk_ref[:, gs] * xc[:, gs]
        ht_ref[g] = ht_ref[g] * e_g[L - 1:L, :] + st_ref[...]
        z_g = z_ref[:, gs]
        gated = y_g * (z_g * _sigmoid(z_g))
        y_ref[:, gs] = (gated * _rms_scale(gated) * gg_ref[:, gs]).astype(y_ref.dtype)

    @pl.when(c == nc - 1)
    def _():
        for g in range(G):
            ho_ref[g] = ht_ref[g].T


def ssd_mixer(proj, col_xbc, col_dt, col_z, t_real, conv_w, conv_b, dt_bias, a_log, d_skip, gate_g,
              conv_prev, h_prev, cast=()):
    bt, tp, _ = proj.shape
    conv_dim = conv_w.shape[1]
    n_heads = dt_bias.shape[0]
    d_inner = n_heads * SSM_HEAD_DIM
    gw = d_inner // SSM_GROUPS
    L = SSD_CHUNK
    assert tp % L == 0 and (tp - t_real) < L
    assert col_xbc % conv_dim == 0 and col_dt % LANES == 0 and col_z % d_inner == 0
    pad = LANES - n_heads
    dtb = jnp.pad(dt_bias, (0, pad)).reshape(1, LANES)
    alog = jnp.pad(a_log, (0, pad)).reshape(1, LANES)
    dsk = jnp.repeat(d_skip, SSM_HEAD_DIM).reshape(1, d_inner)
    tri = jnp.tril(jnp.ones((L, L), BF16))
    expand = jnp.repeat(jnp.eye(LANES, dtype=BF16)[:, :n_heads], SSM_HEAD_DIM, axis=1)
    kern = functools.partial(_ssd_kernel, t_real=t_real, n_heads=n_heads, d_inner=d_inner)
    vec = lambda n: pl.BlockSpec((1, n), lambda b, c: (0, 0))
    nc = tp // L
    side = _SideCast(cast, bt * nc, lambda b, c: b * nc + c)
    outs = pl.pallas_call(
        side.wrap(kern, 13, 3),
        grid=(bt, nc),
        in_specs=[pl.BlockSpec((None, L, conv_dim), lambda b, c: (b, c, col_xbc // conv_dim)),
                  pl.BlockSpec((None, L, LANES), lambda b, c: (b, c, col_dt // LANES)),
                  pl.BlockSpec((None, L, d_inner), lambda b, c: (b, c, col_z // d_inner)),
                  pl.BlockSpec((CONV_W, conv_dim), lambda b, c: (0, 0)),
                  vec(conv_dim), vec(LANES), vec(LANES), vec(d_inner), vec(d_inner),
                  pl.BlockSpec((L, L), lambda b, c: (0, 0)),
                  pl.BlockSpec((LANES, d_inner), lambda b, c: (0, 0)),
                  pl.BlockSpec((None, CONV_W - 1, conv_dim), lambda b, c: (b, 0, 0)),
                  pl.BlockSpec((None, SSM_GROUPS, gw, SSM_D_STATE), lambda b, c: (b, 0, 0, 0))] + side.in_specs,
        out_specs=[pl.BlockSpec((None, L, d_inner), lambda b, c: (b, c, 0)),
                   pl.BlockSpec((None, CONV_W - 1, conv_dim), lambda b, c: (b, 0, 0)),
                   pl.BlockSpec((None, SSM_GROUPS, gw, SSM_D_STATE), lambda b, c: (b, 0, 0, 0))] + side.out_specs,
        out_shape=[jax.ShapeDtypeStruct((bt, tp, d_inner), BF16),
                   jax.ShapeDtypeStruct((bt, CONV_W - 1, conv_dim), F32),
                   jax.ShapeDtypeStruct((bt, SSM_GROUPS, gw, SSM_D_STATE), F32)] + side.out_shape,
        scratch_shapes=[pltpu.VMEM((SUBLANES + L, conv_dim), F32),
                        pltpu.VMEM((SSM_GROUPS, SSM_D_STATE, gw), F32),
                        pltpu.VMEM((L, gw), F32),
                        pltpu.VMEM((SSM_D_STATE, gw), F32)],
        compiler_params=_cparams("parallel", "arbitrary"),
        name="ssd_mixer",
    )(proj, proj, proj, conv_w, conv_b.reshape(1, conv_dim), dtb, alog, dsk, gate_g.reshape(1, d_inner),
      tri, expand, conv_prev, h_prev, *side.views)
    return side.split(outs, 3)


def _ssd_step_kernel(xbc_ref, dt_ref, z_ref, cw_ref, cb_ref, dtb_ref, alog_ref, dsk_ref, gg_ref, expand_ref,
                     conv0_ref, h0_ref, y_ref, convo_ref, ho_ref, *, n_heads, d_inner):
    P, N, G = SSM_HEAD_DIM, SSM_D_STATE, SSM_GROUPS
    gw = n_heads // G * P
    hist = CONV_W - 1
    acc = cb_ref[...] + xbc_ref[...] * cw_ref[hist:CONV_W, :]
    for k in range(hist):
        acc = acc + conv0_ref[k:k + 1, :] * cw_ref[k:k + 1, :]
    xc = acc * _sigmoid(acc)
    convo_ref[0:hist - 1, :] = conv0_ref[1:hist, :]
    convo_ref[hist - 1:hist, :] = xbc_ref[0:1, :]

    dt_raw = dt_ref[...] + dtb_ref[...]
    dt = jnp.maximum(dt_raw, 0.0) + jnp.log1p(jnp.exp(-jnp.abs(dt_raw)))
    decay = jnp.exp(dt * (-jnp.exp(alog_ref[...])))
    expand = lambda v: sum(jnp.dot(part, expand_ref[...], preferred_element_type=F32) for part in _split3(v))
    dt_x = expand(dt) * xc[:, :d_inner]
    decay_c = expand(decay)

    for g in range(G):
        gs = slice(g * gw, (g + 1) * gw)
        b_col = xc[:, d_inner + g * N:d_inner + (g + 1) * N].T[:, 0:1]
        c_g = xc[:, d_inner + G * N + g * N:d_inner + G * N + (g + 1) * N].astype(BF16)
        h_t = h0_ref[g].T * decay_c[0:1, gs] + b_col * dt_x[0:1, gs]
        ho_ref[g] = h_t.T
        y_g = jnp.dot(c_g, h_t.astype(BF16), preferred_element_type=F32) + dsk_ref[:, gs] * xc[:, gs]
        z_g = z_ref[:, gs]
        gated = y_g * (z_g * _sigmoid(z_g))
        y_ref[:, gs] = (gated * _rms_scale(gated) * gg_ref[:, gs]).astype(y_ref.dtype)


def ssd_step(proj, col_xbc, col_dt, col_z, conv_w, conv_b, dt_bias, a_log, d_skip, gate_g, conv_prev, h_prev):
    bt, rows, _ = proj.shape
    conv_dim = conv_w.shape[1]
    n_heads = dt_bias.shape[0]
    d_inner = n_heads * SSM_HEAD_DIM
    gw = d_inner // SSM_GROUPS
    assert col_xbc % conv_dim == 0 and col_dt % LANES == 0 and col_z % d_inner == 0
    pad = LANES - n_heads
    dtb = jnp.pad(dt_bias, (0, pad)).reshape(1, LANES)
    alog = jnp.pad(a_log, (0, pad)).reshape(1, LANES)
    dsk = jnp.repeat(d_skip, SSM_HEAD_DIM).reshape(1, d_inner)
    expand = jnp.repeat(jnp.eye(LANES, dtype=BF16)[:, :n_heads], SSM_HEAD_DIM, axis=1)
    vec = lambda n: pl.BlockSpec((1, n), lambda b: (0, 0))
    state = pl.BlockSpec((None, SSM_GROUPS, gw, SSM_D_STATE), lambda b: (b, 0, 0, 0))
    hist_rows = pl.BlockSpec((None, CONV_W - 1, conv_dim), lambda b: (b, 0, 0))
    return pl.pallas_call(
        functools.partial(_ssd_step_kernel, n_heads=n_heads, d_inner=d_inner),
        grid=(bt,),
        in_specs=[pl.BlockSpec((None, rows, conv_dim), lambda b: (b, 0, col_xbc // conv_dim)),
                  pl.BlockSpec((None, rows, LANES), lambda b: (b, 0, col_dt // LANES)),
                  pl.BlockSpec((None, rows, d_inner), lambda b: (b, 0, col_z // d_inner)),
                  pl.BlockSpec((CONV_W, conv_dim), lambda b: (0, 0)),
                  vec(conv_dim), vec(LANES), vec(LANES), vec(d_inner), vec(d_inner),
                  pl.BlockSpec((LANES, d_inner), lambda b: (0, 0)),
                  hist_rows, state],
        out_specs=[pl.BlockSpec((None, rows, d_inner), lambda b: (b, 0, 0)), hist_rows, state],
        out_shape=[jax.ShapeDtypeStruct((bt, rows, d_inner), BF16),
                   jax.ShapeDtypeStruct((bt, CONV_W - 1, conv_dim), F32),
                   jax.ShapeDtypeStruct((bt, SSM_GROUPS, gw, SSM_D_STATE), F32)],
        compiler_params=_cparams("parallel"),
        name="ssd_step",
    )(proj, proj, proj, conv_w, conv_b.reshape(1, conv_dim), dtb, alog, dsk, gate_g.reshape(1, d_inner),
      expand, conv_prev, h_prev)


def _mem_attn_kernel(q_ref, k_ref, v_ref, o_ref):
    scale = HEAD_DIM ** -0.5
    headed = len(k_ref.shape) == 3
    for h in range(MEM_HEADS):
        sl = slice(h * HEAD_DIM, (h + 1) * HEAD_DIM)
        q = (q_ref[:, sl] * (scale * LOG2_E)).astype(BF16)
        k = (k_ref[:, h, :] if headed else k_ref[:, sl]).astype(BF16)
        v = (v_ref[:, h, :] if headed else v_ref[:, sl]).astype(BF16)
        s = lax.dot_general(q, k, NT_DIMS, preferred_element_type=F32)
        e = jnp.exp2(s - jnp.max(s, axis=-1, keepdims=True))
        den = jnp.sum(e, axis=-1, keepdims=True)
        o = jnp.dot(e.astype(BF16), v, preferred_element_type=F32) / den
        o_ref[:, sl] = o.astype(o_ref.dtype)


def mem_attention(proj, col_q, k, v, kv_at):
    bt, t, _ = proj.shape
    width = MEM_HEADS * HEAD_DIM
    tq = min(t, ROW_TILE)
    if k.ndim == 5:
        mlen = k.shape[2]
        kv_specs = [pl.BlockSpec((None, None, mlen, MEM_HEADS, HEAD_DIM), lambda b, i: (kv_at, b, 0, 0, 0))] * 2
    else:
        mlen = k.shape[1]
        kv_specs = [pl.BlockSpec((None, mlen, width), lambda b, i, c=c: (b, 0, c // width)) for c in kv_at]
    return pl.pallas_call(
        _mem_attn_kernel,
        grid=(bt, t // tq),
        in_specs=[pl.BlockSpec((None, tq, width), lambda b, i: (b, i, col_q // width))] + kv_specs,
        out_specs=pl.BlockSpec((None, tq, width), lambda b, i: (b, i, 0)),
        out_shape=jax.ShapeDtypeStruct((bt, t, width), BF16),
        compiler_params=_cparams("parallel", "parallel"),
        name="mem_attention",
    )(proj, k, v)


def _run_units(first, count, unit, unroll):
    trips = count // unroll
    if trips == 1:
        trips = 0
    if trips:
        def trip(it, carry):
            for k in range(unroll):
                unit(first + it * unroll + k)
            return carry
        lax.fori_loop(0, trips, trip, 0)
    for k in range(trips * unroll, count):
        unit(first + k)


def _dil_attn_kernel(*refs, groups, rb, unroll):
    ng = len(groups)
    q_refs = refs[:ng]
    kp_ref, kc_ref, vp_ref, vc_ref, o_ref = refs[ng:ng + 5]
    og = refs[ng + 5:2 * ng + 5]
    lg = refs[2 * ng + 5:3 * ng + 5]
    i = pl.program_id(1)
    scale = HEAD_DIM ** -0.5

    for gi, (win, d) in enumerate(groups):
        w = win // d
        dw = d * w
        qi = lax.broadcasted_iota(jnp.int32, (w, 2 * w), 0)
        kj = lax.broadcasted_iota(jnp.int32, (w, 2 * w), 1)
        band = (kj >= qi) & (kj <= qi + w)
        band_first = band & (kj >= jnp.where(i > 0, 0, w))

        def rows(start, size, d=d, w=w):
            if d == 1:
                return pl.ds(pl.multiple_of(start, w), size)
            return pl.ds(start, size, stride=d)

        def attend(rows_q, kk, vv, mask, gi=gi, w=w):
            q = (q_refs[gi][rows_q, :] * (scale * LOG2_E)).astype(BF16)
            s = lax.dot_general(q, kk.astype(BF16), NT_DIMS, preferred_element_type=F32)
            s = jnp.where(mask, s, -jnp.inf)
            m = jnp.max(s, axis=-1, keepdims=True)
            e = jnp.exp2(s - m)
            den = jnp.sum(e, axis=-1, keepdims=True)
            o = jnp.dot(e.astype(BF16), vv.astype(BF16), preferred_element_type=F32) / den
            og[gi][rows_q, :] = o
            lg[gi][rows_q, :] = jnp.broadcast_to(m * LN_2 + jnp.log(den), (w, HEAD_DIM))

        def first_unit(r, rows=rows, attend=attend, w=w, dw=dw, mask=band_first):
            rows_q, rows_p = rows(r, w), rows(rb - dw + r, w)
            kk = jnp.concatenate([kp_ref[rows_p, :], kc_ref[rows_q, :]], axis=0)
            vv = jnp.concatenate([vp_ref[rows_p, :], vc_ref[rows_q, :]], axis=0)
            attend(rows_q, kk, vv, mask)

        def later_unit(u, rows=rows, attend=attend, d=d, w=w, dw=dw, mask=band):
            sb = u // d
            start_q = sb * dw + (u - sb * d)
            rows_k = rows(start_q - dw, 2 * w)
            attend(rows(start_q, w), kc_ref[rows_k, :], vc_ref[rows_k, :], mask)

        _run_units(0, d, first_unit, unroll)
        _run_units(d, rb // w - d, later_unit, unroll)

    ls = [r[...] for r in lg]
    mm = functools.reduce(jnp.maximum, ls)
    ws = [jnp.exp(l - mm) for l in ls]
    num = sum(wg * r[...] for wg, r in zip(ws, og))
    o_ref[...] = (num / sum(ws)).astype(o_ref.dtype)


def dilated_attention(proj, kv):
    bt, t, _ = proj.shape
    ng = len(DIL_GROUPS)
    rb = min(t, max(win for win, _ in DIL_GROUPS))
    for win, d in DIL_GROUPS:
        assert win % d == 0 and rb % win == 0
    assert t % rb == 0
    blk = lambda f: pl.BlockSpec((None, rb, HEAD_DIM), f)
    prev = lambda i: jnp.maximum(i - 1, 0)
    q_specs = [blk(lambda b, i, h, g=g: (b, i, g * DIL_HEADS + h)) for g in range(ng)]
    kv_specs = [blk(lambda b, i, h: (b, prev(i), h)), blk(lambda b, i, h: (b, i, h)),
                blk(lambda b, i, h: (b, prev(i), DIL_HEADS + h)), blk(lambda b, i, h: (b, i, DIL_HEADS + h))]
    return pl.pallas_call(
        functools.partial(_dil_attn_kernel, groups=DIL_GROUPS, rb=rb, unroll=DIL_UNROLL),
        grid=(bt, t // rb, DIL_HEADS),
        in_specs=q_specs + kv_specs,
        out_specs=blk(lambda b, i, h: (b, i, h)),
        out_shape=jax.ShapeDtypeStruct((bt, t, DIL_HEADS * HEAD_DIM), BF16),
        scratch_shapes=[pltpu.VMEM((rb, HEAD_DIM), F32)] * (2 * ng),
        compiler_params=_cparams("parallel", "parallel", "parallel"),
        name="dilated_attention",
    )(*([proj] * ng), kv, kv, kv, kv)


def _dil_decode_kernel(*refs):
    ng = len(DIL_GROUPS)
    q_ref, kv_ref = refs[:2]
    k_refs, v_refs = refs[2:2 + ng], refs[2 + ng:2 + 2 * ng]
    o_ref = refs[-1]
    width = DIL_HEADS * HEAD_DIM
    scale = HEAD_DIM ** -0.5
    for h in range(DIL_HEADS):
        sl = slice(h * HEAD_DIM, (h + 1) * HEAD_DIM)
        k_new = kv_ref[:, h * HEAD_DIM:(h + 1) * HEAD_DIM]
        v_new = kv_ref[:, width + h * HEAD_DIM:width + (h + 1) * HEAD_DIM]
        outs, lses = [], []
        for gi in range(ng):
            kc = k_refs[gi][:, h, :].astype(BF16)
            vc = v_refs[gi][:, h, :].astype(BF16)
            q = q_ref[:, gi * width + h * HEAD_DIM:gi * width + (h + 1) * HEAD_DIM]
            s = lax.dot_general(q.astype(BF16), kc, NT_DIMS, preferred_element_type=F32) * scale
            s_self = jnp.sum(q * k_new, axis=-1, keepdims=True) * scale
            m = jnp.maximum(jnp.max(s, axis=-1, keepdims=True), s_self)
            e = jnp.exp(s - m)
            e_self = jnp.exp(s_self - m)
            den = jnp.sum(e, axis=-1, keepdims=True) + e_self
            o = jnp.dot(e.astype(BF16), vc, preferred_element_type=F32) + e_self * v_new
            outs.append(o / den)
            lses.append(m + jnp.log(den))
        mm = functools.reduce(jnp.maximum, lses)
        ws = [jnp.exp(l - mm) for l in lses]
        num = sum(wg * o for wg, o in zip(ws, outs))
        o_ref[:, sl] = (num / sum(ws)).astype(o_ref.dtype)


def dilated_decode(proj, kv_new, k_cache, v_cache):
    bt, rows, nq = proj.shape
    lc = k_cache.shape[1]
    width = DIL_HEADS * HEAD_DIM
    views, specs = [], []
    for cache in (k_cache, v_cache):
        for win, dil in DIL_GROUPS:
            assert lc % win == 0 and win % dil == 0
            w = win // dil
            views.append(cache.reshape(bt, lc // dil, dil, DIL_HEADS, HEAD_DIM))
            specs.append(pl.BlockSpec((None, w, None, DIL_HEADS, HEAD_DIM),
                                      lambda b, last=lc // win - 1: (b, last, 0, 0, 0)))
    return pl.pallas_call(
        _dil_decode_kernel,
        grid=(bt,),
        in_specs=[pl.BlockSpec((None, rows, nq), lambda b: (b, 0, 0)),
                  pl.BlockSpec((None, rows, 2 * width), lambda b: (b, 0, 0))] + specs,
        out_specs=pl.BlockSpec((None, rows, width), lambda b: (b, 0, 0)),
        out_shape=jax.ShapeDtypeStruct((bt, rows, width), BF16),
        compiler_params=_cparams("parallel"),
        name="dilated_decode",
    )(proj, kv_new, *views)


def _split_heads_kernel(kv_ref, k_ref, v_ref):
    width = DIL_HEADS * HEAD_DIM
    for h in range(DIL_HEADS):
        k_ref[:, h, :] = kv_ref[:, h * HEAD_DIM:(h + 1) * HEAD_DIM]
        v_ref[:, h, :] = kv_ref[:, width + h * HEAD_DIM:width + (h + 1) * HEAD_DIM]


def split_kv_heads(kv, first_row, n_rows):
    bt, _, w2 = kv.shape
    tr = min(n_rows, ROW_TILE)
    assert first_row % tr == 0 and n_rows % tr == 0
    heads = pl.BlockSpec((None, tr, DIL_HEADS, HEAD_DIM), lambda b, i: (b, i, 0, 0))
    return pl.pallas_call(
        _split_heads_kernel,
        grid=(bt, n_rows // tr),
        in_specs=[pl.BlockSpec((None, tr, w2), lambda b, i: (b, first_row // tr + i, 0))],
        out_specs=[heads, heads],
        out_shape=[jax.ShapeDtypeStruct((bt, n_rows, DIL_HEADS, HEAD_DIM), F32)] * 2,
        compiler_params=_cparams("parallel", "parallel"),
        name="split_kv_heads",
    )(kv)


def kernel(x_prompt, x_sample, state_conv, state_ssm, cache_win_k, cache_win_v, cache_mem_k, cache_mem_v,
           mem_prompt, norm_mix_g, norm_mlp_g, norm_mem_g, w_mem_k, w_mem_v, mem_q_norm_g, mem_k_norm_g,
           w_up, w_down, w_in_a, conv_w, conv_b, dt_bias, a_log, d_skip, gate_norm_g, w_out_a,
           w_in_b, q_norm_g, w_out_b, kv_norm_g, w_k_shared, w_v_shared, k_norm_g):
    depth = w_up.shape[0]
    n_a = w_in_a.shape[0]
    d_model = x_prompt.shape[-1]
    n_heads = dt_bias.shape[1]
    d_inner = n_heads * SSM_HEAD_DIM
    conv_dim = conv_w.shape[2]
    mem_w = MEM_HEADS * HEAD_DIM
    kv_w = DIL_HEADS * HEAD_DIM
    dil_q_w = len(DIL_GROUPS) * kv_w

    s1, s2, s3 = d_inner, d_inner + conv_dim, d_inner + conv_dim + n_heads
    assert s1 % COL_TILE == 0 and s2 % COL_TILE == 0
    w_in_a_t = jnp.swapaxes(w_in_a, 1, 2)
    w_a_main = round_weight(w_in_a_t, s2)
    tail = lax.optimization_barrier(w_in_a_t[:, s2:]).astype(BF16)
    w_a_tail = jnp.concatenate([tail[:, :n_heads], jnp.zeros((n_a, COL_TILE - n_heads, d_model), BF16),
                                tail[:, n_heads:]], axis=1)
    tiles_a = ([(0, s1 + c, COL_TILE) for c in range(0, conv_dim, COL_TILE)]
               + [(1, COL_TILE + c, COL_TILE) for c in range(0, mem_w, COL_TILE)]
               + [(0, c, COL_TILE) for c in range(0, d_inner, COL_TILE)] + [(1, 0, LANES)])
    kinds_a = (["plain"] * (conv_dim // COL_TILE) + ["norm"] * (mem_w // COL_TILE)
               + ["plain"] * (d_inner // COL_TILE + 1))
    col_xbc, col_qm_a, col_z = 0, conv_dim, conv_dim + mem_w
    col_dt = col_z + d_inner
    a_cols = col_dt + LANES
    kinds_b = ["rope"] * (dil_q_w // COL_TILE) + ["norm"] * (mem_w // COL_TILE)
    kinds_kv = ["rope"] * (kv_w // COL_TILE) + ["plain"] * (kv_w // COL_TILE)
    kinds_mkv = ["norm"] * (mem_w // COL_TILE) + ["plain"] * (mem_w // COL_TILE)
    w_b = w_in_b.astype(BF16)
    w_kv = jnp.concatenate([w_k_shared, w_v_shared], axis=-1).astype(BF16)[None]
    w_mkv = jnp.concatenate([w_mem_k, w_mem_v], axis=-1).astype(BF16)
    w_out_a_b, w_out_b_b = w_out_a.astype(BF16), w_out_b.astype(BF16)
    assert n_a >= 1
    w_up_b, w_down_b = [None] * depth, [None] * depth

    def gain_row(width, pieces):
        parts, at = [], 0
        for start, g, reps in pieces:
            parts += [jnp.ones((start - at,), F32), jnp.tile(g, reps)]
            at = start + reps * HEAD_DIM
        parts.append(jnp.ones((width - at,), F32))
        return jnp.concatenate(parts).reshape(1, width)

    class Group:
        def __init__(self, x, pos_rows, conv_prev, ssm_prev, mem_kv, k_past, v_past, t_real):
            self.bt, self.t, _ = x.shape
            self.m = self.bt * self.t
            self.x2 = x.reshape(self.m, d_model)
            self.cos, self.sin = rope_tables(pos_rows)
            self.conv_prev, self.ssm_prev, self.mem_kv = conv_prev, ssm_prev, mem_kv
            self.k_past, self.v_past, self.t_real = k_past, v_past, t_real
            self.conv_new, self.ssm_new, self.kv3 = [], [], None

    def in_proj_a(prompt, sample, l):
        hg = gain_row(a_cols, [(col_qm_a, mem_q_norm_g[l], MEM_HEADS)])
        proj_p, proj_s, rounded = norm_proj(prompt.x2, norm_mix_g[l], [w_a_main, w_a_tail], l, hg, kinds_a,
                                            tile_src=tiles_a, transposed=True, extra=(sample.x2, None, None),
                                            cast=((w_up, 0),) if l == 0 else ())
        if l == 0:
            (w_up_b[0],) = rounded
        return proj_p, proj_s

    def mix_a(gr, proj, l, round_mlp_weights):
        bt, t, m = gr.bt, gr.t, gr.m
        proj3 = proj.reshape(bt, t, -1)
        h_prev = gr.ssm_prev[l].reshape(bt, SSM_GROUPS, -1, SSM_D_STATE)
        ssm_params = (conv_w[l], conv_b[l], dt_bias[l], a_log[l], d_skip[l], gate_norm_g[l])
        if gr.t_real == 1:
            y, c_new, h_new = ssd_step(proj3, col_xbc, col_dt, col_z, *ssm_params, gr.conv_prev[l], h_prev)
        else:
            assert t % SSD_CHUNK == 0 and gr.t_real == t
            ssd = ssd_mixer(proj3, col_xbc, col_dt, col_z, gr.t_real, *ssm_params, gr.conv_prev[l], h_prev,
                            cast=((w_down, 0),) if round_mlp_weights else ())
            if round_mlp_weights:
                ssd, (w_down_b[0],) = ssd
            y, c_new, h_new = ssd
        gr.conv_new.append(c_new)
        gr.ssm_new.append(h_new.reshape(bt, n_heads, SSM_HEAD_DIM, SSM_D_STATE))
        mo = mem_attention(proj3, col_qm_a, *gr.mem_kv(l)).reshape(m, mem_w)
        return y.reshape(m, d_inner), mo

    def shared_kv(prompt, sample):
        hg = gain_row(2 * kv_w, [(0, k_norm_g, DIL_HEADS)])
        kv_p, kv_s, _ = norm_proj(prompt.x2, kv_norm_g, w_kv, 0, hg, kinds_kv, prompt.cos, prompt.sin,
                                  extra=(sample.x2, sample.cos, sample.sin))
        prompt.kv3 = kv_p.reshape(prompt.bt, prompt.t, 2 * kv_w)
        sample.kv3 = kv_s.reshape(sample.bt, sample.t, 2 * kv_w)

    def in_proj_b(prompt, sample, l):
        j = l - n_a
        hg = gain_row(dil_q_w + mem_w, [(0, q_norm_g[j], dil_q_w // HEAD_DIM),
                                        (dil_q_w, mem_q_norm_g[l], MEM_HEADS)])
        proj_p, proj_s, _ = norm_proj(prompt.x2, norm_mix_g[l], w_b, j, hg, kinds_b, prompt.cos, prompt.sin,
                                      extra=(sample.x2, sample.cos, sample.sin))
        return proj_p, proj_s

    def mix_b(gr, proj, l):
        proj3 = proj.reshape(gr.bt, gr.t, -1)
        if gr.k_past is None:
            att = dilated_attention(proj3, gr.kv3)
        else:
            att = dilated_decode(proj3, gr.kv3, gr.k_past, gr.v_past)
        mo = mem_attention(proj3, dil_q_w, *gr.mem_kv(l)).reshape(gr.m, mem_w)
        return att.reshape(gr.m, kv_w), mo

    bp, t_p, _ = x_prompt.shape
    mlen = mem_prompt.shape[1]
    mem2 = mem_prompt.reshape(bp * mlen, d_model)
    mkv_p = []
    for l in range(depth):
        hg = gain_row(2 * mem_w, [(0, mem_k_norm_g[l], MEM_HEADS)])
        mkv_p.append(norm_proj(mem2, norm_mem_g[l], w_mkv, l, hg, kinds_mkv)[0].reshape(bp, mlen, 2 * mem_w))
    conv0 = jnp.zeros((n_a, bp, CONV_W - 1, conv_dim), F32)
    ssm0 = jnp.zeros((n_a, bp, n_heads, SSM_HEAD_DIM, SSM_D_STATE), F32)
    prompt = Group(x_prompt, jnp.arange(t_p, dtype=jnp.int32), conv0, ssm0,
                   lambda l: (mkv_p[l], mkv_p[l], (0, mem_w)), None, None, t_p)

    bs, t_s, _ = x_sample.shape
    assert t_s == 1
    xs = jnp.pad(x_sample, ((0, 0), (0, SAMPLE_ROWS - t_s), (0, 0)))
    pos_s = jnp.full((bs * SAMPLE_ROWS,), PAST_LEN, jnp.int32)
    sample = Group(xs, pos_s, state_conv, state_ssm, lambda l: (cache_mem_k, cache_mem_v, l),
                   cache_win_k, cache_win_v, t_s)

    for l in range(depth):
        if l < n_a:
            proj_p, proj_s = in_proj_a(prompt, sample, l)
            a_p, b_p = mix_a(prompt, proj_p, l, round_mlp_weights=(l == 0))
            a_s, b_s = mix_a(sample, proj_s, l, round_mlp_weights=False)
            w_out, lw = w_out_a_b, l
        else:
            if l == n_a:
                shared_kv(prompt, sample)
            proj_p, proj_s = in_proj_b(prompt, sample, l)
            a_p, b_p = mix_b(prompt, proj_p, l)
            a_s, b_s = mix_b(sample, proj_s, l)
            w_out, lw = w_out_b_b, l - n_a
        nxt = ((w_up, l + 1), (w_down, l + 1)) if l + 1 < depth else ()
        prompt.x2, sample.x2, rounded = mix_mlp(prompt.x2, a_p, b_p, w_out, lw, norm_mlp_g[l], w_up_b[l], w_down_b[l],
                                                cast=nxt, extra=(sample.x2, a_s, b_s))
        if nxt:
            w_up_b[l + 1], w_down_b[l + 1] = rounded

    y_p = prompt.x2.reshape(bp, t_p, d_model)
    conv_p, ssm_p, kv_p = jnp.stack(prompt.conv_new), jnp.stack(prompt.ssm_new), prompt.kv3
    mkv_all = jnp.stack(mkv_p)
    mem_k_p = mkv_all[..., :mem_w].reshape(depth, bp, mlen, MEM_HEADS, HEAD_DIM)
    mem_v_p = mkv_all[..., mem_w:].reshape(depth, bp, mlen, MEM_HEADS, HEAD_DIM)
    keep = min(max(w for w, _ in DIL_GROUPS), t_p)
    win_k_p, win_v_p = split_kv_heads(kv_p, t_p - keep, keep)

    y_s = sample.x2.reshape(bs, SAMPLE_ROWS, d_model)
    conv_s, ssm_s, kv_s = jnp.stack(sample.conv_new), jnp.stack(sample.ssm_new), sample.kv3
    y_s = y_s[:, :t_s]
    k_s = kv_s[:, :t_s, :kv_w].reshape(bs, t_s, DIL_HEADS, HEAD_DIM)
    v_s = kv_s[:, :t_s, kv_w:].reshape(bs, t_s, DIL_HEADS, HEAD_DIM)

    return (y_p, y_s, conv_p, ssm_p, win_k_p, win_v_p, mem_k_p, mem_v_p, conv_s, ssm_s, k_s, v_s)
```

```python
import functools

import jax
import jax.numpy as jnp
from jax import lax
from jax.experimental import pallas as pl
from jax.experimental.pallas import tpu as pltpu

F32 = jnp.float32
BF16 = jnp.bfloat16

HEAD_DIM = 128
SSM_HEAD_DIM = 64
SSM_GROUPS = 4
SSM_D_STATE = 128
CONV_W = 4
SSD_CHUNK = 128
MEM_HEADS = 4
DIL_GROUPS = ((128, 1), (512, 4), (2048, 16))
DIL_HEADS = 4
PAST_LEN = 16384
ROPE_THETA = 10000.0
EPS = 1e-6
LOG2_E = 1.4426950408889634
LN_2 = 0.6931471805599453

LANES = 128
SUBLANES = 8
VMEM_LIMIT = 48 * 1024 * 1024
MIX_MLP_VMEM_LIMIT = 56 * 1024 * 1024
COL_TILE = 512
ROW_TILE = 512
MEM_Q_TILE = 1024
PROJ_OUT_BLOCK_BYTES = 5 * 1024 * 1024
SAMPLE_ROWS = SUBLANES
DIL_UNROLL = 16

NT_DIMS = (((1,), (1,)), ((), ()))
RESIDENT = pl.Buffered(1)


def _cparams(*sem):
    return pltpu.CompilerParams(dimension_semantics=sem, vmem_limit_bytes=VMEM_LIMIT)


def _sigmoid(x):
    return 1.0 / (1.0 + jnp.exp(-x))


def _rms_scale(x):
    return lax.rsqrt(jnp.mean(x * x, axis=-1, keepdims=True) + EPS)


class _SideCast:
    def __init__(self, mats, steps, step_of):
        self.n = len(mats)
        self.shapes = [a.shape[1:] for a, _ in mats]
        self.views, self.in_specs, self.out_specs, self.out_shape = [], [], [], []
        for a, layer in mats:
            n_layers, r, c = a.shape
            assert r % (steps * 2 * SUBLANES) == 0
            band = r // steps
            self.views.append(a.reshape(n_layers, steps, band, c))
            self.in_specs.append(pl.BlockSpec((None, None, band, c),
                                              lambda *ids, layer=layer: (layer, step_of(*ids), 0, 0)))
            self.out_specs.append(pl.BlockSpec((None, band, c), lambda *ids: (step_of(*ids), 0, 0)))
            self.out_shape.append(jax.ShapeDtypeStruct((steps, band, c), BF16))

    def wrap(self, body, n_in, n_out):
        n = self.n
        if not n:
            return body

        def kern(*refs):
            ins, srcs = refs[:n_in], refs[n_in:n_in + n]
            outs, dsts = refs[n_in + n:n_in + n + n_out], refs[n_in + n + n_out:n_in + 2 * n + n_out]
            for s, d in zip(srcs, dsts):
                d[...] = s[...].astype(BF16)
            body(*ins, *outs, *refs[n_in + 2 * n + n_out:])
        return kern

    def split(self, outs, n_out):
        main = outs[0] if n_out == 1 else tuple(outs[:n_out])
        if not self.n:
            return main
        return main, [o.reshape(s) for o, s in zip(outs[n_out:], self.shapes)]


def _round_kernel(src_ref, dst_ref):
    dst_ref[...] = src_ref[...].astype(BF16)


def round_weight(w, n_rows):
    n_layers, _, c = w.shape
    band = ROW_TILE
    assert n_rows % band == 0
    return pl.pallas_call(
        _round_kernel,
        grid=(n_layers, n_rows // band),
        in_specs=[pl.BlockSpec((None, band, c), lambda l, i: (l, i, 0))],
        out_specs=pl.BlockSpec((None, band, c), lambda l, i: (l, i, 0)),
        out_shape=jax.ShapeDtypeStruct((n_layers, n_rows, c), BF16),
        compiler_params=_cparams("parallel", "parallel"),
        name="round_weight",
    )(w)


def _rope_table_kernel(pos_ref, inv_ref, cos_ref, sin_ref):
    ang = pos_ref[...] * inv_ref[...]
    lane = lax.broadcasted_iota(jnp.int32, ang.shape, 1)
    cos_ref[...] = jnp.cos(ang)
    s = jnp.sin(ang)
    sin_ref[...] = jnp.where(lane < HEAD_DIM // 2, -s, s)


def rope_tables(pos):
    r = pos.shape[0]
    half = HEAD_DIM // 2
    inv = ROPE_THETA ** (-jnp.arange(half, dtype=F32) * (2.0 / HEAD_DIM))
    inv = jnp.concatenate([inv, inv])[None, :]
    posf = jnp.broadcast_to(pos.astype(F32)[:, None], (r, HEAD_DIM))
    tr = min(r, ROW_TILE)
    return pl.pallas_call(
        _rope_table_kernel,
        grid=(r // tr,),
        in_specs=[pl.BlockSpec((tr, HEAD_DIM), lambda i: (i, 0)),
                  pl.BlockSpec((1, HEAD_DIM), lambda i: (0, 0))],
        out_specs=[pl.BlockSpec((tr, HEAD_DIM), lambda i: (i, 0))] * 2,
        out_shape=[jax.ShapeDtypeStruct((r, HEAD_DIM), F32)] * 2,
        compiler_params=_cparams("parallel"),
        name="rope_tables",
    )(posf, inv)


def _proj_kernel(*refs, kinds, tile_src, n_w, transposed, has_extra):
    rope = "rope" in kinds
    refs = list(refs)
    x_ref, g_ref = refs[:2]
    w_refs = refs[2:2 + n_w]
    hg_ref = refs[2 + n_w]
    at = 3 + n_w
    if rope:
        cos_ref, sin_ref = refs[at:at + 2]
        at += 2
    if has_extra:
        xs_ref = refs[at]
        at += 1
        if rope:
            cos_s_ref, sin_s_ref = refs[at:at + 2]
            at += 2
        o_ref, os_ref, xn_ref = refs[at:at + 3]
    else:
        o_ref, xn_ref = refs[at:at + 2]
    tm = x_ref.shape[0]

    def normed(x):
        return (x * _rms_scale(x) * g_ref[...]).astype(BF16)

    def tiles(xn, cos, sin, store):
        for t, kind in enumerate(kinds):
            which, col, width = tile_src[t]
            at_col = sum(w for _, _, w in tile_src[:t])
            if transposed:
                y = lax.dot_general(xn, w_refs[which][col:col + width, :], NT_DIMS, preferred_element_type=F32)
            else:
                y = jnp.dot(xn, w_refs[which][:, col:col + width], preferred_element_type=F32)
            if kind == "plain":
                store(slice(at_col, at_col + width), y)
                continue
            for j in range(width // HEAD_DIM):
                osl = slice(at_col + j * HEAD_DIM, at_col + (j + 1) * HEAD_DIM)
                yh = y[:, j * HEAD_DIM:(j + 1) * HEAD_DIM]
                yh = yh * _rms_scale(yh) * hg_ref[:, osl]
                if kind == "rope":
                    yh = yh * cos + pltpu.roll(yh, HEAD_DIM // 2, axis=1) * sin
                store(osl, yh)

    def store_main(cols, y):
        o_ref[:, cols] = y

    def store_both(cols, y):
        o_ref[:, cols] = y[:tm]
        os_ref[:, cols] = y[tm:]

    xn_ref[:tm, :] = normed(x_ref[...])
    cos = cos_ref[...] if rope else None
    sin = sin_ref[...] if rope else None
    if not has_extra:
        tiles(xn_ref[...], cos, sin, store_main)
        return

    @pl.when(pl.program_id(0) == 0)
    def _():
        xn_ref[tm:, :] = normed(xs_ref[...])
        cos_all = jnp.concatenate([cos, cos_s_ref[...]], axis=0) if rope else None
        sin_all = jnp.concatenate([sin, sin_s_ref[...]], axis=0) if rope else None
        tiles(xn_ref[...], cos_all, sin_all, store_both)

    @pl.when(pl.program_id(0) != 0)
    def _():
        tiles(xn_ref[:tm, :], cos, sin, store_main)


def norm_proj(x, g, ws, layer, head_gain, kinds, cos=None, sin=None, cast=(), tile_src=None, transposed=False,
              extra=None):
    m, d = x.shape
    ws = list(ws) if isinstance(ws, (list, tuple)) else [ws]
    if tile_src is None:
        tile_src = [(0, t * COL_TILE, COL_TILE) for t in range(len(kinds))]
    n_cols = sum(width for _, _, width in tile_src)
    assert len(tile_src) == len(kinds) and head_gain.shape == (1, n_cols)
    assert all(width % HEAD_DIM == 0 for _, _, width in tile_src)
    rope = "rope" in kinds
    has_extra = extra is not None
    ms = extra[0].shape[0] if has_extra else 0
    tm = min(m, ROW_TILE)
    while tm * n_cols * 4 > PROJ_OUT_BLOCK_BYTES and tm % (2 * SUBLANES) == 0:
        tm //= 2
    assert m % tm == 0
    in_specs = [pl.BlockSpec((tm, d), lambda i: (i, 0)),
                pl.BlockSpec((1, d), lambda i: (0, 0))]
    for k, w in enumerate(ws):
        used = max(col + width for which, col, width in tile_src if which == k)
        if transposed:
            assert used <= w.shape[1] and w.shape[2] == d
            in_specs.append(pl.BlockSpec((None, used, d), lambda i: (layer, 0, 0), pipeline_mode=RESIDENT))
            continue
        assert used <= w.shape[2] and (used % LANES == 0 or used == w.shape[2])
        in_specs.append(pl.BlockSpec((None, d, used), lambda i: (layer, 0, 0), pipeline_mode=RESIDENT))
    in_specs.append(pl.BlockSpec((1, n_cols), lambda i: (0, 0)))
    args = [x, g.reshape(1, d), *ws, head_gain]
    if rope:
        assert cos.shape[0] % tm == 0
        tab_blocks = cos.shape[0] // tm
        in_specs += [pl.BlockSpec((tm, HEAD_DIM), lambda i: (i % tab_blocks, 0))] * 2
        args += [cos, sin]
    out_specs = [pl.BlockSpec((tm, n_cols), lambda i: (i, 0))]
    out_shape = [jax.ShapeDtypeStruct((m, n_cols), F32)]
    if has_extra:
        whole = lambda a: pl.BlockSpec(a.shape, lambda i: (0, 0))
        xs, cos_s, sin_s = extra
        extras = [xs] + ([cos_s, sin_s] if rope else [])
        in_specs += [whole(a) for a in extras]
        args += extras
        out_specs.append(pl.BlockSpec((ms, n_cols), lambda i: (0, 0)))
        out_shape.append(jax.ShapeDtypeStruct((ms, n_cols), F32))
    n_out = len(out_specs)
    side = _SideCast(cast, m // tm, lambda i: i)
    outs = pl.pallas_call(
        side.wrap(functools.partial(_proj_kernel, kinds=tuple(kinds), tile_src=tuple(tile_src), n_w=len(ws),
                                    transposed=transposed, has_extra=has_extra),
                  len(args), n_out),
        grid=(m // tm,),
        in_specs=in_specs + side.in_specs,
        out_specs=out_specs + side.out_specs,
        out_shape=out_shape + side.out_shape,
        scratch_shapes=[pltpu.VMEM((tm + ms, d), BF16)],
        compiler_params=_cparams("arbitrary"),
        name="norm_proj",
    )(*args, *side.views)
    casts = [o.reshape(shp) for o, shp in zip(outs[n_out:], side.shapes)]
    return outs[0], (outs[1] if has_extra else None), casts


def _mix_mlp_kernel(*refs, has_extra):
    if has_extra:
        (x_ref, g_ref, wu_ref, wd_ref, a_ref, b_ref, wa_ref, wb_ref, xs_ref, as_ref, bs_ref,
         o_ref, os_ref, xn_ref) = refs
    else:
        x_ref, g_ref, wu_ref, wd_ref, a_ref, b_ref, wa_ref, wb_ref, o_ref, xn_ref = refs
    i, f = pl.program_id(0), pl.program_id(1)
    tm = x_ref.shape[0]

    def normed(x):
        return (x * _rms_scale(x) * g_ref[...]).astype(BF16)

    def mixed(a, b):
        acc = jnp.dot(a, wa_ref[...], preferred_element_type=F32)
        return acc + jnp.dot(b, wb_ref[...], preferred_element_type=F32)

    def ffn(xn):
        h = jnp.dot(xn, wu_ref[...], preferred_element_type=F32)
        h = jnp.maximum(h, 0.0)
        h = (h * h).astype(BF16)
        return jnp.dot(h, wd_ref[...], preferred_element_type=F32)

    def start_main():
        x1 = x_ref[...] + mixed(a_ref[...], b_ref[...])
        xn_ref[:tm, :] = normed(x1)
        o_ref[...] = x1

    def start_both():
        acc = mixed(jnp.concatenate([a_ref[...], as_ref[...]], axis=0),
                    jnp.concatenate([b_ref[...], bs_ref[...]], axis=0))
        x1 = x_ref[...] + acc[:tm]
        xs1 = xs_ref[...] + acc[tm:]
        xn_ref[:tm, :] = normed(x1)
        xn_ref[tm:, :] = normed(xs1)
        o_ref[...] = x1
        os_ref[...] = xs1

    if not has_extra:
        pl.when(f == 0)(start_main)
        o_ref[...] += ffn(xn_ref[...])
        return

    @pl.when(i == 0)
    def _():
        pl.when(f == 0)(start_both)
        upd = ffn(xn_ref[...])
        o_ref[...] += upd[:tm]
        os_ref[...] += upd[tm:]

    @pl.when(i != 0)
    def _():
        pl.when(f == 0)(start_main)
        o_ref[...] += ffn(xn_ref[:tm, :])


def mix_mlp(x, a, b, w_out, layer, g, wu, wd, tf=1024, cast=(), extra=None):
    m, d = x.shape
    ff = wu.shape[1]
    tm = min(m, ROW_TILE)
    nf = ff // tf
    ka, kb = a.shape[1], b.shape[1]
    assert w_out.shape[1] == ka + kb and ka % kb == 0
    has_extra = extra is not None
    side = _SideCast(cast, (m // tm) * nf, lambda i, j: i * nf + j)
    in_specs = [pl.BlockSpec((tm, d), lambda i, j: (i, 0)),
                pl.BlockSpec((1, d), lambda i, j: (0, 0)),
                pl.BlockSpec((d, tf), lambda i, j: (0, j)),
                pl.BlockSpec((tf, d), lambda i, j: (j, 0)),
                pl.BlockSpec((tm, ka), lambda i, j: (i, 0)),
                pl.BlockSpec((tm, kb), lambda i, j: (i, 0)),
                pl.BlockSpec((None, ka, d), lambda i, j: (layer, 0, 0), pipeline_mode=RESIDENT),
                pl.BlockSpec((None, kb, d), lambda i, j: (layer, ka // kb, 0), pipeline_mode=RESIDENT)]
    out_specs = [pl.BlockSpec((tm, d), lambda i, j: (i, 0))]
    out_shape = [jax.ShapeDtypeStruct((m, d), F32)]
    args = [x, g.reshape(1, d), wu, wd, a, b, w_out, w_out]
    ms = extra[0].shape[0] if has_extra else 0
    if has_extra:
        in_specs += [pl.BlockSpec(e.shape, lambda i, j: (0, 0)) for e in extra]
        args += list(extra)
        out_specs.append(pl.BlockSpec((ms, d), lambda i, j: (0, 0)))
        out_shape.append(jax.ShapeDtypeStruct((ms, d), F32))
    n_out = len(out_specs)
    outs = pl.pallas_call(
        side.wrap(functools.partial(_mix_mlp_kernel, has_extra=has_extra), len(args), n_out),
        grid=(m // tm, nf),
        in_specs=in_specs + side.in_specs,
        out_specs=out_specs + side.out_specs,
        out_shape=out_shape + side.out_shape,
        scratch_shapes=[pltpu.VMEM((tm + ms, d), BF16)],
        compiler_params=pltpu.CompilerParams(dimension_semantics=("arbitrary", "arbitrary"),
                                             vmem_limit_bytes=MIX_MLP_VMEM_LIMIT),
        name="mix_mlp",
    )(*args, *side.views)
    casts = [o.reshape(shp) for o, shp in zip(outs[n_out:], side.shapes)]
    return outs[0], (outs[1] if has_extra else None), casts


def _split3(x):
    hi = x.astype(BF16)
    r1 = x - hi.astype(F32)
    mid = r1.astype(BF16)
    lo = (r1 - mid.astype(F32)).astype(BF16)
    return hi, mid, lo


def _ssd_kernel(xbc_ref, dt_ref, z_ref, cw_ref, cb_ref, dtb_ref, alog_ref, dsk_ref, gg_ref, tri_ref, expand_ref,
                conv0_ref, h0_ref, y_ref, convo_ref, ho_ref,
                ext_ref, ht_ref, yd_ref, st_ref, *, t_real, n_heads, d_inner):
    L = SSD_CHUNK
    P = SSM_HEAD_DIM
    N = SSM_D_STATE
    G = SSM_GROUPS
    hpg = n_heads // G
    gw = hpg * P
    hist = CONV_W - 1
    base = SUBLANES - hist
    c = pl.program_id(1)
    nc = pl.num_programs(1)

    @pl.when(c == 0)
    def _():
        ext_ref[base:SUBLANES, :] = conv0_ref[...]
        for g in range(G):
            ht_ref[g] = h0_ref[g].T

    ext_ref[SUBLANES:SUBLANES + L, :] = xbc_ref[...]
    ext = ext_ref[...]
    acc = cb_ref[...] + ext[SUBLANES:, :] * cw_ref[hist:CONV_W, :]
    for k in range(1, CONV_W):
        acc = acc + pltpu.roll(ext, k, axis=0)[SUBLANES:, :] * cw_ref[hist - k:CONV_W - k, :]
    xc = acc * _sigmoid(acc)

    @pl.when(c == nc - 1)
    def _():
        n_in_last = t_real - ((t_real - 1) // L) * L
        convo_ref[...] = ext_ref[base + n_in_last:base + n_in_last + hist, :]

    ext_ref[base:SUBLANES, :] = ext_ref[base + L:SUBLANES + L, :]

    dt_raw = dt_ref[...] + dtb_ref[...]
    dt = jnp.maximum(dt_raw, 0.0) + jnp.log1p(jnp.exp(-jnp.abs(dt_raw)))
    row = lax.broadcasted_iota(jnp.int32, (L, LANES), 0) + c * L
    dt = jnp.where(row < t_real, dt, 0.0)
    da = dt * (-jnp.exp(alog_ref[...]))
    li = lax.broadcasted_iota(jnp.int32, (L, L), 0)
    si = lax.broadcasted_iota(jnp.int32, (L, L), 1)
    causal = li >= si
    a_cs = sum(jnp.dot(tri_ref[...], part, preferred_element_type=F32) for part in _split3(da))
    a_cs_t = a_cs.T
    dt_t = dt.T
    w_t = dt_t * jnp.exp(a_cs_t[:, L - 1:L] - a_cs_t)
    e_exp = sum(jnp.dot(part, expand_ref[...], preferred_element_type=F32) for part in _split3(jnp.exp(a_cs)))
    xb = xc[:, :d_inner].astype(BF16)
    low_half = lax.broadcasted_iota(jnp.int32, (L, 2 * P), 1) < P

    for g in range(G):
        b_g = xc[:, d_inner + g * N:d_inner + (g + 1) * N]
        c_g = xc[:, d_inner + G * N + g * N:d_inner + G * N + (g + 1) * N].astype(BF16)
        cbm = lax.dot_general(c_g, b_g.astype(BF16), NT_DIMS, preferred_element_type=F32)
        b_t = b_g.T
        y_off = jnp.dot(c_g, ht_ref[g].astype(BF16), preferred_element_type=F32)
        for kp in range(hpg // 2):
            h0 = g * hpg + 2 * kp
            x_pair = xb[:, h0 * P:(h0 + 2) * P]
            yd, st = [], []
            for h in (h0, h0 + 1):
                seg = a_cs[:, h:h + 1] - a_cs_t[h:h + 1, :]
                dec = jnp.exp(jnp.where(causal, seg, -jnp.inf))
                mh = (cbm * dec * dt_t[h:h + 1, :]).astype(BF16)
                yd.append(jnp.dot(mh, x_pair, preferred_element_type=F32))
                btw = (b_t * w_t[h:h + 1, :]).astype(BF16)
                st.append(jnp.dot(btw, x_pair, preferred_element_type=F32))
            yd_ref[:, 2 * kp * P:(2 * kp + 2) * P] = jnp.where(low_half, yd[0], yd[1])
            st_ref[:, 2 * kp * P:(2 * kp + 2) * P] = jnp.where(low_half, st[0], st[1])
        gs = slice(g * gw, (g + 1) * gw)
        e_g = e_exp[:, gs]
        y_g = yd_ref[...] + y_off * e_g + dsk_ref[:, gs] * xc[:, gs]
        ht_ref[g] = ht_ref[g] * e_g[L - 1:L, :] + st_ref[...]
        z_g = z_ref[:, gs]
        gated = y_g * (z_g * _sigmoid(z_g))
        y_ref[:, gs] = (gated * _rms_scale(gated) * gg_ref[:, gs]).astype(y_ref.dtype)

    @pl.when(c == nc - 1)
    def _():
        for g in range(G):
            ho_ref[g] = ht_ref[g].T


def ssd_mixer(proj, col_xbc, col_dt, col_z, t_real, conv_w, conv_b, dt_bias, a_log, d_skip, gate_g,
              conv_prev, h_prev, cast=()):
    bt, tp, _ = proj.shape
    conv_dim = conv_w.shape[1]
    n_heads = dt_bias.shape[0]
    d_inner = n_heads * SSM_HEAD_DIM
    gw = d_inner // SSM_GROUPS
    L = SSD_CHUNK
    assert tp % L == 0 and (tp - t_real) < L
    assert col_xbc % conv_dim == 0 and col_dt % LANES == 0 and col_z % d_inner == 0
    pad = LANES - n_heads
    dtb = jnp.pad(dt_bias, (0, pad)).reshape(1, LANES)
    alog = jnp.pad(a_log, (0, pad)).reshape(1, LANES)
    dsk = jnp.repeat(d_skip, SSM_HEAD_DIM).reshape(1, d_inner)
    tri = jnp.tril(jnp.ones((L, L), BF16))
    expand = jnp.repeat(jnp.eye(LANES, dtype=BF16)[:, :n_heads], SSM_HEAD_DIM, axis=1)
    kern = functools.partial(_ssd_kernel, t_real=t_real, n_heads=n_heads, d_inner=d_inner)
    vec = lambda n: pl.BlockSpec((1, n), lambda b, c: (0, 0))
    nc = tp // L
    side = _SideCast(cast, bt * nc, lambda b, c: b * nc + c)
    outs = pl.pallas_call(
        side.wrap(kern, 13, 3),
        grid=(bt, nc),
        in_specs=[pl.BlockSpec((None, L, conv_dim), lambda b, c: (b, c, col_xbc // conv_dim)),
                  pl.BlockSpec((None, L, LANES), lambda b, c: (b, c, col_dt // LANES)),
                  pl.BlockSpec((None, L, d_inner), lambda b, c: (b, c, col_z // d_inner)),
                  pl.BlockSpec((CONV_W, conv_dim), lambda b, c: (0, 0)),
                  vec(conv_dim), vec(LANES), vec(LANES), vec(d_inner), vec(d_inner),
                  pl.BlockSpec((L, L), lambda b, c: (0, 0)),
                  pl.BlockSpec((LANES, d_inner), lambda b, c: (0, 0)),
                  pl.BlockSpec((None, CONV_W - 1, conv_dim), lambda b, c: (b, 0, 0)),
                  pl.BlockSpec((None, SSM_GROUPS, gw, SSM_D_STATE), lambda b, c: (b, 0, 0, 0))] + side.in_specs,
        out_specs=[pl.BlockSpec((None, L, d_inner), lambda b, c: (b, c, 0)),
                   pl.BlockSpec((None, CONV_W - 1, conv_dim), lambda b, c: (b, 0, 0)),
                   pl.BlockSpec((None, SSM_GROUPS, gw, SSM_D_STATE), lambda b, c: (b, 0, 0, 0))] + side.out_specs,
        out_shape=[jax.ShapeDtypeStruct((bt, tp, d_inner), BF16),
                   jax.ShapeDtypeStruct((bt, CONV_W - 1, conv_dim), F32),
                   jax.ShapeDtypeStruct((bt, SSM_GROUPS, gw, SSM_D_STATE), F32)] + side.out_shape,
        scratch_shapes=[pltpu.VMEM((SUBLANES + L, conv_dim), F32),
                        pltpu.VMEM((SSM_GROUPS, SSM_D_STATE, gw), F32),
                        pltpu.VMEM((L, gw), F32),
                        pltpu.VMEM((SSM_D_STATE, gw), F32)],
        compiler_params=_cparams("parallel", "arbitrary"),
        name="ssd_mixer",
    )(proj, proj, proj, conv_w, conv_b.reshape(1, conv_dim), dtb, alog, dsk, gate_g.reshape(1, d_inner),
      tri, expand, conv_prev, h_prev, *side.views)
    return side.split(outs, 3)


def _ssd_step_kernel(xbc_ref, dt_ref, z_ref, cw_ref, cb_ref, dtb_ref, alog_ref, dsk_ref, gg_ref, expand_ref,
                     conv0_ref, h0_ref, y_ref, convo_ref, ho_ref, *, n_heads, d_inner):
    P, N, G = SSM_HEAD_DIM, SSM_D_STATE, SSM_GROUPS
    gw = n_heads // G * P
    hist = CONV_W - 1
    acc = cb_ref[...] + xbc_ref[...] * cw_ref[hist:CONV_W, :]
    for k in range(hist):
        acc = acc + conv0_ref[k:k + 1, :] * cw_ref[k:k + 1, :]
    xc = acc * _sigmoid(acc)
    convo_ref[0:hist - 1, :] = conv0_ref[1:hist, :]
    convo_ref[hist - 1:hist, :] = xbc_ref[0:1, :]

    dt_raw = dt_ref[...] + dtb_ref[...]
    dt = jnp.maximum(dt_raw, 0.0) + jnp.log1p(jnp.exp(-jnp.abs(dt_raw)))
    decay = jnp.exp(dt * (-jnp.exp(alog_ref[...])))
    expand = lambda v: sum(jnp.dot(part, expand_ref[...], preferred_element_type=F32) for part in _split3(v))
    dt_x = expand(dt) * xc[:, :d_inner]
    decay_c = expand(decay)

    for g in range(G):
        gs = slice(g * gw, (g + 1) * gw)
        b_col = xc[:, d_inner + g * N:d_inner + (g + 1) * N].T[:, 0:1]
        c_g = xc[:, d_inner + G * N + g * N:d_inner + G * N + (g + 1) * N].astype(BF16)
        h_t = h0_ref[g].T * decay_c[0:1, gs] + b_col * dt_x[0:1, gs]
        ho_ref[g] = h_t.T
        y_g = jnp.dot(c_g, h_t.astype(BF16), preferred_element_type=F32) + dsk_ref[:, gs] * xc[:, gs]
        z_g = z_ref[:, gs]
        gated = y_g * (z_g * _sigmoid(z_g))
        y_ref[:, gs] = (gated * _rms_scale(gated) * gg_ref[:, gs]).astype(y_ref.dtype)


def ssd_step(proj, col_xbc, col_dt, col_z, conv_w, conv_b, dt_bias, a_log, d_skip, gate_g, conv_prev, h_prev):
    bt, rows, _ = proj.shape
    conv_dim = conv_w.shape[1]
    n_heads = dt_bias.shape[0]
    d_inner = n_heads * SSM_HEAD_DIM
    gw = d_inner // SSM_GROUPS
    assert col_xbc % conv_dim == 0 and col_dt % LANES == 0 and col_z % d_inner == 0
    pad = LANES - n_heads
    dtb = jnp.pad(dt_bias, (0, pad)).reshape(1, LANES)
    alog = jnp.pad(a_log, (0, pad)).reshape(1, LANES)
    dsk = jnp.repeat(d_skip, SSM_HEAD_DIM).reshape(1, d_inner)
    expand = jnp.repeat(jnp.eye(LANES, dtype=BF16)[:, :n_heads], SSM_HEAD_DIM, axis=1)
    vec = lambda n: pl.BlockSpec((1, n), lambda b: (0, 0))
    state = pl.BlockSpec((None, SSM_GROUPS, gw, SSM_D_STATE), lambda b: (b, 0, 0, 0))
    hist_rows = pl.BlockSpec((None, CONV_W - 1, conv_dim), lambda b: (b, 0, 0))
    return pl.pallas_call(
        functools.partial(_ssd_step_kernel, n_heads=n_heads, d_inner=d_inner),
        grid=(bt,),
        in_specs=[pl.BlockSpec((None, rows, conv_dim), lambda b: (b, 0, col_xbc // conv_dim)),
                  pl.BlockSpec((None, rows, LANES), lambda b: (b, 0, col_dt // LANES)),
                  pl.BlockSpec((None, rows, d_inner), lambda b: (b, 0, col_z // d_inner)),
                  pl.BlockSpec((CONV_W, conv_dim), lambda b: (0, 0)),
                  vec(conv_dim), vec(LANES), vec(LANES), vec(d_inner), vec(d_inner),
                  pl.BlockSpec((LANES, d_inner), lambda b: (0, 0)),
                  hist_rows, state],
        out_specs=[pl.BlockSpec((None, rows, d_inner), lambda b: (b, 0, 0)), hist_rows, state],
        out_shape=[jax.ShapeDtypeStruct((bt, rows, d_inner), BF16),
                   jax.ShapeDtypeStruct((bt, CONV_W - 1, conv_dim), F32),
                   jax.ShapeDtypeStruct((bt, SSM_GROUPS, gw, SSM_D_STATE), F32)],
        compiler_params=_cparams("parallel"),
        name="ssd_step",
    )(proj, proj, proj, conv_w, conv_b.reshape(1, conv_dim), dtb, alog, dsk, gate_g.reshape(1, d_inner),
      expand, conv_prev, h_prev)


def _mem_attn_kernel(q_ref, k_ref, v_ref, o_ref):
    scale = HEAD_DIM ** -0.5
    headed = len(k_ref.shape) == 3
    for h in range(MEM_HEADS):
        sl = slice(h * HEAD_DIM, (h + 1) * HEAD_DIM)
        q = (q_ref[:, sl] * (scale * LOG2_E)).astype(BF16)
        k = (k_ref[:, h, :] if headed else k_ref[:, sl]).astype(BF16)
        v = (v_ref[:, h, :] if headed else v_ref[:, sl]).astype(BF16)
        s = lax.dot_general(q, k, NT_DIMS, preferred_element_type=F32)
        e = jnp.exp2(s - jnp.max(s, axis=-1, keepdims=True))
        den = jnp.sum(e, axis=-1, keepdims=True)
        o = jnp.dot(e.astype(BF16), v, preferred_element_type=F32) / den
        o_ref[:, sl] = o.astype(o_ref.dtype)


def mem_attention(proj, col_q, k, v, kv_at):
    bt, t, _ = proj.shape
    width = MEM_HEADS * HEAD_DIM
    tq = min(t, MEM_Q_TILE)
    assert t % tq == 0
    if k.ndim == 5:
        mlen = k.shape[2]
        kv_specs = [pl.BlockSpec((None, None, mlen, MEM_HEADS, HEAD_DIM), lambda b, i: (kv_at, b, 0, 0, 0))] * 2
    else:
        mlen = k.shape[1]
        kv_specs = [pl.BlockSpec((None, mlen, width), lambda b, i, c=c: (b, 0, c // width)) for c in kv_at]
    return pl.pallas_call(
        _mem_attn_kernel,
        grid=(bt, t // tq),
        in_specs=[pl.BlockSpec((None, tq, width), lambda b, i: (b, i, col_q // width))] + kv_specs,
        out_specs=pl.BlockSpec((None, tq, width), lambda b, i: (b, i, 0)),
        out_shape=jax.ShapeDtypeStruct((bt, t, width), BF16),
        compiler_params=_cparams("parallel", "parallel"),
        name="mem_attention",
    )(proj, k, v)


def _run_units(first, count, unit, unroll):
    trips = count // unroll
    if trips == 1:
        trips = 0
    if trips:
        def trip(it, carry):
            for k in range(unroll):
                unit(first + it * unroll + k)
            return carry
        lax.fori_loop(0, trips, trip, 0)
    for k in range(trips * unroll, count):
        unit(first + k)


def _dil_attn_kernel(*refs, groups, rb, unroll):
    ng = len(groups)
    q_refs = refs[:ng]
    kp_ref, kc_ref, vp_ref, vc_ref, o_ref = refs[ng:ng + 5]
    og = refs[ng + 5:2 * ng + 5]
    lg = refs[2 * ng + 5:3 * ng + 5]
    i = pl.program_id(1)
    scale = HEAD_DIM ** -0.5

    for gi, (win, d) in enumerate(groups):
        w = win // d
        dw = d * w
        qi = lax.broadcasted_iota(jnp.int32, (w, 2 * w), 0)
        kj = lax.broadcasted_iota(jnp.int32, (w, 2 * w), 1)
        band = (kj >= qi) & (kj <= qi + w)
        band_first = band & (kj >= jnp.where(i > 0, 0, w))

        def rows(start, size, d=d, w=w):
            if d == 1:
                return pl.ds(pl.multiple_of(start, w), size)
            return pl.ds(start, size, stride=d)

        def attend(rows_q, kk, vv, mask, gi=gi, w=w):
            q = (q_refs[gi][rows_q, :] * (scale * LOG2_E)).astype(BF16)
            s = lax.dot_general(q, kk.astype(BF16), NT_DIMS, preferred_element_type=F32)
            s = jnp.where(mask, s, -jnp.inf)
            m = jnp.max(s, axis=-1, keepdims=True)
            e = jnp.exp2(s - m)
            den = jnp.sum(e, axis=-1, keepdims=True)
            o = jnp.dot(e.astype(BF16), vv.astype(BF16), preferred_element_type=F32) / den
            og[gi][rows_q, :] = o
            lg[gi][rows_q, :] = jnp.broadcast_to(m * LN_2 + jnp.log(den), (w, HEAD_DIM))

        def first_unit(r, rows=rows, attend=attend, w=w, dw=dw, mask=band_first):
            rows_q, rows_p = rows(r, w), rows(rb - dw + r, w)
            kk = jnp.concatenate([kp_ref[rows_p, :], kc_ref[rows_q, :]], axis=0)
            vv = jnp.concatenate([vp_ref[rows_p, :], vc_ref[rows_q, :]], axis=0)
            attend(rows_q, kk, vv, mask)

        def later_unit(u, rows=rows, attend=attend, d=d, w=w, dw=dw, mask=band):
            sb = u // d
            start_q = sb * dw + (u - sb * d)
            rows_k = rows(start_q - dw, 2 * w)
            attend(rows(start_q, w), kc_ref[rows_k, :], vc_ref[rows_k, :], mask)

        _run_units(0, d, first_unit, unroll)
        _run_units(d, rb // w - d, later_unit, unroll)

    ls = [r[...] for r in lg]
    mm = functools.reduce(jnp.maximum, ls)
    ws = [jnp.exp(l - mm) for l in ls]
    num = sum(wg * r[...] for wg, r in zip(ws, og))
    o_ref[...] = (num / sum(ws)).astype(o_ref.dtype)


def dilated_attention(proj, kv):
    bt, t, _ = proj.shape
    ng = len(DIL_GROUPS)
    rb = min(t, max(win for win, _ in DIL_GROUPS))
    for win, d in DIL_GROUPS:
        assert win % d == 0 and rb % win == 0
    assert t % rb == 0
    blk = lambda f: pl.BlockSpec((None, rb, HEAD_DIM), f)
    prev = lambda i: jnp.maximum(i - 1, 0)
    q_specs = [blk(lambda b, i, h, g=g: (b, i, g * DIL_HEADS + h)) for g in range(ng)]
    kv_specs = [blk(lambda b, i, h: (b, prev(i), h)), blk(lambda b, i, h: (b, i, h)),
                blk(lambda b, i, h: (b, prev(i), DIL_HEADS + h)), blk(lambda b, i, h: (b, i, DIL_HEADS + h))]
    return pl.pallas_call(
        functools.partial(_dil_attn_kernel, groups=DIL_GROUPS, rb=rb, unroll=DIL_UNROLL),
        grid=(bt, t // rb, DIL_HEADS),
        in_specs=q_specs + kv_specs,
        out_specs=blk(lambda b, i, h: (b, i, h)),
        out_shape=jax.ShapeDtypeStruct((bt, t, DIL_HEADS * HEAD_DIM), BF16),
        scratch_shapes=[pltpu.VMEM((rb, HEAD_DIM), F32)] * (2 * ng),
        compiler_params=_cparams("parallel", "parallel", "parallel"),
        name="dilated_attention",
    )(*([proj] * ng), kv, kv, kv, kv)


def _dil_decode_kernel(*refs):
    ng = len(DIL_GROUPS)
    q_ref, kv_ref = refs[:2]
    k_refs, v_refs = refs[2:2 + ng], refs[2 + ng:2 + 2 * ng]
    o_ref = refs[-1]
    width = DIL_HEADS * HEAD_DIM
    scale = HEAD_DIM ** -0.5
    for h in range(DIL_HEADS):
        sl = slice(h * HEAD_DIM, (h + 1) * HEAD_DIM)
        k_new = kv_ref[:, h * HEAD_DIM:(h + 1) * HEAD_DIM]
        v_new = kv_ref[:, width + h * HEAD_DIM:width + (h + 1) * HEAD_DIM]
        outs, lses = [], []
        for gi in range(ng):
            kc = k_refs[gi][:, h, :].astype(BF16)
            vc = v_refs[gi][:, h, :].astype(BF16)
            q = q_ref[:, gi * width + h * HEAD_DIM:gi * width + (h + 1) * HEAD_DIM]
            s = lax.dot_general(q.astype(BF16), kc, NT_DIMS, preferred_element_type=F32) * scale
            s_self = jnp.sum(q * k_new, axis=-1, keepdims=True) * scale
            m = jnp.maximum(jnp.max(s, axis=-1, keepdims=True), s_self)
            e = jnp.exp(s - m)
            e_self = jnp.exp(s_self - m)
            den = jnp.sum(e, axis=-1, keepdims=True) + e_self
            o = jnp.dot(e.astype(BF16), vc, preferred_element_type=F32) + e_self * v_new
            outs.append(o / den)
            lses.append(m + jnp.log(den))
        mm = functools.reduce(jnp.maximum, lses)
        ws = [jnp.exp(l - mm) for l in lses]
        num = sum(wg * o for wg, o in zip(ws, outs))
        o_ref[:, sl] = (num / sum(ws)).astype(o_ref.dtype)


def dilated_decode(proj, kv_new, k_cache, v_cache):
    bt, rows, nq = proj.shape
    lc = k_cache.shape[1]
    width = DIL_HEADS * HEAD_DIM
    views, specs = [], []
    for cache in (k_cache, v_cache):
        for win, dil in DIL_GROUPS:
            assert lc % win == 0 and win % dil == 0
            w = win // dil
            views.append(cache.reshape(bt, lc // dil, dil, DIL_HEADS, HEAD_DIM))
            specs.append(pl.BlockSpec((None, w, None, DIL_HEADS, HEAD_DIM),
                                      lambda b, last=lc // win - 1: (b, last, 0, 0, 0)))
    return pl.pallas_call(
        _dil_decode_kernel,
        grid=(bt,),
        in_specs=[pl.BlockSpec((None, rows, nq), lambda b: (b, 0, 0)),
                  pl.BlockSpec((None, rows, 2 * width), lambda b: (b, 0, 0))] + specs,
        out_specs=pl.BlockSpec((None, rows, width), lambda b: (b, 0, 0)),
        out_shape=jax.ShapeDtypeStruct((bt, rows, width), BF16),
        compiler_params=_cparams("parallel"),
        name="dilated_decode",
    )(proj, kv_new, *views)


def _split_heads_kernel(kv_ref, k_ref, v_ref):
    width = DIL_HEADS * HEAD_DIM
    for h in range(DIL_HEADS):
        k_ref[:, h, :] = kv_ref[:, h * HEAD_DIM:(h + 1) * HEAD_DIM]
        v_ref[:, h, :] = kv_ref[:, width + h * HEAD_DIM:width + (h + 1) * HEAD_DIM]


def split_kv_heads(kv, first_row, n_rows):
    bt, _, w2 = kv.shape
    tr = min(n_rows, ROW_TILE)
    assert first_row % tr == 0 and n_rows % tr == 0
    heads = pl.BlockSpec((None, tr, DIL_HEADS, HEAD_DIM), lambda b, i: (b, i, 0, 0))
    return pl.pallas_call(
        _split_heads_kernel,
        grid=(bt, n_rows // tr),
        in_specs=[pl.BlockSpec((None, tr, w2), lambda b, i: (b, first_row // tr + i, 0))],
        out_specs=[heads, heads],
        out_shape=[jax.ShapeDtypeStruct((bt, n_rows, DIL_HEADS, HEAD_DIM), F32)] * 2,
        compiler_params=_cparams("parallel", "parallel"),
        name="split_kv_heads",
    )(kv)


def kernel(x_prompt, x_sample, state_conv, state_ssm, cache_win_k, cache_win_v, cache_mem_k, cache_mem_v,
           mem_prompt, norm_mix_g, norm_mlp_g, norm_mem_g, w_mem_k, w_mem_v, mem_q_norm_g, mem_k_norm_g,
           w_up, w_down, w_in_a, conv_w, conv_b, dt_bias, a_log, d_skip, gate_norm_g, w_out_a,
           w_in_b, q_norm_g, w_out_b, kv_norm_g, w_k_shared, w_v_shared, k_norm_g):
    depth = w_up.shape[0]
    n_a = w_in_a.shape[0]
    d_model = x_prompt.shape[-1]
    n_heads = dt_bias.shape[1]
    d_inner = n_heads * SSM_HEAD_DIM
    conv_dim = conv_w.shape[2]
    mem_w = MEM_HEADS * HEAD_DIM
    kv_w = DIL_HEADS * HEAD_DIM
    dil_q_w = len(DIL_GROUPS) * kv_w

    s1, s2, s3 = d_inner, d_inner + conv_dim, d_inner + conv_dim + n_heads
    assert s1 % COL_TILE == 0 and s2 % COL_TILE == 0
    w_in_a_t = jnp.swapaxes(w_in_a, 1, 2)
    w_a_main = round_weight(w_in_a_t, s2)
    tail = lax.optimization_barrier(w_in_a_t[:, s2:]).astype(BF16)
    w_a_tail = jnp.concatenate([tail[:, :n_heads], jnp.zeros((n_a, COL_TILE - n_heads, d_model), BF16),
                                tail[:, n_heads:]], axis=1)
    tiles_a = ([(0, s1 + c, COL_TILE) for c in range(0, conv_dim, COL_TILE)]
               + [(1, COL_TILE + c, COL_TILE) for c in range(0, mem_w, COL_TILE)]
               + [(0, c, COL_TILE) for c in range(0, d_inner, COL_TILE)] + [(1, 0, LANES)])
    kinds_a = (["plain"] * (conv_dim // COL_TILE) + ["norm"] * (mem_w // COL_TILE)
               + ["plain"] * (d_inner // COL_TILE + 1))
    col_xbc, col_qm_a, col_z = 0, conv_dim, conv_dim + mem_w
    col_dt = col_z + d_inner
    a_cols = col_dt + LANES
    kinds_b = ["rope"] * (dil_q_w // COL_TILE) + ["norm"] * (mem_w // COL_TILE)
    kinds_kv = ["rope"] * (kv_w // COL_TILE) + ["plain"] * (kv_w // COL_TILE)
    kinds_mkv = ["norm"] * (mem_w // COL_TILE) + ["plain"] * (mem_w // COL_TILE)
    w_b = w_in_b.astype(BF16)
    w_kv = jnp.concatenate([w_k_shared, w_v_shared], axis=-1).astype(BF16)[None]
    w_mkv = jnp.concatenate([w_mem_k, w_mem_v], axis=-1).astype(BF16)
    w_out_a_b, w_out_b_b = w_out_a.astype(BF16), w_out_b.astype(BF16)
    assert n_a >= 1
    w_up_b, w_down_b = [None] * depth, [None] * depth

    def gain_row(width, pieces):
        parts, at = [], 0
        for start, g, reps in pieces:
            parts += [jnp.ones((start - at,), F32), jnp.tile(g, reps)]
            at = start + reps * HEAD_DIM
        parts.append(jnp.ones((width - at,), F32))
        return jnp.concatenate(parts).reshape(1, width)

    class Group:
        def __init__(self, x, pos_rows, conv_prev, ssm_prev, mem_kv, k_past, v_past, t_real):
            self.bt, self.t, _ = x.shape
            self.m = self.bt * self.t
            self.x2 = x.reshape(self.m, d_model)
            self.cos, self.sin = rope_tables(pos_rows)
            self.conv_prev, self.ssm_prev, self.mem_kv = conv_prev, ssm_prev, mem_kv
            self.k_past, self.v_past, self.t_real = k_past, v_past, t_real
            self.conv_new, self.ssm_new, self.kv3 = [], [], None

    def in_proj_a(prompt, sample, l):
        hg = gain_row(a_cols, [(col_qm_a, mem_q_norm_g[l], MEM_HEADS)])
        proj_p, proj_s, rounded = norm_proj(prompt.x2, norm_mix_g[l], [w_a_main, w_a_tail], l, hg, kinds_a,
                                            tile_src=tiles_a, transposed=True, extra=(sample.x2, None, None),
                                            cast=((w_up, 0),) if l == 0 else ())
        if l == 0:
            (w_up_b[0],) = rounded
        return proj_p, proj_s

    def mix_a(gr, proj, l, round_mlp_weights):
        bt, t, m = gr.bt, gr.t, gr.m
        proj3 = proj.reshape(bt, t, -1)
        h_prev = gr.ssm_prev[l].reshape(bt, SSM_GROUPS, -1, SSM_D_STATE)
        ssm_params = (conv_w[l], conv_b[l], dt_bias[l], a_log[l], d_skip[l], gate_norm_g[l])
        if gr.t_real == 1:
            y, c_new, h_new = ssd_step(proj3, col_xbc, col_dt, col_z, *ssm_params, gr.conv_prev[l], h_prev)
        else:
            assert t % SSD_CHUNK == 0 and gr.t_real == t
            todo = [(w_down, w_down_b, 0)] if l == 0 else []
            if l + 1 < depth:
                todo += [(w_up, w_up_b, l + 1), (w_down, w_down_b, l + 1)]
            todo = todo if round_mlp_weights else []
            ssd = ssd_mixer(proj3, col_xbc, col_dt, col_z, gr.t_real, *ssm_params, gr.conv_prev[l], h_prev,
                            cast=tuple((w, k) for w, _, k in todo))
            if todo:
                ssd, rounded = ssd
                for (_, dst, k), r in zip(todo, rounded):
                    dst[k] = r
            y, c_new, h_new = ssd
        gr.conv_new.append(c_new)
        gr.ssm_new.append(h_new.reshape(bt, n_heads, SSM_HEAD_DIM, SSM_D_STATE))
        mo = mem_attention(proj3, col_qm_a, *gr.mem_kv(l)).reshape(m, mem_w)
        return y.reshape(m, d_inner), mo

    def shared_kv(prompt, sample):
        hg = gain_row(2 * kv_w, [(0, k_norm_g, DIL_HEADS)])
        kv_p, kv_s, _ = norm_proj(prompt.x2, kv_norm_g, w_kv, 0, hg, kinds_kv, prompt.cos, prompt.sin,
                                  extra=(sample.x2, sample.cos, sample.sin))
        prompt.kv3 = kv_p.reshape(prompt.bt, prompt.t, 2 * kv_w)
        sample.kv3 = kv_s.reshape(sample.bt, sample.t, 2 * kv_w)

    def in_proj_b(prompt, sample, l):
        j = l - n_a
        hg = gain_row(dil_q_w + mem_w, [(0, q_norm_g[j], dil_q_w // HEAD_DIM),
                                        (dil_q_w, mem_q_norm_g[l], MEM_HEADS)])
        proj_p, proj_s, _ = norm_proj(prompt.x2, norm_mix_g[l], w_b, j, hg, kinds_b, prompt.cos, prompt.sin,
                                      extra=(sample.x2, sample.cos, sample.sin))
        return proj_p, proj_s

    def mix_b(gr, proj, l):
        proj3 = proj.reshape(gr.bt, gr.t, -1)
        if gr.k_past is None:
            att = dilated_attention(proj3, gr.kv3)
        else:
            att = dilated_decode(proj3, gr.kv3, gr.k_past, gr.v_past)
        mo = mem_attention(proj3, dil_q_w, *gr.mem_kv(l)).reshape(gr.m, mem_w)
        return att.reshape(gr.m, kv_w), mo

    bp, t_p, _ = x_prompt.shape
    mlen = mem_prompt.shape[1]
    mem2 = mem_prompt.reshape(bp * mlen, d_model)
    mkv_p = []
    for l in range(depth):
        hg = gain_row(2 * mem_w, [(0, mem_k_norm_g[l], MEM_HEADS)])
        mkv_p.append(norm_proj(mem2, norm_mem_g[l], w_mkv, l, hg, kinds_mkv)[0].reshape(bp, mlen, 2 * mem_w))
    conv0 = jnp.zeros((n_a, bp, CONV_W - 1, conv_dim), F32)
    ssm0 = jnp.zeros((n_a, bp, n_heads, SSM_HEAD_DIM, SSM_D_STATE), F32)
    prompt = Group(x_prompt, jnp.arange(t_p, dtype=jnp.int32), conv0, ssm0,
                   lambda l: (mkv_p[l], mkv_p[l], (0, mem_w)), None, None, t_p)

    bs, t_s, _ = x_sample.shape
    assert t_s == 1
    xs = jnp.pad(x_sample, ((0, 0), (0, SAMPLE_ROWS - t_s), (0, 0)))
    pos_s = jnp.full((bs * SAMPLE_ROWS,), PAST_LEN, jnp.int32)
    sample = Group(xs, pos_s, state_conv, state_ssm, lambda l: (cache_mem_k, cache_mem_v, l),
                   cache_win_k, cache_win_v, t_s)

    for l in range(depth):
        if l < n_a:
            proj_p, proj_s = in_proj_a(prompt, sample, l)
            a_p, b_p = mix_a(prompt, proj_p, l, round_mlp_weights=True)
            a_s, b_s = mix_a(sample, proj_s, l, round_mlp_weights=False)
            w_out, lw = w_out_a_b, l
        else:
            if l == n_a:
                shared_kv(prompt, sample)
            proj_p, proj_s = in_proj_b(prompt, sample, l)
            a_p, b_p = mix_b(prompt, proj_p, l)
            a_s, b_s = mix_b(sample, proj_s, l)
            w_out, lw = w_out_b_b, l - n_a
        todo = [(w, dst) for w, dst in ((w_up, w_up_b), (w_down, w_down_b)) if l + 1 < depth and dst[l + 1] is None]
        prompt.x2, sample.x2, rounded = mix_mlp(prompt.x2, a_p, b_p, w_out, lw, norm_mlp_g[l], w_up_b[l], w_down_b[l],
                                                cast=tuple((w, l + 1) for w, _ in todo),
                                                extra=(sample.x2, a_s, b_s))
        for (_, dst), r in zip(todo, rounded):
            dst[l + 1] = r

    y_p = prompt.x2.reshape(bp, t_p, d_model)
    conv_p, ssm_p, kv_p = jnp.stack(prompt.conv_new), jnp.stack(prompt.ssm_new), prompt.kv3
    mkv_all = jnp.stack(mkv_p)
    mem_k_p = mkv_all[..., :mem_w].reshape(depth, bp, mlen, MEM_HEADS, HEAD_DIM)
    mem_v_p = mkv_all[..., mem_w:].reshape(depth, bp, mlen, MEM_HEADS, HEAD_DIM)
    keep = min(max(w for w, _ in DIL_GROUPS), t_p)
    win_k_p, win_v_p = split_kv_heads(kv_p, t_p - keep, keep)

    y_s = sample.x2.reshape(bs, SAMPLE_ROWS, d_model)
    conv_s, ssm_s, kv_s = jnp.stack(sample.conv_new), jnp.stack(sample.ssm_new), sample.kv3
    y_s = y_s[:, :t_s]
    k_s = kv_s[:, :t_s, :kv_w].reshape(bs, t_s, DIL_HEADS, HEAD_DIM)
    v_s = kv_s[:, :t_s, kv_w:].reshape(bs, t_s, DIL_HEADS, HEAD_DIM)

    return (y_p, y_s, conv_p, ssm_p, win_k_p, win_v_p, mem_k_p, mem_v_p, conv_s, ssm_s, k_s, v_s)
```

```python
import functools

import jax
import jax.numpy as jnp
from jax import lax
from jax.experimental import pallas as pl
from jax.experimental.pallas import tpu as pltpu

F32 = jnp.float32
BF16 = jnp.bfloat16

HEAD_DIM = 128
SSM_HEAD_DIM = 64
SSM_GROUPS = 4
SSM_D_STATE = 128
CONV_W = 4
SSD_CHUNK = 128
MEM_HEADS = 4
DIL_GROUPS = ((128, 1), (512, 4), (2048, 16))
DIL_HEADS = 4
PAST_LEN = 16384
ROPE_THETA = 10000.0
EPS = 1e-6
LOG2_E = 1.4426950408889634
LN_2 = 0.6931471805599453

LANES = 128
SUBLANES = 8
VMEM_LIMIT = 48 * 1024 * 1024
MIX_MLP_VMEM_LIMIT = 56 * 1024 * 1024
COL_TILE = 512
ROW_TILE = 512
MEM_Q_TILE = 2048
PROJ_ROW_TILE = 256
SAMPLE_ROWS = SUBLANES
DIL_UNROLL = 16

NT_DIMS = (((1,), (1,)), ((), ()))
RESIDENT = pl.Buffered(1)


def _cparams(*sem):
    return pltpu.CompilerParams(dimension_semantics=sem, vmem_limit_bytes=VMEM_LIMIT)


def _sigmoid(x):
    return 1.0 / (1.0 + jnp.exp(-x))


def _rms_scale(x):
    return lax.rsqrt(jnp.mean(x * x, axis=-1, keepdims=True) + EPS)


class _SideCast:
    def __init__(self, mats, steps, step_of):
        self.n = len(mats)
        self.shapes = [a.shape[1:] for a, _ in mats]
        self.views, self.in_specs, self.out_specs, self.out_shape = [], [], [], []
        for a, layer in mats:
            n_layers, r, c = a.shape
            assert r % (steps * 2 * SUBLANES) == 0
            band = r // steps
            self.views.append(a.reshape(n_layers, steps, band, c))
            self.in_specs.append(pl.BlockSpec((None, None, band, c),
                                              lambda *ids, layer=layer: (layer, step_of(*ids), 0, 0)))
            self.out_specs.append(pl.BlockSpec((None, band, c), lambda *ids: (step_of(*ids), 0, 0)))
            self.out_shape.append(jax.ShapeDtypeStruct((steps, band, c), BF16))

    def wrap(self, body, n_in, n_out):
        n = self.n
        if not n:
            return body

        def kern(*refs):
            ins, srcs = refs[:n_in], refs[n_in:n_in + n]
            outs, dsts = refs[n_in + n:n_in + n + n_out], refs[n_in + n + n_out:n_in + 2 * n + n_out]
            for s, d in zip(srcs, dsts):
                d[...] = s[...].astype(BF16)
            body(*ins, *outs, *refs[n_in + 2 * n + n_out:])
        return kern

    def split(self, outs, n_out):
        main = outs[0] if n_out == 1 else tuple(outs[:n_out])
        if not self.n:
            return main
        return main, [o.reshape(s) for o, s in zip(outs[n_out:], self.shapes)]


def _round_kernel(src_ref, dst_ref):
    dst_ref[...] = src_ref[...].astype(BF16)


def round_weight(w, n_rows):
    n_layers, _, c = w.shape
    band = ROW_TILE
    assert n_rows % band == 0
    return pl.pallas_call(
        _round_kernel,
        grid=(n_layers, n_rows // band),
        in_specs=[pl.BlockSpec((None, band, c), lambda l, i: (l, i, 0))],
        out_specs=pl.BlockSpec((None, band, c), lambda l, i: (l, i, 0)),
        out_shape=jax.ShapeDtypeStruct((n_layers, n_rows, c), BF16),
        compiler_params=_cparams("parallel", "parallel"),
        name="round_weight",
    )(w)


def _rope_table_kernel(pos_ref, inv_ref, cos_ref, sin_ref):
    ang = pos_ref[...] * inv_ref[...]
    lane = lax.broadcasted_iota(jnp.int32, ang.shape, 1)
    cos_ref[...] = jnp.cos(ang)
    s = jnp.sin(ang)
    sin_ref[...] = jnp.where(lane < HEAD_DIM // 2, -s, s)


def rope_tables(pos):
    r = pos.shape[0]
    half = HEAD_DIM // 2
    inv = ROPE_THETA ** (-jnp.arange(half, dtype=F32) * (2.0 / HEAD_DIM))
    inv = jnp.concatenate([inv, inv])[None, :]
    posf = jnp.broadcast_to(pos.astype(F32)[:, None], (r, HEAD_DIM))
    tr = min(r, ROW_TILE)
    return pl.pallas_call(
        _rope_table_kernel,
        grid=(r // tr,),
        in_specs=[pl.BlockSpec((tr, HEAD_DIM), lambda i: (i, 0)),
                  pl.BlockSpec((1, HEAD_DIM), lambda i: (0, 0))],
        out_specs=[pl.BlockSpec((tr, HEAD_DIM), lambda i: (i, 0))] * 2,
        out_shape=[jax.ShapeDtypeStruct((r, HEAD_DIM), F32)] * 2,
        compiler_params=_cparams("parallel"),
        name="rope_tables",
    )(posf, inv)


def _proj_kernel(*refs, kinds, tile_src, n_w, transposed, has_extra):
    rope = "rope" in kinds
    refs = list(refs)
    x_ref, g_ref = refs[:2]
    w_refs = refs[2:2 + n_w]
    hg_ref = refs[2 + n_w]
    at = 3 + n_w
    if rope:
        cos_ref, sin_ref = refs[at:at + 2]
        at += 2
    if has_extra:
        xs_ref = refs[at]
        at += 1
        if rope:
            cos_s_ref, sin_s_ref = refs[at:at + 2]
            at += 2
        o_ref, os_ref, xn_ref = refs[at:at + 3]
    else:
        o_ref, xn_ref = refs[at:at + 2]
    tm = x_ref.shape[0]

    def normed(x):
        return (x * _rms_scale(x) * g_ref[...]).astype(BF16)

    def tiles(xn, cos, sin, store):
        for t, kind in enumerate(kinds):
            which, col, width = tile_src[t]
            at_col = sum(w for _, _, w in tile_src[:t])
            if transposed:
                y = lax.dot_general(xn, w_refs[which][col:col + width, :], NT_DIMS, preferred_element_type=F32)
            else:
                y = jnp.dot(xn, w_refs[which][:, col:col + width], preferred_element_type=F32)
            if kind == "plain":
                store(slice(at_col, at_col + width), y)
                continue
            for j in range(width // HEAD_DIM):
                osl = slice(at_col + j * HEAD_DIM, at_col + (j + 1) * HEAD_DIM)
                yh = y[:, j * HEAD_DIM:(j + 1) * HEAD_DIM]
                yh = yh * _rms_scale(yh) * hg_ref[:, osl]
                if kind == "rope":
                    yh = yh * cos + pltpu.roll(yh, HEAD_DIM // 2, axis=1) * sin
                store(osl, yh)

    def store_main(cols, y):
        o_ref[:, cols] = y

    def store_both(cols, y):
        o_ref[:, cols] = y[:tm]
        os_ref[:, cols] = y[tm:]

    xn_ref[:tm, :] = normed(x_ref[...])
    cos = cos_ref[...] if rope else None
    sin = sin_ref[...] if rope else None
    if not has_extra:
        tiles(xn_ref[...], cos, sin, store_main)
        return

    @pl.when(pl.program_id(0) == 0)
    def _():
        xn_ref[tm:, :] = normed(xs_ref[...])
        cos_all = jnp.concatenate([cos, cos_s_ref[...]], axis=0) if rope else None
        sin_all = jnp.concatenate([sin, sin_s_ref[...]], axis=0) if rope else None
        tiles(xn_ref[...], cos_all, sin_all, store_both)

    @pl.when(pl.program_id(0) != 0)
    def _():
        tiles(xn_ref[:tm, :], cos, sin, store_main)


def norm_proj(x, g, ws, layer, head_gain, kinds, cos=None, sin=None, cast=(), tile_src=None, transposed=False,
              extra=None):
    m, d = x.shape
    ws = list(ws) if isinstance(ws, (list, tuple)) else [ws]
    if tile_src is None:
        tile_src = [(0, t * COL_TILE, COL_TILE) for t in range(len(kinds))]
    n_cols = sum(width for _, _, width in tile_src)
    assert len(tile_src) == len(kinds) and head_gain.shape == (1, n_cols)
    assert all(width % HEAD_DIM == 0 for _, _, width in tile_src)
    rope = "rope" in kinds
    has_extra = extra is not None
    ms = extra[0].shape[0] if has_extra else 0
    tm = min(m, PROJ_ROW_TILE)
    assert m % tm == 0
    in_specs = [pl.BlockSpec((tm, d), lambda i: (i, 0)),
                pl.BlockSpec((1, d), lambda i: (0, 0))]
    for k, w in enumerate(ws):
        used = max(col + width for which, col, width in tile_src if which == k)
        if transposed:
            assert used <= w.shape[1] and w.shape[2] == d
            in_specs.append(pl.BlockSpec((None, used, d), lambda i: (layer, 0, 0), pipeline_mode=RESIDENT))
            continue
        assert used <= w.shape[2] and (used % LANES == 0 or used == w.shape[2])
        in_specs.append(pl.BlockSpec((None, d, used), lambda i: (layer, 0, 0), pipeline_mode=RESIDENT))
    in_specs.append(pl.BlockSpec((1, n_cols), lambda i: (0, 0)))
    args = [x, g.reshape(1, d), *ws, head_gain]
    if rope:
        assert cos.shape[0] % tm == 0
        tab_blocks = cos.shape[0] // tm
        in_specs += [pl.BlockSpec((tm, HEAD_DIM), lambda i: (i % tab_blocks, 0))] * 2
        args += [cos, sin]
    out_specs = [pl.BlockSpec((tm, n_cols), lambda i: (i, 0))]
    out_shape = [jax.ShapeDtypeStruct((m, n_cols), F32)]
    if has_extra:
        whole = lambda a: pl.BlockSpec(a.shape, lambda i: (0, 0))
        xs, cos_s, sin_s = extra
        extras = [xs] + ([cos_s, sin_s] if rope else [])
        in_specs += [whole(a) for a in extras]
        args += extras
        out_specs.append(pl.BlockSpec((ms, n_cols), lambda i: (0, 0)))
        out_shape.append(jax.ShapeDtypeStruct((ms, n_cols), F32))
    n_out = len(out_specs)
    side = _SideCast(cast, m // tm, lambda i: i)
    outs = pl.pallas_call(
        side.wrap(functools.partial(_proj_kernel, kinds=tuple(kinds), tile_src=tuple(tile_src), n_w=len(ws),
                                    transposed=transposed, has_extra=has_extra),
                  len(args), n_out),
        grid=(m // tm,),
        in_specs=in_specs + side.in_specs,
        out_specs=out_specs + side.out_specs,
        out_shape=out_shape + side.out_shape,
        scratch_shapes=[pltpu.VMEM((tm + ms, d), BF16)],
        compiler_params=_cparams("arbitrary"),
        name="norm_proj",
    )(*args, *side.views)
    casts = [o.reshape(shp) for o, shp in zip(outs[n_out:], side.shapes)]
    return outs[0], (outs[1] if has_extra else None), casts


def _mix_mlp_kernel(*refs, has_extra):
    if has_extra:
        (x_ref, g_ref, wu_ref, wd_ref, a_ref, b_ref, wa_ref, wb_ref, xs_ref, as_ref, bs_ref,
         o_ref, os_ref, xn_ref) = refs
    else:
        x_ref, g_ref, wu_ref, wd_ref, a_ref, b_ref, wa_ref, wb_ref, o_ref, xn_ref = refs
    i, f = pl.program_id(0), pl.program_id(1)
    tm = x_ref.shape[0]

    def normed(x):
        return (x * _rms_scale(x) * g_ref[...]).astype(BF16)

    def mixed(a, b):
        acc = jnp.dot(a, wa_ref[...], preferred_element_type=F32)
        return acc + jnp.dot(b, wb_ref[...], preferred_element_type=F32)

    def ffn(xn):
        h = jnp.dot(xn, wu_ref[...], preferred_element_type=F32)
        h = jnp.maximum(h, 0.0)
        h = (h * h).astype(BF16)
        return jnp.dot(h, wd_ref[...], preferred_element_type=F32)

    def start_main():
        x1 = x_ref[...] + mixed(a_ref[...], b_ref[...])
        xn_ref[:tm, :] = normed(x1)
        o_ref[...] = x1

    def start_both():
        acc = mixed(jnp.concatenate([a_ref[...], as_ref[...]], axis=0),
                    jnp.concatenate([b_ref[...], bs_ref[...]], axis=0))
        x1 = x_ref[...] + acc[:tm]
        xs1 = xs_ref[...] + acc[tm:]
        xn_ref[:tm, :] = normed(x1)
        xn_ref[tm:, :] = normed(xs1)
        o_ref[...] = x1
        os_ref[...] = xs1

    if not has_extra:
        pl.when(f == 0)(start_main)
        o_ref[...] += ffn(xn_ref[...])
        return

    @pl.when(i == 0)
    def _():
        pl.when(f == 0)(start_both)
        upd = ffn(xn_ref[...])
        o_ref[...] += upd[:tm]
        os_ref[...] += upd[tm:]

    @pl.when(i != 0)
    def _():
        pl.when(f == 0)(start_main)
        o_ref[...] += ffn(xn_ref[:tm, :])


def mix_mlp(x, a, b, w_out, layer, g, wu, wd, tf=1024, cast=(), extra=None):
    m, d = x.shape
    ff = wu.shape[1]
    tm = min(m, ROW_TILE)
    nf = ff // tf
    ka, kb = a.shape[1], b.shape[1]
    assert w_out.shape[1] == ka + kb and ka % kb == 0
    has_extra = extra is not None
    side = _SideCast(cast, (m // tm) * nf, lambda i, j: i * nf + j)
    in_specs = [pl.BlockSpec((tm, d), lambda i, j: (i, 0)),
                pl.BlockSpec((1, d), lambda i, j: (0, 0)),
                pl.BlockSpec((d, tf), lambda i, j: (0, j)),
                pl.BlockSpec((tf, d), lambda i, j: (j, 0)),
                pl.BlockSpec((tm, ka), lambda i, j: (i, 0)),
                pl.BlockSpec((tm, kb), lambda i, j: (i, 0)),
                pl.BlockSpec((None, ka, d), lambda i, j: (layer, 0, 0), pipeline_mode=RESIDENT),
                pl.BlockSpec((None, kb, d), lambda i, j: (layer, ka // kb, 0), pipeline_mode=RESIDENT)]
    out_specs = [pl.BlockSpec((tm, d), lambda i, j: (i, 0))]
    out_shape = [jax.ShapeDtypeStruct((m, d), F32)]
    args = [x, g.reshape(1, d), wu, wd, a, b, w_out, w_out]
    ms = extra[0].shape[0] if has_extra else 0
    if has_extra:
        in_specs += [pl.BlockSpec(e.shape, lambda i, j: (0, 0)) for e in extra]
        args += list(extra)
        out_specs.append(pl.BlockSpec((ms, d), lambda i, j: (0, 0)))
        out_shape.append(jax.ShapeDtypeStruct((ms, d), F32))
    n_out = len(out_specs)
    outs = pl.pallas_call(
        side.wrap(functools.partial(_mix_mlp_kernel, has_extra=has_extra), len(args), n_out),
        grid=(m // tm, nf),
        in_specs=in_specs + side.in_specs,
        out_specs=out_specs + side.out_specs,
        out_shape=out_shape + side.out_shape,
        scratch_shapes=[pltpu.VMEM((tm + ms, d), BF16)],
        compiler_params=pltpu.CompilerParams(dimension_semantics=("arbitrary", "arbitrary"),
                                             vmem_limit_bytes=MIX_MLP_VMEM_LIMIT),
        name="mix_mlp",
    )(*args, *side.views)
    casts = [o.reshape(shp) for o, shp in zip(outs[n_out:], side.shapes)]
    return outs[0], (outs[1] if has_extra else None), casts


def _split3(x):
    hi = x.astype(BF16)
    r1 = x - hi.astype(F32)
    mid = r1.astype(BF16)
    lo = (r1 - mid.astype(F32)).astype(BF16)
    return hi, mid, lo


def _ssd_kernel(xbc_ref, dt_ref, z_ref, cw_ref, cb_ref, dtb_ref, alog_ref, dsk_ref, gg_ref, tri_ref, expand_ref,
                conv0_ref, h0_ref, y_ref, convo_ref, ho_ref,
                ext_ref, ht_ref, yd_ref, st_ref, *, t_real, n_heads, d_inner):
    L = SSD_CHUNK
    P = SSM_HEAD_DIM
    N = SSM_D_STATE
    G = SSM_GROUPS
    hpg = n_heads // G
    gw = hpg * P
    hist = CONV_W - 1
    base = SUBLANES - hist
    c = pl.program_id(1)
    nc = pl.num_programs(1)

    @pl.when(c == 0)
    def _():
        ext_ref[base:SUBLANES, :] = conv0_ref[...]
        for g in range(G):
            ht_ref[g] = h0_ref[g].T

    ext_ref[SUBLANES:SUBLANES + L, :] = xbc_ref[...]
    ext = ext_ref[...]
    acc = cb_ref[...] + ext[SUBLANES:, :] * cw_ref[hist:CONV_W, :]
    for k in range(1, CONV_W):
        acc = acc + pltpu.roll(ext, k, axis=0)[SUBLANES:, :] * cw_ref[hist - k:CONV_W - k, :]
    xc = acc * _sigmoid(acc)

    @pl.when(c == nc - 1)
    def _():
        n_in_last = t_real - ((t_real - 1) // L) * L
        convo_ref[...] = ext_ref[base + n_in_last:base + n_in_last + hist, :]

    ext_ref[base:SUBLANES, :] = ext_ref[base + L:SUBLANES + L, :]

    dt_raw = dt_ref[...] + dtb_ref[...]
    dt = jnp.maximum(dt_raw, 0.0) + jnp.log1p(jnp.exp(-jnp.abs(dt_raw)))
    row = lax.broadcasted_iota(jnp.int32, (L, LANES), 0) + c * L
    dt = jnp.where(row < t_real, dt, 0.0)
    da = dt * (-jnp.exp(alog_ref[...]))
    li = lax.broadcasted_iota(jnp.int32, (L, L), 0)
    si = lax.broadcasted_iota(jnp.int32, (L, L), 1)
    causal = li >= si
    a_cs = sum(jnp.dot(tri_ref[...], part, preferred_element_type=F32) for part in _split3(da))
    a_cs_t = a_cs.T
    dt_t = dt.T
    w_t = dt_t * jnp.exp(a_cs_t[:, L - 1:L] - a_cs_t)
    e_exp = sum(jnp.dot(part, expand_ref[...], preferred_element_type=F32) for part in _split3(jnp.exp(a_cs)))
    xb = xc[:, :d_inner].astype(BF16)
    low_half = lax.broadcasted_iota(jnp.int32, (L, 2 * P), 1) < P

    for g in range(G):
        b_g = xc[:, d_inner + g * N:d_inner + (g + 1) * N]
        c_g = xc[:, d_inner + G * N + g * N:d_inner + G * N + (g + 1) * N].astype(BF16)
        cbm = lax.dot_general(c_g, b_g.astype(BF16), NT_DIMS, preferred_element_type=F32)
        b_t = b_g.T
        y_off = jnp.dot(c_g, ht_ref[g].astype(BF16), preferred_element_type=F32)
        for kp in range(hpg // 2):
            h0 = g * hpg + 2 * kp
            x_pair = xb[:, h0 * P:(h0 + 2) * P]
            yd, st = [], []
            for h in (h0, h0 + 1):
                seg = a_cs[:, h:h + 1] - a_cs_t[h:h + 1, :]
                dec = jnp.exp(jnp.where(causal, seg, -jnp.inf))
                mh = (cbm * dec * dt_t[h:h + 1, :]).astype(BF16)
                yd.append(jnp.dot(mh, x_pair, preferred_element_type=F32))
                btw = (b_t * w_t[h:h + 1, :]).astype(BF16)
                st.append(jnp.dot(btw, x_pair, preferred_element_type=F32))
            yd_ref[:, 2 * kp * P:(2 * kp + 2) * P] = jnp.where(low_half, yd[0], yd[1])
            st_ref[:, 2 * kp * P:(2 * kp + 2) * P] = jnp.where(low_half, st[0], st[1])
        gs = slice(g * gw, (g + 1) * gw)
        e_g = e_exp[:, gs]
        y_g = yd_ref[...] + y_off * e_g + dsk_ref[:, gs] * xc[:, gs]
        ht_ref[g] = ht_ref[g] * e_g[L - 1:L, :] + st_ref[...]
        z_g = z_ref[:, gs]
        gated = y_g * (z_g * _sigmoid(z_g))
        y_ref[:, gs] = (gated * _rms_scale(gated) * gg_ref[:, gs]).astype(y_ref.dtype)

    @pl.when(c == nc - 1)
    def _():
        for g in range(G):
            ho_ref[g] = ht_ref[g].T


def ssd_mixer(proj, col_xbc, col_dt, col_z, t_real, conv_w, conv_b, dt_bias, a_log, d_skip, gate_g,
              conv_prev, h_prev, cast=()):
    bt, tp, _ = proj.shape
    conv_dim = conv_w.shape[1]
    n_heads = dt_bias.shape[0]
    d_inner = n_heads * SSM_HEAD_DIM
    gw = d_inner // SSM_GROUPS
    L = SSD_CHUNK
    assert tp % L == 0 and (tp - t_real) < L
    assert col_xbc % conv_dim == 0 and col_dt % LANES == 0 and col_z % d_inner == 0
    pad = LANES - n_heads
    dtb = jnp.pad(dt_bias, (0, pad)).reshape(1, LANES)
    alog = jnp.pad(a_log, (0, pad)).reshape(1, LANES)
    dsk = jnp.repeat(d_skip, SSM_HEAD_DIM).reshape(1, d_inner)
    tri = jnp.tril(jnp.ones((L, L), BF16))
    expand = jnp.repeat(jnp.eye(LANES, dtype=BF16)[:, :n_heads], SSM_HEAD_DIM, axis=1)
    kern = functools.partial(_ssd_kernel, t_real=t_real, n_heads=n_heads, d_inner=d_inner)
    vec = lambda n: pl.BlockSpec((1, n), lambda b, c: (0, 0))
    nc = tp // L
    side = _SideCast(cast, bt * nc, lambda b, c: b * nc + c)
    outs = pl.pallas_call(
        side.wrap(kern, 13, 3),
        grid=(bt, nc),
        in_specs=[pl.BlockSpec((None, L, conv_dim), lambda b, c: (b, c, col_xbc // conv_dim)),
                  pl.BlockSpec((None, L, LANES), lambda b, c: (b, c, col_dt // LANES)),
                  pl.BlockSpec((None, L, d_inner), lambda b, c: (b, c, col_z // d_inner)),
                  pl.BlockSpec((CONV_W, conv_dim), lambda b, c: (0, 0)),
                  vec(conv_dim), vec(LANES), vec(LANES), vec(d_inner), vec(d_inner),
                  pl.BlockSpec((L, L), lambda b, c: (0, 0)),
                  pl.BlockSpec((LANES, d_inner), lambda b, c: (0, 0)),
                  pl.BlockSpec((None, CONV_W - 1, conv_dim), lambda b, c: (b, 0, 0)),
                  pl.BlockSpec((None, SSM_GROUPS, gw, SSM_D_STATE), lambda b, c: (b, 0, 0, 0))] + side.in_specs,
        out_specs=[pl.BlockSpec((None, L, d_inner), lambda b, c: (b, c, 0)),
                   pl.BlockSpec((None, CONV_W - 1, conv_dim), lambda b, c: (b, 0, 0)),
                   pl.BlockSpec((None, SSM_GROUPS, gw, SSM_D_STATE), lambda b, c: (b, 0, 0, 0))] + side.out_specs,
        out_shape=[jax.ShapeDtypeStruct((bt, tp, d_inner), BF16),
                   jax.ShapeDtypeStruct((bt, CONV_W - 1, conv_dim), F32),
                   jax.ShapeDtypeStruct((bt, SSM_GROUPS, gw, SSM_D_STATE), F32)] + side.out_shape,
        scratch_shapes=[pltpu.VMEM((SUBLANES + L, conv_dim), F32),
                        pltpu.VMEM((SSM_GROUPS, SSM_D_STATE, gw), F32),
                        pltpu.VMEM((L, gw), F32),
                        pltpu.VMEM((SSM_D_STATE, gw), F32)],
        compiler_params=_cparams("parallel", "arbitrary"),
        name="ssd_mixer",
    )(proj, proj, proj, conv_w, conv_b.reshape(1, conv_dim), dtb, alog, dsk, gate_g.reshape(1, d_inner),
      tri, expand, conv_prev, h_prev, *side.views)
    return side.split(outs, 3)


def _ssd_step_kernel(xbc_ref, dt_ref, z_ref, cw_ref, cb_ref, dtb_ref, alog_ref, dsk_ref, gg_ref, expand_ref,
                     conv0_ref, h0_ref, y_ref, convo_ref, ho_ref, *, n_heads, d_inner):
    P, N, G = SSM_HEAD_DIM, SSM_D_STATE, SSM_GROUPS
    gw = n_heads // G * P
    hist = CONV_W - 1
    acc = cb_ref[...] + xbc_ref[...] * cw_ref[hist:CONV_W, :]
    for k in range(hist):
        acc = acc + conv0_ref[k:k + 1, :] * cw_ref[k:k + 1, :]
    xc = acc * _sigmoid(acc)
    convo_ref[0:hist - 1, :] = conv0_ref[1:hist, :]
    convo_ref[hist - 1:hist, :] = xbc_ref[0:1, :]

    dt_raw = dt_ref[...] + dtb_ref[...]
    dt = jnp.maximum(dt_raw, 0.0) + jnp.log1p(jnp.exp(-jnp.abs(dt_raw)))
    decay = jnp.exp(dt * (-jnp.exp(alog_ref[...])))
    expand = lambda v: sum(jnp.dot(part, expand_ref[...], preferred_element_type=F32) for part in _split3(v))
    dt_x = expand(dt) * xc[:, :d_inner]
    decay_c = expand(decay)

    for g in range(G):
        gs = slice(g * gw, (g + 1) * gw)
        b_col = xc[:, d_inner + g * N:d_inner + (g + 1) * N].T[:, 0:1]
        c_g = xc[:, d_inner + G * N + g * N:d_inner + G * N + (g + 1) * N].astype(BF16)
        h_t = h0_ref[g].T * decay_c[0:1, gs] + b_col * dt_x[0:1, gs]
        ho_ref[g] = h_t.T
        y_g = jnp.dot(c_g, h_t.astype(BF16), preferred_element_type=F32) + dsk_ref[:, gs] * xc[:, gs]
        z_g = z_ref[:, gs]
        gated = y_g * (z_g * _sigmoid(z_g))
        y_ref[:, gs] = (gated * _rms_scale(gated) * gg_ref[:, gs]).astype(y_ref.dtype)


def ssd_step(proj, col_xbc, col_dt, col_z, conv_w, conv_b, dt_bias, a_log, d_skip, gate_g, conv_prev, h_prev):
    bt, rows, _ = proj.shape
    conv_dim = conv_w.shape[1]
    n_heads = dt_bias.shape[0]
    d_inner = n_heads * SSM_HEAD_DIM
    gw = d_inner // SSM_GROUPS
    assert col_xbc % conv_dim == 0 and col_dt % LANES == 0 and col_z % d_inner == 0
    pad = LANES - n_heads
    dtb = jnp.pad(dt_bias, (0, pad)).reshape(1, LANES)
    alog = jnp.pad(a_log, (0, pad)).reshape(1, LANES)
    dsk = jnp.repeat(d_skip, SSM_HEAD_DIM).reshape(1, d_inner)
    expand = jnp.repeat(jnp.eye(LANES, dtype=BF16)[:, :n_heads], SSM_HEAD_DIM, axis=1)
    vec = lambda n: pl.BlockSpec((1, n), lambda b: (0, 0))
    state = pl.BlockSpec((None, SSM_GROUPS, gw, SSM_D_STATE), lambda b: (b, 0, 0, 0))
    hist_rows = pl.BlockSpec((None, CONV_W - 1, conv_dim), lambda b: (b, 0, 0))
    return pl.pallas_call(
        functools.partial(_ssd_step_kernel, n_heads=n_heads, d_inner=d_inner),
        grid=(bt,),
        in_specs=[pl.BlockSpec((None, rows, conv_dim), lambda b: (b, 0, col_xbc // conv_dim)),
                  pl.BlockSpec((None, rows, LANES), lambda b: (b, 0, col_dt // LANES)),
                  pl.BlockSpec((None, rows, d_inner), lambda b: (b, 0, col_z // d_inner)),
                  pl.BlockSpec((CONV_W, conv_dim), lambda b: (0, 0)),
                  vec(conv_dim), vec(LANES), vec(LANES), vec(d_inner), vec(d_inner),
                  pl.BlockSpec((LANES, d_inner), lambda b: (0, 0)),
                  hist_rows, state],
        out_specs=[pl.BlockSpec((None, rows, d_inner), lambda b: (b, 0, 0)), hist_rows, state],
        out_shape=[jax.ShapeDtypeStruct((bt, rows, d_inner), BF16),
                   jax.ShapeDtypeStruct((bt, CONV_W - 1, conv_dim), F32),
                   jax.ShapeDtypeStruct((bt, SSM_GROUPS, gw, SSM_D_STATE), F32)],
        compiler_params=_cparams("parallel"),
        name="ssd_step",
    )(proj, proj, proj, conv_w, conv_b.reshape(1, conv_dim), dtb, alog, dsk, gate_g.reshape(1, d_inner),
      expand, conv_prev, h_prev)


def _mem_attn_kernel(q_ref, k_ref, v_ref, o_ref):
    scale = HEAD_DIM ** -0.5
    headed = len(k_ref.shape) == 3
    for h in range(MEM_HEADS):
        sl = slice(h * HEAD_DIM, (h + 1) * HEAD_DIM)
        q = (q_ref[:, sl] * (scale * LOG2_E)).astype(BF16)
        k = (k_ref[:, h, :] if headed else k_ref[:, sl]).astype(BF16)
        v = (v_ref[:, h, :] if headed else v_ref[:, sl]).astype(BF16)
        s = lax.dot_general(q, k, NT_DIMS, preferred_element_type=F32)
        e = jnp.exp2(s - jnp.max(s, axis=-1, keepdims=True))
        den = jnp.sum(e, axis=-1, keepdims=True)
        o = jnp.dot(e.astype(BF16), v, preferred_element_type=F32) / den
        o_ref[:, sl] = o.astype(o_ref.dtype)


def mem_attention(proj, col_q, k, v, kv_at):
    bt, t, _ = proj.shape
    width = MEM_HEADS * HEAD_DIM
    tq = min(t, MEM_Q_TILE)
    assert t % tq == 0
    if k.ndim == 5:
        mlen = k.shape[2]
        kv_specs = [pl.BlockSpec((None, None, mlen, MEM_HEADS, HEAD_DIM), lambda b, i: (kv_at, b, 0, 0, 0))] * 2
    else:
        mlen = k.shape[1]
        kv_specs = [pl.BlockSpec((None, mlen, width), lambda b, i, c=c: (b, 0, c // width)) for c in kv_at]
    return pl.pallas_call(
        _mem_attn_kernel,
        grid=(bt, t // tq),
        in_specs=[pl.BlockSpec((None, tq, width), lambda b, i: (b, i, col_q // width))] + kv_specs,
        out_specs=pl.BlockSpec((None, tq, width), lambda b, i: (b, i, 0)),
        out_shape=jax.ShapeDtypeStruct((bt, t, width), BF16),
        compiler_params=_cparams("parallel", "parallel"),
        name="mem_attention",
    )(proj, k, v)


def _run_units(first, count, unit, unroll):
    trips = count // unroll
    if trips == 1:
        trips = 0
    if trips:
        def trip(it, carry):
            for k in range(unroll):
                unit(first + it * unroll + k)
            return carry
        lax.fori_loop(0, trips, trip, 0)
    for k in range(trips * unroll, count):
        unit(first + k)


def _dil_attn_kernel(*refs, groups, rb, unroll):
    ng = len(groups)
    q_refs = refs[:ng]
    kp_ref, kc_ref, vp_ref, vc_ref, o_ref = refs[ng:ng + 5]
    og = refs[ng + 5:2 * ng + 5]
    lg = refs[2 * ng + 5:3 * ng + 5]
    i = pl.program_id(1)
    scale = HEAD_DIM ** -0.5

    for gi, (win, d) in enumerate(groups):
        w = win // d
        dw = d * w
        qi = lax.broadcasted_iota(jnp.int32, (w, 2 * w), 0)
        kj = lax.broadcasted_iota(jnp.int32, (w, 2 * w), 1)
        band = (kj >= qi) & (kj <= qi + w)
        band_first = band & (kj >= jnp.where(i > 0, 0, w))

        def rows(start, size, d=d, w=w):
            if d == 1:
                return pl.ds(pl.multiple_of(start, w), size)
            return pl.ds(start, size, stride=d)

        def attend(rows_q, kk, vv, mask, gi=gi, w=w):
            q = (q_refs[gi][rows_q, :] * (scale * LOG2_E)).astype(BF16)
            s = lax.dot_general(q, kk.astype(BF16), NT_DIMS, preferred_element_type=F32)
            s = jnp.where(mask, s, -jnp.inf)
            m = jnp.max(s, axis=-1, keepdims=True)
            e = jnp.exp2(s - m)
            den = jnp.sum(e, axis=-1, keepdims=True)
            o = jnp.dot(e.astype(BF16), vv.astype(BF16), preferred_element_type=F32) / den
            og[gi][rows_q, :] = o
            lg[gi][rows_q, :] = jnp.broadcast_to(m * LN_2 + jnp.log(den), (w, HEAD_DIM))

        def first_unit(r, rows=rows, attend=attend, w=w, dw=dw, mask=band_first):
            rows_q, rows_p = rows(r, w), rows(rb - dw + r, w)
            kk = jnp.concatenate([kp_ref[rows_p, :], kc_ref[rows_q, :]], axis=0)
            vv = jnp.concatenate([vp_ref[rows_p, :], vc_ref[rows_q, :]], axis=0)
            attend(rows_q, kk, vv, mask)

        def later_unit(u, rows=rows, attend=attend, d=d, w=w, dw=dw, mask=band):
            sb = u // d
            start_q = sb * dw + (u - sb * d)
            rows_k = rows(start_q - dw, 2 * w)
            attend(rows(start_q, w), kc_ref[rows_k, :], vc_ref[rows_k, :], mask)

        _run_units(0, d, first_unit, unroll)
        _run_units(d, rb // w - d, later_unit, unroll)

    ls = [r[...] for r in lg]
    mm = functools.reduce(jnp.maximum, ls)
    ws = [jnp.exp(l - mm) for l in ls]
    num = sum(wg * r[...] for wg, r in zip(ws, og))
    o_ref[...] = (num / sum(ws)).astype(o_ref.dtype)


def dilated_attention(proj, kv):
    bt, t, _ = proj.shape
    ng = len(DIL_GROUPS)
    rb = min(t, max(win for win, _ in DIL_GROUPS))
    for win, d in DIL_GROUPS:
        assert win % d == 0 and rb % win == 0
    assert t % rb == 0
    blk = lambda f: pl.BlockSpec((None, rb, HEAD_DIM), f)
    prev = lambda i: jnp.maximum(i - 1, 0)
    q_specs = [blk(lambda b, i, h, g=g: (b, i, g * DIL_HEADS + h)) for g in range(ng)]
    kv_specs = [blk(lambda b, i, h: (b, prev(i), h)), blk(lambda b, i, h: (b, i, h)),
                blk(lambda b, i, h: (b, prev(i), DIL_HEADS + h)), blk(lambda b, i, h: (b, i, DIL_HEADS + h))]
    return pl.pallas_call(
        functools.partial(_dil_attn_kernel, groups=DIL_GROUPS, rb=rb, unroll=DIL_UNROLL),
        grid=(bt, t // rb, DIL_HEADS),
        in_specs=q_specs + kv_specs,
        out_specs=blk(lambda b, i, h: (b, i, h)),
        out_shape=jax.ShapeDtypeStruct((bt, t, DIL_HEADS * HEAD_DIM), BF16),
        scratch_shapes=[pltpu.VMEM((rb, HEAD_DIM), F32)] * (2 * ng),
        compiler_params=_cparams("parallel", "parallel", "parallel"),
        name="dilated_attention",
    )(*([proj] * ng), kv, kv, kv, kv)


def _dil_decode_kernel(*refs):
    ng = len(DIL_GROUPS)
    q_ref, kv_ref = refs[:2]
    k_refs, v_refs = refs[2:2 + ng], refs[2 + ng:2 + 2 * ng]
    o_ref = refs[-1]
    width = DIL_HEADS * HEAD_DIM
    scale = HEAD_DIM ** -0.5
    for h in range(DIL_HEADS):
        sl = slice(h * HEAD_DIM, (h + 1) * HEAD_DIM)
        k_new = kv_ref[:, h * HEAD_DIM:(h + 1) * HEAD_DIM]
        v_new = kv_ref[:, width + h * HEAD_DIM:width + (h + 1) * HEAD_DIM]
        outs, lses = [], []
        for gi in range(ng):
            kc = k_refs[gi][:, h, :].astype(BF16)
            vc = v_refs[gi][:, h, :].astype(BF16)
            q = q_ref[:, gi * width + h * HEAD_DIM:gi * width + (h + 1) * HEAD_DIM]
            s = lax.dot_general(q.astype(BF16), kc, NT_DIMS, preferred_element_type=F32) * scale
            s_self = jnp.sum(q * k_new, axis=-1, keepdims=True) * scale
            m = jnp.maximum(jnp.max(s, axis=-1, keepdims=True), s_self)
            e = jnp.exp(s - m)
            e_self = jnp.exp(s_self - m)
            den = jnp.sum(e, axis=-1, keepdims=True) + e_self
            o = jnp.dot(e.astype(BF16), vc, preferred_element_type=F32) + e_self * v_new
            outs.append(o / den)
            lses.append(m + jnp.log(den))
        mm = functools.reduce(jnp.maximum, lses)
        ws = [jnp.exp(l - mm) for l in lses]
        num = sum(wg * o for wg, o in zip(ws, outs))
        o_ref[:, sl] = (num / sum(ws)).astype(o_ref.dtype)


def dilated_decode(proj, kv_new, k_cache, v_cache):
    bt, rows, nq = proj.shape
    lc = k_cache.shape[1]
    width = DIL_HEADS * HEAD_DIM
    views, specs = [], []
    for cache in (k_cache, v_cache):
        for win, dil in DIL_GROUPS:
            assert lc % win == 0 and win % dil == 0
            w = win // dil
            views.append(cache.reshape(bt, lc // dil, dil, DIL_HEADS, HEAD_DIM))
            specs.append(pl.BlockSpec((None, w, None, DIL_HEADS, HEAD_DIM),
                                      lambda b, last=lc // win - 1: (b, last, 0, 0, 0)))
    return pl.pallas_call(
        _dil_decode_kernel,
        grid=(bt,),
        in_specs=[pl.BlockSpec((None, rows, nq), lambda b: (b, 0, 0)),
                  pl.BlockSpec((None, rows, 2 * width), lambda b: (b, 0, 0))] + specs,
        out_specs=pl.BlockSpec((None, rows, width), lambda b: (b, 0, 0)),
        out_shape=jax.ShapeDtypeStruct((bt, rows, width), BF16),
        compiler_params=_cparams("parallel"),
        name="dilated_decode",
    )(proj, kv_new, *views)


def _split_heads_kernel(kv_ref, k_ref, v_ref):
    width = DIL_HEADS * HEAD_DIM
    for h in range(DIL_HEADS):
        k_ref[:, h, :] = kv_ref[:, h * HEAD_DIM:(h + 1) * HEAD_DIM]
        v_ref[:, h, :] = kv_ref[:, width + h * HEAD_DIM:width + (h + 1) * HEAD_DIM]


def split_kv_heads(kv, first_row, n_rows):
    bt, _, w2 = kv.shape
    tr = min(n_rows, ROW_TILE)
    assert first_row % tr == 0 and n_rows % tr == 0
    heads = pl.BlockSpec((None, tr, DIL_HEADS, HEAD_DIM), lambda b, i: (b, i, 0, 0))
    return pl.pallas_call(
        _split_heads_kernel,
        grid=(bt, n_rows // tr),
        in_specs=[pl.BlockSpec((None, tr, w2), lambda b, i: (b, first_row // tr + i, 0))],
        out_specs=[heads, heads],
        out_shape=[jax.ShapeDtypeStruct((bt, n_rows, DIL_HEADS, HEAD_DIM), F32)] * 2,
        compiler_params=_cparams("parallel", "parallel"),
        name="split_kv_heads",
    )(kv)


def kernel(x_prompt, x_sample, state_conv, state_ssm, cache_win_k, cache_win_v, cache_mem_k, cache_mem_v,
           mem_prompt, norm_mix_g, norm_mlp_g, norm_mem_g, w_mem_k, w_mem_v, mem_q_norm_g, mem_k_norm_g,
           w_up, w_down, w_in_a, conv_w, conv_b, dt_bias, a_log, d_skip, gate_norm_g, w_out_a,
           w_in_b, q_norm_g, w_out_b, kv_norm_g, w_k_shared, w_v_shared, k_norm_g):
    depth = w_up.shape[0]
    n_a = w_in_a.shape[0]
    d_model = x_prompt.shape[-1]
    n_heads = dt_bias.shape[1]
    d_inner = n_heads * SSM_HEAD_DIM
    conv_dim = conv_w.shape[2]
    mem_w = MEM_HEADS * HEAD_DIM
    kv_w = DIL_HEADS * HEAD_DIM
    dil_q_w = len(DIL_GROUPS) * kv_w

    s1, s2, s3 = d_inner, d_inner + conv_dim, d_inner + conv_dim + n_heads
    assert s1 % COL_TILE == 0 and s2 % COL_TILE == 0
    w_in_a_t = jnp.swapaxes(w_in_a, 1, 2)
    w_a_main = round_weight(w_in_a_t, s2)
    tail = lax.optimization_barrier(w_in_a_t[:, s2:]).astype(BF16)
    w_a_tail = jnp.concatenate([tail[:, :n_heads], jnp.zeros((n_a, COL_TILE - n_heads, d_model), BF16),
                                tail[:, n_heads:]], axis=1)
    tiles_a = ([(0, s1 + c, COL_TILE) for c in range(0, conv_dim, COL_TILE)]
               + [(1, COL_TILE + c, COL_TILE) for c in range(0, mem_w, COL_TILE)]
               + [(0, c, COL_TILE) for c in range(0, d_inner, COL_TILE)] + [(1, 0, LANES)])
    kinds_a = (["plain"] * (conv_dim // COL_TILE) + ["norm"] * (mem_w // COL_TILE)
               + ["plain"] * (d_inner // COL_TILE + 1))
    col_xbc, col_qm_a, col_z = 0, conv_dim, conv_dim + mem_w
    col_dt = col_z + d_inner
    a_cols = col_dt + LANES
    kinds_b = ["rope"] * (dil_q_w // COL_TILE) + ["norm"] * (mem_w // COL_TILE)
    kinds_kv = ["rope"] * (kv_w // COL_TILE) + ["plain"] * (kv_w // COL_TILE)
    kinds_mkv = ["norm"] * (mem_w // COL_TILE) + ["plain"] * (mem_w // COL_TILE)
    w_b = w_in_b.astype(BF16)
    w_kv = jnp.concatenate([w_k_shared, w_v_shared], axis=-1).astype(BF16)[None]
    w_mkv = jnp.concatenate([w_mem_k, w_mem_v], axis=-1).astype(BF16)
    w_out_a_b, w_out_b_b = w_out_a.astype(BF16), w_out_b.astype(BF16)
    assert n_a >= 1
    w_up_b, w_down_b = [None] * depth, [None] * depth

    def gain_row(width, pieces):
        parts, at = [], 0
        for start, g, reps in pieces:
            parts += [jnp.ones((start - at,), F32), jnp.tile(g, reps)]
            at = start + reps * HEAD_DIM
        parts.append(jnp.ones((width - at,), F32))
        return jnp.concatenate(parts).reshape(1, width)

    class Group:
        def __init__(self, x, pos_rows, conv_prev, ssm_prev, mem_kv, k_past, v_past, t_real):
            self.bt, self.t, _ = x.shape
            self.m = self.bt * self.t
            self.x2 = x.reshape(self.m, d_model)
            self.cos, self.sin = rope_tables(pos_rows)
            self.conv_prev, self.ssm_prev, self.mem_kv = conv_prev, ssm_prev, mem_kv
            self.k_past, self.v_past, self.t_real = k_past, v_past, t_real
            self.conv_new, self.ssm_new, self.kv3 = [], [], None

    def in_proj_a(prompt, sample, l):
        hg = gain_row(a_cols, [(col_qm_a, mem_q_norm_g[l], MEM_HEADS)])
        proj_p, proj_s, rounded = norm_proj(prompt.x2, norm_mix_g[l], [w_a_main, w_a_tail], l, hg, kinds_a,
                                            tile_src=tiles_a, transposed=True, extra=(sample.x2, None, None),
                                            cast=((w_up, 0),) if l == 0 else ())
        if l == 0:
            (w_up_b[0],) = rounded
        return proj_p, proj_s

    def mix_a(gr, proj, l, round_mlp_weights):
        bt, t, m = gr.bt, gr.t, gr.m
        proj3 = proj.reshape(bt, t, -1)
        h_prev = gr.ssm_prev[l].reshape(bt, SSM_GROUPS, -1, SSM_D_STATE)
        ssm_params = (conv_w[l], conv_b[l], dt_bias[l], a_log[l], d_skip[l], gate_norm_g[l])
        if gr.t_real == 1:
            y, c_new, h_new = ssd_step(proj3, col_xbc, col_dt, col_z, *ssm_params, gr.conv_prev[l], h_prev)
        else:
            assert t % SSD_CHUNK == 0 and gr.t_real == t
            todo = [(w_down, w_down_b, 0)] if l == 0 else []
            if l + 1 < depth:
                todo += [(w_up, w_up_b, l + 1), (w_down, w_down_b, l + 1)]
            todo = todo if round_mlp_weights else []
            ssd = ssd_mixer(proj3, col_xbc, col_dt, col_z, gr.t_real, *ssm_params, gr.conv_prev[l], h_prev,
                            cast=tuple((w, k) for w, _, k in todo))
            if todo:
                ssd, rounded = ssd
                for (_, dst, k), r in zip(todo, rounded):
                    dst[k] = r
            y, c_new, h_new = ssd
        gr.conv_new.append(c_new)
        gr.ssm_new.append(h_new.reshape(bt, n_heads, SSM_HEAD_DIM, SSM_D_STATE))
        mo = mem_attention(proj3, col_qm_a, *gr.mem_kv(l)).reshape(m, mem_w)
        return y.reshape(m, d_inner), mo

    def shared_kv(prompt, sample):
        hg = gain_row(2 * kv_w, [(0, k_norm_g, DIL_HEADS)])
        kv_p, kv_s, _ = norm_proj(prompt.x2, kv_norm_g, w_kv, 0, hg, kinds_kv, prompt.cos, prompt.sin,
                                  extra=(sample.x2, sample.cos, sample.sin))
        prompt.kv3 = kv_p.reshape(prompt.bt, prompt.t, 2 * kv_w)
        sample.kv3 = kv_s.reshape(sample.bt, sample.t, 2 * kv_w)

    def in_proj_b(prompt, sample, l):
        j = l - n_a
        hg = gain_row(dil_q_w + mem_w, [(0, q_norm_g[j], dil_q_w // HEAD_DIM),
                                        (dil_q_w, mem_q_norm_g[l], MEM_HEADS)])
        proj_p, proj_s, _ = norm_proj(prompt.x2, norm_mix_g[l], w_b, j, hg, kinds_b, prompt.cos, prompt.sin,
                                      extra=(sample.x2, sample.cos, sample.sin))
        return proj_p, proj_s

    def mix_b(gr, proj, l):
        proj3 = proj.reshape(gr.bt, gr.t, -1)
        if gr.k_past is None:
            att = dilated_attention(proj3, gr.kv3)
        else:
            att = dilated_decode(proj3, gr.kv3, gr.k_past, gr.v_past)
        mo = mem_attention(proj3, dil_q_w, *gr.mem_kv(l)).reshape(gr.m, mem_w)
        return att.reshape(gr.m, kv_w), mo

    bp, t_p, _ = x_prompt.shape
    mlen = mem_prompt.shape[1]
    mem2 = mem_prompt.reshape(bp * mlen, d_model)
    mkv_p = []
    for l in range(depth):
        hg = gain_row(2 * mem_w, [(0, mem_k_norm_g[l], MEM_HEADS)])
        mkv_p.append(norm_proj(mem2, norm_mem_g[l], w_mkv, l, hg, kinds_mkv)[0].reshape(bp, mlen, 2 * mem_w))
    conv0 = jnp.zeros((n_a, bp, CONV_W - 1, conv_dim), F32)
    ssm0 = jnp.zeros((n_a, bp, n_heads, SSM_HEAD_DIM, SSM_D_STATE), F32)
    prompt = Group(x_prompt, jnp.arange(t_p, dtype=jnp.int32), conv0, ssm0,
                   lambda l: (mkv_p[l], mkv_p[l], (0, mem_w)), None, None, t_p)

    bs, t_s, _ = x_sample.shape
    assert t_s == 1
    xs = jnp.pad(x_sample, ((0, 0), (0, SAMPLE_ROWS - t_s), (0, 0)))
    pos_s = jnp.full((bs * SAMPLE_ROWS,), PAST_LEN, jnp.int32)
    sample = Group(xs, pos_s, state_conv, state_ssm, lambda l: (cache_mem_k, cache_mem_v, l),
                   cache_win_k, cache_win_v, t_s)

    for l in range(depth):
        if l < n_a:
            proj_p, proj_s = in_proj_a(prompt, sample, l)
            a_p, b_p = mix_a(prompt, proj_p, l, round_mlp_weights=True)
            a_s, b_s = mix_a(sample, proj_s, l, round_mlp_weights=False)
            w_out, lw = w_out_a_b, l
        else:
            if l == n_a:
                shared_kv(prompt, sample)
            proj_p, proj_s = in_proj_b(prompt, sample, l)
            a_p, b_p = mix_b(prompt, proj_p, l)
            a_s, b_s = mix_b(sample, proj_s, l)
            w_out, lw = w_out_b_b, l - n_a
        todo = [(w, dst) for w, dst in ((w_up, w_up_b), (w_down, w_down_b)) if l + 1 < depth and dst[l + 1] is None]
        prompt.x2, sample.x2, rounded = mix_mlp(prompt.x2, a_p, b_p, w_out, lw, norm_mlp_g[l], w_up_b[l], w_down_b[l],
                                                cast=tuple((w, l + 1) for w, _ in todo),
                                                extra=(sample.x2, a_s, b_s))
        for (_, dst), r in zip(todo, rounded):
            dst[l + 1] = r

    y_p = prompt.x2.reshape(bp, t_p, d_model)
    conv_p, ssm_p, kv_p = jnp.stack(prompt.conv_new), jnp.stack(prompt.ssm_new), prompt.kv3
    mkv_all = jnp.stack(mkv_p)
    mem_k_p = mkv_all[..., :mem_w].reshape(depth, bp, mlen, MEM_HEADS, HEAD_DIM)
    mem_v_p = mkv_all[..., mem_w:].reshape(depth, bp, mlen, MEM_HEADS, HEAD_DIM)
    keep = min(max(w for w, _ in DIL_GROUPS), t_p)
    win_k_p, win_v_p = split_kv_heads(kv_p, t_p - keep, keep)

    y_s = sample.x2.reshape(bs, SAMPLE_ROWS, d_model)
    conv_s, ssm_s, kv_s = jnp.stack(sample.conv_new), jnp.stack(sample.ssm_new), sample.kv3
    y_s = y_s[:, :t_s]
    k_s = kv_s[:, :t_s, :kv_w].reshape(bs, t_s, DIL_HEADS, HEAD_DIM)
    v_s = kv_s[:, :t_s, kv_w:].reshape(bs, t_s, DIL_HEADS, HEAD_DIM)

    return (y_p, y_s, conv_p, ssm_p, win_k_p, win_v_p, mem_k_p, mem_v_p, conv_s, ssm_s, k_s, v_s)
```

```python
import functools

import jax
import jax.numpy as jnp
from jax import lax
from jax.experimental import pallas as pl
from jax.experimental.pallas import tpu as pltpu

F32 = jnp.float32
BF16 = jnp.bfloat16

HEAD_DIM = 128
SSM_HEAD_DIM = 64
SSM_GROUPS = 4
SSM_D_STATE = 128
CONV_W = 4
SSD_CHUNK = 128
MEM_HEADS = 4
DIL_GROUPS = ((128, 1), (512, 4), (2048, 16))
DIL_HEADS = 4
PAST_LEN = 16384
ROPE_THETA = 10000.0
EPS = 1e-6
LOG2_E = 1.4426950408889634
LN_2 = 0.6931471805599453

LANES = 128
SUBLANES = 8
VMEM_LIMIT = 48 * 1024 * 1024
MIX_MLP_VMEM_LIMIT = 56 * 1024 * 1024
COL_TILE = 512
ROW_TILE = 512
MEM_Q_TILE = 2048
PROJ_ROW_TILE = 256
SAMPLE_ROWS = SUBLANES
DIL_UNROLL = 16

NT_DIMS = (((1,), (1,)), ((), ()))
RESIDENT = pl.Buffered(1)


def _cparams(*sem):
    return pltpu.CompilerParams(dimension_semantics=sem, vmem_limit_bytes=VMEM_LIMIT)


def _sigmoid(x):
    return 1.0 / (1.0 + jnp.exp(-x))


def _rms_scale(x):
    return lax.rsqrt(jnp.mean(x * x, axis=-1, keepdims=True) + EPS)


class _SideCast:
    def __init__(self, mats, steps, step_of):
        self.n = len(mats)
        self.shapes = [a.shape[1:] for a, _ in mats]
        self.views, self.in_specs, self.out_specs, self.out_shape = [], [], [], []
        for a, layer in mats:
            n_layers, r, c = a.shape
            assert r % (steps * 2 * SUBLANES) == 0
            band = r // steps
            self.views.append(a.reshape(n_layers, steps, band, c))
            self.in_specs.append(pl.BlockSpec((None, None, band, c),
                                              lambda *ids, layer=layer: (layer, step_of(*ids), 0, 0)))
            self.out_specs.append(pl.BlockSpec((None, band, c), lambda *ids: (step_of(*ids), 0, 0)))
            self.out_shape.append(jax.ShapeDtypeStruct((steps, band, c), BF16))

    def wrap(self, body, n_in, n_out):
        n = self.n
        if not n:
            return body

        def kern(*refs):
            ins, srcs = refs[:n_in], refs[n_in:n_in + n]
            outs, dsts = refs[n_in + n:n_in + n + n_out], refs[n_in + n + n_out:n_in + 2 * n + n_out]
            for s, d in zip(srcs, dsts):
                d[...] = s[...].astype(BF16)
            body(*ins, *outs, *refs[n_in + 2 * n + n_out:])
        return kern

    def split(self, outs, n_out):
        main = outs[0] if n_out == 1 else tuple(outs[:n_out])
        if not self.n:
            return main
        return main, [o.reshape(s) for o, s in zip(outs[n_out:], self.shapes)]


def _round_kernel(src_ref, dst_ref):
    dst_ref[...] = src_ref[...].astype(BF16)


def round_weight(w, n_rows):
    n_layers, _, c = w.shape
    band = ROW_TILE
    assert n_rows % band == 0
    return pl.pallas_call(
        _round_kernel,
        grid=(n_layers, n_rows // band),
        in_specs=[pl.BlockSpec((None, band, c), lambda l, i: (l, i, 0))],
        out_specs=pl.BlockSpec((None, band, c), lambda l, i: (l, i, 0)),
        out_shape=jax.ShapeDtypeStruct((n_layers, n_rows, c), BF16),
        compiler_params=_cparams("parallel", "parallel"),
        name="round_weight",
    )(w)


def _rope_table_kernel(pos_ref, inv_ref, cos_ref, sin_ref):
    ang = pos_ref[...] * inv_ref[...]
    lane = lax.broadcasted_iota(jnp.int32, ang.shape, 1)
    cos_ref[...] = jnp.cos(ang)
    s = jnp.sin(ang)
    sin_ref[...] = jnp.where(lane < HEAD_DIM // 2, -s, s)


def rope_tables(pos):
    r = pos.shape[0]
    half = HEAD_DIM // 2
    inv = ROPE_THETA ** (-jnp.arange(half, dtype=F32) * (2.0 / HEAD_DIM))
    inv = jnp.concatenate([inv, inv])[None, :]
    posf = jnp.broadcast_to(pos.astype(F32)[:, None], (r, HEAD_DIM))
    tr = min(r, ROW_TILE)
    return pl.pallas_call(
        _rope_table_kernel,
        grid=(r // tr,),
        in_specs=[pl.BlockSpec((tr, HEAD_DIM), lambda i: (i, 0)),
                  pl.BlockSpec((1, HEAD_DIM), lambda i: (0, 0))],
        out_specs=[pl.BlockSpec((tr, HEAD_DIM), lambda i: (i, 0))] * 2,
        out_shape=[jax.ShapeDtypeStruct((r, HEAD_DIM), F32)] * 2,
        compiler_params=_cparams("parallel"),
        name="rope_tables",
    )(posf, inv)


def _proj_kernel(*refs, kinds, tile_src, n_w, transposed, has_extra):
    rope = "rope" in kinds
    refs = list(refs)
    x_ref, g_ref = refs[:2]
    w_refs = refs[2:2 + n_w]
    hg_ref = refs[2 + n_w]
    at = 3 + n_w
    if rope:
        cos_ref, sin_ref = refs[at:at + 2]
        at += 2
    if has_extra:
        xs_ref = refs[at]
        at += 1
        if rope:
            cos_s_ref, sin_s_ref = refs[at:at + 2]
            at += 2
        o_ref, os_ref, xn_ref = refs[at:at + 3]
    else:
        o_ref, xn_ref = refs[at:at + 2]
    tm = x_ref.shape[0]

    def normed(x):
        return (x * _rms_scale(x) * g_ref[...]).astype(BF16)

    def tiles(xn, cos, sin, store):
        for t, kind in enumerate(kinds):
            which, col, width = tile_src[t]
            at_col = sum(w for _, _, w in tile_src[:t])
            if transposed:
                y = lax.dot_general(xn, w_refs[which][col:col + width, :], NT_DIMS, preferred_element_type=F32)
            else:
                y = jnp.dot(xn, w_refs[which][:, col:col + width], preferred_element_type=F32)
            if kind == "plain":
                store(slice(at_col, at_col + width), y)
                continue
            for j in range(width // HEAD_DIM):
                osl = slice(at_col + j * HEAD_DIM, at_col + (j + 1) * HEAD_DIM)
                yh = y[:, j * HEAD_DIM:(j + 1) * HEAD_DIM]
                yh = yh * _rms_scale(yh) * hg_ref[:, osl]
                if kind == "rope":
                    yh = yh * cos + pltpu.roll(yh, HEAD_DIM // 2, axis=1) * sin
                store(osl, yh)

    def store_main(cols, y):
        o_ref[:, cols] = y

    def store_both(cols, y):
        o_ref[:, cols] = y[:tm]
        os_ref[:, cols] = y[tm:]

    xn_ref[:tm, :] = normed(x_ref[...])
    cos = cos_ref[...] if rope else None
    sin = sin_ref[...] if rope else None
    if not has_extra:
        tiles(xn_ref[...], cos, sin, store_main)
        return

    @pl.when(pl.program_id(0) == 0)
    def _():
        xn_ref[tm:, :] = normed(xs_ref[...])
        cos_all = jnp.concatenate([cos, cos_s_ref[...]], axis=0) if rope else None
        sin_all = jnp.concatenate([sin, sin_s_ref[...]], axis=0) if rope else None
        tiles(xn_ref[...], cos_all, sin_all, store_both)

    @pl.when(pl.program_id(0) != 0)
    def _():
        tiles(xn_ref[:tm, :], cos, sin, store_main)


def norm_proj(x, g, ws, layer, head_gain, kinds, cos=None, sin=None, cast=(), tile_src=None, transposed=False,
              extra=None):
    m, d = x.shape
    ws = list(ws) if isinstance(ws, (list, tuple)) else [ws]
    if tile_src is None:
        tile_src = [(0, t * COL_TILE, COL_TILE) for t in range(len(kinds))]
    n_cols = sum(width for _, _, width in tile_src)
    assert len(tile_src) == len(kinds) and head_gain.shape == (1, n_cols)
    assert all(width % HEAD_DIM == 0 for _, _, width in tile_src)
    rope = "rope" in kinds
    has_extra = extra is not None
    ms = extra[0].shape[0] if has_extra else 0
    tm = min(m, ROW_TILE if n_cols <= 2 * COL_TILE else PROJ_ROW_TILE)
    assert m % tm == 0
    in_specs = [pl.BlockSpec((tm, d), lambda i: (i, 0)),
                pl.BlockSpec((1, d), lambda i: (0, 0))]
    for k, w in enumerate(ws):
        used = max(col + width for which, col, width in tile_src if which == k)
        if transposed:
            assert used <= w.shape[1] and w.shape[2] == d
            in_specs.append(pl.BlockSpec((None, used, d), lambda i: (layer, 0, 0), pipeline_mode=RESIDENT))
            continue
        assert used <= w.shape[2] and (used % LANES == 0 or used == w.shape[2])
        in_specs.append(pl.BlockSpec((None, d, used), lambda i: (layer, 0, 0), pipeline_mode=RESIDENT))
    in_specs.append(pl.BlockSpec((1, n_cols), lambda i: (0, 0)))
    args = [x, g.reshape(1, d), *ws, head_gain]
    if rope:
        assert cos.shape[0] % tm == 0
        tab_blocks = cos.shape[0] // tm
        in_specs += [pl.BlockSpec((tm, HEAD_DIM), lambda i: (i % tab_blocks, 0))] * 2
        args += [cos, sin]
    out_specs = [pl.BlockSpec((tm, n_cols), lambda i: (i, 0))]
    out_shape = [jax.ShapeDtypeStruct((m, n_cols), F32)]
    if has_extra:
        whole = lambda a: pl.BlockSpec(a.shape, lambda i: (0, 0))
        xs, cos_s, sin_s = extra
        extras = [xs] + ([cos_s, sin_s] if rope else [])
        in_specs += [whole(a) for a in extras]
        args += extras
        out_specs.append(pl.BlockSpec((ms, n_cols), lambda i: (0, 0)))
        out_shape.append(jax.ShapeDtypeStruct((ms, n_cols), F32))
    n_out = len(out_specs)
    side = _SideCast(cast, m // tm, lambda i: i)
    outs = pl.pallas_call(
        side.wrap(functools.partial(_proj_kernel, kinds=tuple(kinds), tile_src=tuple(tile_src), n_w=len(ws),
                                    transposed=transposed, has_extra=has_extra),
                  len(args), n_out),
        grid=(m // tm,),
        in_specs=in_specs + side.in_specs,
        out_specs=out_specs + side.out_specs,
        out_shape=out_shape + side.out_shape,
        scratch_shapes=[pltpu.VMEM((tm + ms, d), BF16)],
        compiler_params=_cparams("arbitrary"),
        name="norm_proj",
    )(*args, *side.views)
    casts = [o.reshape(shp) for o, shp in zip(outs[n_out:], side.shapes)]
    return outs[0], (outs[1] if has_extra else None), casts


def _mix_mlp_kernel(*refs, has_extra):
    if has_extra:
        (x_ref, g_ref, wu_ref, wd_ref, a_ref, b_ref, wa_ref, wb_ref, xs_ref, as_ref, bs_ref,
         o_ref, os_ref, xn_ref) = refs
    else:
        x_ref, g_ref, wu_ref, wd_ref, a_ref, b_ref, wa_ref, wb_ref, o_ref, xn_ref = refs
    i, f = pl.program_id(0), pl.program_id(1)
    tm = x_ref.shape[0]

    def normed(x):
        return (x * _rms_scale(x) * g_ref[...]).astype(BF16)

    def mixed(a, b):
        acc = jnp.dot(a, wa_ref[...], preferred_element_type=F32)
        return acc + jnp.dot(b, wb_ref[...], preferred_element_type=F32)

    def ffn(xn):
        h = jnp.dot(xn, wu_ref[...], preferred_element_type=F32)
        h = jnp.maximum(h, 0.0)
        h = (h * h).astype(BF16)
        return jnp.dot(h, wd_ref[...], preferred_element_type=F32)

    def start_main():
        x1 = x_ref[...] + mixed(a_ref[...], b_ref[...])
        xn_ref[:tm, :] = normed(x1)
        o_ref[...] = x1

    def start_both():
        acc = mixed(jnp.concatenate([a_ref[...], as_ref[...]], axis=0),
                    jnp.concatenate([b_ref[...], bs_ref[...]], axis=0))
        x1 = x_ref[...] + acc[:tm]
        xs1 = xs_ref[...] + acc[tm:]
        xn_ref[:tm, :] = normed(x1)
        xn_ref[tm:, :] = normed(xs1)
        o_ref[...] = x1
        os_ref[...] = xs1

    if not has_extra:
        pl.when(f == 0)(start_main)
        o_ref[...] += ffn(xn_ref[...])
        return

    @pl.when(i == 0)
    def _():
        pl.when(f == 0)(start_both)
        upd = ffn(xn_ref[...])
        o_ref[...] += upd[:tm]
        os_ref[...] += upd[tm:]

    @pl.when(i != 0)
    def _():
        pl.when(f == 0)(start_main)
        o_ref[...] += ffn(xn_ref[:tm, :])


def mix_mlp(x, a, b, w_out, layer, g, wu, wd, tf=1024, cast=(), extra=None):
    m, d = x.shape
    ff = wu.shape[1]
    tm = min(m, ROW_TILE)
    nf = ff // tf
    ka, kb = a.shape[1], b.shape[1]
    assert w_out.shape[1] == ka + kb and ka % kb == 0
    has_extra = extra is not None
    side = _SideCast(cast, (m // tm) * nf, lambda i, j: i * nf + j)
    in_specs = [pl.BlockSpec((tm, d), lambda i, j: (i, 0)),
                pl.BlockSpec((1, d), lambda i, j: (0, 0)),
                pl.BlockSpec((d, tf), lambda i, j: (0, j)),
                pl.BlockSpec((tf, d), lambda i, j: (j, 0)),
                pl.BlockSpec((tm, ka), lambda i, j: (i, 0)),
                pl.BlockSpec((tm, kb), lambda i, j: (i, 0)),
                pl.BlockSpec((None, ka, d), lambda i, j: (layer, 0, 0), pipeline_mode=RESIDENT),
                pl.BlockSpec((None, kb, d), lambda i, j: (layer, ka // kb, 0), pipeline_mode=RESIDENT)]
    out_specs = [pl.BlockSpec((tm, d), lambda i, j: (i, 0))]
    out_shape = [jax.ShapeDtypeStruct((m, d), F32)]
    args = [x, g.reshape(1, d), wu, wd, a, b, w_out, w_out]
    ms = extra[0].shape[0] if has_extra else 0
    if has_extra:
        in_specs += [pl.BlockSpec(e.shape, lambda i, j: (0, 0)) for e in extra]
        args += list(extra)
        out_specs.append(pl.BlockSpec((ms, d), lambda i, j: (0, 0)))
        out_shape.append(jax.ShapeDtypeStruct((ms, d), F32))
    n_out = len(out_specs)
    outs = pl.pallas_call(
        side.wrap(functools.partial(_mix_mlp_kernel, has_extra=has_extra), len(args), n_out),
        grid=(m // tm, nf),
        in_specs=in_specs + side.in_specs,
        out_specs=out_specs + side.out_specs,
        out_shape=out_shape + side.out_shape,
        scratch_shapes=[pltpu.VMEM((tm + ms, d), BF16)],
        compiler_params=pltpu.CompilerParams(dimension_semantics=("arbitrary", "arbitrary"),
                                             vmem_limit_bytes=MIX_MLP_VMEM_LIMIT),
        name="mix_mlp",
    )(*args, *side.views)
    casts = [o.reshape(shp) for o, shp in zip(outs[n_out:], side.shapes)]
    return outs[0], (outs[1] if has_extra else None), casts


def _split3(x):
    hi = x.astype(BF16)
    r1 = x - hi.astype(F32)
    mid = r1.astype(BF16)
    lo = (r1 - mid.astype(F32)).astype(BF16)
    return hi, mid, lo


def _ssd_kernel(xbc_ref, dt_ref, z_ref, cw_ref, cb_ref, dtb_ref, alog_ref, dsk_ref, gg_ref, tri_ref, expand_ref,
                conv0_ref, h0_ref, y_ref, convo_ref, ho_ref,
                ext_ref, ht_ref, yd_ref, st_ref, *, t_real, n_heads, d_inner):
    L = SSD_CHUNK
    P = SSM_HEAD_DIM
    N = SSM_D_STATE
    G = SSM_GROUPS
    hpg = n_heads // G
    gw = hpg * P
    hist = CONV_W - 1
    base = SUBLANES - hist
    c = pl.program_id(1)
    nc = pl.num_programs(1)

    @pl.when(c == 0)
    def _():
        ext_ref[base:SUBLANES, :] = conv0_ref[...]
        for g in range(G):
            ht_ref[g] = h0_ref[g].T

    ext_ref[SUBLANES:SUBLANES + L, :] = xbc_ref[...]
    ext = ext_ref[...]
    acc = cb_ref[...] + ext[SUBLANES:, :] * cw_ref[hist:CONV_W, :]
    for k in range(1, CONV_W):
        acc = acc + pltpu.roll(ext, k, axis=0)[SUBLANES:, :] * cw_ref[hist - k:CONV_W - k, :]
    xc = acc * _sigmoid(acc)

    @pl.when(c == nc - 1)
    def _():
        n_in_last = t_real - ((t_real - 1) // L) * L
        convo_ref[...] = ext_ref[base + n_in_last:base + n_in_last + hist, :]

    ext_ref[base:SUBLANES, :] = ext_ref[base + L:SUBLANES + L, :]

    dt_raw = dt_ref[...] + dtb_ref[...]
    dt = jnp.maximum(dt_raw, 0.0) + jnp.log1p(jnp.exp(-jnp.abs(dt_raw)))
    row = lax.broadcasted_iota(jnp.int32, (L, LANES), 0) + c * L
    dt = jnp.where(row < t_real, dt, 0.0)
    da = dt * (-jnp.exp(alog_ref[...]))
    li = lax.broadcasted_iota(jnp.int32, (L, L), 0)
    si = lax.broadcasted_iota(jnp.int32, (L, L), 1)
    causal = li >= si
    a_cs = sum(jnp.dot(tri_ref[...], part, preferred_element_type=F32) for part in _split3(da))
    a_cs_t = a_cs.T
    dt_t = dt.T
    w_t = dt_t * jnp.exp(a_cs_t[:, L - 1:L] - a_cs_t)
    e_exp = sum(jnp.dot(part, expand_ref[...], preferred_element_type=F32) for part in _split3(jnp.exp(a_cs)))
    xb = xc[:, :d_inner].astype(BF16)
    low_half = lax.broadcasted_iota(jnp.int32, (L, 2 * P), 1) < P

    for g in range(G):
        b_g = xc[:, d_inner + g * N:d_inner + (g + 1) * N]
        c_g = xc[:, d_inner + G * N + g * N:d_inner + G * N + (g + 1) * N].astype(BF16)
        cbm = lax.dot_general(c_g, b_g.astype(BF16), NT_DIMS, preferred_element_type=F32)
        b_t = b_g.T
        y_off = jnp.dot(c_g, ht_ref[g].astype(BF16), preferred_element_type=F32)
        for kp in range(hpg // 2):
            h0 = g * hpg + 2 * kp
            x_pair = xb[:, h0 * P:(h0 + 2) * P]
            yd, st = [], []
            for h in (h0, h0 + 1):
                seg = a_cs[:, h:h + 1] - a_cs_t[h:h + 1, :]
                dec = jnp.exp(jnp.where(causal, seg, -jnp.inf))
                mh = (cbm * dec * dt_t[h:h + 1, :]).astype(BF16)
                yd.append(jnp.dot(mh, x_pair, preferred_element_type=F32))
                btw = (b_t * w_t[h:h + 1, :]).astype(BF16)
                st.append(jnp.dot(btw, x_pair, preferred_element_type=F32))
            yd_ref[:, 2 * kp * P:(2 * kp + 2) * P] = jnp.where(low_half, yd[0], yd[1])
            st_ref[:, 2 * kp * P:(2 * kp + 2) * P] = jnp.where(low_half, st[0], st[1])
        gs = slice(g * gw, (g + 1) * gw)
        e_g = e_exp[:, gs]
        y_g = yd_ref[...] + y_off * e_g + dsk_ref[:, gs] * xc[:, gs]
        ht_ref[g] = ht_ref[g] * e_g[L - 1:L, :] + st_ref[...]
        z_g = z_ref[:, gs]
        gated = y_g * (z_g * _sigmoid(z_g))
        y_ref[:, gs] = (gated * _rms_scale(gated) * gg_ref[:, gs]).astype(y_ref.dtype)

    @pl.when(c == nc - 1)
    def _():
        for g in range(G):
            ho_ref[g] = ht_ref[g].T


def ssd_mixer(proj, col_xbc, col_dt, col_z, t_real, conv_w, conv_b, dt_bias, a_log, d_skip, gate_g,
              conv_prev, h_prev, cast=()):
    bt, tp, _ = proj.shape
    conv_dim = conv_w.shape[1]
    n_heads = dt_bias.shape[0]
    d_inner = n_heads * SSM_HEAD_DIM
    gw = d_inner // SSM_GROUPS
    L = SSD_CHUNK
    assert tp % L == 0 and (tp - t_real) < L
    assert col_xbc % conv_dim == 0 and col_dt % LANES == 0 and col_z % d_inner == 0
    pad = LANES - n_heads
    dtb = jnp.pad(dt_bias, (0, pad)).reshape(1, LANES)
    alog = jnp.pad(a_log, (0, pad)).reshape(1, LANES)
    dsk = jnp.repeat(d_skip, SSM_HEAD_DIM).reshape(1, d_inner)
    tri = jnp.tril(jnp.ones((L, L), BF16))
    expand = jnp.repeat(jnp.eye(LANES, dtype=BF16)[:, :n_heads], SSM_HEAD_DIM, axis=1)
    kern = functools.partial(_ssd_kernel, t_real=t_real, n_heads=n_heads, d_inner=d_inner)
    vec = lambda n: pl.BlockSpec((1, n), lambda b, c: (0, 0))
    nc = tp // L
    side = _SideCast(cast, bt * nc, lambda b, c: b * nc + c)
    outs = pl.pallas_call(
        side.wrap(kern, 13, 3),
        grid=(bt, nc),
        in_specs=[pl.BlockSpec((None, L, conv_dim), lambda b, c: (b, c, col_xbc // conv_dim)),
                  pl.BlockSpec((None, L, LANES), lambda b, c: (b, c, col_dt // LANES)),
                  pl.BlockSpec((None, L, d_inner), lambda b, c: (b, c, col_z // d_inner)),
                  pl.BlockSpec((CONV_W, conv_dim), lambda b, c: (0, 0)),
                  vec(conv_dim), vec(LANES), vec(LANES), vec(d_inner), vec(d_inner),
                  pl.BlockSpec((L, L), lambda b, c: (0, 0)),
                  pl.BlockSpec((LANES, d_inner), lambda b, c: (0, 0)),
                  pl.BlockSpec((None, CONV_W - 1, conv_dim), lambda b, c: (b, 0, 0)),
                  pl.BlockSpec((None, SSM_GROUPS, gw, SSM_D_STATE), lambda b, c: (b, 0, 0, 0))] + side.in_specs,
        out_specs=[pl.BlockSpec((None, L, d_inner), lambda b, c: (b, c, 0)),
                   pl.BlockSpec((None, CONV_W - 1, conv_dim), lambda b, c: (b, 0, 0)),
                   pl.BlockSpec((None, SSM_GROUPS, gw, SSM_D_STATE), lambda b, c: (b, 0, 0, 0))] + side.out_specs,
        out_shape=[jax.ShapeDtypeStruct((bt, tp, d_inner), BF16),
                   jax.ShapeDtypeStruct((bt, CONV_W - 1, conv_dim), F32),
                   jax.ShapeDtypeStruct((bt, SSM_GROUPS, gw, SSM_D_STATE), F32)] + side.out_shape,
        scratch_shapes=[pltpu.VMEM((SUBLANES + L, conv_dim), F32),
                        pltpu.VMEM((SSM_GROUPS, SSM_D_STATE, gw), F32),
                        pltpu.VMEM((L, gw), F32),
                        pltpu.VMEM((SSM_D_STATE, gw), F32)],
        compiler_params=_cparams("parallel", "arbitrary"),
        name="ssd_mixer",
    )(proj, proj, proj, conv_w, conv_b.reshape(1, conv_dim), dtb, alog, dsk, gate_g.reshape(1, d_inner),
      tri, expand, conv_prev, h_prev, *side.views)
    return side.split(outs, 3)


def _ssd_step_kernel(xbc_ref, dt_ref, z_ref, cw_ref, cb_ref, dtb_ref, alog_ref, dsk_ref, gg_ref, expand_ref,
                     conv0_ref, h0_ref, y_ref, convo_ref, ho_ref, *, n_heads, d_inner):
    P, N, G = SSM_HEAD_DIM, SSM_D_STATE, SSM_GROUPS
    gw = n_heads // G * P
    hist = CONV_W - 1
    acc = cb_ref[...] + xbc_ref[...] * cw_ref[hist:CONV_W, :]
    for k in range(hist):
        acc = acc + conv0_ref[k:k + 1, :] * cw_ref[k:k + 1, :]
    xc = acc * _sigmoid(acc)
    convo_ref[0:hist - 1, :] = conv0_ref[1:hist, :]
    convo_ref[hist - 1:hist, :] = xbc_ref[0:1, :]

    dt_raw = dt_ref[...] + dtb_ref[...]
    dt = jnp.maximum(dt_raw, 0.0) + jnp.log1p(jnp.exp(-jnp.abs(dt_raw)))
    decay = jnp.exp(dt * (-jnp.exp(alog_ref[...])))
    expand = lambda v: sum(jnp.dot(part, expand_ref[...], preferred_element_type=F32) for part in _split3(v))
    dt_x = expand(dt) * xc[:, :d_inner]
    decay_c = expand(decay)

    for g in range(G):
        gs = slice(g * gw, (g + 1) * gw)
        b_col = xc[:, d_inner + g * N:d_inner + (g + 1) * N].T[:, 0:1]
        c_g = xc[:, d_inner + G * N + g * N:d_inner + G * N + (g + 1) * N].astype(BF16)
        h_t = h0_ref[g].T * decay_c[0:1, gs] + b_col * dt_x[0:1, gs]
        ho_ref[g] = h_t.T
        y_g = jnp.dot(c_g, h_t.astype(BF16), preferred_element_type=F32) + dsk_ref[:, gs] * xc[:, gs]
        z_g = z_ref[:, gs]
        gated = y_g * (z_g * _sigmoid(z_g))
        y_ref[:, gs] = (gated * _rms_scale(gated) * gg_ref[:, gs]).astype(y_ref.dtype)


def ssd_step(proj, col_xbc, col_dt, col_z, conv_w, conv_b, dt_bias, a_log, d_skip, gate_g, conv_prev, h_prev):
    bt, rows, _ = proj.shape
    conv_dim = conv_w.shape[1]
    n_heads = dt_bias.shape[0]
    d_inner = n_heads * SSM_HEAD_DIM
    gw = d_inner // SSM_GROUPS
    assert col_xbc % conv_dim == 0 and col_dt % LANES == 0 and col_z % d_inner == 0
    pad = LANES - n_heads
    dtb = jnp.pad(dt_bias, (0, pad)).reshape(1, LANES)
    alog = jnp.pad(a_log, (0, pad)).reshape(1, LANES)
    dsk = jnp.repeat(d_skip, SSM_HEAD_DIM).reshape(1, d_inner)
    expand = jnp.repeat(jnp.eye(LANES, dtype=BF16)[:, :n_heads], SSM_HEAD_DIM, axis=1)
    vec = lambda n: pl.BlockSpec((1, n), lambda b: (0, 0))
    state = pl.BlockSpec((None, SSM_GROUPS, gw, SSM_D_STATE), lambda b: (b, 0, 0, 0))
    hist_rows = pl.BlockSpec((None, CONV_W - 1, conv_dim), lambda b: (b, 0, 0))
    return pl.pallas_call(
        functools.partial(_ssd_step_kernel, n_heads=n_heads, d_inner=d_inner),
        grid=(bt,),
        in_specs=[pl.BlockSpec((None, rows, conv_dim), lambda b: (b, 0, col_xbc // conv_dim)),
                  pl.BlockSpec((None, rows, LANES), lambda b: (b, 0, col_dt // LANES)),
                  pl.BlockSpec((None, rows, d_inner), lambda b: (b, 0, col_z // d_inner)),
                  pl.BlockSpec((CONV_W, conv_dim), lambda b: (0, 0)),
                  vec(conv_dim), vec(LANES), vec(LANES), vec(d_inner), vec(d_inner),
                  pl.BlockSpec((LANES, d_inner), lambda b: (0, 0)),
                  hist_rows, state],
        out_specs=[pl.BlockSpec((None, rows, d_inner), lambda b: (b, 0, 0)), hist_rows, state],
        out_shape=[jax.ShapeDtypeStruct((bt, rows, d_inner), BF16),
                   jax.ShapeDtypeStruct((bt, CONV_W - 1, conv_dim), F32),
                   jax.ShapeDtypeStruct((bt, SSM_GROUPS, gw, SSM_D_STATE), F32)],
        compiler_params=_cparams("parallel"),
        name="ssd_step",
    )(proj, proj, proj, conv_w, conv_b.reshape(1, conv_dim), dtb, alog, dsk, gate_g.reshape(1, d_inner),
      expand, conv_prev, h_prev)


def _mem_attn_kernel(q_ref, k_ref, v_ref, o_ref):
    scale = HEAD_DIM ** -0.5
    headed = len(k_ref.shape) == 3
    for h in range(MEM_HEADS):
        sl = slice(h * HEAD_DIM, (h + 1) * HEAD_DIM)
        q = (q_ref[:, sl] * (scale * LOG2_E)).astype(BF16)
        k = (k_ref[:, h, :] if headed else k_ref[:, sl]).astype(BF16)
        v = (v_ref[:, h, :] if headed else v_ref[:, sl]).astype(BF16)
        s = lax.dot_general(q, k, NT_DIMS, preferred_element_type=F32)
        e = jnp.exp2(s - jnp.max(s, axis=-1, keepdims=True))
        den = jnp.sum(e, axis=-1, keepdims=True)
        o = jnp.dot(e.astype(BF16), v, preferred_element_type=F32) / den
        o_ref[:, sl] = o.astype(o_ref.dtype)


def mem_attention(proj, col_q, k, v, kv_at):
    bt, t, _ = proj.shape
    width = MEM_HEADS * HEAD_DIM
    tq = min(t, MEM_Q_TILE)
    assert t % tq == 0
    if k.ndim == 5:
        mlen = k.shape[2]
        kv_specs = [pl.BlockSpec((None, None, mlen, MEM_HEADS, HEAD_DIM), lambda b, i: (kv_at, b, 0, 0, 0))] * 2
    else:
        mlen = k.shape[1]
        kv_specs = [pl.BlockSpec((None, mlen, width), lambda b, i, c=c: (b, 0, c // width)) for c in kv_at]
    return pl.pallas_call(
        _mem_attn_kernel,
        grid=(bt, t // tq),
        in_specs=[pl.BlockSpec((None, tq, width), lambda b, i: (b, i, col_q // width))] + kv_specs,
        out_specs=pl.BlockSpec((None, tq, width), lambda b, i: (b, i, 0)),
        out_shape=jax.ShapeDtypeStruct((bt, t, width), BF16),
        compiler_params=_cparams("parallel", "parallel"),
        name="mem_attention",
    )(proj, k, v)


def _run_units(first, count, unit, unroll):
    trips = count // unroll
    if trips == 1:
        trips = 0
    if trips:
        def trip(it, carry):
            for k in range(unroll):
                unit(first + it * unroll + k)
            return carry
        lax.fori_loop(0, trips, trip, 0)
    for k in range(trips * unroll, count):
        unit(first + k)


def _dil_attn_kernel(*refs, groups, rb, unroll):
    ng = len(groups)
    q_refs = refs[:ng]
    kp_ref, kc_ref, vp_ref, vc_ref, o_ref = refs[ng:ng + 5]
    og = refs[ng + 5:2 * ng + 5]
    lg = refs[2 * ng + 5:3 * ng + 5]
    i = pl.program_id(1)
    scale = HEAD_DIM ** -0.5

    for gi, (win, d) in enumerate(groups):
        w = win // d
        dw = d * w
        qi = lax.broadcasted_iota(jnp.int32, (w, 2 * w), 0)
        kj = lax.broadcasted_iota(jnp.int32, (w, 2 * w), 1)
        band = (kj >= qi) & (kj <= qi + w)
        band_first = band & (kj >= jnp.where(i > 0, 0, w))

        def rows(start, size, d=d, w=w):
            if d == 1:
                return pl.ds(pl.multiple_of(start, w), size)
            return pl.ds(start, size, stride=d)

        def attend(rows_q, kk, vv, mask, gi=gi, w=w):
            q = (q_refs[gi][rows_q, :] * (scale * LOG2_E)).astype(BF16)
            s = lax.dot_general(q, kk.astype(BF16), NT_DIMS, preferred_element_type=F32)
            s = jnp.where(mask, s, -jnp.inf)
            m = jnp.max(s, axis=-1, keepdims=True)
            e = jnp.exp2(s - m)
            den = jnp.sum(e, axis=-1, keepdims=True)
            o = jnp.dot(e.astype(BF16), vv.astype(BF16), preferred_element_type=F32) / den
            og[gi][rows_q, :] = o
            lg[gi][rows_q, :] = jnp.broadcast_to(m * LN_2 + jnp.log(den), (w, HEAD_DIM))

        def first_unit(r, rows=rows, attend=attend, w=w, dw=dw, mask=band_first):
            rows_q, rows_p = rows(r, w), rows(rb - dw + r, w)
            kk = jnp.concatenate([kp_ref[rows_p, :], kc_ref[rows_q, :]], axis=0)
            vv = jnp.concatenate([vp_ref[rows_p, :], vc_ref[rows_q, :]], axis=0)
            attend(rows_q, kk, vv, mask)

        def later_unit(u, rows=rows, attend=attend, d=d, w=w, dw=dw, mask=band):
            sb = u // d
            start_q = sb * dw + (u - sb * d)
            rows_k = rows(start_q - dw, 2 * w)
            attend(rows(start_q, w), kc_ref[rows_k, :], vc_ref[rows_k, :], mask)

        _run_units(0, d, first_unit, unroll)
        _run_units(d, rb // w - d, later_unit, unroll)

    ls = [r[...] for r in lg]
    mm = functools.reduce(jnp.maximum, ls)
    ws = [jnp.exp(l - mm) for l in ls]
    num = sum(wg * r[...] for wg, r in zip(ws, og))
    o_ref[...] = (num / sum(ws)).astype(o_ref.dtype)


def dilated_attention(proj, kv):
    bt, t, _ = proj.shape
    ng = len(DIL_GROUPS)
    rb = min(t, max(win for win, _ in DIL_GROUPS))
    for win, d in DIL_GROUPS:
        assert win % d == 0 and rb % win == 0
    assert t % rb == 0
    blk = lambda f: pl.BlockSpec((None, rb, HEAD_DIM), f)
    prev = lambda i: jnp.maximum(i - 1, 0)
    q_specs = [blk(lambda b, i, h, g=g: (b, i, g * DIL_HEADS + h)) for g in range(ng)]
    kv_specs = [blk(lambda b, i, h: (b, prev(i), h)), blk(lambda b, i, h: (b, i, h)),
                blk(lambda b, i, h: (b, prev(i), DIL_HEADS + h)), blk(lambda b, i, h: (b, i, DIL_HEADS + h))]
    return pl.pallas_call(
        functools.partial(_dil_attn_kernel, groups=DIL_GROUPS, rb=rb, unroll=DIL_UNROLL),
        grid=(bt, t // rb, DIL_HEADS),
        in_specs=q_specs + kv_specs,
        out_specs=blk(lambda b, i, h: (b, i, h)),
        out_shape=jax.ShapeDtypeStruct((bt, t, DIL_HEADS * HEAD_DIM), BF16),
        scratch_shapes=[pltpu.VMEM((rb, HEAD_DIM), F32)] * (2 * ng),
        compiler_params=_cparams("parallel", "parallel", "parallel"),
        name="dilated_attention",
    )(*([proj] * ng), kv, kv, kv, kv)


def _dil_decode_kernel(*refs):
    ng = len(DIL_GROUPS)
    q_ref, kv_ref = refs[:2]
    k_refs, v_refs = refs[2:2 + ng], refs[2 + ng:2 + 2 * ng]
    o_ref = refs[-1]
    width = DIL_HEADS * HEAD_DIM
    scale = HEAD_DIM ** -0.5
    for h in range(DIL_HEADS):
        sl = slice(h * HEAD_DIM, (h + 1) * HEAD_DIM)
        k_new = kv_ref[:, h * HEAD_DIM:(h + 1) * HEAD_DIM]
        v_new = kv_ref[:, width + h * HEAD_DIM:width + (h + 1) * HEAD_DIM]
        outs, lses = [], []
        for gi in range(ng):
            kc = k_refs[gi][:, h, :].astype(BF16)
            vc = v_refs[gi][:, h, :].astype(BF16)
            q = q_ref[:, gi * width + h * HEAD_DIM:gi * width + (h + 1) * HEAD_DIM]
            s = lax.dot_general(q.astype(BF16), kc, NT_DIMS, preferred_element_type=F32) * scale
            s_self = jnp.sum(q * k_new, axis=-1, keepdims=True) * scale
            m = jnp.maximum(jnp.max(s, axis=-1, keepdims=True), s_self)
            e = jnp.exp(s - m)
            e_self = jnp.exp(s_self - m)
            den = jnp.sum(e, axis=-1, keepdims=True) + e_self
            o = jnp.dot(e.astype(BF16), vc, preferred_element_type=F32) + e_self * v_new
            outs.append(o / den)
            lses.append(m + jnp.log(den))
        mm = functools.reduce(jnp.maximum, lses)
        ws = [jnp.exp(l - mm) for l in lses]
        num = sum(wg * o for wg, o in zip(ws, outs))
        o_ref[:, sl] = (num / sum(ws)).astype(o_ref.dtype)


def dilated_decode(proj, kv_new, k_cache, v_cache):
    bt, rows, nq = proj.shape
    lc = k_cache.shape[1]
    width = DIL_HEADS * HEAD_DIM
    views, specs = [], []
    for cache in (k_cache, v_cache):
        for win, dil in DIL_GROUPS:
            assert lc % win == 0 and win % dil == 0
            w = win // dil
            views.append(cache.reshape(bt, lc // dil, dil, DIL_HEADS, HEAD_DIM))
            specs.append(pl.BlockSpec((None, w, None, DIL_HEADS, HEAD_DIM),
                                      lambda b, last=lc // win - 1: (b, last, 0, 0, 0)))
    return pl.pallas_call(
        _dil_decode_kernel,
        grid=(bt,),
        in_specs=[pl.BlockSpec((None, rows, nq), lambda b: (b, 0, 0)),
                  pl.BlockSpec((None, rows, 2 * width), lambda b: (b, 0, 0))] + specs,
        out_specs=pl.BlockSpec((None, rows, width), lambda b: (b, 0, 0)),
        out_shape=jax.ShapeDtypeStruct((bt, rows, width), BF16),
        compiler_params=_cparams("parallel"),
        name="dilated_decode",
    )(proj, kv_new, *views)


def _split_heads_kernel(kv_ref, k_ref, v_ref):
    width = DIL_HEADS * HEAD_DIM
    for h in range(DIL_HEADS):
        k_ref[:, h, :] = kv_ref[:, h * HEAD_DIM:(h + 1) * HEAD_DIM]
        v_ref[:, h, :] = kv_ref[:, width + h * HEAD_DIM:width + (h + 1) * HEAD_DIM]


def split_kv_heads(kv, first_row, n_rows):
    bt, _, w2 = kv.shape
    tr = min(n_rows, ROW_TILE)
    assert first_row % tr == 0 and n_rows % tr == 0
    heads = pl.BlockSpec((None, tr, DIL_HEADS, HEAD_DIM), lambda b, i: (b, i, 0, 0))
    return pl.pallas_call(
        _split_heads_kernel,
        grid=(bt, n_rows // tr),
        in_specs=[pl.BlockSpec((None, tr, w2), lambda b, i: (b, first_row // tr + i, 0))],
        out_specs=[heads, heads],
        out_shape=[jax.ShapeDtypeStruct((bt, n_rows, DIL_HEADS, HEAD_DIM), F32)] * 2,
        compiler_params=_cparams("parallel", "parallel"),
        name="split_kv_heads",
    )(kv)


def kernel(x_prompt, x_sample, state_conv, state_ssm, cache_win_k, cache_win_v, cache_mem_k, cache_mem_v,
           mem_prompt, norm_mix_g, norm_mlp_g, norm_mem_g, w_mem_k, w_mem_v, mem_q_norm_g, mem_k_norm_g,
           w_up, w_down, w_in_a, conv_w, conv_b, dt_bias, a_log, d_skip, gate_norm_g, w_out_a,
           w_in_b, q_norm_g, w_out_b, kv_norm_g, w_k_shared, w_v_shared, k_norm_g):
    depth = w_up.shape[0]
    n_a = w_in_a.shape[0]
    d_model = x_prompt.shape[-1]
    n_heads = dt_bias.shape[1]
    d_inner = n_heads * SSM_HEAD_DIM
    conv_dim = conv_w.shape[2]
    mem_w = MEM_HEADS * HEAD_DIM
    kv_w = DIL_HEADS * HEAD_DIM
    dil_q_w = len(DIL_GROUPS) * kv_w

    s1, s2, s3 = d_inner, d_inner + conv_dim, d_inner + conv_dim + n_heads
    assert s1 % COL_TILE == 0 and s2 % COL_TILE == 0
    w_in_a_t = jnp.swapaxes(w_in_a, 1, 2)
    w_a_main = round_weight(w_in_a_t, s2)
    tail = lax.optimization_barrier(w_in_a_t[:, s2:]).astype(BF16)
    w_a_tail = jnp.concatenate([tail[:, :n_heads], jnp.zeros((n_a, COL_TILE - n_heads, d_model), BF16),
                                tail[:, n_heads:]], axis=1)
    tiles_a = ([(0, s1 + c, COL_TILE) for c in range(0, conv_dim, COL_TILE)]
               + [(1, COL_TILE + c, COL_TILE) for c in range(0, mem_w, COL_TILE)]
               + [(0, c, COL_TILE) for c in range(0, d_inner, COL_TILE)] + [(1, 0, LANES)])
    kinds_a = (["plain"] * (conv_dim // COL_TILE) + ["norm"] * (mem_w // COL_TILE)
               + ["plain"] * (d_inner // COL_TILE + 1))
    col_xbc, col_qm_a, col_z = 0, conv_dim, conv_dim + mem_w
    col_dt = col_z + d_inner
    a_cols = col_dt + LANES
    kinds_b = ["rope"] * (dil_q_w // COL_TILE) + ["norm"] * (mem_w // COL_TILE)
    kinds_kv = ["rope"] * (kv_w // COL_TILE) + ["plain"] * (kv_w // COL_TILE)
    kinds_mkv = ["norm"] * (mem_w // COL_TILE) + ["plain"] * (mem_w // COL_TILE)
    w_b = w_in_b.astype(BF16)
    w_kv = jnp.concatenate([w_k_shared, w_v_shared], axis=-1).astype(BF16)[None]
    w_mkv = jnp.concatenate([w_mem_k, w_mem_v], axis=-1).astype(BF16)
    w_out_a_b, w_out_b_b = w_out_a.astype(BF16), w_out_b.astype(BF16)
    assert n_a >= 1
    w_up_b, w_down_b = [None] * depth, [None] * depth

    def gain_row(width, pieces):
        parts, at = [], 0
        for start, g, reps in pieces:
            parts += [jnp.ones((start - at,), F32), jnp.tile(g, reps)]
            at = start + reps * HEAD_DIM
        parts.append(jnp.ones((width - at,), F32))
        return jnp.concatenate(parts).reshape(1, width)

    class Group:
        def __init__(self, x, pos_rows, conv_prev, ssm_prev, mem_kv, k_past, v_past, t_real):
            self.bt, self.t, _ = x.shape
            self.m = self.bt * self.t
            self.x2 = x.reshape(self.m, d_model)
            self.cos, self.sin = rope_tables(pos_rows)
            self.conv_prev, self.ssm_prev, self.mem_kv = conv_prev, ssm_prev, mem_kv
            self.k_past, self.v_past, self.t_real = k_past, v_past, t_real
            self.conv_new, self.ssm_new, self.kv3 = [], [], None

    def in_proj_a(prompt, sample, l):
        hg = gain_row(a_cols, [(col_qm_a, mem_q_norm_g[l], MEM_HEADS)])
        proj_p, proj_s, rounded = norm_proj(prompt.x2, norm_mix_g[l], [w_a_main, w_a_tail], l, hg, kinds_a,
                                            tile_src=tiles_a, transposed=True, extra=(sample.x2, None, None),
                                            cast=((w_up, 0),) if l == 0 else ())
        if l == 0:
            (w_up_b[0],) = rounded
        return proj_p, proj_s

    def mix_a(gr, proj, l):
        bt, t, m = gr.bt, gr.t, gr.m
        proj3 = proj.reshape(bt, t, -1)
        h_prev = gr.ssm_prev[l].reshape(bt, SSM_GROUPS, -1, SSM_D_STATE)
        ssm_params = (conv_w[l], conv_b[l], dt_bias[l], a_log[l], d_skip[l], gate_norm_g[l])
        if gr.t_real == 1:
            y, c_new, h_new = ssd_step(proj3, col_xbc, col_dt, col_z, *ssm_params, gr.conv_prev[l], h_prev)
        else:
            assert t % SSD_CHUNK == 0 and gr.t_real == t
            todo = [(w_down, w_down_b, 0)] if l == 0 else []
            if l + 1 < depth:
                todo += [(w_up, w_up_b, l + 1), (w_down, w_down_b, l + 1)]
            ssd = ssd_mixer(proj3, col_xbc, col_dt, col_z, gr.t_real, *ssm_params, gr.conv_prev[l], h_prev,
                            cast=tuple((w, k) for w, _, k in todo))
            if todo:
                ssd, rounded = ssd
                for (_, dst, k), r in zip(todo, rounded):
                    dst[k] = r
            y, c_new, h_new = ssd
        gr.conv_new.append(c_new)
        gr.ssm_new.append(h_new.reshape(bt, n_heads, SSM_HEAD_DIM, SSM_D_STATE))
        mo = mem_attention(proj3, col_qm_a, *gr.mem_kv(l)).reshape(m, mem_w)
        return y.reshape(m, d_inner), mo

    def shared_kv(prompt, sample):
        hg = gain_row(2 * kv_w, [(0, k_norm_g, DIL_HEADS)])
        kv_p, kv_s, _ = norm_proj(prompt.x2, kv_norm_g, w_kv, 0, hg, kinds_kv, prompt.cos, prompt.sin,
                                  extra=(sample.x2, sample.cos, sample.sin))
        prompt.kv3 = kv_p.reshape(prompt.bt, prompt.t, 2 * kv_w)
        sample.kv3 = kv_s.reshape(sample.bt, sample.t, 2 * kv_w)

    def in_proj_b(prompt, sample, l):
        j = l - n_a
        hg = gain_row(dil_q_w + mem_w, [(0, q_norm_g[j], dil_q_w // HEAD_DIM),
                                        (dil_q_w, mem_q_norm_g[l], MEM_HEADS)])
        proj_p, proj_s, _ = norm_proj(prompt.x2, norm_mix_g[l], w_b, j, hg, kinds_b, prompt.cos, prompt.sin,
                                      extra=(sample.x2, sample.cos, sample.sin))
        return proj_p, proj_s

    def mix_b(gr, proj, l):
        proj3 = proj.reshape(gr.bt, gr.t, -1)
        if gr.k_past is None:
            att = dilated_attention(proj3, gr.kv3)
        else:
            att = dilated_decode(proj3, gr.kv3, gr.k_past, gr.v_past)
        mo = mem_attention(proj3, dil_q_w, *gr.mem_kv(l)).reshape(gr.m, mem_w)
        return att.reshape(gr.m, kv_w), mo

    bp, t_p, _ = x_prompt.shape
    mlen = mem_prompt.shape[1]
    mem2 = mem_prompt.reshape(bp * mlen, d_model)
    mkv_p = []
    for l in range(depth):
        hg = gain_row(2 * mem_w, [(0, mem_k_norm_g[l], MEM_HEADS)])
        mkv_p.append(norm_proj(mem2, norm_mem_g[l], w_mkv, l, hg, kinds_mkv)[0].reshape(bp, mlen, 2 * mem_w))
    conv0 = jnp.zeros((n_a, bp, CONV_W - 1, conv_dim), F32)
    ssm0 = jnp.zeros((n_a, bp, n_heads, SSM_HEAD_DIM, SSM_D_STATE), F32)
    prompt = Group(x_prompt, jnp.arange(t_p, dtype=jnp.int32), conv0, ssm0,
                   lambda l: (mkv_p[l], mkv_p[l], (0, mem_w)), None, None, t_p)

    bs, t_s, _ = x_sample.shape
    assert t_s == 1
    xs = jnp.pad(x_sample, ((0, 0), (0, SAMPLE_ROWS - t_s), (0, 0)))
    pos_s = jnp.full((bs * SAMPLE_ROWS,), PAST_LEN, jnp.int32)
    sample = Group(xs, pos_s, state_conv, state_ssm, lambda l: (cache_mem_k, cache_mem_v, l),
                   cache_win_k, cache_win_v, t_s)

    for l in range(depth):
        if l < n_a:
            proj_p, proj_s = in_proj_a(prompt, sample, l)
            a_p, b_p = mix_a(prompt, proj_p, l)
            a_s, b_s = mix_a(sample, proj_s, l)
            w_out, lw = w_out_a_b, l
        else:
            if l == n_a:
                shared_kv(prompt, sample)
            proj_p, proj_s = in_proj_b(prompt, sample, l)
            a_p, b_p = mix_b(prompt, proj_p, l)
            a_s, b_s = mix_b(sample, proj_s, l)
            w_out, lw = w_out_b_b, l - n_a
        todo = [(w, dst) for w, dst in ((w_up, w_up_b), (w_down, w_down_b)) if l + 1 < depth and dst[l + 1] is None]
        prompt.x2, sample.x2, rounded = mix_mlp(prompt.x2, a_p, b_p, w_out, lw, norm_mlp_g[l], w_up_b[l], w_down_b[l],
                                                cast=tuple((w, l + 1) for w, _ in todo),
                                                extra=(sample.x2, a_s, b_s))
        for (_, dst), r in zip(todo, rounded):
            dst[l + 1] = r

    y_p = prompt.x2.reshape(bp, t_p, d_model)
    conv_p, ssm_p, kv_p = jnp.stack(prompt.conv_new), jnp.stack(prompt.ssm_new), prompt.kv3
    mkv_all = jnp.stack(mkv_p)
    mem_k_p = mkv_all[..., :mem_w].reshape(depth, bp, mlen, MEM_HEADS, HEAD_DIM)
    mem_v_p = mkv_all[..., mem_w:].reshape(depth, bp, mlen, MEM_HEADS, HEAD_DIM)
    keep = min(max(w for w, _ in DIL_GROUPS), t_p)
    win_k_p, win_v_p = split_kv_heads(kv_p, t_p - keep, keep)

    y_s = sample.x2.reshape(bs, SAMPLE_ROWS, d_model)
    conv_s, ssm_s, kv_s = jnp.stack(sample.conv_new), jnp.stack(sample.ssm_new), sample.kv3
    y_s = y_s[:, :t_s]
    k_s = kv_s[:, :t_s, :kv_w].reshape(bs, t_s, DIL_HEADS, HEAD_DIM)
    v_s = kv_s[:, :t_s, kv_w:].reshape(bs, t_s, DIL_HEADS, HEAD_DIM)

    return (y_p, y_s, conv_p, ssm_p, win_k_p, win_v_p, mem_k_p, mem_v_p, conv_s, ssm_s, k_s, v_s)
```

```python
import functools

import jax
import jax.numpy as jnp
from jax import lax
from jax.experimental import pallas as pl
from jax.experimental.pallas import tpu as pltpu

F32 = jnp.float32
BF16 = jnp.bfloat16

HEAD_DIM = 128
SSM_HEAD_DIM = 64
SSM_GROUPS = 4
SSM_D_STATE = 128
CONV_W = 4
SSD_CHUNK = 128
MEM_HEADS = 4
DIL_GROUPS = ((128, 1), (512, 4), (2048, 16))
DIL_HEADS = 4
PAST_LEN = 16384
ROPE_THETA = 10000.0
EPS = 1e-6
LOG2_E = 1.4426950408889634
LN_2 = 0.6931471805599453

LANES = 128
SUBLANES = 8
VMEM_LIMIT = 48 * 1024 * 1024
MIX_MLP_VMEM_LIMIT = 56 * 1024 * 1024
COL_TILE = 512
ROW_TILE = 512
MEM_Q_TILE = 2048
PROJ_ROW_TILE = 256
SAMPLE_ROWS = SUBLANES
DIL_UNROLL = 16

NT_DIMS = (((1,), (1,)), ((), ()))
RESIDENT = pl.Buffered(1)


def _cparams(*sem):
    return pltpu.CompilerParams(dimension_semantics=sem, vmem_limit_bytes=VMEM_LIMIT)


def _sigmoid(x):
    return 1.0 / (1.0 + jnp.exp(-x))


def _rms_scale(x):
    return lax.rsqrt(jnp.mean(x * x, axis=-1, keepdims=True) + EPS)


class _SideCast:
    def __init__(self, mats, steps, step_of):
        self.n = len(mats)
        self.shapes = [a.shape[1:] for a, _ in mats]
        self.views, self.in_specs, self.out_specs, self.out_shape = [], [], [], []
        for a, layer in mats:
            n_layers, r, c = a.shape
            assert r % (steps * 2 * SUBLANES) == 0
            band = r // steps
            self.views.append(a.reshape(n_layers, steps, band, c))
            self.in_specs.append(pl.BlockSpec((None, None, band, c),
                                              lambda *ids, layer=layer: (layer, step_of(*ids), 0, 0)))
            self.out_specs.append(pl.BlockSpec((None, band, c), lambda *ids: (step_of(*ids), 0, 0)))
            self.out_shape.append(jax.ShapeDtypeStruct((steps, band, c), BF16))

    def wrap(self, body, n_in, n_out):
        n = self.n
        if not n:
            return body

        def kern(*refs):
            ins, srcs = refs[:n_in], refs[n_in:n_in + n]
            outs, dsts = refs[n_in + n:n_in + n + n_out], refs[n_in + n + n_out:n_in + 2 * n + n_out]
            for s, d in zip(srcs, dsts):
                d[...] = s[...].astype(BF16)
            body(*ins, *outs, *refs[n_in + 2 * n + n_out:])
        return kern

    def split(self, outs, n_out):
        main = outs[0] if n_out == 1 else tuple(outs[:n_out])
        if not self.n:
            return main
        return main, [o.reshape(s) for o, s in zip(outs[n_out:], self.shapes)]


def _round_kernel(src_ref, dst_ref):
    dst_ref[...] = src_ref[...].astype(BF16)


def round_weight(w, n_rows):
    n_layers, _, c = w.shape
    band = ROW_TILE
    assert n_rows % band == 0
    return pl.pallas_call(
        _round_kernel,
        grid=(n_layers, n_rows // band),
        in_specs=[pl.BlockSpec((None, band, c), lambda l, i: (l, i, 0))],
        out_specs=pl.BlockSpec((None, band, c), lambda l, i: (l, i, 0)),
        out_shape=jax.ShapeDtypeStruct((n_layers, n_rows, c), BF16),
        compiler_params=_cparams("parallel", "parallel"),
        name="round_weight",
    )(w)


def _rope_table_kernel(pos_ref, inv_ref, cos_ref, sin_ref):
    ang = pos_ref[...] * inv_ref[...]
    lane = lax.broadcasted_iota(jnp.int32, ang.shape, 1)
    cos_ref[...] = jnp.cos(ang)
    s = jnp.sin(ang)
    sin_ref[...] = jnp.where(lane < HEAD_DIM // 2, -s, s)


def rope_tables(pos):
    r = pos.shape[0]
    half = HEAD_DIM // 2
    inv = ROPE_THETA ** (-jnp.arange(half, dtype=F32) * (2.0 / HEAD_DIM))
    inv = jnp.concatenate([inv, inv])[None, :]
    posf = jnp.broadcast_to(pos.astype(F32)[:, None], (r, HEAD_DIM))
    tr = min(r, ROW_TILE)
    return pl.pallas_call(
        _rope_table_kernel,
        grid=(r // tr,),
        in_specs=[pl.BlockSpec((tr, HEAD_DIM), lambda i: (i, 0)),
                  pl.BlockSpec((1, HEAD_DIM), lambda i: (0, 0))],
        out_specs=[pl.BlockSpec((tr, HEAD_DIM), lambda i: (i, 0))] * 2,
        out_shape=[jax.ShapeDtypeStruct((r, HEAD_DIM), F32)] * 2,
        compiler_params=_cparams("parallel"),
        name="rope_tables",
    )(posf, inv)


def _proj_kernel(*refs, kinds, tile_src, n_w, transposed, has_extra):
    rope = "rope" in kinds
    refs = list(refs)
    x_ref, g_ref = refs[:2]
    w_refs = refs[2:2 + n_w]
    hg_ref = refs[2 + n_w]
    at = 3 + n_w
    if rope:
        cos_ref, sin_ref = refs[at:at + 2]
        at += 2
    if has_extra:
        xs_ref = refs[at]
        at += 1
        if rope:
            cos_s_ref, sin_s_ref = refs[at:at + 2]
            at += 2
        o_ref, os_ref, xn_ref = refs[at:at + 3]
    else:
        o_ref, xn_ref = refs[at:at + 2]
    tm = x_ref.shape[0]

    def normed(x):
        return (x * _rms_scale(x) * g_ref[...]).astype(BF16)

    def tiles(xn, cos, sin, store):
        for t, kind in enumerate(kinds):
            which, col, width = tile_src[t]
            at_col = sum(w for _, _, w in tile_src[:t])
            if transposed:
                y = lax.dot_general(xn, w_refs[which][col:col + width, :], NT_DIMS, preferred_element_type=F32)
            else:
                y = jnp.dot(xn, w_refs[which][:, col:col + width], preferred_element_type=F32)
            if kind == "plain":
                store(slice(at_col, at_col + width), y)
                continue
            for j in range(width // HEAD_DIM):
                osl = slice(at_col + j * HEAD_DIM, at_col + (j + 1) * HEAD_DIM)
                yh = y[:, j * HEAD_DIM:(j + 1) * HEAD_DIM]
                yh = yh * _rms_scale(yh) * hg_ref[:, osl]
                if kind == "rope":
                    yh = yh * cos + pltpu.roll(yh, HEAD_DIM // 2, axis=1) * sin
                store(osl, yh)

    def store_main(cols, y):
        o_ref[:, cols] = y

    def store_both(cols, y):
        o_ref[:, cols] = y[:tm]
        os_ref[:, cols] = y[tm:]

    xn_ref[:tm, :] = normed(x_ref[...])
    cos = cos_ref[...] if rope else None
    sin = sin_ref[...] if rope else None
    if not has_extra:
        tiles(xn_ref[...], cos, sin, store_main)
        return

    @pl.when(pl.program_id(0) == 0)
    def _():
        xn_ref[tm:, :] = normed(xs_ref[...])
        cos_all = jnp.concatenate([cos, cos_s_ref[...]], axis=0) if rope else None
        sin_all = jnp.concatenate([sin, sin_s_ref[...]], axis=0) if rope else None
        tiles(xn_ref[...], cos_all, sin_all, store_both)

    @pl.when(pl.program_id(0) != 0)
    def _():
        tiles(xn_ref[:tm, :], cos, sin, store_main)


def norm_proj(x, g, ws, layer, head_gain, kinds, cos=None, sin=None, cast=(), tile_src=None, transposed=False,
              extra=None):
    m, d = x.shape
    ws = list(ws) if isinstance(ws, (list, tuple)) else [ws]
    if tile_src is None:
        tile_src = [(0, t * COL_TILE, COL_TILE) for t in range(len(kinds))]
    n_cols = sum(width for _, _, width in tile_src)
    assert len(tile_src) == len(kinds) and head_gain.shape == (1, n_cols)
    assert all(width % HEAD_DIM == 0 for _, _, width in tile_src)
    rope = "rope" in kinds
    has_extra = extra is not None
    ms = extra[0].shape[0] if has_extra else 0
    tm = min(m, ROW_TILE if n_cols <= 2 * COL_TILE else PROJ_ROW_TILE)
    assert m % tm == 0
    in_specs = [pl.BlockSpec((tm, d), lambda i: (i, 0)),
                pl.BlockSpec((1, d), lambda i: (0, 0))]
    for k, w in enumerate(ws):
        used = max(col + width for which, col, width in tile_src if which == k)
        if transposed:
            assert used <= w.shape[1] and w.shape[2] == d
            in_specs.append(pl.BlockSpec((None, used, d), lambda i: (layer, 0, 0), pipeline_mode=RESIDENT))
            continue
        assert used <= w.shape[2] and (used % LANES == 0 or used == w.shape[2])
        in_specs.append(pl.BlockSpec((None, d, used), lambda i: (layer, 0, 0), pipeline_mode=RESIDENT))
    in_specs.append(pl.BlockSpec((1, n_cols), lambda i: (0, 0)))
    args = [x, g.reshape(1, d), *ws, head_gain]
    if rope:
        assert cos.shape[0] % tm == 0
        tab_blocks = cos.shape[0] // tm
        in_specs += [pl.BlockSpec((tm, HEAD_DIM), lambda i: (i % tab_blocks, 0))] * 2
        args += [cos, sin]
    out_specs = [pl.BlockSpec((tm, n_cols), lambda i: (i, 0))]
    out_shape = [jax.ShapeDtypeStruct((m, n_cols), F32)]
    if has_extra:
        whole = lambda a: pl.BlockSpec(a.shape, lambda i: (0, 0))
        xs, cos_s, sin_s = extra
        extras = [xs] + ([cos_s, sin_s] if rope else [])
        in_specs += [whole(a) for a in extras]
        args += extras
        out_specs.append(pl.BlockSpec((ms, n_cols), lambda i: (0, 0)))
        out_shape.append(jax.ShapeDtypeStruct((ms, n_cols), F32))
    n_out = len(out_specs)
    side = _SideCast(cast, m // tm, lambda i: i)
    outs = pl.pallas_call(
        side.wrap(functools.partial(_proj_kernel, kinds=tuple(kinds), tile_src=tuple(tile_src), n_w=len(ws),
                                    transposed=transposed, has_extra=has_extra),
                  len(args), n_out),
        grid=(m // tm,),
        in_specs=in_specs + side.in_specs,
        out_specs=out_specs + side.out_specs,
        out_shape=out_shape + side.out_shape,
        scratch_shapes=[pltpu.VMEM((tm + ms, d), BF16)],
        compiler_params=_cparams("arbitrary"),
        name="norm_proj",
    )(*args, *side.views)
    casts = [o.reshape(shp) for o, shp in zip(outs[n_out:], side.shapes)]
    return outs[0], (outs[1] if has_extra else None), casts


def _mix_mlp_kernel(*refs, has_extra):
    if has_extra:
        (x_ref, g_ref, wu_ref, wd_ref, a_ref, b_ref, wa_ref, wb_ref, xs_ref, as_ref, bs_ref,
         o_ref, os_ref, xn_ref) = refs
    else:
        x_ref, g_ref, wu_ref, wd_ref, a_ref, b_ref, wa_ref, wb_ref, o_ref, xn_ref = refs
    i, f = pl.program_id(0), pl.program_id(1)
    tm = x_ref.shape[0]

    def normed(x):
        return (x * _rms_scale(x) * g_ref[...]).astype(BF16)

    def mixed(a, b):
        acc = jnp.dot(a, wa_ref[...], preferred_element_type=F32)
        return acc + jnp.dot(b, wb_ref[...], preferred_element_type=F32)

    def ffn(xn):
        h = jnp.dot(xn, wu_ref[...], preferred_element_type=F32)
        h = jnp.maximum(h, 0.0)
        h = (h * h).astype(BF16)
        return jnp.dot(h, wd_ref[...], preferred_element_type=F32)

    def start_main():
        x1 = x_ref[...] + mixed(a_ref[...], b_ref[...])
        xn_ref[:tm, :] = normed(x1)
        o_ref[...] = x1

    def start_both():
        acc = mixed(jnp.concatenate([a_ref[...], as_ref[...]], axis=0),
                    jnp.concatenate([b_ref[...], bs_ref[...]], axis=0))
        x1 = x_ref[...] + acc[:tm]
        xs1 = xs_ref[...] + acc[tm:]
        xn_ref[:tm, :] = normed(x1)
        xn_ref[tm:, :] = normed(xs1)
        o_ref[...] = x1
        os_ref[...] = xs1

    if not has_extra:
        pl.when(f == 0)(start_main)
        o_ref[...] += ffn(xn_ref[...])
        return

    @pl.when(i == 0)
    def _():
        pl.when(f == 0)(start_both)
        upd = ffn(xn_ref[...])
        o_ref[...] += upd[:tm]
        os_ref[...] += upd[tm:]

    @pl.when(i != 0)
    def _():
        pl.when(f == 0)(start_main)
        o_ref[...] += ffn(xn_ref[:tm, :])


def mix_mlp(x, a, b, w_out, layer, g, wu, wd, tf=1024, cast=(), extra=None):
    m, d = x.shape
    ff = wu.shape[1]
    tm = min(m, ROW_TILE)
    nf = ff // tf
    ka, kb = a.shape[1], b.shape[1]
    assert w_out.shape[1] == ka + kb and ka % kb == 0
    has_extra = extra is not None
    side = _SideCast(cast, (m // tm) * nf, lambda i, j: i * nf + j)
    in_specs = [pl.BlockSpec((tm, d), lambda i, j: (i, 0)),
                pl.BlockSpec((1, d), lambda i, j: (0, 0)),
                pl.BlockSpec((d, tf), lambda i, j: (0, j)),
                pl.BlockSpec((tf, d), lambda i, j: (j, 0)),
                pl.BlockSpec((tm, ka), lambda i, j: (i, 0)),
                pl.BlockSpec((tm, kb), lambda i, j: (i, 0)),
                pl.BlockSpec((None, ka, d), lambda i, j: (layer, 0, 0), pipeline_mode=RESIDENT),
                pl.BlockSpec((None, kb, d), lambda i, j: (layer, ka // kb, 0), pipeline_mode=RESIDENT)]
    out_specs = [pl.BlockSpec((tm, d), lambda i, j: (i, 0))]
    out_shape = [jax.ShapeDtypeStruct((m, d), F32)]
    args = [x, g.reshape(1, d), wu, wd, a, b, w_out, w_out]
    ms = extra[0].shape[0] if has_extra else 0
    if has_extra:
        in_specs += [pl.BlockSpec(e.shape, lambda i, j: (0, 0)) for e in extra]
        args += list(extra)
        out_specs.append(pl.BlockSpec((ms, d), lambda i, j: (0, 0)))
        out_shape.append(jax.ShapeDtypeStruct((ms, d), F32))
    n_out = len(out_specs)
    outs = pl.pallas_call(
        side.wrap(functools.partial(_mix_mlp_kernel, has_extra=has_extra), len(args), n_out),
        grid=(m // tm, nf),
        in_specs=in_specs + side.in_specs,
        out_specs=out_specs + side.out_specs,
        out_shape=out_shape + side.out_shape,
        scratch_shapes=[pltpu.VMEM((tm + ms, d), BF16)],
        compiler_params=pltpu.CompilerParams(dimension_semantics=("arbitrary", "arbitrary"),
                                             vmem_limit_bytes=MIX_MLP_VMEM_LIMIT),
        name="mix_mlp",
    )(*args, *side.views)
    casts = [o.reshape(shp) for o, shp in zip(outs[n_out:], side.shapes)]
    return outs[0], (outs[1] if has_extra else None), casts


def _split3(x):
    hi = x.astype(BF16)
    r1 = x - hi.astype(F32)
    mid = r1.astype(BF16)
    lo = (r1 - mid.astype(F32)).astype(BF16)
    return hi, mid, lo


def _ssd_kernel(xbc_ref, dt_ref, z_ref, cw_ref, cb_ref, dtb_ref, alog_ref, dsk_ref, gg_ref, tri_ref, expand_ref,
                conv0_ref, h0_ref, y_ref, convo_ref, ho_ref,
                ext_ref, ht_ref, yd_ref, st_ref, *, t_real, n_heads, d_inner):
    L = SSD_CHUNK
    P = SSM_HEAD_DIM
    N = SSM_D_STATE
    G = SSM_GROUPS
    hpg = n_heads // G
    gw = hpg * P
    hist = CONV_W - 1
    base = SUBLANES - hist
    c = pl.program_id(1)
    nc = pl.num_programs(1)

    @pl.when(c == 0)
    def _():
        ext_ref[base:SUBLANES, :] = conv0_ref[...]
        for g in range(G):
            ht_ref[g] = h0_ref[g].T

    ext_ref[SUBLANES:SUBLANES + L, :] = xbc_ref[...]
    ext = ext_ref[...]
    acc = cb_ref[...] + ext[SUBLANES:, :] * cw_ref[hist:CONV_W, :]
    for k in range(1, CONV_W):
        acc = acc + pltpu.roll(ext, k, axis=0)[SUBLANES:, :] * cw_ref[hist - k:CONV_W - k, :]
    xc = acc * _sigmoid(acc)

    @pl.when(c == nc - 1)
    def _():
        n_in_last = t_real - ((t_real - 1) // L) * L
        convo_ref[...] = ext_ref[base + n_in_last:base + n_in_last + hist, :]

    ext_ref[base:SUBLANES, :] = ext_ref[base + L:SUBLANES + L, :]

    dt_raw = dt_ref[...] + dtb_ref[...]
    dt = jnp.maximum(dt_raw, 0.0) + jnp.log1p(jnp.exp(-jnp.abs(dt_raw)))
    row = lax.broadcasted_iota(jnp.int32, (L, LANES), 0) + c * L
    dt = jnp.where(row < t_real, dt, 0.0)
    da = dt * (-jnp.exp(alog_ref[...]))
    li = lax.broadcasted_iota(jnp.int32, (L, L), 0)
    si = lax.broadcasted_iota(jnp.int32, (L, L), 1)
    causal = li >= si
    a_cs = sum(jnp.dot(tri_ref[...], part, preferred_element_type=F32) for part in _split3(da))
    a_cs_t = a_cs.T
    dt_t = dt.T
    w_t = dt_t * jnp.exp(a_cs_t[:, L - 1:L] - a_cs_t)
    e_exp = sum(jnp.dot(part, expand_ref[...], preferred_element_type=F32) for part in _split3(jnp.exp(a_cs)))
    xb = xc[:, :d_inner].astype(BF16)
    low_half = lax.broadcasted_iota(jnp.int32, (L, 2 * P), 1) < P

    for g in range(G):
        b_g = xc[:, d_inner + g * N:d_inner + (g + 1) * N]
        c_g = xc[:, d_inner + G * N + g * N:d_inner + G * N + (g + 1) * N].astype(BF16)
        cbm = lax.dot_general(c_g, b_g.astype(BF16), NT_DIMS, preferred_element_type=F32)
        b_t = b_g.T
        y_off = jnp.dot(c_g, ht_ref[g].astype(BF16), preferred_element_type=F32)
        for kp in range(hpg // 2):
            h0 = g * hpg + 2 * kp
            x_pair = xb[:, h0 * P:(h0 + 2) * P]
            yd, st = [], []
            for h in (h0, h0 + 1):
                seg = a_cs[:, h:h + 1] - a_cs_t[h:h + 1, :]
                dec = jnp.exp(jnp.where(causal, seg, -jnp.inf))
                mh = (cbm * dec * dt_t[h:h + 1, :]).astype(BF16)
                yd.append(jnp.dot(mh, x_pair, preferred_element_type=F32))
                btw = (b_t * w_t[h:h + 1, :]).astype(BF16)
                st.append(jnp.dot(btw, x_pair, preferred_element_type=F32))
            yd_ref[:, 2 * kp * P:(2 * kp + 2) * P] = jnp.where(low_half, yd[0], yd[1])
            st_ref[:, 2 * kp * P:(2 * kp + 2) * P] = jnp.where(low_half, st[0], st[1])
        gs = slice(g * gw, (g + 1) * gw)
        e_g = e_exp[:, gs]
        y_g = yd_ref[...] + y_off * e_g + dsk_ref[:, gs] * xc[:, gs]
        ht_ref[g] = ht_ref[g] * e_g[L - 1:L, :] + st_ref[...]
        z_g = z_ref[:, gs]
        gated = y_g * (z_g * _sigmoid(z_g))
        y_ref[:, gs] = (gated * _rms_scale(gated) * gg_ref[:, gs]).astype(y_ref.dtype)

    @pl.when(c == nc - 1)
    def _():
        for g in range(G):
            ho_ref[g] = ht_ref[g].T


def ssd_mixer(proj, col_xbc, col_dt, col_z, t_real, conv_w, conv_b, dt_bias, a_log, d_skip, gate_g,
              conv_prev, h_prev, cast=()):
    bt, tp, _ = proj.shape
    conv_dim = conv_w.shape[1]
    n_heads = dt_bias.shape[0]
    d_inner = n_heads * SSM_HEAD_DIM
    gw = d_inner // SSM_GROUPS
    L = SSD_CHUNK
    assert tp % L == 0 and (tp - t_real) < L
    assert col_xbc % conv_dim == 0 and col_dt % LANES == 0 and col_z % d_inner == 0
    pad = LANES - n_heads
    dtb = jnp.pad(dt_bias, (0, pad)).reshape(1, LANES)
    alog = jnp.pad(a_log, (0, pad)).reshape(1, LANES)
    dsk = jnp.repeat(d_skip, SSM_HEAD_DIM).reshape(1, d_inner)
    tri = jnp.tril(jnp.ones((L, L), BF16))
    expand = jnp.repeat(jnp.eye(LANES, dtype=BF16)[:, :n_heads], SSM_HEAD_DIM, axis=1)
    kern = functools.partial(_ssd_kernel, t_real=t_real, n_heads=n_heads, d_inner=d_inner)
    vec = lambda n: pl.BlockSpec((1, n), lambda b, c: (0, 0))
    nc = tp // L
    side = _SideCast(cast, bt * nc, lambda b, c: b * nc + c)
    outs = pl.pallas_call(
        side.wrap(kern, 13, 3),
        grid=(bt, nc),
        in_specs=[pl.BlockSpec((None, L, conv_dim), lambda b, c: (b, c, col_xbc // conv_dim)),
                  pl.BlockSpec((None, L, LANES), lambda b, c: (b, c, col_dt // LANES)),
                  pl.BlockSpec((None, L, d_inner), lambda b, c: (b, c, col_z // d_inner)),
                  pl.BlockSpec((CONV_W, conv_dim), lambda b, c: (0, 0)),
                  vec(conv_dim), vec(LANES), vec(LANES), vec(d_inner), vec(d_inner),
                  pl.BlockSpec((L, L), lambda b, c: (0, 0)),
                  pl.BlockSpec((LANES, d_inner), lambda b, c: (0, 0)),
                  pl.BlockSpec((None, CONV_W - 1, conv_dim), lambda b, c: (b, 0, 0)),
                  pl.BlockSpec((None, SSM_GROUPS, gw, SSM_D_STATE), lambda b, c: (b, 0, 0, 0))] + side.in_specs,
        out_specs=[pl.BlockSpec((None, L, d_inner), lambda b, c: (b, c, 0)),
                   pl.BlockSpec((None, CONV_W - 1, conv_dim), lambda b, c: (b, 0, 0)),
                   pl.BlockSpec((None, SSM_GROUPS, gw, SSM_D_STATE), lambda b, c: (b, 0, 0, 0))] + side.out_specs,
        out_shape=[jax.ShapeDtypeStruct((bt, tp, d_inner), BF16),
                   jax.ShapeDtypeStruct((bt, CONV_W - 1, conv_dim), F32),
                   jax.ShapeDtypeStruct((bt, SSM_GROUPS, gw, SSM_D_STATE), F32)] + side.out_shape,
        scratch_shapes=[pltpu.VMEM((SUBLANES + L, conv_dim), F32),
                        pltpu.VMEM((SSM_GROUPS, SSM_D_STATE, gw), F32),
                        pltpu.VMEM((L, gw), F32),
                        pltpu.VMEM((SSM_D_STATE, gw), F32)],
        compiler_params=_cparams("parallel", "arbitrary"),
        name="ssd_mixer",
    )(proj, proj, proj, conv_w, conv_b.reshape(1, conv_dim), dtb, alog, dsk, gate_g.reshape(1, d_inner),
      tri, expand, conv_prev, h_prev, *side.views)
    return side.split(outs, 3)


def _ssd_step_kernel(xbc_ref, dt_ref, z_ref, cw_ref, cb_ref, dtb_ref, alog_ref, dsk_ref, gg_ref, expand_ref,
                     conv0_ref, h0_ref, y_ref, convo_ref, ho_ref, *, n_heads, d_inner):
    P, N, G = SSM_HEAD_DIM, SSM_D_STATE, SSM_GROUPS
    gw = n_heads // G * P
    hist = CONV_W - 1
    acc = cb_ref[...] + xbc_ref[...] * cw_ref[hist:CONV_W, :]
    for k in range(hist):
        acc = acc + conv0_ref[k:k + 1, :] * cw_ref[k:k + 1, :]
    xc = acc * _sigmoid(acc)
    convo_ref[0:hist - 1, :] = conv0_ref[1:hist, :]
    convo_ref[hist - 1:hist, :] = xbc_ref[0:1, :]

    dt_raw = dt_ref[...] + dtb_ref[...]
    dt = jnp.maximum(dt_raw, 0.0) + jnp.log1p(jnp.exp(-jnp.abs(dt_raw)))
    decay = jnp.exp(dt * (-jnp.exp(alog_ref[...])))
    expand = lambda v: sum(jnp.dot(part, expand_ref[...], preferred_element_type=F32) for part in _split3(v))
    dt_x = expand(dt) * xc[:, :d_inner]
    decay_c = expand(decay)

    for g in range(G):
        gs = slice(g * gw, (g + 1) * gw)
        b_col = xc[:, d_inner + g * N:d_inner + (g + 1) * N].T[:, 0:1]
        c_g = xc[:, d_inner + G * N + g * N:d_inner + G * N + (g + 1) * N].astype(BF16)
        h_t = h0_ref[g].T * decay_c[0:1, gs] + b_col * dt_x[0:1, gs]
        ho_ref[g] = h_t.T
        y_g = jnp.dot(c_g, h_t.astype(BF16), preferred_element_type=F32) + dsk_ref[:, gs] * xc[:, gs]
        z_g = z_ref[:, gs]
        gated = y_g * (z_g * _sigmoid(z_g))
        y_ref[:, gs] = (gated * _rms_scale(gated) * gg_ref[:, gs]).astype(y_ref.dtype)


def ssd_step(proj, col_xbc, col_dt, col_z, conv_w, conv_b, dt_bias, a_log, d_skip, gate_g, conv_prev, h_prev):
    bt, rows, _ = proj.shape
    conv_dim = conv_w.shape[1]
    n_heads = dt_bias.shape[0]
    d_inner = n_heads * SSM_HEAD_DIM
    gw = d_inner // SSM_GROUPS
    assert col_xbc % conv_dim == 0 and col_dt % LANES == 0 and col_z % d_inner == 0
    pad = LANES - n_heads
    dtb = jnp.pad(dt_bias, (0, pad)).reshape(1, LANES)
    alog = jnp.pad(a_log, (0, pad)).reshape(1, LANES)
    dsk = jnp.repeat(d_skip, SSM_HEAD_DIM).reshape(1, d_inner)
    expand = jnp.repeat(jnp.eye(LANES, dtype=BF16)[:, :n_heads], SSM_HEAD_DIM, axis=1)
    vec = lambda n: pl.BlockSpec((1, n), lambda b: (0, 0))
    state = pl.BlockSpec((None, SSM_GROUPS, gw, SSM_D_STATE), lambda b: (b, 0, 0, 0))
    hist_rows = pl.BlockSpec((None, CONV_W - 1, conv_dim), lambda b: (b, 0, 0))
    return pl.pallas_call(
        functools.partial(_ssd_step_kernel, n_heads=n_heads, d_inner=d_inner),
        grid=(bt,),
        in_specs=[pl.BlockSpec((None, rows, conv_dim), lambda b: (b, 0, col_xbc // conv_dim)),
                  pl.BlockSpec((None, rows, LANES), lambda b: (b, 0, col_dt // LANES)),
                  pl.BlockSpec((None, rows, d_inner), lambda b: (b, 0, col_z // d_inner)),
                  pl.BlockSpec((CONV_W, conv_dim), lambda b: (0, 0)),
                  vec(conv_dim), vec(LANES), vec(LANES), vec(d_inner), vec(d_inner),
                  pl.BlockSpec((LANES, d_inner), lambda b: (0, 0)),
                  hist_rows, state],
        out_specs=[pl.BlockSpec((None, rows, d_inner), lambda b: (b, 0, 0)), hist_rows, state],
        out_shape=[jax.ShapeDtypeStruct((bt, rows, d_inner), BF16),
                   jax.ShapeDtypeStruct((bt, CONV_W - 1, conv_dim), F32),
                   jax.ShapeDtypeStruct((bt, SSM_GROUPS, gw, SSM_D_STATE), F32)],
        compiler_params=_cparams("parallel"),
        name="ssd_step",
    )(proj, proj, proj, conv_w, conv_b.reshape(1, conv_dim), dtb, alog, dsk, gate_g.reshape(1, d_inner),
      expand, conv_prev, h_prev)


def _mem_attn_kernel(q_ref, k_ref, v_ref, o_ref):
    scale = HEAD_DIM ** -0.5
    headed = len(k_ref.shape) == 3
    for h in range(MEM_HEADS):
        sl = slice(h * HEAD_DIM, (h + 1) * HEAD_DIM)
        q = (q_ref[:, sl] * (scale * LOG2_E)).astype(BF16)
        k = (k_ref[:, h, :] if headed else k_ref[:, sl]).astype(BF16)
        v = (v_ref[:, h, :] if headed else v_ref[:, sl]).astype(BF16)
        s = lax.dot_general(q, k, NT_DIMS, preferred_element_type=F32)
        e = jnp.exp2(s - jnp.max(s, axis=-1, keepdims=True))
        den = jnp.sum(e, axis=-1, keepdims=True)
        o = jnp.dot(e.astype(BF16), v, preferred_element_type=F32) / den
        o_ref[:, sl] = o.astype(o_ref.dtype)


def mem_attention(proj, col_q, k, v, kv_at):
    bt, t, _ = proj.shape
    width = MEM_HEADS * HEAD_DIM
    tq = min(t, MEM_Q_TILE)
    assert t % tq == 0
    if k.ndim == 5:
        mlen = k.shape[2]
        kv_specs = [pl.BlockSpec((None, None, mlen, MEM_HEADS, HEAD_DIM), lambda b, i: (kv_at, b, 0, 0, 0))] * 2
    else:
        mlen = k.shape[1]
        kv_specs = [pl.BlockSpec((None, mlen, width), lambda b, i, c=c: (b, 0, c // width)) for c in kv_at]
    return pl.pallas_call(
        _mem_attn_kernel,
        grid=(bt, t // tq),
        in_specs=[pl.BlockSpec((None, tq, width), lambda b, i: (b, i, col_q // width))] + kv_specs,
        out_specs=pl.BlockSpec((None, tq, width), lambda b, i: (b, i, 0)),
        out_shape=jax.ShapeDtypeStruct((bt, t, width), BF16),
        compiler_params=_cparams("parallel", "parallel"),
        name="mem_attention",
    )(proj, k, v)


def _run_units(first, count, unit, unroll):
    trips = count // unroll
    if trips == 1:
        trips = 0
    if trips:
        def trip(it, carry):
            for k in range(unroll):
                unit(first + it * unroll + k)
            return carry
        lax.fori_loop(0, trips, trip, 0)
    for k in range(trips * unroll, count):
        unit(first + k)


def _dil_attn_kernel(*refs, groups, rb, unroll):
    ng = len(groups)
    q_refs = refs[:ng]
    kp_ref, kc_ref, vp_ref, vc_ref, o_ref = refs[ng:ng + 5]
    og = refs[ng + 5:2 * ng + 5]
    lg = refs[2 * ng + 5:3 * ng + 5]
    i = pl.program_id(1)
    scale = HEAD_DIM ** -0.5

    for gi, (win, d) in enumerate(groups):
        w = win // d
        dw = d * w
        qi = lax.broadcasted_iota(jnp.int32, (w, 2 * w), 0)
        kj = lax.broadcasted_iota(jnp.int32, (w, 2 * w), 1)
        band = (kj >= qi) & (kj <= qi + w)
        band_first = band & (kj >= jnp.where(i > 0, 0, w))

        def rows(start, size, d=d, w=w):
            if d == 1:
                return pl.ds(pl.multiple_of(start, w), size)
            return pl.ds(start, size, stride=d)

        def attend(rows_q, kk, vv, mask, gi=gi, w=w):
            q = (q_refs[gi][rows_q, :] * (scale * LOG2_E)).astype(BF16)
            s = lax.dot_general(q, kk.astype(BF16), NT_DIMS, preferred_element_type=F32)
            s = jnp.where(mask, s, -jnp.inf)
            m = jnp.max(s, axis=-1, keepdims=True)
            e = jnp.exp2(s - m)
            den = jnp.sum(e, axis=-1, keepdims=True)
            o = jnp.dot(e.astype(BF16), vv.astype(BF16), preferred_element_type=F32) / den
            og[gi][rows_q, :] = o
            lg[gi][rows_q, :] = jnp.broadcast_to(m * LN_2 + jnp.log(den), (w, HEAD_DIM))

        def first_unit(r, rows=rows, attend=attend, w=w, dw=dw, mask=band_first):
            rows_q, rows_p = rows(r, w), rows(rb - dw + r, w)
            kk = jnp.concatenate([kp_ref[rows_p, :], kc_ref[rows_q, :]], axis=0)
            vv = jnp.concatenate([vp_ref[rows_p, :], vc_ref[rows_q, :]], axis=0)
            attend(rows_q, kk, vv, mask)

        def later_unit(u, rows=rows, attend=attend, d=d, w=w, dw=dw, mask=band):
            sb = u // d
            start_q = sb * dw + (u - sb * d)
            rows_k = rows(start_q - dw, 2 * w)
            attend(rows(start_q, w), kc_ref[rows_k, :], vc_ref[rows_k, :], mask)

        _run_units(0, d, first_unit, unroll)
        _run_units(d, rb // w - d, later_unit, unroll)

    ls = [r[...] for r in lg]
    mm = functools.reduce(jnp.maximum, ls)
    ws = [jnp.exp(l - mm) for l in ls]
    num = sum(wg * r[...] for wg, r in zip(ws, og))
    o_ref[...] = (num / sum(ws)).astype(o_ref.dtype)


def dilated_attention(proj, kv):
    bt, t, _ = proj.shape
    ng = len(DIL_GROUPS)
    rb = min(t, max(win for win, _ in DIL_GROUPS))
    for win, d in DIL_GROUPS:
        assert win % d == 0 and rb % win == 0
    assert t % rb == 0
    blk = lambda f: pl.BlockSpec((None, rb, HEAD_DIM), f)
    prev = lambda i: jnp.maximum(i - 1, 0)
    q_specs = [blk(lambda b, i, h, g=g: (b, i, g * DIL_HEADS + h)) for g in range(ng)]
    kv_specs = [blk(lambda b, i, h: (b, prev(i), h)), blk(lambda b, i, h: (b, i, h)),
                blk(lambda b, i, h: (b, prev(i), DIL_HEADS + h)), blk(lambda b, i, h: (b, i, DIL_HEADS + h))]
    return pl.pallas_call(
        functools.partial(_dil_attn_kernel, groups=DIL_GROUPS, rb=rb, unroll=DIL_UNROLL),
        grid=(bt, t // rb, DIL_HEADS),
        in_specs=q_specs + kv_specs,
        out_specs=blk(lambda b, i, h: (b, i, h)),
        out_shape=jax.ShapeDtypeStruct((bt, t, DIL_HEADS * HEAD_DIM), BF16),
        scratch_shapes=[pltpu.VMEM((rb, HEAD_DIM), F32)] * (2 * ng),
        compiler_params=_cparams("parallel", "parallel", "parallel"),
        name="dilated_attention",
    )(*([proj] * ng), kv, kv, kv, kv)


def _dil_decode_kernel(*refs):
    ng = len(DIL_GROUPS)
    q_ref, kv_ref = refs[:2]
    k_refs, v_refs = refs[2:2 + ng], refs[2 + ng:2 + 2 * ng]
    o_ref = refs[-1]
    width = DIL_HEADS * HEAD_DIM
    scale = HEAD_DIM ** -0.5
    for h in range(DIL_HEADS):
        sl = slice(h * HEAD_DIM, (h + 1) * HEAD_DIM)
        k_new = kv_ref[:, h * HEAD_DIM:(h + 1) * HEAD_DIM]
        v_new = kv_ref[:, width + h * HEAD_DIM:width + (h + 1) * HEAD_DIM]
        outs, lses = [], []
        for gi in range(ng):
            kc = k_refs[gi][:, h, :].astype(BF16)
            vc = v_refs[gi][:, h, :].astype(BF16)
            q = q_ref[:, gi * width + h * HEAD_DIM:gi * width + (h + 1) * HEAD_DIM]
            s = lax.dot_general(q.astype(BF16), kc, NT_DIMS, preferred_element_type=F32) * scale
            s_self = jnp.sum(q * k_new, axis=-1, keepdims=True) * scale
            m = jnp.maximum(jnp.max(s, axis=-1, keepdims=True), s_self)
            e = jnp.exp(s - m)
            e_self = jnp.exp(s_self - m)
            den = jnp.sum(e, axis=-1, keepdims=True) + e_self
            o = jnp.dot(e.astype(BF16), vc, preferred_element_type=F32) + e_self * v_new
            outs.append(o / den)
            lses.append(m + jnp.log(den))
        mm = functools.reduce(jnp.maximum, lses)
        ws = [jnp.exp(l - mm) for l in lses]
        num = sum(wg * o for wg, o in zip(ws, outs))
        o_ref[:, sl] = (num / sum(ws)).astype(o_ref.dtype)


def dilated_decode(proj, kv_new, k_cache, v_cache):
    bt, rows, nq = proj.shape
    lc = k_cache.shape[1]
    width = DIL_HEADS * HEAD_DIM
    views, specs = [], []
    for cache in (k_cache, v_cache):
        for win, dil in DIL_GROUPS:
            assert lc % win == 0 and win % dil == 0
            w = win // dil
            views.append(cache.reshape(bt, lc // dil, dil, DIL_HEADS, HEAD_DIM))
            specs.append(pl.BlockSpec((None, w, None, DIL_HEADS, HEAD_DIM),
                                      lambda b, last=lc // win - 1: (b, last, 0, 0, 0)))
    return pl.pallas_call(
        _dil_decode_kernel,
        grid=(bt,),
        in_specs=[pl.BlockSpec((None, rows, nq), lambda b: (b, 0, 0)),
                  pl.BlockSpec((None, rows, 2 * width), lambda b: (b, 0, 0))] + specs,
        out_specs=pl.BlockSpec((None, rows, width), lambda b: (b, 0, 0)),
        out_shape=jax.ShapeDtypeStruct((bt, rows, width), BF16),
        compiler_params=_cparams("parallel"),
        name="dilated_decode",
    )(proj, kv_new, *views)


def _split_heads_kernel(kv_ref, k_ref, v_ref):
    width = DIL_HEADS * HEAD_DIM
    for h in range(DIL_HEADS):
        k_ref[:, h, :] = kv_ref[:, h * HEAD_DIM:(h + 1) * HEAD_DIM]
        v_ref[:, h, :] = kv_ref[:, width + h * HEAD_DIM:width + (h + 1) * HEAD_DIM]


def split_kv_heads(kv, first_row, n_rows):
    bt, _, w2 = kv.shape
    tr = min(n_rows, ROW_TILE)
    assert first_row % tr == 0 and n_rows % tr == 0
    heads = pl.BlockSpec((None, tr, DIL_HEADS, HEAD_DIM), lambda b, i: (b, i, 0, 0))
    return pl.pallas_call(
        _split_heads_kernel,
        grid=(bt, n_rows // tr),
        in_specs=[pl.BlockSpec((None, tr, w2), lambda b, i: (b, first_row // tr + i, 0))],
        out_specs=[heads, heads],
        out_shape=[jax.ShapeDtypeStruct((bt, n_rows, DIL_HEADS, HEAD_DIM), F32)] * 2,
        compiler_params=_cparams("parallel", "parallel"),
        name="split_kv_heads",
    )(kv)


def kernel(x_prompt, x_sample, state_conv, state_ssm, cache_win_k, cache_win_v, cache_mem_k, cache_mem_v,
           mem_prompt, norm_mix_g, norm_mlp_g, norm_mem_g, w_mem_k, w_mem_v, mem_q_norm_g, mem_k_norm_g,
           w_up, w_down, w_in_a, conv_w, conv_b, dt_bias, a_log, d_skip, gate_norm_g, w_out_a,
           w_in_b, q_norm_g, w_out_b, kv_norm_g, w_k_shared, w_v_shared, k_norm_g):
    depth = w_up.shape[0]
    n_a = w_in_a.shape[0]
    d_model = x_prompt.shape[-1]
    n_heads = dt_bias.shape[1]
    d_inner = n_heads * SSM_HEAD_DIM
    conv_dim = conv_w.shape[2]
    mem_w = MEM_HEADS * HEAD_DIM
    kv_w = DIL_HEADS * HEAD_DIM
    dil_q_w = len(DIL_GROUPS) * kv_w

    s1, s2, s3 = d_inner, d_inner + conv_dim, d_inner + conv_dim + n_heads
    assert s1 % COL_TILE == 0 and s2 % COL_TILE == 0
    w_in_a_t = jnp.swapaxes(w_in_a, 1, 2)
    w_a_main = round_weight(w_in_a_t, s2)
    tail = lax.optimization_barrier(w_in_a_t[:, s2:]).astype(BF16)
    w_a_tail = jnp.concatenate([tail[:, :n_heads], jnp.zeros((n_a, COL_TILE - n_heads, d_model), BF16),
                                tail[:, n_heads:]], axis=1)
    tiles_a = ([(0, s1 + c, COL_TILE) for c in range(0, conv_dim, COL_TILE)]
               + [(1, COL_TILE + c, COL_TILE) for c in range(0, mem_w, COL_TILE)]
               + [(0, c, COL_TILE) for c in range(0, d_inner, COL_TILE)] + [(1, 0, LANES)])
    kinds_a = (["plain"] * (conv_dim // COL_TILE) + ["norm"] * (mem_w // COL_TILE)
               + ["plain"] * (d_inner // COL_TILE + 1))
    col_xbc, col_qm_a, col_z = 0, conv_dim, conv_dim + mem_w
    col_dt = col_z + d_inner
    a_cols = col_dt + LANES
    kinds_b = ["rope"] * (dil_q_w // COL_TILE) + ["norm"] * (mem_w // COL_TILE)
    kinds_kv = ["rope"] * (kv_w // COL_TILE) + ["plain"] * (kv_w // COL_TILE)
    kinds_mkv = ["norm"] * (mem_w // COL_TILE) + ["plain"] * (mem_w // COL_TILE)
    w_b = w_in_b.astype(BF16)
    w_kv = jnp.concatenate([w_k_shared, w_v_shared], axis=-1).astype(BF16)[None]
    w_mkv = jnp.concatenate([w_mem_k, w_mem_v], axis=-1).astype(BF16)
    w_out_a_b, w_out_b_b = w_out_a.astype(BF16), w_out_b.astype(BF16)
    assert n_a >= 1
    w_up_b, w_down_b = [None] * depth, [None] * depth

    def gain_row(width, pieces):
        parts, at = [], 0
        for start, g, reps in pieces:
            parts += [jnp.ones((start - at,), F32), jnp.tile(g, reps)]
            at = start + reps * HEAD_DIM
        parts.append(jnp.ones((width - at,), F32))
        return jnp.concatenate(parts).reshape(1, width)

    class Group:
        def __init__(self, x, pos_rows, conv_prev, ssm_prev, mem_kv, k_past, v_past, t_real):
            self.bt, self.t, _ = x.shape
            self.m = self.bt * self.t
            self.x2 = x.reshape(self.m, d_model)
            self.cos, self.sin = rope_tables(pos_rows)
            self.conv_prev, self.ssm_prev, self.mem_kv = conv_prev, ssm_prev, mem_kv
            self.k_past, self.v_past, self.t_real = k_past, v_past, t_real
            self.conv_new, self.ssm_new, self.kv3 = [], [], None

    def in_proj_a(prompt, sample, l):
        hg = gain_row(a_cols, [(col_qm_a, mem_q_norm_g[l], MEM_HEADS)])
        proj_p, proj_s, rounded = norm_proj(prompt.x2, norm_mix_g[l], [w_a_main, w_a_tail], l, hg, kinds_a,
                                            tile_src=tiles_a, transposed=True, extra=(sample.x2, None, None),
                                            cast=((w_up, 0),) if l == 0 else ())
        if l == 0:
            (w_up_b[0],) = rounded
        return proj_p, proj_s

    def mix_a(gr, proj, l):
        bt, t, m = gr.bt, gr.t, gr.m
        proj3 = proj.reshape(bt, t, -1)
        h_prev = gr.ssm_prev[l].reshape(bt, SSM_GROUPS, -1, SSM_D_STATE)
        ssm_params = (conv_w[l], conv_b[l], dt_bias[l], a_log[l], d_skip[l], gate_norm_g[l])
        if gr.t_real == 1:
            y, c_new, h_new = ssd_step(proj3, col_xbc, col_dt, col_z, *ssm_params, gr.conv_prev[l], h_prev)
        else:
            assert t % SSD_CHUNK == 0 and gr.t_real == t
            todo = [(w_down, w_down_b, 0)] if l == 0 else []
            if l + 1 < depth:
                todo += [(w_up, w_up_b, l + 1), (w_down, w_down_b, l + 1)]
            ssd = ssd_mixer(proj3, col_xbc, col_dt, col_z, gr.t_real, *ssm_params, gr.conv_prev[l], h_prev,
                            cast=tuple((w, k) for w, _, k in todo))
            if todo:
                ssd, rounded = ssd
                for (_, dst, k), r in zip(todo, rounded):
                    dst[k] = r
            y, c_new, h_new = ssd
        gr.conv_new.append(c_new)
        gr.ssm_new.append(h_new.reshape(bt, n_heads, SSM_HEAD_DIM, SSM_D_STATE))
        mo = mem_attention(proj3, col_qm_a, *gr.mem_kv(l)).reshape(m, mem_w)
        return y.reshape(m, d_inner), mo

    def shared_kv(prompt, sample):
        hg = gain_row(2 * kv_w, [(0, k_norm_g, DIL_HEADS)])
        todo = n_a + 1 < depth and w_down_b[n_a + 1] is None
        kv_p, kv_s, rounded = norm_proj(prompt.x2, kv_norm_g, w_kv, 0, hg, kinds_kv, prompt.cos, prompt.sin,
                                        extra=(sample.x2, sample.cos, sample.sin),
                                        cast=((w_down, n_a + 1),) if todo else ())
        if todo:
            (w_down_b[n_a + 1],) = rounded
        prompt.kv3 = kv_p.reshape(prompt.bt, prompt.t, 2 * kv_w)
        sample.kv3 = kv_s.reshape(sample.bt, sample.t, 2 * kv_w)

    def in_proj_b(prompt, sample, l):
        j = l - n_a
        hg = gain_row(dil_q_w + mem_w, [(0, q_norm_g[j], dil_q_w // HEAD_DIM),
                                        (dil_q_w, mem_q_norm_g[l], MEM_HEADS)])
        todo = l + 1 < depth and w_up_b[l + 1] is None
        proj_p, proj_s, rounded = norm_proj(prompt.x2, norm_mix_g[l], w_b, j, hg, kinds_b, prompt.cos, prompt.sin,
                                            extra=(sample.x2, sample.cos, sample.sin),
                                            cast=((w_up, l + 1),) if todo else ())
        if todo:
            (w_up_b[l + 1],) = rounded
        return proj_p, proj_s

    def mix_b(gr, proj, l):
        proj3 = proj.reshape(gr.bt, gr.t, -1)
        if gr.k_past is None:
            att = dilated_attention(proj3, gr.kv3)
        else:
            att = dilated_decode(proj3, gr.kv3, gr.k_past, gr.v_past)
        mo = mem_attention(proj3, dil_q_w, *gr.mem_kv(l)).reshape(gr.m, mem_w)
        return att.reshape(gr.m, kv_w), mo

    bp, t_p, _ = x_prompt.shape
    mlen = mem_prompt.shape[1]
    mem2 = mem_prompt.reshape(bp * mlen, d_model)
    mkv_p = []
    for l in range(depth):
        hg = gain_row(2 * mem_w, [(0, mem_k_norm_g[l], MEM_HEADS)])
        mkv_p.append(norm_proj(mem2, norm_mem_g[l], w_mkv, l, hg, kinds_mkv)[0].reshape(bp, mlen, 2 * mem_w))
    conv0 = jnp.zeros((n_a, bp, CONV_W - 1, conv_dim), F32)
    ssm0 = jnp.zeros((n_a, bp, n_heads, SSM_HEAD_DIM, SSM_D_STATE), F32)
    prompt = Group(x_prompt, jnp.arange(t_p, dtype=jnp.int32), conv0, ssm0,
                   lambda l: (mkv_p[l], mkv_p[l], (0, mem_w)), None, None, t_p)

    bs, t_s, _ = x_sample.shape
    assert t_s == 1
    xs = jnp.pad(x_sample, ((0, 0), (0, SAMPLE_ROWS - t_s), (0, 0)))
    pos_s = jnp.full((bs * SAMPLE_ROWS,), PAST_LEN, jnp.int32)
    sample = Group(xs, pos_s, state_conv, state_ssm, lambda l: (cache_mem_k, cache_mem_v, l),
                   cache_win_k, cache_win_v, t_s)

    for l in range(depth):
        if l < n_a:
            proj_p, proj_s = in_proj_a(prompt, sample, l)
            a_p, b_p = mix_a(prompt, proj_p, l)
            a_s, b_s = mix_a(sample, proj_s, l)
            w_out, lw = w_out_a_b, l
        else:
            if l == n_a:
                shared_kv(prompt, sample)
            proj_p, proj_s = in_proj_b(prompt, sample, l)
            a_p, b_p = mix_b(prompt, proj_p, l)
            a_s, b_s = mix_b(sample, proj_s, l)
            w_out, lw = w_out_b_b, l - n_a
        todo = [(w, dst) for w, dst in ((w_up, w_up_b), (w_down, w_down_b)) if l + 1 < depth and dst[l + 1] is None]
        prompt.x2, sample.x2, rounded = mix_mlp(prompt.x2, a_p, b_p, w_out, lw, norm_mlp_g[l], w_up_b[l], w_down_b[l],
                                                cast=tuple((w, l + 1) for w, _ in todo),
                                                extra=(sample.x2, a_s, b_s))
        for (_, dst), r in zip(todo, rounded):
            dst[l + 1] = r

    y_p = prompt.x2.reshape(bp, t_p, d_model)
    conv_p, ssm_p, kv_p = jnp.stack(prompt.conv_new), jnp.stack(prompt.ssm_new), prompt.kv3
    mkv_all = jnp.stack(mkv_p)
    mem_k_p = mkv_all[..., :mem_w].reshape(depth, bp, mlen, MEM_HEADS, HEAD_DIM)
    mem_v_p = mkv_all[..., mem_w:].reshape(depth, bp, mlen, MEM_HEADS, HEAD_DIM)
    keep = min(max(w for w, _ in DIL_GROUPS), t_p)
    win_k_p, win_v_p = split_kv_heads(kv_p, t_p - keep, keep)

    y_s = sample.x2.reshape(bs, SAMPLE_ROWS, d_model)
    conv_s, ssm_s, kv_s = jnp.stack(sample.conv_new), jnp.stack(sample.ssm_new), sample.kv3
    y_s = y_s[:, :t_s]
    k_s = kv_s[:, :t_s, :kv_w].reshape(bs, t_s, DIL_HEADS, HEAD_DIM)
    v_s = kv_s[:, :t_s, kv_w:].reshape(bs, t_s, DIL_HEADS, HEAD_DIM)

    return (y_p, y_s, conv_p, ssm_p, win_k_p, win_v_p, mem_k_p, mem_v_p, conv_s, ssm_s, k_s, v_s)
```
